```python
import math
import jax, jax.numpy as jnp
from jax import lax
import numpy as np

D_MODEL = 1024
BATCH = 8
SEQ = 16384
DEPTH = 1

D_RNN = 1280
LRU_BLOCKS = 10
LRU_BLOCK_W = D_RNN // LRU_BLOCKS
CONV_W = 4
LRU_C = 8.0
N_HEADS = 16
HEAD_DIM = 64
ATTN_W = N_HEADS * HEAD_DIM
DILATED_GROUPS = ((128, 1), (512, 4), (2048, 16))
ATTN_BLOCK = 128
D_FF = ((math.ceil(8 * D_MODEL / 3) + 255) // 256) * 256
DEEPNORM_ALPHA = (2.0 * DEPTH) ** 0.25
DEEPNORM_BETA = (8.0 * DEPTH) ** -0.25
LN_EPS = 1e-5
IN_SPLITS = (D_RNN, D_RNN, ATTN_W, ATTN_W, ATTN_W, D_MODEL, D_MODEL)
IN_WIDTH = sum(IN_SPLITS)

kernel_name = "hybrid_rglru_dilated_attn_swiglu_deepnorm_adaln"


def layernorm(x, g=None, b=None):
    xf = x.astype(jnp.float32)
    mu = jnp.mean(xf, axis=-1, keepdims=True)
    var = jnp.mean(jnp.square(xf - mu), axis=-1, keepdims=True)
    y = (xf - mu) * lax.rsqrt(var + LN_EPS)
    if g is not None:
        y = y * g.astype(jnp.float32) + b.astype(jnp.float32)
    return y.astype(x.dtype)


def causal_depthwise_conv(x, w, b):
    S = x.shape[1]
    xp = jnp.pad(x, ((0, 0), (CONV_W - 1, 0), (0, 0)))
    out = b
    for tap in range(CONV_W):
        out = out + xp[:, tap:tap + S] * w[tap]
    return out


def rg_lru(x, wa, ba, wi, bi, lam):
    B, S, _ = x.shape
    xf = x.astype(jnp.float32)
    xb = xf.reshape(B, S, LRU_BLOCKS, LRU_BLOCK_W)
    r = jax.nn.sigmoid(jnp.einsum('bsnc,ncd->bsnd', xb, wa.astype(jnp.float32)) + ba.astype(jnp.float32)).reshape(B, S, D_RNN)
    i = jax.nn.sigmoid(jnp.einsum('bsnc,ncd->bsnd', xb, wi.astype(jnp.float32)) + bi.astype(jnp.float32)).reshape(B, S, D_RNN)
    log_a = -LRU_C * r * jax.nn.softplus(-lam.astype(jnp.float32))
    a = jnp.exp(log_a)
    mult = jnp.sqrt(1.0 - jnp.exp(2.0 * log_a))
    mult = jnp.where(jnp.arange(S)[None, :, None] == 0, 1.0, mult)
    u = mult * (i * xf)

    def step(h, au):
        a_t, u_t = au
        h = a_t * h + u_t
        return h, h

    _, hs = lax.scan(step, jnp.zeros((B, D_RNN), jnp.float32),
                     (jnp.swapaxes(a, 0, 1), jnp.swapaxes(u, 0, 1)))
    return jnp.swapaxes(hs, 0, 1).astype(x.dtype)


def dilated_window_group(q, k, v, window, dilation):
    B, S, H, Dh = q.shape
    L = S // dilation
    Z = B * dilation
    nb = -(-L // ATTN_BLOCK)
    Lp = nb * ATTN_BLOCK
    span = window // dilation

    def strided_blocks(t):
        t = t.reshape(B, L, dilation, H, Dh).transpose(0, 2, 1, 3, 4).reshape(Z, L, H, Dh)
        t = jnp.pad(t, ((0, 0), (0, Lp - L), (0, 0), (0, 0)))
        return t.reshape(Z, nb, ATTN_BLOCK, H, Dh)

    def with_previous_block(t):
        prev = jnp.pad(t[:, :-1], ((0, 0), (1, 0), (0, 0), (0, 0), (0, 0)))
        return jnp.concatenate([prev, t], axis=2)

    qb = strided_blocks(q)
    kc = with_previous_block(strided_blocks(k))
    vc = with_previous_block(strided_blocks(v))
    s = jnp.einsum('znqhd,znkhd->znhqk', qb, kc, preferred_element_type=jnp.float32) * (Dh ** -0.5)
    qi = jnp.arange(ATTN_BLOCK)[:, None]
    ki = jnp.arange(2 * ATTN_BLOCK)[None, :]
    dist = ATTN_BLOCK + qi - ki
    blk = jnp.arange(nb)[:, None, None]
    valid = (dist >= 0) & (dist <= span) & ((blk > 0) | (ki >= ATTN_BLOCK))
    s = jnp.where(valid[None, :, None], s, -jnp.inf)
    m = jnp.max(s, axis=-1, keepdims=True)
    p = jnp.exp(s - m)
    denom = jnp.sum(p, axis=-1, keepdims=True)
    o = jnp.einsum('znhqk,znkhd->znqhd', p, vc.astype(jnp.float32)) / denom.transpose(0, 1, 3, 2, 4)
    lse = (m + jnp.log(denom))[..., 0].transpose(0, 1, 3, 2)

    def unstride(t):
        t = t.reshape(B, dilation, Lp, *t.shape[3:])[:, :, :L]
        t = jnp.moveaxis(t, 1, 2)
        return t.reshape(B, S, *t.shape[3:])

    return unstride(o), unstride(lse)


def dilated_mixture_attention(q, k, v):
    outs, lses = [], []
    for window, dilation in DILATED_GROUPS:
        o, l = dilated_window_group(q, k, v, window, dilation)
        outs.append(o)
        lses.append(l)
    w = jax.nn.softmax(jnp.stack(lses), axis=0)
    o = jnp.einsum('gbsh,gbshd->bshd', w, jnp.stack(outs))
    return o.astype(q.dtype)


def token_mixing(h, w_in, conv_w, conv_b, lru_wa, lru_ba, lru_wi, lru_bi, lru_lambda,
                 w_proj_rnn, w_proj_attn, b_gate, w_out):
    B, S, _ = h.shape
    proj = h @ w_in
    cuts = [int(c) for c in np.cumsum(IN_SPLITS)[:-1]]
    y, xr, q, k, v, ga, gb = jnp.split(proj, cuts, axis=-1)
    xr = causal_depthwise_conv(xr, conv_w, conv_b)
    hr = rg_lru(xr, lru_wa, lru_ba, lru_wi, lru_bi, lru_lambda)
    u_rnn = jax.nn.gelu(y) * hr
    shp = (B, S, N_HEADS, HEAD_DIM)
    u_attn = dilated_mixture_attention(q.reshape(shp), k.reshape(shp), v.reshape(shp)).reshape(B, S, ATTN_W)
    g_rnn = jax.nn.sigmoid(ga + b_gate[0])
    g_attn = jax.nn.sigmoid(gb + b_gate[1])
    merged = g_rnn * (u_rnn @ w_proj_rnn) + g_attn * (u_attn @ w_proj_attn)
    return merged @ w_out


def swiglu(h, w_gate, w_up, w_down):
    return (jax.nn.silu(h @ w_gate) * (h @ w_up)) @ w_down


def _fwd_setup_inputs(seed: int = 0) -> dict:
    key = jax.random.key(seed)
    ks = jax.random.split(key, 24)
    f32 = jnp.float32

    def nrm(k, shape, fan_in, scale=1.0):
        return (scale * fan_in ** -0.5) * jax.random.normal(k, shape, f32)

    def small(k, shape):
        return 0.01 * jax.random.normal(k, shape, f32)

    a_target = jax.random.uniform(ks[11], (DEPTH, D_RNN), f32, 0.9, 0.999)
    p = a_target ** (1.0 / LRU_C)
    lru_lambda = jnp.log(p) - jnp.log1p(-p)
    return {
        "x": jax.random.normal(ks[0], (BATCH, SEQ, D_MODEL), f32),
        "c": jax.random.normal(ks[1], (BATCH, D_MODEL), f32),
        "w_ada": nrm(ks[2], (DEPTH, D_MODEL, 6 * D_MODEL), D_MODEL),
        "b_ada": small(ks[3], (DEPTH, 6 * D_MODEL)),
        "w_in": nrm(ks[4], (DEPTH, D_MODEL, IN_WIDTH), D_MODEL),
        "conv_w": nrm(ks[5], (DEPTH, CONV_W, D_RNN), CONV_W),
        "conv_b": small(ks[6], (DEPTH, D_RNN)),
        "lru_wa": nrm(ks[7], (DEPTH, LRU_BLOCKS, LRU_BLOCK_W, LRU_BLOCK_W), LRU_BLOCK_W),
        "lru_ba": small(ks[8], (DEPTH, LRU_BLOCKS, LRU_BLOCK_W)),
        "lru_wi": nrm(ks[9], (DEPTH, LRU_BLOCKS, LRU_BLOCK_W, LRU_BLOCK_W), LRU_BLOCK_W),
        "lru_bi": small(ks[10], (DEPTH, LRU_BLOCKS, LRU_BLOCK_W)),
        "lru_lambda": lru_lambda,
        "w_proj_rnn": nrm(ks[12], (DEPTH, D_RNN, D_MODEL), D_RNN),
        "w_proj_attn": nrm(ks[13], (DEPTH, ATTN_W, D_MODEL), ATTN_W),
        "b_gate": small(ks[14], (DEPTH, 2, D_MODEL)),
        "w_out": nrm(ks[15], (DEPTH, D_MODEL, D_MODEL), D_MODEL, DEEPNORM_BETA),
        "ln1_g": 1.0 + small(ks[16], (DEPTH, D_MODEL)),
        "ln1_b": small(ks[17], (DEPTH, D_MODEL)),
        "w_ffn_gate": nrm(ks[18], (DEPTH, D_MODEL, D_FF), D_MODEL),
        "w_ffn_up": nrm(ks[19], (DEPTH, D_MODEL, D_FF), D_MODEL),
        "w_ffn_down": nrm(ks[20], (DEPTH, D_FF, D_MODEL), D_FF, DEEPNORM_BETA),
        "ln2_g": 1.0 + small(ks[21], (DEPTH, D_MODEL)),
        "ln2_b": small(ks[22], (DEPTH, D_MODEL)),
    }


def _fwd_reference(x, c, w_ada, b_ada, w_in, conv_w, conv_b, lru_wa, lru_ba, lru_wi, lru_bi, lru_lambda,
              w_proj_rnn, w_proj_attn, b_gate, w_out, ln1_g, ln1_b, w_ffn_gate, w_ffn_up, w_ffn_down,
              ln2_g, ln2_b):
    cond = jax.nn.silu(c)
    for l in range(DEPTH):
        mod = cond @ w_ada[l] + b_ada[l]
        sh1, sc1, g1, sh2, sc2, g2 = [m[:, None, :] for m in jnp.split(mod, 6, axis=-1)]
        h = layernorm(x) * (1.0 + sc1) + sh1
        f = token_mixing(h, w_in[l], conv_w[l], conv_b[l], lru_wa[l], lru_ba[l], lru_wi[l], lru_bi[l],
                         lru_lambda[l], w_proj_rnn[l], w_proj_attn[l], b_gate[l], w_out[l])
        x = layernorm(DEEPNORM_ALPHA * x + g1 * f, ln1_g[l], ln1_b[l])
        h = layernorm(x) * (1.0 + sc2) + sh2
        f = swiglu(h, w_ffn_gate[l], w_ffn_up[l], w_ffn_down[l])
        x = layernorm(DEEPNORM_ALPHA * x + g2 * f, ln2_g[l], ln2_b[l])
    return x


import jax as _jax
import jax.numpy as _jnp

TWIN_FORMAT = 'train_step'
FWD_PARAMS = ['x', 'c', 'w_ada', 'b_ada', 'w_in', 'conv_w', 'conv_b', 'lru_wa', 'lru_ba', 'lru_wi', 'lru_bi', 'lru_lambda', 'w_proj_rnn', 'w_proj_attn', 'b_gate', 'w_out', 'ln1_g', 'ln1_b', 'w_ffn_gate', 'w_ffn_up', 'w_ffn_down', 'ln2_g', 'ln2_b']
TWIN_WEIGHTS = ['w_ada', 'b_ada', 'w_in', 'conv_w', 'conv_b', 'lru_wa', 'lru_ba', 'lru_wi', 'lru_bi', 'lru_lambda', 'w_proj_rnn', 'w_proj_attn', 'b_gate', 'w_out', 'ln1_g', 'ln1_b', 'w_ffn_gate', 'w_ffn_up', 'w_ffn_down', 'ln2_g', 'ln2_b']
TWIN_DIFF_INPUT = 'x'
TWIN_INPUTS = ['x', 'c', 'w_ada', 'b_ada', 'w_in', 'conv_w', 'conv_b', 'lru_wa', 'lru_ba', 'lru_wi', 'lru_bi', 'lru_lambda', 'w_proj_rnn', 'w_proj_attn', 'b_gate', 'w_out', 'ln1_g', 'ln1_b', 'w_ffn_gate', 'w_ffn_up', 'w_ffn_down', 'ln2_g', 'ln2_b', 'loss_target', 'm_w_ada', 'm_b_ada', 'm_w_in', 'm_conv_w', 'm_conv_b', 'm_lru_wa', 'm_lru_ba', 'm_lru_wi', 'm_lru_bi', 'm_lru_lambda', 'm_w_proj_rnn', 'm_w_proj_attn', 'm_b_gate', 'm_w_out', 'm_ln1_g', 'm_ln1_b', 'm_w_ffn_gate', 'm_w_ffn_up', 'm_w_ffn_down', 'm_ln2_g', 'm_ln2_b', 'v_w_ada', 'v_b_ada', 'v_w_in', 'v_conv_w', 'v_conv_b', 'v_lru_wa', 'v_lru_ba', 'v_lru_wi', 'v_lru_bi', 'v_lru_lambda', 'v_w_proj_rnn', 'v_w_proj_attn', 'v_b_gate', 'v_w_out', 'v_ln1_g', 'v_ln1_b', 'v_w_ffn_gate', 'v_w_ffn_up', 'v_w_ffn_down', 'v_ln2_g', 'v_ln2_b']
TWIN_OUTPUTS = ['loss', 'grad_x', 'grad_w_ada', 'grad_b_ada', 'grad_w_in', 'grad_conv_w', 'grad_conv_b', 'grad_lru_wa', 'grad_lru_ba', 'grad_lru_wi', 'grad_lru_bi', 'grad_lru_lambda', 'grad_w_proj_rnn', 'grad_w_proj_attn', 'grad_b_gate', 'grad_w_out', 'grad_ln1_g', 'grad_ln1_b', 'grad_w_ffn_gate', 'grad_w_ffn_up', 'grad_w_ffn_down', 'grad_ln2_g', 'grad_ln2_b', 'delta_w_ada', 'delta_b_ada', 'delta_w_in', 'delta_conv_w', 'delta_conv_b', 'delta_lru_wa', 'delta_lru_ba', 'delta_lru_wi', 'delta_lru_bi', 'delta_lru_lambda', 'delta_w_proj_rnn', 'delta_w_proj_attn', 'delta_b_gate', 'delta_w_out', 'delta_ln1_g', 'delta_ln1_b', 'delta_w_ffn_gate', 'delta_w_ffn_up', 'delta_w_ffn_down', 'delta_ln2_g', 'delta_ln2_b', 'new_m_w_ada', 'new_m_b_ada', 'new_m_w_in', 'new_m_conv_w', 'new_m_conv_b', 'new_m_lru_wa', 'new_m_lru_ba', 'new_m_lru_wi', 'new_m_lru_bi', 'new_m_lru_lambda', 'new_m_w_proj_rnn', 'new_m_w_proj_attn', 'new_m_b_gate', 'new_m_w_out', 'new_m_ln1_g', 'new_m_ln1_b', 'new_m_w_ffn_gate', 'new_m_w_ffn_up', 'new_m_w_ffn_down', 'new_m_ln2_g', 'new_m_ln2_b', 'new_v_w_ada', 'new_v_b_ada', 'new_v_w_in', 'new_v_conv_w', 'new_v_conv_b', 'new_v_lru_wa', 'new_v_lru_ba', 'new_v_lru_wi', 'new_v_lru_bi', 'new_v_lru_lambda', 'new_v_w_proj_rnn', 'new_v_w_proj_attn', 'new_v_b_gate', 'new_v_w_out', 'new_v_ln1_g', 'new_v_ln1_b', 'new_v_w_ffn_gate', 'new_v_w_ffn_up', 'new_v_w_ffn_down', 'new_v_ln2_g', 'new_v_ln2_b']
TWIN_LEAF_KINDS = {'loss': 'loss', 'grad_x': 'grad_x', 'grad_w_ada': 'grad_w', 'grad_b_ada': 'grad_w', 'grad_w_in': 'grad_w', 'grad_conv_w': 'grad_w', 'grad_conv_b': 'grad_w', 'grad_lru_wa': 'grad_w', 'grad_lru_ba': 'grad_w', 'grad_lru_wi': 'grad_w', 'grad_lru_bi': 'grad_w', 'grad_lru_lambda': 'grad_w', 'grad_w_proj_rnn': 'grad_w', 'grad_w_proj_attn': 'grad_w', 'grad_b_gate': 'grad_w', 'grad_w_out': 'grad_w', 'grad_ln1_g': 'grad_w', 'grad_ln1_b': 'grad_w', 'grad_w_ffn_gate': 'grad_w', 'grad_w_ffn_up': 'grad_w', 'grad_w_ffn_down': 'grad_w', 'grad_ln2_g': 'grad_w', 'grad_ln2_b': 'grad_w', 'delta_w_ada': 'delta_w', 'delta_b_ada': 'delta_w', 'delta_w_in': 'delta_w', 'delta_conv_w': 'delta_w', 'delta_conv_b': 'delta_w', 'delta_lru_wa': 'delta_w', 'delta_lru_ba': 'delta_w', 'delta_lru_wi': 'delta_w', 'delta_lru_bi': 'delta_w', 'delta_lru_lambda': 'delta_w', 'delta_w_proj_rnn': 'delta_w', 'delta_w_proj_attn': 'delta_w', 'delta_b_gate': 'delta_w', 'delta_w_out': 'delta_w', 'delta_ln1_g': 'delta_w', 'delta_ln1_b': 'delta_w', 'delta_w_ffn_gate': 'delta_w', 'delta_w_ffn_up': 'delta_w', 'delta_w_ffn_down': 'delta_w', 'delta_ln2_g': 'delta_w', 'delta_ln2_b': 'delta_w', 'new_m_w_ada': 'new_m', 'new_m_b_ada': 'new_m', 'new_m_w_in': 'new_m', 'new_m_conv_w': 'new_m', 'new_m_conv_b': 'new_m', 'new_m_lru_wa': 'new_m', 'new_m_lru_ba': 'new_m', 'new_m_lru_wi': 'new_m', 'new_m_lru_bi': 'new_m', 'new_m_lru_lambda': 'new_m', 'new_m_w_proj_rnn': 'new_m', 'new_m_w_proj_attn': 'new_m', 'new_m_b_gate': 'new_m', 'new_m_w_out': 'new_m', 'new_m_ln1_g': 'new_m', 'new_m_ln1_b': 'new_m', 'new_m_w_ffn_gate': 'new_m', 'new_m_w_ffn_up': 'new_m', 'new_m_w_ffn_down': 'new_m', 'new_m_ln2_g': 'new_m', 'new_m_ln2_b': 'new_m', 'new_v_w_ada': 'new_v', 'new_v_b_ada': 'new_v', 'new_v_w_in': 'new_v', 'new_v_conv_w': 'new_v', 'new_v_conv_b': 'new_v', 'new_v_lru_wa': 'new_v', 'new_v_lru_ba': 'new_v', 'new_v_lru_wi': 'new_v', 'new_v_lru_bi': 'new_v', 'new_v_lru_lambda': 'new_v', 'new_v_w_proj_rnn': 'new_v', 'new_v_w_proj_attn': 'new_v', 'new_v_b_gate': 'new_v', 'new_v_w_out': 'new_v', 'new_v_ln1_g': 'new_v', 'new_v_ln1_b': 'new_v', 'new_v_w_ffn_gate': 'new_v', 'new_v_w_ffn_up': 'new_v', 'new_v_w_ffn_down': 'new_v', 'new_v_ln2_g': 'new_v', 'new_v_ln2_b': 'new_v'}


def _forward(args):
    return _fwd_reference(*[args[k] for k in FWD_PARAMS])


def _output_shape():
    def fwd():
        inp = _fwd_setup_inputs(0)
        return _fwd_reference(*[inp[k] for k in FWD_PARAMS])
    out = _jax.eval_shape(fwd)
    return out.shape, out.dtype

N_MICROBATCH = 1
ADAM_LR = 0.001
ADAM_B1 = 0.9
ADAM_B2 = 0.999
ADAM_EPS = 1e-08
ADAM_WD = 0.01
ADAM_STEP = 10
PER_EXAMPLE_BATCH_AXIS = {'x': 0, 'c': 0, 'loss_target': 0}
SHARED_INPUTS = []
_WEIGHT_DTYPES = {'w_ada': _jnp.float32, 'b_ada': _jnp.float32, 'w_in': _jnp.float32, 'conv_w': _jnp.float32, 'conv_b': _jnp.float32, 'lru_wa': _jnp.float32, 'lru_ba': _jnp.float32, 'lru_wi': _jnp.float32, 'lru_bi': _jnp.float32, 'lru_lambda': _jnp.float32, 'w_proj_rnn': _jnp.float32, 'w_proj_attn': _jnp.float32, 'b_gate': _jnp.float32, 'w_out': _jnp.float32, 'ln1_g': _jnp.float32, 'ln1_b': _jnp.float32, 'w_ffn_gate': _jnp.float32, 'w_ffn_up': _jnp.float32, 'w_ffn_down': _jnp.float32, 'ln2_g': _jnp.float32, 'ln2_b': _jnp.float32}
MOMENT_SCALE = {'w_ada': 1.819888e-01, 'b_ada': 2.955528e-01, 'w_in': 1.417184e-01, 'conv_w': 2.215772e-01, 'conv_b': 3.361512e-01, 'lru_wa': 2.699597e-02, 'lru_ba': 5.157581e-02, 'lru_wi': 5.739930e-02, 'lru_bi': 8.481462e-02, 'lru_lambda': 1.439038e-01, 'w_proj_rnn': 2.812547e-01, 'w_proj_attn': 4.184965e-02, 'b_gate': 6.136828e-02, 'w_out': 4.531617e-01, 'ln1_g': 1.490423e+00, 'ln1_b': 7.496923e-01, 'w_ffn_gate': 7.014453e-02, 'w_ffn_up': 6.832404e-02, 'w_ffn_down': 1.905146e-01, 'ln2_g': 1.286497e+02, 'ln2_b': 1.047722e+01}


def _to_microbatches(a, axis):
    t = _jnp.moveaxis(a, axis, 0)
    t = t.reshape((N_MICROBATCH, t.shape[0] // N_MICROBATCH) + t.shape[1:])
    return _jnp.moveaxis(t, 1, axis + 1)


def setup_inputs(seed: int = 0) -> dict:
    inp = _fwd_setup_inputs(seed)
    key = _jax.random.fold_in(_jax.random.key(seed), 7919)
    shape, _ = _output_shape()
    out = dict(inp)
    out["loss_target"] = _jax.random.normal(_jax.random.fold_in(key, 0), shape, _jnp.float32)
    for i, name in enumerate(TWIN_WEIGHTS):
        w = inp[name].astype(_jnp.float32)
        if MOMENT_SCALE is None:
            s = _jnp.sqrt(_jnp.mean(_jnp.square(w)) + 1e-30)
        else:
            s = MOMENT_SCALE[name]
        km, kv = _jax.random.split(_jax.random.fold_in(key, i + 1))
        out[name] = w
        out["m_" + name] = s * _jax.random.normal(km, w.shape, _jnp.float32)
        out["v_" + name] = (s * s) * _jax.random.uniform(kv, w.shape, _jnp.float32, 0.5, 1.5)
    if N_MICROBATCH > 1:
        for name, axis in PER_EXAMPLE_BATCH_AXIS.items():
            out[name] = _to_microbatches(out[name], axis)
    return {'x': out['x'], 'c': out['c'], 'w_ada': out['w_ada'], 'b_ada': out['b_ada'], 'w_in': out['w_in'], 'conv_w': out['conv_w'], 'conv_b': out['conv_b'], 'lru_wa': out['lru_wa'], 'lru_ba': out['lru_ba'], 'lru_wi': out['lru_wi'], 'lru_bi': out['lru_bi'], 'lru_lambda': out['lru_lambda'], 'w_proj_rnn': out['w_proj_rnn'], 'w_proj_attn': out['w_proj_attn'], 'b_gate': out['b_gate'], 'w_out': out['w_out'], 'ln1_g': out['ln1_g'], 'ln1_b': out['ln1_b'], 'w_ffn_gate': out['w_ffn_gate'], 'w_ffn_up': out['w_ffn_up'], 'w_ffn_down': out['w_ffn_down'], 'ln2_g': out['ln2_g'], 'ln2_b': out['ln2_b'], 'loss_target': out['loss_target'], 'm_w_ada': out['m_w_ada'], 'm_b_ada': out['m_b_ada'], 'm_w_in': out['m_w_in'], 'm_conv_w': out['m_conv_w'], 'm_conv_b': out['m_conv_b'], 'm_lru_wa': out['m_lru_wa'], 'm_lru_ba': out['m_lru_ba'], 'm_lru_wi': out['m_lru_wi'], 'm_lru_bi': out['m_lru_bi'], 'm_lru_lambda': out['m_lru_lambda'], 'm_w_proj_rnn': out['m_w_proj_rnn'], 'm_w_proj_attn': out['m_w_proj_attn'], 'm_b_gate': out['m_b_gate'], 'm_w_out': out['m_w_out'], 'm_ln1_g': out['m_ln1_g'], 'm_ln1_b': out['m_ln1_b'], 'm_w_ffn_gate': out['m_w_ffn_gate'], 'm_w_ffn_up': out['m_w_ffn_up'], 'm_w_ffn_down': out['m_w_ffn_down'], 'm_ln2_g': out['m_ln2_g'], 'm_ln2_b': out['m_ln2_b'], 'v_w_ada': out['v_w_ada'], 'v_b_ada': out['v_b_ada'], 'v_w_in': out['v_w_in'], 'v_conv_w': out['v_conv_w'], 'v_conv_b': out['v_conv_b'], 'v_lru_wa': out['v_lru_wa'], 'v_lru_ba': out['v_lru_ba'], 'v_lru_wi': out['v_lru_wi'], 'v_lru_bi': out['v_lru_bi'], 'v_lru_lambda': out['v_lru_lambda'], 'v_w_proj_rnn': out['v_w_proj_rnn'], 'v_w_proj_attn': out['v_w_proj_attn'], 'v_b_gate': out['v_b_gate'], 'v_w_out': out['v_w_out'], 'v_ln1_g': out['v_ln1_g'], 'v_ln1_b': out['v_ln1_b'], 'v_w_ffn_gate': out['v_w_ffn_gate'], 'v_w_ffn_up': out['v_w_ffn_up'], 'v_w_ffn_down': out['v_w_ffn_down'], 'v_ln2_g': out['v_ln2_g'], 'v_ln2_b': out['v_ln2_b']}


def _loss(weights, diff, rest, loss_target):
    with _jax.named_scope("forward"):
        args = {**rest, TWIN_DIFF_INPUT: diff, **{k: w.astype(_WEIGHT_DTYPES[k]) for k, w in weights.items()}}
        y = _forward(args)
    with _jax.named_scope("loss_head"):
        err = _jnp.square(y.astype(_jnp.float32) - loss_target)
        return 0.5 * _jnp.sum(_jnp.mean(err, axis=-1)) if err.ndim else 0.5 * err


def _adamw(w, g, m, v):
    m = ADAM_B1 * m + (1.0 - ADAM_B1) * g
    v = ADAM_B2 * v + (1.0 - ADAM_B2) * _jnp.square(g)
    m_hat = m / (1.0 - ADAM_B1 ** ADAM_STEP)
    v_hat = v / (1.0 - ADAM_B2 ** ADAM_STEP)
    delta = -ADAM_LR * (m_hat / (_jnp.sqrt(v_hat) + ADAM_EPS) + ADAM_WD * w)
    return delta, m, v


def reference(x, c, w_ada, b_ada, w_in, conv_w, conv_b, lru_wa, lru_ba, lru_wi, lru_bi, lru_lambda, w_proj_rnn, w_proj_attn, b_gate, w_out, ln1_g, ln1_b, w_ffn_gate, w_ffn_up, w_ffn_down, ln2_g, ln2_b, loss_target, m_w_ada, m_b_ada, m_w_in, m_conv_w, m_conv_b, m_lru_wa, m_lru_ba, m_lru_wi, m_lru_bi, m_lru_lambda, m_w_proj_rnn, m_w_proj_attn, m_b_gate, m_w_out, m_ln1_g, m_ln1_b, m_w_ffn_gate, m_w_ffn_up, m_w_ffn_down, m_ln2_g, m_ln2_b, v_w_ada, v_b_ada, v_w_in, v_conv_w, v_conv_b, v_lru_wa, v_lru_ba, v_lru_wi, v_lru_bi, v_lru_lambda, v_w_proj_rnn, v_w_proj_attn, v_b_gate, v_w_out, v_ln1_g, v_ln1_b, v_w_ffn_gate, v_w_ffn_up, v_w_ffn_down, v_ln2_g, v_ln2_b):
    given = dict(x=x, c=c, w_ada=w_ada, b_ada=b_ada, w_in=w_in, conv_w=conv_w, conv_b=conv_b, lru_wa=lru_wa, lru_ba=lru_ba, lru_wi=lru_wi, lru_bi=lru_bi, lru_lambda=lru_lambda, w_proj_rnn=w_proj_rnn, w_proj_attn=w_proj_attn, b_gate=b_gate, w_out=w_out, ln1_g=ln1_g, ln1_b=ln1_b, w_ffn_gate=w_ffn_gate, w_ffn_up=w_ffn_up, w_ffn_down=w_ffn_down, ln2_g=ln2_g, ln2_b=ln2_b, loss_target=loss_target, m_w_ada=m_w_ada, m_b_ada=m_b_ada, m_w_in=m_w_in, m_conv_w=m_conv_w, m_conv_b=m_conv_b, m_lru_wa=m_lru_wa, m_lru_ba=m_lru_ba, m_lru_wi=m_lru_wi, m_lru_bi=m_lru_bi, m_lru_lambda=m_lru_lambda, m_w_proj_rnn=m_w_proj_rnn, m_w_proj_attn=m_w_proj_attn, m_b_gate=m_b_gate, m_w_out=m_w_out, m_ln1_g=m_ln1_g, m_ln1_b=m_ln1_b, m_w_ffn_gate=m_w_ffn_gate, m_w_ffn_up=m_w_ffn_up, m_w_ffn_down=m_w_ffn_down, m_ln2_g=m_ln2_g, m_ln2_b=m_ln2_b, v_w_ada=v_w_ada, v_b_ada=v_b_ada, v_w_in=v_w_in, v_conv_w=v_conv_w, v_conv_b=v_conv_b, v_lru_wa=v_lru_wa, v_lru_ba=v_lru_ba, v_lru_wi=v_lru_wi, v_lru_bi=v_lru_bi, v_lru_lambda=v_lru_lambda, v_w_proj_rnn=v_w_proj_rnn, v_w_proj_attn=v_w_proj_attn, v_b_gate=v_b_gate, v_w_out=v_w_out, v_ln1_g=v_ln1_g, v_ln1_b=v_ln1_b, v_w_ffn_gate=v_w_ffn_gate, v_w_ffn_up=v_w_ffn_up, v_w_ffn_down=v_w_ffn_down, v_ln2_g=v_ln2_g, v_ln2_b=v_ln2_b)
    weights = {n: given[n] for n in TWIN_WEIGHTS}
    shared = {n: given[n] for n in SHARED_INPUTS}
    per_example = {n: given[n] for n in ['x', 'c']}
    grad_fn = _jax.value_and_grad(_loss, argnums=(0, 1))

    def one_microbatch(ex, loss_target):
        ex = dict(ex)
        diff = ex.pop(TWIN_DIFF_INPUT)
        return grad_fn(weights, diff, {**shared, **ex}, loss_target)

    if N_MICROBATCH == 1:
        loss, (grad_w, grad_x) = one_microbatch(per_example, given["loss_target"])
    else:
        def body(carry, xs):
            loss_sum, grad_sum = carry
            l_k, (gw_k, gx_k) = one_microbatch(xs[0], xs[1])
            with _jax.named_scope("update"):
                return (loss_sum + l_k, _jax.tree.map(_jnp.add, grad_sum, gw_k)), gx_k

        init = (_jnp.zeros((), _jnp.float32), _jax.tree.map(_jnp.zeros_like, weights))
        (loss, grad_w), grad_x = _jax.lax.scan(body, init, (per_example, given["loss_target"]))
    with _jax.named_scope("update"):
        delta_w, new_m, new_v = {}, {}, {}
        for n in TWIN_WEIGHTS:
            delta_w[n], new_m[n], new_v[n] = _adamw(weights[n], grad_w[n], given["m_" + n], given["v_" + n])
    return (loss, grad_x, *[grad_w[n] for n in TWIN_WEIGHTS], *[delta_w[n] for n in TWIN_WEIGHTS],
            *[new_m[n] for n in TWIN_WEIGHTS], *[new_v[n] for n in TWIN_WEIGHTS])
```

```python
import functools

import jax
import jax.numpy as jnp
from jax import lax
from jax.experimental import pallas as pl
from jax.experimental.pallas import tpu as pltpu

F32 = jnp.float32
BF16 = jnp.bfloat16
MESH = pl.DeviceIdType.MESH

N_DEV = 8
D = 1024
DR = 1280
N_BLK = 10
BW = 128
HEAD_PAIR = 128
N_PAIR = 8
QB = 128
DFF = 2816
CONV_W = 4
DILATIONS = (1, 4, 16)
ALPHA = 2.0 ** 0.25
LN_EPS = 1e-5
LRU_C = 8.0
SCALE = 0.125
NEG = -1e30
ADAM_LR, ADAM_B1, ADAM_B2, ADAM_EPS, ADAM_WD, ADAM_STEP = 0.001, 0.9, 0.999, 1e-08, 0.01, 10

VMEM_LIMIT = 48 * 1024 * 1024
ROW_TILE = 256
RNN_CHUNK = 256

PACK_ROWS = (("w_in", 960), ("w_ffn_gate", 352), ("w_ffn_up", 352), ("w_proj_rnn", 160),
             ("w_proj_attn", 128), ("w_out", 128), ("w_ffn_down", 352))
PACK_R = sum(r for _, r in PACK_ROWS)


def _cparams(sem):
    return pltpu.CompilerParams(dimension_semantics=sem, vmem_limit_bytes=VMEM_LIMIT)


def _ln(z):
    mu = jnp.mean(z, axis=-1, keepdims=True)
    zc = z - mu
    var = jnp.mean(zc * zc, axis=-1, keepdims=True)
    rstd = lax.rsqrt(var + LN_EPS)
    return zc * rstd, rstd


def _ln_bwd(dn, n, rstd):
    return rstd * (dn - jnp.mean(dn, axis=-1, keepdims=True) - n * jnp.mean(dn * n, axis=-1, keepdims=True))


def _sigmoid(x):
    return 1.0 / (1.0 + jnp.exp(-x))


_GELU_K = 0.7978845608028654
_GELU_C = 0.044715


def _gelu(y):
    t = jnp.tanh(_GELU_K * (y + _GELU_C * y * y * y))
    return 0.5 * y * (1.0 + t), t


def _gelu_grad(y, t):
    return 0.5 * (1.0 + t) + 0.5 * y * (1.0 - t * t) * _GELU_K * (1.0 + 3.0 * _GELU_C * y * y)


def _rowsum(v):
    return jnp.sum(v, axis=0, keepdims=True)


def _rowwise(body, name, n_rows, tm, row_ins, full_ins, row_outs, acc_outs):
    nri, nfi, nro = len(row_ins), len(full_ins), len(row_outs)

    def kern(*refs):
        ri, fi = refs[:nri], refs[nri:nri + nfi]
        ro, ao = refs[nri + nfi:nri + nfi + nro], refs[nri + nfi + nro:]
        if ao:
            @pl.when(pl.program_id(0) == 0)
            def _():
                for a in ao:
                    a[...] = jnp.zeros(a.shape, a.dtype)
        body(ri, fi, ro, ao)

    in_specs = [pl.BlockSpec((tm, w), functools.partial(lambda i, cb: (i, cb), cb=cb)) for _, w, cb in row_ins]
    in_specs += [pl.BlockSpec(a.shape, lambda i: (0, 0)) for a in full_ins]
    out_specs = [pl.BlockSpec((tm, w), lambda i: (i, 0)) for w, _ in row_outs]
    out_specs += [pl.BlockSpec(s, lambda i: (0, 0)) for s, _ in acc_outs]
    out_shape = [jax.ShapeDtypeStruct((n_rows, w), dt) for w, dt in row_outs]
    out_shape += [jax.ShapeDtypeStruct(s, dt) for s, dt in acc_outs]
    return pl.pallas_call(
        kern, name=name, grid=(n_rows // tm,), in_specs=in_specs, out_specs=out_specs, out_shape=out_shape,
        compiler_params=_cparams(("arbitrary",)),
    )(*[a for a, _, _ in row_ins], *full_ins)


def _mm(name, pairs, out_dtype, tm, tn):
    m_rows, n_cols = pairs[0][0].shape[0], pairs[0][1].shape[1]
    n_pairs = len(pairs)

    def kern(*refs):
        acc = None
        for p in range(n_pairs):
            t = jnp.dot(refs[2 * p][...], refs[2 * p + 1][...], preferred_element_type=F32)
            acc = t if acc is None else acc + t
        refs[-1][...] = acc.astype(refs[-1].dtype)

    in_specs, flat = [], []
    for a, w in pairs:
        k = a.shape[1]
        in_specs += [pl.BlockSpec((tm, k), lambda j, i: (i, 0)), pl.BlockSpec((k, tn), lambda j, i: (0, j))]
        flat += [a, w]
    return pl.pallas_call(
        kern, name=name, grid=(n_cols // tn, m_rows // tm), in_specs=in_specs,
        out_specs=pl.BlockSpec((tm, tn), lambda j, i: (i, j)),
        out_shape=jax.ShapeDtypeStruct((m_rows, n_cols), out_dtype),
        compiler_params=_cparams(("parallel", "parallel")),
    )(*flat)


def _mm_tn(name, a, g, tm, tn):
    m_rows, k = a.shape
    n_cols = g.shape[1]

    def kern(a_ref, g_ref, o_ref):
        @pl.when(pl.program_id(1) == 0)
        def _():
            o_ref[...] = jnp.zeros(o_ref.shape, F32)
        o_ref[...] += lax.dot_general(a_ref[...], g_ref[...], (((0,), (0,)), ((), ())), preferred_element_type=F32)

    return pl.pallas_call(
        kern, name=name, grid=(n_cols // tn, m_rows // tm),
        in_specs=[pl.BlockSpec((tm, k), lambda j, m: (m, 0)), pl.BlockSpec((tm, tn), lambda j, m: (m, j))],
        out_specs=pl.BlockSpec((k, tn), lambda j, m: (0, j)),
        out_shape=jax.ShapeDtypeStruct((k, n_cols), F32),
        compiler_params=_cparams(("parallel", "arbitrary")),
    )(a, g)


def _peers():
    x, y, c = lax.axis_index("x"), lax.axis_index("y"), lax.axis_index("c")
    me = 4 * x + 2 * y + c
    peers = []
    for k in range(1, N_DEV):
        px = 1 - x if (k >> 2) & 1 else x
        py = 1 - y if (k >> 1) & 1 else y
        pc = 1 - c if k & 1 else c
        peers.append(((px, py, pc), 4 * px + 2 * py + pc))
    return me, peers


def _exchange(me, peers, items, send_sems, recv_sems, t0=0):
    started = []
    for t, (src_of, dst_of) in enumerate(items, start=t0):
        for k, (pid, plin) in enumerate(peers):
            cp = pltpu.make_async_remote_copy(
                src_ref=src_of(plin), dst_ref=dst_of(me), send_sem=send_sems.at[t, k], recv_sem=recv_sems.at[t, k],
                device_id=pid, device_id_type=MESH)
            cp.start()
            started.append(cp)
    for t, (src_of, dst_of) in enumerate(items, start=t0):
        for k, (pid, plin) in enumerate(peers):
            pltpu.make_async_remote_copy(
                src_ref=src_of(plin), dst_ref=dst_of(plin), send_sem=send_sems.at[t, k], recv_sem=recv_sems.at[t, k],
                device_id=pid, device_id_type=MESH).wait_recv()
    for cp in started:
        cp.wait_send()


def _hbm_spec():
    return pl.BlockSpec(memory_space=pltpu.HBM)


def _gather_shards(shard):
    def body(src, out, send_sems, recv_sems, own_sem):
        me, peers = _peers()
        own = pltpu.make_async_copy(src, out.at[me], own_sem)
        own.start()
        _exchange(me, peers, [(lambda p: src, lambda s: out.at[s])], send_sems, recv_sems)
        own.wait()

    return pl.pallas_call(
        body, name="gather_weights", in_specs=[_hbm_spec()], out_specs=_hbm_spec(),
        out_shape=jax.ShapeDtypeStruct((N_DEV,) + shard.shape, shard.dtype),
        scratch_shapes=[pltpu.SemaphoreType.DMA((1, N_DEV - 1)), pltpu.SemaphoreType.DMA((1, N_DEV - 1)),
                        pltpu.SemaphoreType.DMA(())],
    )(shard)


def _exchange_grads(g_big, g_small):
    def body(big, small, out_big, out_small, send_sems, recv_sems, own_sems):
        me, peers = _peers()
        own_b = pltpu.make_async_copy(big.at[me], out_big.at[me], own_sems.at[0])
        own_s = pltpu.make_async_copy(small, out_small.at[me], own_sems.at[1])
        own_b.start()
        own_s.start()
        _exchange(me, peers, [(lambda p: small, lambda s: out_small.at[s]),
                              (lambda p: big.at[p], lambda s: out_big.at[s])], send_sems, recv_sems)
        own_b.wait()
        own_s.wait()

    return pl.pallas_call(
        body, name="exchange_grads", in_specs=[_hbm_spec(), _hbm_spec()], out_specs=[_hbm_spec(), _hbm_spec()],
        out_shape=[jax.ShapeDtypeStruct(g_big.shape, g_big.dtype),
                   jax.ShapeDtypeStruct((N_DEV,) + g_small.shape, g_small.dtype)],
        scratch_shapes=[pltpu.SemaphoreType.DMA((2, N_DEV - 1)), pltpu.SemaphoreType.DMA((2, N_DEV - 1)),
                        pltpu.SemaphoreType.DMA((2,))],
    )(g_big, g_small)


def _ada_modulation(c8, w_ada, b_ada_cols):
    wcols = w_ada.shape[1]

    def body(c_ref, w_ref, b_ref, call_ref, cond_ref, mod_ref, res, send_sems, recv_sems):
        me, peers = _peers()
        call_ref[me] = c_ref[...]
        _exchange(me, peers, [(lambda p: c_ref, lambda s: call_ref.at[s])], send_sems, recv_sems, t0=0)
        for dev in range(N_DEV):
            cv = call_ref[dev]
            cond = cv * _sigmoid(cv)
            cond_ref[dev] = cond
            res[dev] = jnp.dot(cond, w_ref[...], preferred_element_type=F32,
                               precision=lax.Precision.HIGHEST) + b_ref[...]
        mod_ref[me] = res[me]
        _exchange(me, peers, [(lambda p: res.at[p], lambda s: mod_ref.at[s])], send_sems, recv_sems, t0=1)

    vm = pl.BlockSpec(memory_space=pltpu.VMEM)
    return pl.pallas_call(
        body, name="ada_modulation", in_specs=[vm, vm, vm], out_specs=[vm, vm, vm],
        out_shape=[jax.ShapeDtypeStruct((N_DEV, 8, D), F32), jax.ShapeDtypeStruct((N_DEV, 8, D), F32),
                   jax.ShapeDtypeStruct((N_DEV, 8, wcols), F32)],
        scratch_shapes=[pltpu.VMEM((N_DEV, 8, wcols), F32),
                        pltpu.SemaphoreType.DMA((2, N_DEV - 1)), pltpu.SemaphoreType.DMA((2, N_DEV - 1))],
        compiler_params=pltpu.CompilerParams(vmem_limit_bytes=VMEM_LIMIT),
    )(c8, w_ada, b_ada_cols)


def _lru_gates(xc, wa_ref, ba_ref, wi_ref, bi_ref, sp_ref, row0):
    xcb = xc.astype(BF16)
    pre_r, pre_i = [], []
    for n in range(N_BLK):
        xb = xcb[:, n * BW:(n + 1) * BW]
        pre_r.append(jnp.dot(xb, wa_ref[n], preferred_element_type=F32))
        pre_i.append(jnp.dot(xb, wi_ref[n], preferred_element_type=F32))
    r = _sigmoid(jnp.concatenate(pre_r, axis=1) + ba_ref[...])
    ig = _sigmoid(jnp.concatenate(pre_i, axis=1) + bi_ref[...])
    log_a = (-LRU_C) * r * sp_ref[...]
    a = jnp.exp(log_a)
    e2 = jnp.exp(2.0 * log_a)
    mult_raw = jnp.sqrt(1.0 - e2)
    rows = row0 + lax.broadcasted_iota(jnp.int32, xc.shape, 0)
    start = rows == 0
    mult = jnp.where(start, 1.0, mult_raw)
    return xcb, r, ig, a, e2, mult_raw, mult, start


def _conv(xpad, cw_ref, cb_ref, tc):
    out = cb_ref[...]
    for tap in range(CONV_W):
        out = out + cw_ref[tap:tap + 1, :] * xpad[pl.ds(8 - (CONV_W - 1) + tap, tc), :]
    return out


def _rnn_fwd(proj_rnn, cw, cb, wa, ba, wi, bi, sp):
    s_len = proj_rnn.shape[0]
    tc = RNN_CHUNK

    def kern(y_ref, xr_ref, cw_ref, cb_ref, wa_ref, ba_ref, wi_ref, bi_ref, sp_ref, hs_ref, ur_ref,
             xpad, a_scr, u_scr, h_scr):
        i = pl.program_id(0)

        @pl.when(i == 0)
        def _():
            xpad[0:8, :] = jnp.zeros((8, DR), F32)
            h_scr[...] = jnp.zeros((1, DR), F32)

        xpad[8:8 + tc, :] = xr_ref[...]
        xc = _conv(xpad, cw_ref, cb_ref, tc)
        xpad[0:8, :] = xpad[tc:tc + 8, :]
        _, r, ig, a, e2, mult_raw, mult, start = _lru_gates(xc, wa_ref, ba_ref, wi_ref, bi_ref, sp_ref, i * tc)
        a_scr[...] = a
        u_scr[...] = mult * (ig * xc)

        def step(t, h):
            h = a_scr[pl.ds(t, 1), :] * h + u_scr[pl.ds(t, 1), :]
            hs_ref[pl.ds(t, 1), :] = h
            return h

        h_scr[...] = lax.fori_loop(0, tc, step, h_scr[...], unroll=8)
        gy, _ = _gelu(y_ref[...])
        ur_ref[...] = (gy * hs_ref[...]).astype(BF16)

    full = lambda a: pl.BlockSpec(a.shape, lambda i: (0,) * a.ndim)
    return pl.pallas_call(
        kern, name="rnn_fwd", grid=(s_len // tc,),
        in_specs=[pl.BlockSpec((tc, DR), lambda i: (i, 0)), pl.BlockSpec((tc, DR), lambda i: (i, 1)),
                  full(cw), full(cb), full(wa), full(ba), full(wi), full(bi), full(sp)],
        out_specs=[pl.BlockSpec((tc, DR), lambda i: (i, 0)), pl.BlockSpec((tc, DR), lambda i: (i, 0))],
        out_shape=[jax.ShapeDtypeStruct((s_len, DR), F32), jax.ShapeDtypeStruct((s_len, DR), BF16)],
        scratch_shapes=[pltpu.VMEM((tc + 8, DR), F32), pltpu.VMEM((tc, DR), F32), pltpu.VMEM((tc, DR), F32),
                        pltpu.VMEM((1, DR), F32)],
        compiler_params=_cparams(("arbitrary",)),
    )(proj_rnn, proj_rnn, cw, cb, wa, ba, wi, bi, sp)


def _rnn_bwd(proj_rnn, hs, du, cw, cb, wa, wat, ba, wi, wit, bi, sp, dsp_dlam):
    s_len = proj_rnn.shape[0]
    tc = RNN_CHUNK
    nch = s_len // tc

    def kern(y_ref, xr_ref, xrp_ref, hs_ref, hsp_ref, du_ref, cw_ref, cb_ref, wa_ref, wat_ref, ba_ref, wi_ref,
             wit_ref, bi_ref, sp_ref, dspl_ref, drnn_ref, dwa_ref, dwi_ref, vec_ref,
             xpad, hpad, a_scr, d_scr, g_scr, dxcpad, ag_scr):
        j = pl.program_id(0)
        n = nch - 1 - j

        @pl.when(j == 0)
        def _():
            dwa_ref[...] = jnp.zeros(dwa_ref.shape, F32)
            dwi_ref[...] = jnp.zeros(dwi_ref.shape, F32)
            vec_ref[...] = jnp.zeros(vec_ref.shape, F32)
            ag_scr[...] = jnp.zeros((1, DR), F32)
            dxcpad[tc:tc + 8, :] = jnp.zeros((8, DR), F32)

        has_prev = n > 0
        xpad[0:8, :] = jnp.where(has_prev, xrp_ref[...], 0.0)
        xpad[8:8 + tc, :] = xr_ref[...]
        hpad[0:8, :] = jnp.where(has_prev, hsp_ref[...], 0.0)
        hpad[8:8 + tc, :] = hs_ref[...]
        xc = _conv(xpad, cw_ref, cb_ref, tc)
        xcb, r, ig, a, e2, mult_raw, mult, start = _lru_gates(xc, wa_ref, ba_ref, wi_ref, bi_ref, sp_ref, n * tc)

        y = y_ref[...]
        gy, th = _gelu(y)
        duv = du_ref[...].astype(F32)
        drnn_ref[:, 0:DR] = (duv * hs_ref[...] * _gelu_grad(y, th)).astype(BF16)
        a_scr[...] = a
        d_scr[...] = duv * gy

        def step(tt, ag):
            t = tc - 1 - tt
            g = d_scr[pl.ds(t, 1), :] + ag
            g_scr[pl.ds(t, 1), :] = g
            return a_scr[pl.ds(t, 1), :] * g

        ag_scr[...] = lax.fori_loop(0, tc, step, ag_scr[...], unroll=8)
        g = g_scr[...]
        da = g * hpad[pl.ds(7, tc), :]
        gx = g * xc
        dmult = jnp.where(start, 0.0, gx * ig)
        di = gx * mult
        dxc = g * mult * ig
        dlog_a = da * a - jnp.where(start, 0.0, dmult * e2 / mult_raw)
        dr = dlog_a * ((-LRU_C) * sp_ref[...])
        vec_ref[7:8, :] += _rowsum(dlog_a * ((-LRU_C) * r))
        dpr = dr * r * (1.0 - r)
        dpi = di * ig * (1.0 - ig)
        vec_ref[5:6, :] += _rowsum(dpr)
        vec_ref[6:7, :] += _rowsum(dpi)
        dprb, dpib = dpr.astype(BF16), dpi.astype(BF16)
        extra = []
        for b in range(N_BLK):
            sl = slice(b * BW, (b + 1) * BW)
            extra.append(jnp.dot(dprb[:, sl], wat_ref[b], preferred_element_type=F32)
                         + jnp.dot(dpib[:, sl], wit_ref[b], preferred_element_type=F32))
            dn = (((0,), (0,)), ((), ()))
            dwa_ref[b] += lax.dot_general(xcb[:, sl], dprb[:, sl], dn, preferred_element_type=F32)
            dwi_ref[b] += lax.dot_general(xcb[:, sl], dpib[:, sl], dn, preferred_element_type=F32)
        dxc = dxc + jnp.concatenate(extra, axis=1)
        vec_ref[4:5, :] += _rowsum(dxc)
        dxcpad[0:tc, :] = dxc
        dxr = jnp.zeros((tc, DR), F32)
        for tap in range(CONV_W):
            shift = CONV_W - 1 - tap
            dxr = dxr + cw_ref[tap:tap + 1, :] * dxcpad[pl.ds(shift, tc), :]
            vec_ref[tap:tap + 1, :] += _rowsum(dxc * xpad[pl.ds(8 - shift, tc), :])
        dxcpad[tc:tc + 8, :] = dxcpad[0:8, :]
        drnn_ref[:, DR:2 * DR] = dxr.astype(BF16)

        @pl.when(j == nch - 1)
        def _():
            vec_ref[7:8, :] = vec_ref[7:8, :] * dspl_ref[...]

    full = lambda a: pl.BlockSpec(a.shape, lambda j: (0,) * a.ndim)
    rev = lambda j: nch - 1 - j
    prev8 = lambda j: jnp.maximum((nch - 1 - j) * (tc // 8) - 1, 0)
    return pl.pallas_call(
        kern, name="rnn_bwd", grid=(nch,),
        in_specs=[pl.BlockSpec((tc, DR), lambda j: (rev(j), 0)), pl.BlockSpec((tc, DR), lambda j: (rev(j), 1)),
                  pl.BlockSpec((8, DR), lambda j: (prev8(j), 1)),
                  pl.BlockSpec((tc, DR), lambda j: (rev(j), 0)), pl.BlockSpec((8, DR), lambda j: (prev8(j), 0)),
                  pl.BlockSpec((tc, DR), lambda j: (rev(j), 0)),
                  full(cw), full(cb), full(wa), full(wat), full(ba), full(wi), full(wit), full(bi), full(sp),
                  full(dsp_dlam)],
        out_specs=[pl.BlockSpec((tc, 2 * DR), lambda j: (rev(j), 0)),
                   pl.BlockSpec((N_BLK, BW, BW), lambda j: (0, 0, 0)), pl.BlockSpec((N_BLK, BW, BW), lambda j: (0, 0, 0)),
                   pl.BlockSpec((8, DR), lambda j: (0, 0))],
        out_shape=[jax.ShapeDtypeStruct((s_len, 2 * DR), BF16), jax.ShapeDtypeStruct((N_BLK, BW, BW), F32),
                   jax.ShapeDtypeStruct((N_BLK, BW, BW), F32), jax.ShapeDtypeStruct((8, DR), F32)],
        scratch_shapes=[pltpu.VMEM((tc + 8, DR), F32), pltpu.VMEM((tc + 8, DR), F32), pltpu.VMEM((tc, DR), F32),
                        pltpu.VMEM((tc, DR), F32), pltpu.VMEM((tc, DR), F32), pltpu.VMEM((tc + 8, DR), F32),
                        pltpu.VMEM((1, DR), F32)],
        compiler_params=_cparams(("arbitrary",)),
    )(proj_rnn, proj_rnn, proj_rnn, hs, hs, du, cw, cb, wa, wat, ba, wi, wit, bi, sp, dsp_dlam)


def _attn_masks(n):
    qi = lax.broadcasted_iota(jnp.int32, (QB, 2 * QB), 0)
    ki = lax.broadcasted_iota(jnp.int32, (QB, 2 * QB), 1)
    dist = QB + qi - ki
    valid = (dist >= 0) & (dist <= QB) & ((n > 0) | (ki >= QB))
    lane = lax.broadcasted_iota(jnp.int32, (1, HEAD_PAIR), 1)
    return valid, lane


def _qkv_specs(d, n_of):
    prev = lambda r, j: jnp.maximum(n_of(j) - 1, 0)
    blk = (QB, D)
    return [pl.BlockSpec(blk, lambda r, j: (n_of(j), 3 * r)),
            pl.BlockSpec(blk, lambda r, j: (prev(r, j), 3 * r + 1)), pl.BlockSpec(blk, lambda r, j: (n_of(j), 3 * r + 1)),
            pl.BlockSpec(blk, lambda r, j: (prev(r, j), 3 * r + 2)), pl.BlockSpec(blk, lambda r, j: (n_of(j), 3 * r + 2))]


def _attn_fwd(qkv, d):
    s_len = qkv.shape[0]
    n_l = s_len // d
    nb = n_l // QB
    qv = qkv.reshape(n_l, d * 3 * D)

    def kern(q_ref, kp_ref, kc_ref, vp_ref, vc_ref, o_ref, lse_ref):
        valid, lane = _attn_masks(pl.program_id(1))
        first = lane < 64
        for hp in range(N_PAIR):
            sl = slice(hp * HEAD_PAIR, (hp + 1) * HEAD_PAIR)
            q2 = q_ref[:, sl]
            k2 = jnp.concatenate([kp_ref[:, sl], kc_ref[:, sl]], axis=0)
            v2 = jnp.concatenate([vp_ref[:, sl], vc_ref[:, sl]], axis=0)
            outs, lses = [], []
            for half in range(2):
                hm = first if half == 0 else jnp.logical_not(first)
                qh = jnp.where(hm, q2, jnp.zeros_like(q2))
                s = lax.dot_general(qh, k2, (((1,), (1,)), ((), ())), preferred_element_type=F32) * SCALE
                s = jnp.where(valid, s, NEG)
                mx = jnp.max(s, axis=-1, keepdims=True)
                p = jnp.exp(s - mx)
                den = jnp.sum(p, axis=-1, keepdims=True)
                outs.append(jnp.dot(p.astype(BF16), v2, preferred_element_type=F32) / den)
                lses.append(mx + jnp.log(den))
            o_ref[:, sl] = jnp.where(first, outs[0], outs[1]).astype(BF16)
            lse_ref[:, sl] = jnp.where(first, lses[0], lses[1])

    o, lse = pl.pallas_call(
        kern, name=f"attn_fwd_d{d}", grid=(d, nb), in_specs=_qkv_specs(d, lambda j: j),
        out_specs=[pl.BlockSpec((QB, D), lambda r, j: (j, r)), pl.BlockSpec((QB, D), lambda r, j: (j, r))],
        out_shape=[jax.ShapeDtypeStruct((n_l, d * D), BF16), jax.ShapeDtypeStruct((n_l, d * D), F32)],
        compiler_params=_cparams(("arbitrary", "arbitrary")),
    )(qv, qv, qv, qv, qv)
    return o.reshape(s_len, D), lse.reshape(s_len, D)


def _attn_bwd(qkv, do, o, lse, d):
    s_len = qkv.shape[0]
    n_l = s_len // d
    nb = n_l // QB
    qv = qkv.reshape(n_l, d * 3 * D)
    view = lambda t: t.reshape(n_l, d * D)

    def kern(q_ref, kp_ref, kc_ref, vp_ref, vc_ref, do_ref, o_ref, lse_ref, dq_ref, dk_ref, dv_ref, dk_scr, dv_scr):
        j = pl.program_id(1)

        @pl.when(j == 0)
        def _():
            dk_scr[...] = jnp.zeros((QB, D), F32)
            dv_scr[...] = jnp.zeros((QB, D), F32)

        valid, lane = _attn_masks(nb - 1 - j)
        first = lane < 64
        nt = (((1,), (1,)), ((), ()))
        tn = (((0,), (0,)), ((), ()))
        for hp in range(N_PAIR):
            sl = slice(hp * HEAD_PAIR, (hp + 1) * HEAD_PAIR)
            q2 = q_ref[:, sl]
            k2 = jnp.concatenate([kp_ref[:, sl], kc_ref[:, sl]], axis=0)
            v2 = jnp.concatenate([vp_ref[:, sl], vc_ref[:, sl]], axis=0)
            do2 = do_ref[:, sl]
            lse2 = lse_ref[:, sl]
            prod = do2.astype(F32) * o_ref[:, sl].astype(F32)
            dqs = []
            dk2 = jnp.zeros((2 * QB, HEAD_PAIR), F32)
            dv2 = jnp.zeros((2 * QB, HEAD_PAIR), F32)
            for half in range(2):
                hm = first if half == 0 else jnp.logical_not(first)
                qh = jnp.where(hm, q2, jnp.zeros_like(q2))
                doh = jnp.where(hm, do2, jnp.zeros_like(do2))
                dsum = jnp.sum(jnp.where(hm, prod, 0.0), axis=-1, keepdims=True)
                lseh = jnp.sum(jnp.where(lane == 64 * half, lse2, 0.0), axis=-1, keepdims=True)
                s = lax.dot_general(qh, k2, nt, preferred_element_type=F32) * SCALE
                p = jnp.exp(jnp.where(valid, s, NEG) - lseh)
                dp = lax.dot_general(doh, v2, nt, preferred_element_type=F32)
                ds = (p * (dp - dsum) * SCALE).astype(BF16)
                dqs.append(jnp.dot(ds, k2, preferred_element_type=F32))
                dk2 = dk2 + lax.dot_general(ds, qh, tn, preferred_element_type=F32)
                dv2 = dv2 + lax.dot_general(p.astype(BF16), doh, tn, preferred_element_type=F32)
            dq_ref[:, sl] = jnp.where(first, dqs[0], dqs[1])
            dk_ref[:, sl] = dk2[QB:, :] + dk_scr[:, sl]
            dv_ref[:, sl] = dv2[QB:, :] + dv_scr[:, sl]
            dk_scr[:, sl] = dk2[:QB, :]
            dv_scr[:, sl] = dv2[:QB, :]

    rev = lambda j: nb - 1 - j
    row = pl.BlockSpec((QB, D), lambda r, j: (rev(j), r))
    dq, dk, dv = pl.pallas_call(
        kern, name=f"attn_bwd_d{d}", grid=(d, nb), in_specs=_qkv_specs(d, rev) + [row, row, row],
        out_specs=[row, row, row], out_shape=[jax.ShapeDtypeStruct((n_l, d * D), F32)] * 3,
        scratch_shapes=[pltpu.VMEM((QB, D), F32), pltpu.VMEM((QB, D), F32)],
        compiler_params=_cparams(("arbitrary", "arbitrary")),
    )(qv, qv, qv, qv, qv, view(do), view(o), view(lse))
    return dq.reshape(s_len, D), dk.reshape(s_len, D), dv.reshape(s_len, D)


def _adamw(w, g, m, v):
    m = ADAM_B1 * m + (1.0 - ADAM_B1) * g
    v = ADAM_B2 * v + (1.0 - ADAM_B2) * (g * g)
    m_hat = m / (1.0 - ADAM_B1 ** ADAM_STEP)
    v_hat = v / (1.0 - ADAM_B2 ** ADAM_STEP)
    delta = -ADAM_LR * (m_hat / (jnp.sqrt(v_hat) + ADAM_EPS) + ADAM_WD * w)
    return delta, m, v


def _adam_packed(name, recv, w, m, v, tm):
    n_rows, width = w.shape
    summed = recv.ndim == 3

    def kern(r_ref, w_ref, m_ref, v_ref, g_out, d_out, m_out, v_out):
        if summed:
            g = r_ref[0]
            for s in range(1, N_DEV):
                g = g + r_ref[s]
        else:
            g = r_ref[...]
        delta, mn, vn = _adamw(w_ref[...], g, m_ref[...], v_ref[...])
        g_out[...] = g
        d_out[...] = delta
        m_out[...] = mn
        v_out[...] = vn

    tile = pl.BlockSpec((tm, width), lambda i: (i, 0))
    rspec = pl.BlockSpec((N_DEV, tm, width), lambda i: (0, i, 0)) if summed else tile
    return pl.pallas_call(
        kern, name=name, grid=(n_rows // tm,), in_specs=[rspec, tile, tile, tile], out_specs=[tile] * 4,
        out_shape=[jax.ShapeDtypeStruct((n_rows, width), F32)] * 4, compiler_params=_cparams(("parallel",)),
    )(recv, w, m, v)


def _sum_slots(recv):
    _, n_rows, width = recv.shape

    def kern(r_ref, o_ref):
        g = r_ref[0]
        for s in range(1, N_DEV):
            g = g + r_ref[s]
        o_ref[...] = g

    return pl.pallas_call(
        kern, name="sum_small", grid=(1,), in_specs=[pl.BlockSpec(recv.shape, lambda i: (0, 0, 0))],
        out_specs=pl.BlockSpec((n_rows, width), lambda i: (0, 0)),
        out_shape=jax.ShapeDtypeStruct((n_rows, width), F32), compiler_params=_cparams(("arbitrary",)),
    )(recv)


def _adam_w_ada(cond_t, dmod_cols, w, m, v):
    n_rows, width = w.shape
    tm = ROW_TILE

    def kern(c_ref, d_ref, w_ref, m_ref, v_ref, g_out, d_out, m_out, v_out):
        g = c_ref[:, 0:1] * d_ref[0:1, :]
        for b in range(1, N_DEV):
            g = g + c_ref[:, b:b + 1] * d_ref[b:b + 1, :]
        delta, mn, vn = _adamw(w_ref[...], g, m_ref[...], v_ref[...])
        g_out[...] = g
        d_out[...] = delta
        m_out[...] = mn
        v_out[...] = vn

    tile = pl.BlockSpec((tm, width), lambda i: (i, 0))
    return pl.pallas_call(
        kern, name="adam_w_ada", grid=(n_rows // tm,),
        in_specs=[pl.BlockSpec((tm, N_DEV), lambda i: (i, 0)), pl.BlockSpec((N_DEV, width), lambda i: (0, 0)),
                  tile, tile, tile],
        out_specs=[tile] * 4, out_shape=[jax.ShapeDtypeStruct((n_rows, width), F32)] * 4,
        compiler_params=_cparams(("parallel",)),
    )(cond_t, dmod_cols, w, m, v)


def _rows(a, n_rows=None):
    flat = a.reshape(-1)
    need = (n_rows if n_rows is not None else -(-flat.shape[0] // D)) * D
    if need != flat.shape[0]:
        flat = jnp.concatenate([flat, jnp.zeros((need - flat.shape[0],), flat.dtype)])
    return flat.reshape(-1, D)


def _pack_shards(ws):
    return jnp.concatenate([ws[name][0].reshape(r, D) for name, r in PACK_ROWS], axis=0)


def _cols_to_slots(g, cols):
    k = g.shape[0]
    return g.reshape(k, N_DEV, cols).transpose(1, 0, 2).reshape(N_DEV, k * cols // D, D)


def _slots_to_cols(s, k, cols):
    return s.reshape(N_DEV, k, cols).transpose(1, 0, 2).reshape(k, N_DEV * cols)


def kernel(x, c, w_ada, b_ada, w_in, conv_w, conv_b, lru_wa, lru_ba, lru_wi, lru_bi, lru_lambda, w_proj_rnn, w_proj_attn, b_gate, w_out, ln1_g, ln1_b, w_ffn_gate, w_ffn_up, w_ffn_down, ln2_g, ln2_b, loss_target, m_w_ada, m_b_ada, m_w_in, m_conv_w, m_conv_b, m_lru_wa, m_lru_ba, m_lru_wi, m_lru_bi, m_lru_lambda, m_w_proj_rnn, m_w_proj_attn, m_b_gate, m_w_out, m_ln1_g, m_ln1_b, m_w_ffn_gate, m_w_ffn_up, m_w_ffn_down, m_ln2_g, m_ln2_b, v_w_ada, v_b_ada, v_w_in, v_conv_w, v_conv_b, v_lru_wa, v_lru_ba, v_lru_wi, v_lru_bi, v_lru_lambda, v_w_proj_rnn, v_w_proj_attn, v_b_gate, v_w_out, v_ln1_g, v_ln1_b, v_w_ffn_gate, v_w_ffn_up, v_w_ffn_down, v_ln2_g, v_ln2_b):
    s_len = x.shape[1]
    me = 4 * lax.axis_index("x") + 2 * lax.axis_index("y") + lax.axis_index("c")
    xs = x[0]
    tgt = loss_target[0]
    tm = ROW_TILE

    b_ada_cols = lax.dynamic_slice(b_ada, (0, me * 768), (1, 768))
    pad_cols = lambda a: jnp.concatenate([a, jnp.zeros((a.shape[0], D - a.shape[1]), F32)], axis=1)
    c8 = jnp.concatenate([c, pad_cols(conv_w[0]), pad_cols(b_gate[0]), jnp.zeros((1, D), F32)], axis=0)
    c_all, cond_blocks, mod_parts = _ada_modulation(c8, w_ada[0], b_ada_cols)
    cond_all = cond_blocks[:, 0, :]
    mod = mod_parts[:, 0, :].reshape(6, D)
    mod8 = jnp.concatenate([mod, jnp.zeros((2, D), F32)], axis=0)
    cw = c_all[:, 1:1 + CONV_W, :DR // N_DEV].transpose(1, 0, 2).reshape(CONV_W, DR)
    bg = c_all[:, 1 + CONV_W:3 + CONV_W, :D // N_DEV].transpose(1, 0, 2).reshape(2, D)
    bg8 = jnp.concatenate([bg, jnp.zeros((6, D), F32)], axis=0)

    big = dict(w_in=w_in, w_ffn_gate=w_ffn_gate, w_ffn_up=w_ffn_up, w_proj_rnn=w_proj_rnn, w_proj_attn=w_proj_attn,
               w_out=w_out, w_ffn_down=w_ffn_down)
    gathered = _gather_shards(_pack_shards(big).astype(BF16))
    offs, o = {}, 0
    for name, r in PACK_ROWS:
        offs[name] = (o, o + r)
        o += r
    part = lambda name: gathered[:, offs[name][0]:offs[name][1], :]
    w_in_f = _slots_to_cols(part("w_in"), D, 960)
    w_rnn, w_qkv, w_gates = w_in_f[:, :2 * DR], w_in_f[:, 2 * DR:2 * DR + 3 * D], w_in_f[:, 2 * DR + 3 * D:]
    w_gu = jnp.concatenate([_slots_to_cols(part("w_ffn_gate"), D, 352), _slots_to_cols(part("w_ffn_up"), D, 352)], axis=1)
    w_pr = part("w_proj_rnn").reshape(DR, D)
    w_pa = part("w_proj_attn").reshape(D, D)
    w_o = part("w_out").reshape(D, D)
    w_dn = part("w_ffn_down").reshape(DFF, D)

    cb = conv_b
    wa_b, wi_b = lru_wa[0].astype(BF16), lru_wi[0].astype(BF16)
    wat_b, wit_b = jnp.swapaxes(wa_b, 1, 2), jnp.swapaxes(wi_b, 1, 2)
    ba, bi = lru_ba.reshape(1, DR), lru_bi.reshape(1, DR)
    sp = jax.nn.softplus(-lru_lambda)
    dsp_dlam = -jax.nn.sigmoid(-lru_lambda)
    ln1 = jnp.concatenate([ln1_g, ln1_b, jnp.zeros((6, D), F32)], axis=0)
    ln2 = jnp.concatenate([ln2_g, ln2_b, jnp.zeros((6, D), F32)], axis=0)

    def f1(ri, fi, ro, ao):
        n, _ = _ln(ri[0][...])
        ro[0][...] = (n * (1.0 + fi[0][1:2, :]) + fi[0][0:1, :]).astype(BF16)

    (h1,) = _rowwise(f1, "ln_mod1", s_len, tm, [(xs, D, 0)], [mod8], [(D, BF16)], [])
    proj_rnn = _mm("mm_in_rnn", [(h1, w_rnn)], F32, 512, DR)
    qkv = _mm("mm_in_qkv", [(h1, w_qkv)], BF16, 512, 1536)
    gates = _mm("mm_in_gates", [(h1, w_gates)], F32, 512, D)

    hs, u_rnn = _rnn_fwd(proj_rnn, cw, cb, wa_b, ba, wi_b, bi, sp)

    group = [_attn_fwd(qkv, d) for d in DILATIONS]

    def f_mix(ri, fi, ro, ao):
        ls = [ri[3 + g][...] for g in range(3)]
        mx = jnp.maximum(jnp.maximum(ls[0], ls[1]), ls[2])
        es = [jnp.exp(l - mx) for l in ls]
        tot = es[0] + es[1] + es[2]
        acc = None
        for g in range(3):
            t = (es[g] / tot) * ri[g][...].astype(F32)
            acc = t if acc is None else acc + t
        ro[0][...] = acc.astype(BF16)
        ro[1][...] = mx + jnp.log(tot)

    u_attn, lse = _rowwise(f_mix, "attn_mix", s_len, tm, [(o_g, D, 0) for o_g, _ in group] + [(l_g, D, 0) for _, l_g in group],
                           [], [(D, BF16), (D, F32)], [])

    pr = _mm("mm_proj_rnn", [(u_rnn, w_pr)], F32, 512, D)
    pa = _mm("mm_proj_attn", [(u_attn, w_pa)], F32, 512, D)

    def f2(ri, fi, ro, ao):
        g_r = _sigmoid(ri[0][...] + fi[0][0:1, :])
        g_a = _sigmoid(ri[1][...] + fi[0][1:2, :])
        ro[0][...] = (g_r * ri[2][...] + g_a * ri[3][...]).astype(BF16)

    (merged,) = _rowwise(f2, "merge", s_len, tm, [(gates, D, 0), (gates, D, 1), (pr, D, 0), (pa, D, 0)], [bg8],
                         [(D, BF16)], [])
    f = _mm("mm_out", [(merged, w_o)], F32, 512, D)

    def f3(ri, fi, ro, ao):
        md, l1 = fi[0], fi[1]
        n1, _ = _ln(ALPHA * ri[0][...] + md[2:3, :] * ri[1][...])
        x1 = n1 * l1[0:1, :] + l1[1:2, :]
        n0, _ = _ln(x1)
        ro[0][...] = x1
        ro[1][...] = (n0 * (1.0 + md[4:5, :]) + md[3:4, :]).astype(BF16)

    x1, h2 = _rowwise(f3, "post1", s_len, tm, [(xs, D, 0), (f, D, 0)], [mod8, ln1], [(D, F32), (D, BF16)], [])
    gu = _mm("mm_ffn_in", [(h2, w_gu)], BF16, 512, DFF)

    def f4(ri, fi, ro, ao):
        gp = ri[0][...].astype(F32)
        ro[0][...] = (gp * _sigmoid(gp) * ri[1][...].astype(F32)).astype(BF16)

    (act,) = _rowwise(f4, "swiglu", s_len, tm, [(gu, DFF, 0), (gu, DFF, 1)], [], [(DFF, BF16)], [])
    f2o = _mm("mm_ffn_out", [(act, w_dn)], F32, 512, D)

    def f5(ri, fi, ro, ao):
        md, l2 = fi[0], fi[1]
        f2v = ri[1][...]
        n2, rstd = _ln(ALPHA * ri[0][...] + md[5:6, :] * f2v)
        err = n2 * l2[0:1, :] + l2[1:2, :] - ri[2][...]
        dx2 = err * (1.0 / D)
        dz2 = _ln_bwd(dx2 * l2[0:1, :], n2, rstd)
        ro[0][...] = dz2
        ro[1][...] = (md[5:6, :] * dz2).astype(BF16)
        acc = ao[0]
        acc[0:1, :] += _rowsum(err * err) * (0.5 / D)
        acc[1:2, :] += _rowsum(dx2 * n2)
        acc[2:3, :] += _rowsum(dx2)
        acc[3:4, :] += _rowsum(dz2 * f2v)

    dz2, df2, acc5 = _rowwise(f5, "loss_post2", s_len, tm, [(x1, D, 0), (f2o, D, 0), (tgt, D, 0)], [mod8, ln2],
                              [(D, F32), (D, BF16)], [((8, D), F32)])

    dact = _mm("mm_d_ffn_out", [(df2, w_dn.T)], BF16, 512, DFF)
    d_w_dn = _mm_tn("mmt_ffn_down", act, df2, 1024, 512)

    def b1(ri, fi, ro, ao):
        gp, up, da = ri[0][...].astype(F32), ri[1][...].astype(F32), ri[2][...].astype(F32)
        sg = _sigmoid(gp)
        ro[0][:, 0:DFF] = (da * up * sg * (1.0 + gp * (1.0 - sg))).astype(BF16)
        ro[0][:, DFF:2 * DFF] = (da * gp * sg).astype(BF16)

    (dgu,) = _rowwise(b1, "d_swiglu", s_len, tm, [(gu, DFF, 0), (gu, DFF, 1), (dact, DFF, 0)], [], [(2 * DFF, BF16)], [])
    dh2 = _mm("mm_d_ffn_in", [(dgu, w_gu.T)], F32, 256, D)
    d_w_gu = _mm_tn("mmt_ffn_in", h2, dgu, 1024, 512)

    def b2(ri, fi, ro, ao):
        md, l1 = fi[0], fi[1]
        fv, dh2v = ri[1][...], ri[2][...]
        n1, rstd1 = _ln(ALPHA * ri[0][...] + md[2:3, :] * fv)
        n0, rstd0 = _ln(n1 * l1[0:1, :] + l1[1:2, :])
        dx1 = ALPHA * ri[3][...] + _ln_bwd(dh2v * (1.0 + md[4:5, :]), n0, rstd0)
        dz1 = _ln_bwd(dx1 * l1[0:1, :], n1, rstd1)
        ro[0][...] = ALPHA * dz1
        ro[1][...] = (md[2:3, :] * dz1).astype(BF16)
        acc = ao[0]
        acc[0:1, :] += _rowsum(dh2v * n0)
        acc[1:2, :] += _rowsum(dh2v)
        acc[2:3, :] += _rowsum(dx1 * n1)
        acc[3:4, :] += _rowsum(dx1)
        acc[4:5, :] += _rowsum(dz1 * fv)

    dxp, df, acc2 = _rowwise(b2, "d_post1", s_len, tm, [(xs, D, 0), (f, D, 0), (dh2, D, 0), (dz2, D, 0)], [mod8, ln1],
                             [(D, F32), (D, BF16)], [((8, D), F32)])
    dmerged = _mm("mm_d_out", [(df, w_o.T)], F32, 512, D)
    d_w_o = _mm_tn("mmt_out", merged, df, 1024, 512)

    def b3(ri, fi, ro, ao):
        g_r = _sigmoid(ri[0][...] + fi[0][0:1, :])
        g_a = _sigmoid(ri[1][...] + fi[0][1:2, :])
        dm = ri[4][...]
        ro[0][...] = (dm * g_r).astype(BF16)
        ro[1][...] = (dm * g_a).astype(BF16)
        dga = dm * ri[2][...] * g_r * (1.0 - g_r)
        dgb = dm * ri[3][...] * g_a * (1.0 - g_a)
        ro[2][:, 0:D] = dga.astype(BF16)
        ro[2][:, D:2 * D] = dgb.astype(BF16)
        ao[0][0:1, :] += _rowsum(dga)
        ao[0][1:2, :] += _rowsum(dgb)

    dpr, dpa, dgates, acc3 = _rowwise(b3, "d_merge", s_len, tm,
                                      [(gates, D, 0), (gates, D, 1), (pr, D, 0), (pa, D, 0), (dmerged, D, 0)], [bg8],
                                      [(D, BF16), (D, BF16), (2 * D, BF16)], [((8, D), F32)])
    du_rnn = _mm("mm_d_proj_rnn", [(dpr, w_pr.T)], F32, 512, DR)
    du_attn = _mm("mm_d_proj_attn", [(dpa, w_pa.T)], BF16, 512, D)
    d_w_pr = _mm_tn("mmt_proj_rnn", u_rnn, dpr, 1024, 512)
    d_w_pa = _mm_tn("mmt_proj_attn", u_attn, dpa, 1024, 512)

    drnn, d_wa, d_wi, vec = _rnn_bwd(proj_rnn, hs, du_rnn, cw, cb, wa_b, wat_b, ba, wi_b, wit_b, bi, sp, dsp_dlam)

    dparts = [_attn_bwd(qkv, du_attn, u_attn, lse, d) for d in DILATIONS]

    def b4(ri, fi, ro, ao):
        for col in range(3):
            ro[0][:, col * D:(col + 1) * D] = (ri[col][...] + ri[3 + col][...] + ri[6 + col][...]).astype(BF16)

    (dqkv,) = _rowwise(b4, "d_qkv_sum", s_len, tm, [(t, D, 0) for grp in dparts for t in grp], [], [(3 * D, BF16)], [])

    dh1 = _mm("mm_d_in", [(drnn, w_rnn.T), (dqkv, w_qkv.T), (dgates, w_gates.T)], F32, 256, 512)
    d_w_in = jnp.concatenate([_mm_tn("mmt_in_rnn", h1, drnn, 1024, DR), _mm_tn("mmt_in_qkv", h1, dqkv, 1024, D),
                              _mm_tn("mmt_in_gates", h1, dgates, 1024, D)], axis=1)

    def b5(ri, fi, ro, ao):
        md = fi[0]
        n0, rstd0 = _ln(ri[0][...])
        dh = ri[1][...]
        ro[0][...] = ri[2][...] + _ln_bwd(dh * (1.0 + md[1:2, :]), n0, rstd0)
        ao[0][0:1, :] += _rowsum(dh * n0)
        ao[0][1:2, :] += _rowsum(dh)

    grad_x, acc1 = _rowwise(b5, "d_ln_mod1", s_len, tm, [(xs, D, 0), (dh1, D, 0), (dxp, D, 0)], [mod8], [(D, F32)],
                            [((8, D), F32)])

    g_big = jnp.concatenate([
        _cols_to_slots(d_w_in, 960), _cols_to_slots(d_w_gu[:, :DFF], 352), _cols_to_slots(d_w_gu[:, DFF:], 352),
        d_w_pr.reshape(N_DEV, 160, D), d_w_pa.reshape(N_DEV, 128, D), d_w_o.reshape(N_DEV, 128, D),
        d_w_dn.reshape(N_DEV, 352, D)], axis=1)
    dmod = jnp.concatenate([acc1[1:2], acc1[0:1], acc2[4:5], acc2[1:2], acc2[0:1], acc5[3:4]], axis=0)
    dmod_slots = lax.dynamic_update_slice(jnp.zeros((N_DEV, 6, D), F32), dmod[None], (me, 0, 0)).reshape(N_DEV * 6, D)
    g_small = jnp.concatenate([
        dmod_slots,
        _rows(vec[4], 2), d_wa.reshape(160, D), _rows(vec[5], 2), d_wi.reshape(160, D), _rows(vec[6], 2), _rows(vec[7], 2),
        acc2[2:4], acc5[1:3],
        _rows(vec[0:4], 5), acc3[0:2], acc5[0:1], jnp.zeros((4, D), F32)], axis=0)
    recv_big, recv_small = _exchange_grads(g_big, g_small)
    red = _sum_slots(recv_small)
    dmod_all = red[0:48].reshape(N_DEV, 6 * D)
    loss = jnp.sum(red[387])

    packs = [_pack_shards(dict(w_in=a, w_ffn_gate=b, w_ffn_up=cc, w_proj_rnn=d_, w_proj_attn=e, w_out=ff, w_ffn_down=gg))
             for a, b, cc, d_, e, ff, gg in ((w_in, w_ffn_gate, w_ffn_up, w_proj_rnn, w_proj_attn, w_out, w_ffn_down),
                                            (m_w_in, m_w_ffn_gate, m_w_ffn_up, m_w_proj_rnn, m_w_proj_attn, m_w_out, m_w_ffn_down),
                                            (v_w_in, v_w_ffn_gate, v_w_ffn_up, v_w_proj_rnn, v_w_proj_attn, v_w_out, v_w_ffn_down))]
    big_out = _adam_packed("adam_big", recv_big, packs[0], packs[1], packs[2], 128)
    ada_out = _adam_w_ada(cond_all.T, lax.dynamic_slice(dmod_all, (0, me * 768), (N_DEV, 768)),
                          w_ada[0], m_w_ada[0], v_w_ada[0])

    small_spec = (("b_ada", 6), ("conv_b", 2), ("lru_wa", 160), ("lru_ba", 2), ("lru_wi", 160), ("lru_bi", 2),
                  ("lru_lambda", 2), ("ln1_g", 1), ("ln1_b", 1), ("ln2_g", 1), ("ln2_b", 1), ("conv_w", 1), ("b_gate", 1))
    small_rows = 344

    def pack_small(ps):
        return jnp.concatenate([_rows(ps[name], r) for name, r in small_spec] + [jnp.zeros((4, D), F32)], axis=0)

    names = [n for n, _ in small_spec]
    smalls = [pack_small(dict(zip(names, t))) for t in (
        (b_ada, conv_b, lru_wa, lru_ba, lru_wi, lru_bi, lru_lambda, ln1_g, ln1_b, ln2_g, ln2_b, conv_w, b_gate),
        (m_b_ada, m_conv_b, m_lru_wa, m_lru_ba, m_lru_wi, m_lru_bi, m_lru_lambda, m_ln1_g, m_ln1_b, m_ln2_g, m_ln2_b,
         m_conv_w, m_b_gate),
        (v_b_ada, v_conv_b, v_lru_wa, v_lru_ba, v_lru_wi, v_lru_bi, v_lru_lambda, v_ln1_g, v_ln1_b, v_ln2_g, v_ln2_b,
         v_conv_w, v_b_gate))]
    g_conv_w = lax.dynamic_slice(red[380:385].reshape(-1)[:CONV_W * DR].reshape(CONV_W, DR), (0, me * 160), (CONV_W, 160))
    g_b_gate = lax.dynamic_slice(red[385:387], (0, me * 128), (2, 128))
    g_small_pack = jnp.concatenate([jnp.sum(dmod_all, axis=0).reshape(6, D), red[48:380], _rows(g_conv_w, 1),
                                    _rows(g_b_gate, 1), jnp.zeros((4, D), F32)], axis=0)
    assert g_small_pack.shape == (small_rows, D) and smalls[0].shape == (small_rows, D)
    small_out = _adam_packed("adam_small", g_small_pack, smalls[0], smalls[1], smalls[2], small_rows)

    shapes = dict(w_ada=w_ada.shape, b_ada=b_ada.shape, w_in=w_in.shape, conv_w=conv_w.shape, conv_b=conv_b.shape,
                  lru_wa=lru_wa.shape, lru_ba=lru_ba.shape, lru_wi=lru_wi.shape, lru_bi=lru_bi.shape,
                  lru_lambda=lru_lambda.shape, w_proj_rnn=w_proj_rnn.shape, w_proj_attn=w_proj_attn.shape,
                  b_gate=b_gate.shape, w_out=w_out.shape, ln1_g=ln1_g.shape, ln1_b=ln1_b.shape,
                  w_ffn_gate=w_ffn_gate.shape, w_ffn_up=w_ffn_up.shape, w_ffn_down=w_ffn_down.shape,
                  ln2_g=ln2_g.shape, ln2_b=ln2_b.shape)

    def unpack(kind):
        out = {"w_ada": ada_out[kind].reshape(shapes["w_ada"])}
        o = 0
        for name, r in PACK_ROWS:
            out[name] = big_out[kind][o:o + r].reshape(shapes[name])
            o += r
        o = 0
        for name, r in small_spec:
            size = 1
            for dim in shapes[name]:
                size *= dim
            out[name] = small_out[kind][o:o + r].reshape(-1)[:size].reshape(shapes[name])
            o += r
        return [out[name] for name in shapes]

    return (loss, grad_x[None], *unpack(0), *unpack(1), *unpack(2), *unpack(3))
```

```python
import functools

import jax
import jax.numpy as jnp
from jax import lax
from jax.experimental import pallas as pl
from jax.experimental.pallas import tpu as pltpu

F32 = jnp.float32
BF16 = jnp.bfloat16
MESH = pl.DeviceIdType.MESH

N_DEV = 8
D = 1024
DR = 1280
N_BLK = 10
BW = 128
HEAD_PAIR = 128
N_PAIR = 8
QB = 128
DFF = 2816
CONV_W = 4
DILATIONS = (1, 4, 16)
ALPHA = 2.0 ** 0.25
LN_EPS = 1e-5
LRU_C = 8.0
SCALE = 0.125
NEG = -1e30
ADAM_LR, ADAM_B1, ADAM_B2, ADAM_EPS, ADAM_WD, ADAM_STEP = 0.001, 0.9, 0.999, 1e-08, 0.01, 10

VMEM_LIMIT = 48 * 1024 * 1024
ROW_TILE = 256
RNN_CHUNK = 256

PACK_ROWS = (("w_in", 960), ("w_ffn_gate", 352), ("w_ffn_up", 352), ("w_proj_rnn", 160),
             ("w_proj_attn", 128), ("w_out", 128), ("w_ffn_down", 352))
PACK_R = sum(r for _, r in PACK_ROWS)


def _cparams(sem):
    return pltpu.CompilerParams(dimension_semantics=sem, vmem_limit_bytes=VMEM_LIMIT)


def _ln(z):
    mu = jnp.mean(z, axis=-1, keepdims=True)
    zc = z - mu
    var = jnp.mean(zc * zc, axis=-1, keepdims=True)
    rstd = lax.rsqrt(var + LN_EPS)
    return zc * rstd, rstd


def _ln_bwd(dn, n, rstd):
    return rstd * (dn - jnp.mean(dn, axis=-1, keepdims=True) - n * jnp.mean(dn * n, axis=-1, keepdims=True))


def _sigmoid(x):
    return 1.0 / (1.0 + jnp.exp(-x))


_GELU_K = 0.7978845608028654
_GELU_C = 0.044715


def _gelu(y):
    t = jnp.tanh(_GELU_K * (y + _GELU_C * y * y * y))
    return 0.5 * y * (1.0 + t), t


def _gelu_grad(y, t):
    return 0.5 * (1.0 + t) + 0.5 * y * (1.0 - t * t) * _GELU_K * (1.0 + 3.0 * _GELU_C * y * y)


def _rowsum(v):
    return jnp.sum(v, axis=0, keepdims=True)


def _rowwise(body, name, n_rows, tm, row_ins, full_ins, row_outs, acc_outs):
    nri, nfi, nro = len(row_ins), len(full_ins), len(row_outs)

    def kern(*refs):
        ri, fi = refs[:nri], refs[nri:nri + nfi]
        ro, ao = refs[nri + nfi:nri + nfi + nro], refs[nri + nfi + nro:]
        if ao:
            @pl.when(pl.program_id(0) == 0)
            def _():
                for a in ao:
                    a[...] = jnp.zeros(a.shape, a.dtype)
        body(ri, fi, ro, ao)

    in_specs = [pl.BlockSpec((tm, w), functools.partial(lambda i, cb: (i, cb), cb=cb)) for _, w, cb in row_ins]
    in_specs += [pl.BlockSpec(a.shape, lambda i: (0, 0)) for a in full_ins]
    out_specs = [pl.BlockSpec((tm, w), lambda i: (i, 0)) for w, _ in row_outs]
    out_specs += [pl.BlockSpec(s, lambda i: (0, 0)) for s, _ in acc_outs]
    out_shape = [jax.ShapeDtypeStruct((n_rows, w), dt) for w, dt in row_outs]
    out_shape += [jax.ShapeDtypeStruct(s, dt) for s, dt in acc_outs]
    return pl.pallas_call(
        kern, name=name, grid=(n_rows // tm,), in_specs=in_specs, out_specs=out_specs, out_shape=out_shape,
        compiler_params=_cparams(("arbitrary",)),
    )(*[a for a, _, _ in row_ins], *full_ins)


def _mm(name, pairs, out_dtype, tm, tn):
    m_rows, n_cols = pairs[0][0].shape[0], pairs[0][1].shape[1]
    n_pairs = len(pairs)

    def kern(*refs):
        acc = None
        for p in range(n_pairs):
            t = jnp.dot(refs[2 * p][...], refs[2 * p + 1][...], preferred_element_type=F32)
            acc = t if acc is None else acc + t
        refs[-1][...] = acc.astype(refs[-1].dtype)

    in_specs, flat = [], []
    for a, w in pairs:
        k = a.shape[1]
        in_specs += [pl.BlockSpec((tm, k), lambda j, i: (i, 0)), pl.BlockSpec((k, tn), lambda j, i: (0, j))]
        flat += [a, w]
    return pl.pallas_call(
        kern, name=name, grid=(n_cols // tn, m_rows // tm), in_specs=in_specs,
        out_specs=pl.BlockSpec((tm, tn), lambda j, i: (i, j)),
        out_shape=jax.ShapeDtypeStruct((m_rows, n_cols), out_dtype),
        compiler_params=_cparams(("parallel", "parallel")),
    )(*flat)


def _mm_tn(name, a, g, tm, tn):
    m_rows, k = a.shape
    n_cols = g.shape[1]

    def kern(a_ref, g_ref, o_ref):
        @pl.when(pl.program_id(1) == 0)
        def _():
            o_ref[...] = jnp.zeros(o_ref.shape, F32)
        o_ref[...] += lax.dot_general(a_ref[...], g_ref[...], (((0,), (0,)), ((), ())), preferred_element_type=F32)

    return pl.pallas_call(
        kern, name=name, grid=(n_cols // tn, m_rows // tm),
        in_specs=[pl.BlockSpec((tm, k), lambda j, m: (m, 0)), pl.BlockSpec((tm, tn), lambda j, m: (m, j))],
        out_specs=pl.BlockSpec((k, tn), lambda j, m: (0, j)),
        out_shape=jax.ShapeDtypeStruct((k, n_cols), F32),
        compiler_params=_cparams(("parallel", "arbitrary")),
    )(a, g)


def _mm_fused(name, pairs, tile_ins, full_ins, outs, accs, epilogue, tm, tn):
    m_rows, n_cols = pairs[0][0].shape[0], pairs[0][1].shape[1]
    n_p, n_t, n_f, n_o = len(pairs), len(tile_ins), len(full_ins), len(outs)

    def kern(*refs):
        dots = [jnp.dot(refs[2 * p][...], refs[2 * p + 1][...], preferred_element_type=F32) for p in range(n_p)]
        base = 2 * n_p
        t_refs, f_refs = refs[base:base + n_t], refs[base + n_t:base + n_t + n_f]
        o_refs, a_refs = refs[base + n_t + n_f:base + n_t + n_f + n_o], refs[base + n_t + n_f + n_o:]
        if a_refs:
            @pl.when((pl.program_id(0) == 0) & (pl.program_id(1) == 0))
            def _():
                for a in a_refs:
                    a[...] = jnp.zeros(a.shape, F32)
        epilogue(dots, t_refs, f_refs, o_refs, a_refs)

    in_specs, flat = [], []
    for a, w in pairs:
        k = a.shape[1]
        in_specs += [pl.BlockSpec((tm, k), lambda j, i: (i, 0)), pl.BlockSpec((k, tn), lambda j, i: (0, j))]
        flat += [a, w]
    for arr, cb in tile_ins:
        in_specs.append(pl.BlockSpec((tm, tn), functools.partial(lambda j, i, cb: (i, cb + j), cb=cb)))
        flat.append(arr)
    for arr in full_ins:
        in_specs.append(pl.BlockSpec(arr.shape, lambda j, i: (0, 0)))
        flat.append(arr)
    out_specs = [pl.BlockSpec((tm, f * tn), lambda j, i: (i, j)) for _, f in outs]
    out_specs += [pl.BlockSpec(s, lambda j, i: (0, 0)) for s in accs]
    out_shape = [jax.ShapeDtypeStruct((m_rows, f * n_cols), dt) for dt, f in outs]
    out_shape += [jax.ShapeDtypeStruct(s, F32) for s in accs]
    sem = ("arbitrary", "arbitrary") if accs else ("parallel", "parallel")
    return pl.pallas_call(
        kern, name=name, grid=(n_cols // tn, m_rows // tm), in_specs=in_specs, out_specs=out_specs,
        out_shape=out_shape, compiler_params=_cparams(sem),
    )(*flat)


def _peers():
    x, y, c = lax.axis_index("x"), lax.axis_index("y"), lax.axis_index("c")
    me = 4 * x + 2 * y + c
    peers = []
    for k in range(1, N_DEV):
        px = 1 - x if (k >> 2) & 1 else x
        py = 1 - y if (k >> 1) & 1 else y
        pc = 1 - c if k & 1 else c
        peers.append(((px, py, pc), 4 * px + 2 * py + pc))
    return me, peers


def _exchange(me, peers, items, send_sems, recv_sems, t0=0):
    started = []
    for t, (src_of, dst_of) in enumerate(items, start=t0):
        for k, (pid, plin) in enumerate(peers):
            cp = pltpu.make_async_remote_copy(
                src_ref=src_of(plin), dst_ref=dst_of(me), send_sem=send_sems.at[t, k], recv_sem=recv_sems.at[t, k],
                device_id=pid, device_id_type=MESH)
            cp.start()
            started.append(cp)
    for t, (src_of, dst_of) in enumerate(items, start=t0):
        for k, (pid, plin) in enumerate(peers):
            pltpu.make_async_remote_copy(
                src_ref=src_of(plin), dst_ref=dst_of(plin), send_sem=send_sems.at[t, k], recv_sem=recv_sems.at[t, k],
                device_id=pid, device_id_type=MESH).wait_recv()
    for cp in started:
        cp.wait_send()


def _hbm_spec():
    return pl.BlockSpec(memory_space=pltpu.HBM)


def _gather_shards(shard):
    def body(src, out, send_sems, recv_sems, own_sem):
        x, y, c = lax.axis_index("x"), lax.axis_index("y"), lax.axis_index("c")
        lin = lambda px, py, pc: 4 * px + 2 * py + pc
        sibling = (x, y, 1 - c)
        chips = [(1 - x, y), (x, 1 - y), (1 - x, 1 - y)]

        def copy(k, slot, to, from_src):
            return pltpu.make_async_remote_copy(
                src_ref=src if from_src else out.at[slot], dst_ref=out.at[slot], send_sem=send_sems.at[k],
                recv_sem=recv_sems.at[k], device_id=to, device_id_type=MESH)

        own = pltpu.make_async_copy(src, out.at[lin(x, y, c)], own_sem)
        own.start()
        first = [copy(0, lin(x, y, c), sibling, True)]
        first += [copy(1 + j, lin(x, y, c), (px, py, c), True) for j, (px, py) in enumerate(chips)]
        for cp in first:
            cp.start()
        passed = [copy(4 + j, lin(px, py, c), sibling, False) for j, (px, py) in enumerate(chips)]
        for j, (px, py) in enumerate(chips):
            copy(1 + j, lin(px, py, c), sibling, True).wait_recv()
            passed[j].start()
        copy(0, lin(x, y, 1 - c), sibling, True).wait_recv()
        for j, (px, py) in enumerate(chips):
            copy(4 + j, lin(px, py, 1 - c), sibling, False).wait_recv()
        for cp in first + passed:
            cp.wait_send()
        own.wait()

    return pl.pallas_call(
        body, name="gather_weights", in_specs=[_hbm_spec()], out_specs=_hbm_spec(),
        out_shape=jax.ShapeDtypeStruct((N_DEV,) + shard.shape, shard.dtype),
        scratch_shapes=[pltpu.SemaphoreType.DMA((N_DEV - 1,)), pltpu.SemaphoreType.DMA((N_DEV - 1,)),
                        pltpu.SemaphoreType.DMA(())],
    )(shard)


def _exchange_grads(g_big, g_small):
    def body(big, small, out_big, out_small, send_sems, recv_sems, own_sems):
        me, peers = _peers()
        own_b = pltpu.make_async_copy(big.at[me], out_big.at[me], own_sems.at[0])
        own_s = pltpu.make_async_copy(small, out_small.at[me], own_sems.at[1])
        own_b.start()
        own_s.start()
        _exchange(me, peers, [(lambda p: small, lambda s: out_small.at[s]),
                              (lambda p: big.at[p], lambda s: out_big.at[s])], send_sems, recv_sems)
        own_b.wait()
        own_s.wait()

    return pl.pallas_call(
        body, name="exchange_grads", in_specs=[_hbm_spec(), _hbm_spec()], out_specs=[_hbm_spec(), _hbm_spec()],
        out_shape=[jax.ShapeDtypeStruct(g_big.shape, g_big.dtype),
                   jax.ShapeDtypeStruct((N_DEV,) + g_small.shape, g_small.dtype)],
        scratch_shapes=[pltpu.SemaphoreType.DMA((2, N_DEV - 1)), pltpu.SemaphoreType.DMA((2, N_DEV - 1)),
                        pltpu.SemaphoreType.DMA((2,))],
    )(g_big, g_small)


def _ada_modulation(c8, w_ada, b_ada_cols):
    wcols = w_ada.shape[1]

    def body(c_ref, w_ref, b_ref, call_ref, cond_ref, mod_ref, res, send_sems, recv_sems):
        me, peers = _peers()
        call_ref[me] = c_ref[...]
        _exchange(me, peers, [(lambda p: c_ref, lambda s: call_ref.at[s])], send_sems, recv_sems, t0=0)
        for dev in range(N_DEV):
            cv = call_ref[dev]
            cond = cv * _sigmoid(cv)
            cond_ref[dev] = cond
            res[dev] = jnp.dot(cond, w_ref[...], preferred_element_type=F32,
                               precision=lax.Precision.HIGHEST) + b_ref[...]
        mod_ref[me] = res[me]
        _exchange(me, peers, [(lambda p: res.at[p], lambda s: mod_ref.at[s])], send_sems, recv_sems, t0=1)

    vm = pl.BlockSpec(memory_space=pltpu.VMEM)
    return pl.pallas_call(
        body, name="ada_modulation", in_specs=[vm, vm, vm], out_specs=[vm, vm, vm],
        out_shape=[jax.ShapeDtypeStruct((N_DEV, 8, D), F32), jax.ShapeDtypeStruct((N_DEV, 8, D), F32),
                   jax.ShapeDtypeStruct((N_DEV, 8, wcols), F32)],
        scratch_shapes=[pltpu.VMEM((N_DEV, 8, wcols), F32),
                        pltpu.SemaphoreType.DMA((2, N_DEV - 1)), pltpu.SemaphoreType.DMA((2, N_DEV - 1))],
        compiler_params=pltpu.CompilerParams(vmem_limit_bytes=VMEM_LIMIT),
    )(c8, w_ada, b_ada_cols)


def _lru_gates(xc, wa_ref, ba_ref, wi_ref, bi_ref, sp_ref, row0):
    xcb = xc.astype(BF16)
    pre_r, pre_i = [], []
    for n in range(N_BLK):
        xb = xcb[:, n * BW:(n + 1) * BW]
        pre_r.append(jnp.dot(xb, wa_ref[n], preferred_element_type=F32))
        pre_i.append(jnp.dot(xb, wi_ref[n], preferred_element_type=F32))
    r = _sigmoid(jnp.concatenate(pre_r, axis=1) + ba_ref[...])
    ig = _sigmoid(jnp.concatenate(pre_i, axis=1) + bi_ref[...])
    log_a = (-LRU_C) * r * sp_ref[...]
    a = jnp.exp(log_a)
    e2 = jnp.exp(2.0 * log_a)
    mult_raw = jnp.sqrt(1.0 - e2)
    rows = row0 + lax.broadcasted_iota(jnp.int32, xc.shape, 0)
    start = rows == 0
    mult = jnp.where(start, 1.0, mult_raw)
    return xcb, r, ig, a, e2, mult_raw, mult, start


def _conv(xpad, cw_ref, cb_ref, tc):
    out = cb_ref[...]
    for tap in range(CONV_W):
        out = out + cw_ref[tap:tap + 1, :] * xpad[pl.ds(8 - (CONV_W - 1) + tap, tc), :]
    return out


def _rnn_fwd(proj_rnn, cw, cb, wa, ba, wi, bi, sp):
    s_len = proj_rnn.shape[0]
    tc = RNN_CHUNK

    def kern(y_ref, xr_ref, cw_ref, cb_ref, wa_ref, ba_ref, wi_ref, bi_ref, sp_ref, hs_ref, ur_ref,
             xpad, a_scr, u_scr, h_scr):
        i = pl.program_id(0)

        @pl.when(i == 0)
        def _():
            xpad[0:8, :] = jnp.zeros((8, DR), F32)
            h_scr[...] = jnp.zeros((1, DR), F32)

        xpad[8:8 + tc, :] = xr_ref[...]
        xc = _conv(xpad, cw_ref, cb_ref, tc)
        xpad[0:8, :] = xpad[tc:tc + 8, :]
        _, r, ig, a, e2, mult_raw, mult, start = _lru_gates(xc, wa_ref, ba_ref, wi_ref, bi_ref, sp_ref, i * tc)
        a_scr[...] = a
        u_scr[...] = mult * (ig * xc)

        def step(t, h):
            h = a_scr[pl.ds(t, 1), :] * h + u_scr[pl.ds(t, 1), :]
            hs_ref[pl.ds(t, 1), :] = h
            return h

        h_scr[...] = lax.fori_loop(0, tc, step, h_scr[...], unroll=8)
        gy, _ = _gelu(y_ref[...])
        ur_ref[...] = (gy * hs_ref[...]).astype(BF16)

    full = lambda a: pl.BlockSpec(a.shape, lambda i: (0,) * a.ndim)
    return pl.pallas_call(
        kern, name="rnn_fwd", grid=(s_len // tc,),
        in_specs=[pl.BlockSpec((tc, DR), lambda i: (i, 0)), pl.BlockSpec((tc, DR), lambda i: (i, 1)),
                  full(cw), full(cb), full(wa), full(ba), full(wi), full(bi), full(sp)],
        out_specs=[pl.BlockSpec((tc, DR), lambda i: (i, 0)), pl.BlockSpec((tc, DR), lambda i: (i, 0))],
        out_shape=[jax.ShapeDtypeStruct((s_len, DR), F32), jax.ShapeDtypeStruct((s_len, DR), BF16)],
        scratch_shapes=[pltpu.VMEM((tc + 8, DR), F32), pltpu.VMEM((tc, DR), F32), pltpu.VMEM((tc, DR), F32),
                        pltpu.VMEM((1, DR), F32)],
        compiler_params=_cparams(("arbitrary",)),
    )(proj_rnn, proj_rnn, cw, cb, wa, ba, wi, bi, sp)


def _rnn_bwd(proj_rnn, hs, du, cw, cb, wa, wat, ba, wi, wit, bi, sp, dsp_dlam):
    s_len = proj_rnn.shape[0]
    tc = RNN_CHUNK
    nch = s_len // tc

    def kern(y_ref, xr_ref, xrp_ref, hs_ref, hsp_ref, du_ref, cw_ref, cb_ref, wa_ref, wat_ref, ba_ref, wi_ref,
             wit_ref, bi_ref, sp_ref, dspl_ref, drnn_ref, dwa_ref, dwi_ref, vec_ref,
             xpad, hpad, a_scr, d_scr, g_scr, dxcpad, ag_scr):
        j = pl.program_id(0)
        n = nch - 1 - j

        @pl.when(j == 0)
        def _():
            dwa_ref[...] = jnp.zeros(dwa_ref.shape, F32)
            dwi_ref[...] = jnp.zeros(dwi_ref.shape, F32)
            vec_ref[...] = jnp.zeros(vec_ref.shape, F32)
            ag_scr[...] = jnp.zeros((1, DR), F32)
            dxcpad[tc:tc + 8, :] = jnp.zeros((8, DR), F32)

        has_prev = n > 0
        xpad[0:8, :] = jnp.where(has_prev, xrp_ref[...], 0.0)
        xpad[8:8 + tc, :] = xr_ref[...]
        hpad[0:8, :] = jnp.where(has_prev, hsp_ref[...], 0.0)
        hpad[8:8 + tc, :] = hs_ref[...]
        xc = _conv(xpad, cw_ref, cb_ref, tc)
        xcb, r, ig, a, e2, mult_raw, mult, start = _lru_gates(xc, wa_ref, ba_ref, wi_ref, bi_ref, sp_ref, n * tc)

        y = y_ref[...]
        gy, th = _gelu(y)
        duv = du_ref[...].astype(F32)
        drnn_ref[:, 0:DR] = (duv * hs_ref[...] * _gelu_grad(y, th)).astype(BF16)
        a_scr[...] = a
        d_scr[...] = duv * gy

        def step(tt, ag):
            t = tc - 1 - tt
            g = d_scr[pl.ds(t, 1), :] + ag
            g_scr[pl.ds(t, 1), :] = g
            return a_scr[pl.ds(t, 1), :] * g

        ag_scr[...] = lax.fori_loop(0, tc, step, ag_scr[...], unroll=8)
        g = g_scr[...]
        da = g * hpad[pl.ds(7, tc), :]
        gx = g * xc
        dmult = jnp.where(start, 0.0, gx * ig)
        di = gx * mult
        dxc = g * mult * ig
        dlog_a = da * a - jnp.where(start, 0.0, dmult * e2 / mult_raw)
        dr = dlog_a * ((-LRU_C) * sp_ref[...])
        vec_ref[7:8, :] += _rowsum(dlog_a * ((-LRU_C) * r))
        dpr = dr * r * (1.0 - r)
        dpi = di * ig * (1.0 - ig)
        vec_ref[5:6, :] += _rowsum(dpr)
        vec_ref[6:7, :] += _rowsum(dpi)
        dprb, dpib = dpr.astype(BF16), dpi.astype(BF16)
        extra = []
        for b in range(N_BLK):
            sl = slice(b * BW, (b + 1) * BW)
            extra.append(jnp.dot(dprb[:, sl], wat_ref[b], preferred_element_type=F32)
                         + jnp.dot(dpib[:, sl], wit_ref[b], preferred_element_type=F32))
            dn = (((0,), (0,)), ((), ()))
            dwa_ref[b] += lax.dot_general(xcb[:, sl], dprb[:, sl], dn, preferred_element_type=F32)
            dwi_ref[b] += lax.dot_general(xcb[:, sl], dpib[:, sl], dn, preferred_element_type=F32)
        dxc = dxc + jnp.concatenate(extra, axis=1)
        vec_ref[4:5, :] += _rowsum(dxc)
        dxcpad[0:tc, :] = dxc
        dxr = jnp.zeros((tc, DR), F32)
        for tap in range(CONV_W):
            shift = CONV_W - 1 - tap
            dxr = dxr + cw_ref[tap:tap + 1, :] * dxcpad[pl.ds(shift, tc), :]
            vec_ref[tap:tap + 1, :] += _rowsum(dxc * xpad[pl.ds(8 - shift, tc), :])
        dxcpad[tc:tc + 8, :] = dxcpad[0:8, :]
        drnn_ref[:, DR:2 * DR] = dxr.astype(BF16)

        @pl.when(j == nch - 1)
        def _():
            vec_ref[7:8, :] = vec_ref[7:8, :] * dspl_ref[...]

    full = lambda a: pl.BlockSpec(a.shape, lambda j: (0,) * a.ndim)
    rev = lambda j: nch - 1 - j
    prev8 = lambda j: jnp.maximum((nch - 1 - j) * (tc // 8) - 1, 0)
    return pl.pallas_call(
        kern, name="rnn_bwd", grid=(nch,),
        in_specs=[pl.BlockSpec((tc, DR), lambda j: (rev(j), 0)), pl.BlockSpec((tc, DR), lambda j: (rev(j), 1)),
                  pl.BlockSpec((8, DR), lambda j: (prev8(j), 1)),
                  pl.BlockSpec((tc, DR), lambda j: (rev(j), 0)), pl.BlockSpec((8, DR), lambda j: (prev8(j), 0)),
                  pl.BlockSpec((tc, DR), lambda j: (rev(j), 0)),
                  full(cw), full(cb), full(wa), full(wat), full(ba), full(wi), full(wit), full(bi), full(sp),
                  full(dsp_dlam)],
        out_specs=[pl.BlockSpec((tc, 2 * DR), lambda j: (rev(j), 0)),
                   pl.BlockSpec((N_BLK, BW, BW), lambda j: (0, 0, 0)), pl.BlockSpec((N_BLK, BW, BW), lambda j: (0, 0, 0)),
                   pl.BlockSpec((8, DR), lambda j: (0, 0))],
        out_shape=[jax.ShapeDtypeStruct((s_len, 2 * DR), BF16), jax.ShapeDtypeStruct((N_BLK, BW, BW), F32),
                   jax.ShapeDtypeStruct((N_BLK, BW, BW), F32), jax.ShapeDtypeStruct((8, DR), F32)],
        scratch_shapes=[pltpu.VMEM((tc + 8, DR), F32), pltpu.VMEM((tc + 8, DR), F32), pltpu.VMEM((tc, DR), F32),
                        pltpu.VMEM((tc, DR), F32), pltpu.VMEM((tc, DR), F32), pltpu.VMEM((tc + 8, DR), F32),
                        pltpu.VMEM((1, DR), F32)],
        compiler_params=_cparams(("arbitrary",)),
    )(proj_rnn, proj_rnn, proj_rnn, hs, hs, du, cw, cb, wa, wat, ba, wi, wit, bi, sp, dsp_dlam)


def _attn_masks(n):
    qi = lax.broadcasted_iota(jnp.int32, (QB, 2 * QB), 0)
    ki = lax.broadcasted_iota(jnp.int32, (QB, 2 * QB), 1)
    dist = QB + qi - ki
    valid = (dist >= 0) & (dist <= QB) & ((n > 0) | (ki >= QB))
    lane = lax.broadcasted_iota(jnp.int32, (1, HEAD_PAIR), 1)
    return valid, lane


def _qkv_specs(d, n_of):
    prev = lambda r, j: jnp.maximum(n_of(j) - 1, 0)
    blk = (QB, D)
    return [pl.BlockSpec(blk, lambda r, j: (n_of(j), 3 * r)),
            pl.BlockSpec(blk, lambda r, j: (prev(r, j), 3 * r + 1)), pl.BlockSpec(blk, lambda r, j: (n_of(j), 3 * r + 1)),
            pl.BlockSpec(blk, lambda r, j: (prev(r, j), 3 * r + 2)), pl.BlockSpec(blk, lambda r, j: (n_of(j), 3 * r + 2))]


def _attn_fwd(qkv, d):
    s_len = qkv.shape[0]
    n_l = s_len // d
    nb = n_l // QB
    qv = qkv.reshape(n_l, d * 3 * D)

    def kern(q_ref, kp_ref, kc_ref, vp_ref, vc_ref, o_ref, lse_ref):
        valid, lane = _attn_masks(pl.program_id(1))
        first = lane < 64
        lse_blk = jnp.zeros((QB, HEAD_PAIR), F32)
        for hp in range(N_PAIR):
            sl = slice(hp * HEAD_PAIR, (hp + 1) * HEAD_PAIR)
            q2 = q_ref[:, sl]
            k2 = jnp.concatenate([kp_ref[:, sl], kc_ref[:, sl]], axis=0)
            v2 = jnp.concatenate([vp_ref[:, sl], vc_ref[:, sl]], axis=0)
            outs = []
            for half in range(2):
                hm = first if half == 0 else jnp.logical_not(first)
                qh = jnp.where(hm, q2, jnp.zeros_like(q2))
                s = lax.dot_general(qh, k2, (((1,), (1,)), ((), ())), preferred_element_type=F32) * SCALE
                s = jnp.where(valid, s, NEG)
                mx = jnp.max(s, axis=-1, keepdims=True)
                p = jnp.exp(s - mx)
                den = jnp.sum(p, axis=-1, keepdims=True)
                outs.append(jnp.dot(p.astype(BF16), v2, preferred_element_type=F32) / den)
                lse_blk = jnp.where(lane == 2 * hp + half, mx + jnp.log(den), lse_blk)
            o_ref[:, sl] = jnp.where(first, outs[0], outs[1]).astype(BF16)
        lse_ref[...] = lse_blk

    o, lse = pl.pallas_call(
        kern, name=f"attn_fwd_d{d}", grid=(d, nb), in_specs=_qkv_specs(d, lambda j: j),
        out_specs=[pl.BlockSpec((QB, D), lambda r, j: (j, r)), pl.BlockSpec((QB, HEAD_PAIR), lambda r, j: (j, r))],
        out_shape=[jax.ShapeDtypeStruct((n_l, d * D), BF16), jax.ShapeDtypeStruct((n_l, d * HEAD_PAIR), F32)],
        compiler_params=_cparams(("arbitrary", "arbitrary")),
    )(qv, qv, qv, qv, qv)
    return o.reshape(s_len, D), lse.reshape(s_len, HEAD_PAIR)


def _attn_bwd(qkv, do, o, lse, d):
    s_len = qkv.shape[0]
    n_l = s_len // d
    nb = n_l // QB
    qv = qkv.reshape(n_l, d * 3 * D)
    view = lambda t: t.reshape(n_l, d * D)

    def kern(q_ref, kp_ref, kc_ref, vp_ref, vc_ref, do_ref, o_ref, lse_ref, dq_ref, dk_ref, dv_ref, dk_scr, dv_scr):
        j = pl.program_id(1)

        @pl.when(j == 0)
        def _():
            dk_scr[...] = jnp.zeros((QB, D), F32)
            dv_scr[...] = jnp.zeros((QB, D), F32)

        valid, lane = _attn_masks(nb - 1 - j)
        first = lane < 64
        lse_blk = lse_ref[...]
        nt = (((1,), (1,)), ((), ()))
        tn = (((0,), (0,)), ((), ()))
        for hp in range(N_PAIR):
            sl = slice(hp * HEAD_PAIR, (hp + 1) * HEAD_PAIR)
            q2 = q_ref[:, sl]
            k2 = jnp.concatenate([kp_ref[:, sl], kc_ref[:, sl]], axis=0)
            v2 = jnp.concatenate([vp_ref[:, sl], vc_ref[:, sl]], axis=0)
            do2 = do_ref[:, sl]
            prod = do2.astype(F32) * o_ref[:, sl].astype(F32)
            dqs = []
            dk2 = jnp.zeros((2 * QB, HEAD_PAIR), F32)
            dv2 = jnp.zeros((2 * QB, HEAD_PAIR), F32)
            for half in range(2):
                hm = first if half == 0 else jnp.logical_not(first)
                qh = jnp.where(hm, q2, jnp.zeros_like(q2))
                doh = jnp.where(hm, do2, jnp.zeros_like(do2))
                dsum = jnp.sum(jnp.where(hm, prod, 0.0), axis=-1, keepdims=True)
                lseh = jnp.sum(jnp.where(lane == 2 * hp + half, lse_blk, 0.0), axis=-1, keepdims=True)
                s = lax.dot_general(qh, k2, nt, preferred_element_type=F32) * SCALE
                p = jnp.exp(jnp.where(valid, s, NEG) - lseh)
                dp = lax.dot_general(doh, v2, nt, preferred_element_type=F32)
                ds = (p * (dp - dsum) * SCALE).astype(BF16)
                dqs.append(jnp.dot(ds, k2, preferred_element_type=F32))
                dk2 = dk2 + lax.dot_general(ds, qh, tn, preferred_element_type=F32)
                dv2 = dv2 + lax.dot_general(p.astype(BF16), doh, tn, preferred_element_type=F32)
            dq_ref[:, sl] = jnp.where(first, dqs[0], dqs[1]).astype(BF16)
            dk_ref[:, sl] = (dk2[QB:, :] + dk_scr[:, sl]).astype(BF16)
            dv_ref[:, sl] = (dv2[QB:, :] + dv_scr[:, sl]).astype(BF16)
            dk_scr[:, sl] = dk2[:QB, :]
            dv_scr[:, sl] = dv2[:QB, :]

    rev = lambda j: nb - 1 - j
    row = pl.BlockSpec((QB, D), lambda r, j: (rev(j), r))
    dq, dk, dv = pl.pallas_call(
        kern, name=f"attn_bwd_d{d}", grid=(d, nb),
        in_specs=_qkv_specs(d, rev) + [row, row, pl.BlockSpec((QB, HEAD_PAIR), lambda r, j: (rev(j), r))],
        out_specs=[row, row, row], out_shape=[jax.ShapeDtypeStruct((n_l, d * D), BF16)] * 3,
        scratch_shapes=[pltpu.VMEM((QB, D), F32), pltpu.VMEM((QB, D), F32)],
        compiler_params=_cparams(("arbitrary", "arbitrary")),
    )(qv, qv, qv, qv, qv, view(do), view(o), lse.reshape(n_l, d * HEAD_PAIR))
    return dq.reshape(s_len, D), dk.reshape(s_len, D), dv.reshape(s_len, D)


def _adamw(w, g, m, v):
    m = ADAM_B1 * m + (1.0 - ADAM_B1) * g
    v = ADAM_B2 * v + (1.0 - ADAM_B2) * (g * g)
    m_hat = m / (1.0 - ADAM_B1 ** ADAM_STEP)
    v_hat = v / (1.0 - ADAM_B2 ** ADAM_STEP)
    delta = -ADAM_LR * (m_hat / (jnp.sqrt(v_hat) + ADAM_EPS) + ADAM_WD * w)
    return delta, m, v


def _adam_packed(name, recv, w, m, v, tm):
    n_rows, width = w.shape
    summed = recv.ndim == 3

    def kern(r_ref, w_ref, m_ref, v_ref, g_out, d_out, m_out, v_out):
        if summed:
            g = r_ref[0].astype(F32)
            for s in range(1, N_DEV):
                g = g + r_ref[s].astype(F32)
        else:
            g = r_ref[...]
        delta, mn, vn = _adamw(w_ref[...], g, m_ref[...], v_ref[...])
        g_out[...] = g
        d_out[...] = delta
        m_out[...] = mn
        v_out[...] = vn

    tile = pl.BlockSpec((tm, width), lambda i: (i, 0))
    rspec = pl.BlockSpec((N_DEV, tm, width), lambda i: (0, i, 0)) if summed else tile
    return pl.pallas_call(
        kern, name=name, grid=(n_rows // tm,), in_specs=[rspec, tile, tile, tile], out_specs=[tile] * 4,
        out_shape=[jax.ShapeDtypeStruct((n_rows, width), F32)] * 4, compiler_params=_cparams(("parallel",)),
    )(recv, w, m, v)


def _sum_slots(recv):
    _, n_rows, width = recv.shape

    def kern(r_ref, o_ref):
        g = r_ref[0]
        for s in range(1, N_DEV):
            g = g + r_ref[s]
        o_ref[...] = g

    return pl.pallas_call(
        kern, name="sum_small", grid=(1,), in_specs=[pl.BlockSpec(recv.shape, lambda i: (0, 0, 0))],
        out_specs=pl.BlockSpec((n_rows, width), lambda i: (0, 0)),
        out_shape=jax.ShapeDtypeStruct((n_rows, width), F32), compiler_params=_cparams(("arbitrary",)),
    )(recv)


def _adam_w_ada(cond_t, dmod_cols, w, m, v):
    n_rows, width = w.shape
    tm = ROW_TILE

    def kern(c_ref, d_ref, w_ref, m_ref, v_ref, g_out, d_out, m_out, v_out):
        g = c_ref[:, 0:1] * d_ref[0:1, :]
        for b in range(1, N_DEV):
            g = g + c_ref[:, b:b + 1] * d_ref[b:b + 1, :]
        delta, mn, vn = _adamw(w_ref[...], g, m_ref[...], v_ref[...])
        g_out[...] = g
        d_out[...] = delta
        m_out[...] = mn
        v_out[...] = vn

    tile = pl.BlockSpec((tm, width), lambda i: (i, 0))
    return pl.pallas_call(
        kern, name="adam_w_ada", grid=(n_rows // tm,),
        in_specs=[pl.BlockSpec((tm, N_DEV), lambda i: (i, 0)), pl.BlockSpec((N_DEV, width), lambda i: (0, 0)),
                  tile, tile, tile],
        out_specs=[tile] * 4, out_shape=[jax.ShapeDtypeStruct((n_rows, width), F32)] * 4,
        compiler_params=_cparams(("parallel",)),
    )(cond_t, dmod_cols, w, m, v)


def _rows(a, n_rows=None):
    flat = a.reshape(-1)
    need = (n_rows if n_rows is not None else -(-flat.shape[0] // D)) * D
    if need != flat.shape[0]:
        flat = jnp.concatenate([flat, jnp.zeros((need - flat.shape[0],), flat.dtype)])
    return flat.reshape(-1, D)


def _pack_shards(ws):
    return jnp.concatenate([ws[name][0].reshape(r, D) for name, r in PACK_ROWS], axis=0)


def _cols_to_slots(g, cols):
    k = g.shape[0]
    return g.reshape(k, N_DEV, cols).transpose(1, 0, 2).reshape(N_DEV, k * cols // D, D)


def _slots_to_cols(s, k, cols):
    return s.reshape(N_DEV, k, cols).transpose(1, 0, 2).reshape(k, N_DEV * cols)


def kernel(x, c, w_ada, b_ada, w_in, conv_w, conv_b, lru_wa, lru_ba, lru_wi, lru_bi, lru_lambda, w_proj_rnn, w_proj_attn, b_gate, w_out, ln1_g, ln1_b, w_ffn_gate, w_ffn_up, w_ffn_down, ln2_g, ln2_b, loss_target, m_w_ada, m_b_ada, m_w_in, m_conv_w, m_conv_b, m_lru_wa, m_lru_ba, m_lru_wi, m_lru_bi, m_lru_lambda, m_w_proj_rnn, m_w_proj_attn, m_b_gate, m_w_out, m_ln1_g, m_ln1_b, m_w_ffn_gate, m_w_ffn_up, m_w_ffn_down, m_ln2_g, m_ln2_b, v_w_ada, v_b_ada, v_w_in, v_conv_w, v_conv_b, v_lru_wa, v_lru_ba, v_lru_wi, v_lru_bi, v_lru_lambda, v_w_proj_rnn, v_w_proj_attn, v_b_gate, v_w_out, v_ln1_g, v_ln1_b, v_w_ffn_gate, v_w_ffn_up, v_w_ffn_down, v_ln2_g, v_ln2_b):
    s_len = x.shape[1]
    me = 4 * lax.axis_index("x") + 2 * lax.axis_index("y") + lax.axis_index("c")
    xs = x[0]
    tgt = loss_target[0]
    tm = ROW_TILE

    b_ada_cols = lax.dynamic_slice(b_ada, (0, me * 768), (1, 768))
    pad_cols = lambda a: jnp.concatenate([a, jnp.zeros((a.shape[0], D - a.shape[1]), F32)], axis=1)
    c8 = jnp.concatenate([c, pad_cols(conv_w[0]), pad_cols(b_gate[0]), jnp.zeros((1, D), F32)], axis=0)
    c_all, cond_blocks, mod_parts = _ada_modulation(c8, w_ada[0], b_ada_cols)
    cond_all = cond_blocks[:, 0, :]
    mod = mod_parts[:, 0, :].reshape(6, D)
    mod8 = jnp.concatenate([mod, jnp.zeros((2, D), F32)], axis=0)
    cw = c_all[:, 1:1 + CONV_W, :DR // N_DEV].transpose(1, 0, 2).reshape(CONV_W, DR)
    bg = c_all[:, 1 + CONV_W:3 + CONV_W, :D // N_DEV].transpose(1, 0, 2).reshape(2, D)
    bg8 = jnp.concatenate([bg, jnp.zeros((6, D), F32)], axis=0)

    big = dict(w_in=w_in, w_ffn_gate=w_ffn_gate, w_ffn_up=w_ffn_up, w_proj_rnn=w_proj_rnn, w_proj_attn=w_proj_attn,
               w_out=w_out, w_ffn_down=w_ffn_down)
    gathered = _gather_shards(_pack_shards(big).astype(BF16))
    offs, o = {}, 0
    for name, r in PACK_ROWS:
        offs[name] = (o, o + r)
        o += r
    part = lambda name: gathered[:, offs[name][0]:offs[name][1], :]
    w_in_f = _slots_to_cols(part("w_in"), D, 960)
    w_rnn, w_qkv, w_gates = w_in_f[:, :2 * DR], w_in_f[:, 2 * DR:2 * DR + 3 * D], w_in_f[:, 2 * DR + 3 * D:]
    w_g, w_u = _slots_to_cols(part("w_ffn_gate"), D, 352), _slots_to_cols(part("w_ffn_up"), D, 352)
    w_pr = part("w_proj_rnn").reshape(DR, D)
    w_pa = part("w_proj_attn").reshape(D, D)
    w_o = part("w_out").reshape(D, D)
    w_dn = part("w_ffn_down").reshape(DFF, D)

    cb = conv_b
    wa_b, wi_b = lru_wa[0].astype(BF16), lru_wi[0].astype(BF16)
    wat_b, wit_b = jnp.swapaxes(wa_b, 1, 2), jnp.swapaxes(wi_b, 1, 2)
    ba, bi = lru_ba.reshape(1, DR), lru_bi.reshape(1, DR)
    sp = jax.nn.softplus(-lru_lambda)
    dsp_dlam = -jax.nn.sigmoid(-lru_lambda)
    ln1 = jnp.concatenate([ln1_g, ln1_b, jnp.zeros((6, D), F32)], axis=0)
    ln2 = jnp.concatenate([ln2_g, ln2_b, jnp.zeros((6, D), F32)], axis=0)

    def f1(ri, fi, ro, ao):
        n, _ = _ln(ri[0][...])
        ro[0][...] = (n * (1.0 + fi[0][1:2, :]) + fi[0][0:1, :]).astype(BF16)

    (h1,) = _rowwise(f1, "ln_mod1", s_len, tm, [(xs, D, 0)], [mod8], [(D, BF16)], [])
    proj_rnn = _mm("mm_in_rnn", [(h1, w_rnn)], F32, 512, DR)
    qkv = _mm("mm_in_qkv", [(h1, w_qkv)], BF16, 512, 1536)
    gates = _mm("mm_in_gates", [(h1, w_gates)], F32, 512, D)

    hs, u_rnn = _rnn_fwd(proj_rnn, cw, cb, wa_b, ba, wi_b, bi, sp)

    group = [_attn_fwd(qkv, d) for d in DILATIONS]

    expand = (lax.broadcasted_iota(jnp.int32, (HEAD_PAIR, D), 1) // 64
              == lax.broadcasted_iota(jnp.int32, (HEAD_PAIR, D), 0)).astype(BF16)

    def f_mix(ri, fi, ro, ao):
        ls = [ri[3 + g][...] for g in range(3)]
        mx = jnp.maximum(jnp.maximum(ls[0], ls[1]), ls[2])
        es = [jnp.exp(l - mx) for l in ls]
        tot = es[0] + es[1] + es[2]
        acc = None
        for g in range(3):
            wgt = es[g] / tot
            hi = wgt.astype(BF16)
            lo = (wgt - hi.astype(F32)).astype(BF16)
            wide = (jnp.dot(hi, fi[0][...], preferred_element_type=F32)
                    + jnp.dot(lo, fi[0][...], preferred_element_type=F32))
            t = wide * ri[g][...].astype(F32)
            acc = t if acc is None else acc + t
        ro[0][...] = acc.astype(BF16)
        ro[1][...] = mx + jnp.log(tot)

    u_attn, lse = _rowwise(f_mix, "attn_mix", s_len, tm,
                           [(o_g, D, 0) for o_g, _ in group] + [(l_g, HEAD_PAIR, 0) for _, l_g in group],
                           [expand], [(D, BF16), (HEAD_PAIR, F32)], [])

    def e_merge(dots, ti, fi, ro, ao):
        g_r = _sigmoid(ti[0][...] + fi[0][0:1, :])
        g_a = _sigmoid(ti[1][...] + fi[0][1:2, :])
        ro[0][...] = dots[0].astype(BF16)
        ro[1][...] = dots[1].astype(BF16)
        ro[2][...] = (g_r * dots[0] + g_a * dots[1]).astype(BF16)

    pr, pa, merged = _mm_fused("mm_proj_merge", [(u_rnn, w_pr), (u_attn, w_pa)], [(gates, 0), (gates, 1)], [bg8],
                               [(BF16, 1), (BF16, 1), (BF16, 1)], [], e_merge, 512, D)
    f = _mm("mm_out", [(merged, w_o)], F32, 512, D)

    def f3(ri, fi, ro, ao):
        md, l1 = fi[0], fi[1]
        n1, _ = _ln(ALPHA * ri[0][...] + md[2:3, :] * ri[1][...])
        x1 = n1 * l1[0:1, :] + l1[1:2, :]
        n0, _ = _ln(x1)
        ro[0][...] = x1
        ro[1][...] = (n0 * (1.0 + md[4:5, :]) + md[3:4, :]).astype(BF16)

    x1, h2 = _rowwise(f3, "post1", s_len, tm, [(xs, D, 0), (f, D, 0)], [mod8, ln1], [(D, F32), (D, BF16)], [])
    def e_swiglu(dots, ti, fi, ro, ao):
        gp, up = dots
        ro[0][...] = gp.astype(BF16)
        ro[1][...] = up.astype(BF16)
        ro[2][...] = (gp * _sigmoid(gp) * up).astype(BF16)

    gpre, upre, act = _mm_fused("mm_ffn_in", [(h2, w_g), (h2, w_u)], [], [], [(BF16, 1)] * 3, [], e_swiglu, 512, DFF // 2)
    f2o = _mm("mm_ffn_out", [(act, w_dn)], F32, 512, D)

    def f5(ri, fi, ro, ao):
        md, l2 = fi[0], fi[1]
        f2v = ri[1][...]
        n2, rstd = _ln(ALPHA * ri[0][...] + md[5:6, :] * f2v)
        err = n2 * l2[0:1, :] + l2[1:2, :] - ri[2][...]
        dx2 = err * (1.0 / D)
        dz2 = _ln_bwd(dx2 * l2[0:1, :], n2, rstd)
        ro[0][...] = dz2
        ro[1][...] = (md[5:6, :] * dz2).astype(BF16)
        acc = ao[0]
        acc[0:1, :] += _rowsum(err * err) * (0.5 / D)
        acc[1:2, :] += _rowsum(dx2 * n2)
        acc[2:3, :] += _rowsum(dx2)
        acc[3:4, :] += _rowsum(dz2 * f2v)

    dz2, df2, acc5 = _rowwise(f5, "loss_post2", s_len, tm, [(x1, D, 0), (f2o, D, 0), (tgt, D, 0)], [mod8, ln2],
                              [(D, F32), (D, BF16)], [((8, D), F32)])

    d_w_dn = _mm_tn("mmt_ffn_down", act, df2, 1024, 512)

    def e_dswiglu(dots, ti, fi, ro, ao):
        da = dots[0]
        gp, up = ti[0][...].astype(F32), ti[1][...].astype(F32)
        sg = _sigmoid(gp)
        ro[0][...] = (da * up * sg * (1.0 + gp * (1.0 - sg))).astype(BF16)
        ro[1][...] = (da * gp * sg).astype(BF16)

    dgp, dup = _mm_fused("mm_d_ffn_out", [(df2, w_dn.T)], [(gpre, 0), (upre, 0)], [], [(BF16, 1)] * 2, [], e_dswiglu,
                         512, DFF // 2)
    dh2 = _mm("mm_d_ffn_in", [(dgp, w_g.T), (dup, w_u.T)], F32, 256, D)
    d_w_g = _mm_tn("mmt_ffn_gate", h2, dgp, 1024, DFF // 2)
    d_w_u = _mm_tn("mmt_ffn_up", h2, dup, 1024, DFF // 2)

    def b2(ri, fi, ro, ao):
        md, l1 = fi[0], fi[1]
        fv, dh2v = ri[1][...], ri[2][...]
        n1, rstd1 = _ln(ALPHA * ri[0][...] + md[2:3, :] * fv)
        n0, rstd0 = _ln(n1 * l1[0:1, :] + l1[1:2, :])
        dx1 = ALPHA * ri[3][...] + _ln_bwd(dh2v * (1.0 + md[4:5, :]), n0, rstd0)
        dz1 = _ln_bwd(dx1 * l1[0:1, :], n1, rstd1)
        ro[0][...] = ALPHA * dz1
        ro[1][...] = (md[2:3, :] * dz1).astype(BF16)
        acc = ao[0]
        acc[0:1, :] += _rowsum(dh2v * n0)
        acc[1:2, :] += _rowsum(dh2v)
        acc[2:3, :] += _rowsum(dx1 * n1)
        acc[3:4, :] += _rowsum(dx1)
        acc[4:5, :] += _rowsum(dz1 * fv)

    dxp, df, acc2 = _rowwise(b2, "d_post1", s_len, tm, [(xs, D, 0), (f, D, 0), (dh2, D, 0), (dz2, D, 0)], [mod8, ln1],
                             [(D, F32), (D, BF16)], [((8, D), F32)])
    d_w_o = _mm_tn("mmt_out", merged, df, 1024, 512)

    def e_dmerge(dots, ti, fi, ro, ao):
        dm = dots[0]
        g_r = _sigmoid(ti[0][...] + fi[0][0:1, :])
        g_a = _sigmoid(ti[1][...] + fi[0][1:2, :])
        ro[0][...] = (dm * g_r).astype(BF16)
        ro[1][...] = (dm * g_a).astype(BF16)
        dga = dm * ti[2][...].astype(F32) * g_r * (1.0 - g_r)
        dgb = dm * ti[3][...].astype(F32) * g_a * (1.0 - g_a)
        ro[2][:, 0:D] = dga.astype(BF16)
        ro[2][:, D:2 * D] = dgb.astype(BF16)
        ao[0][0:1, :] += _rowsum(dga)
        ao[0][1:2, :] += _rowsum(dgb)

    dpr, dpa, dgates, acc3 = _mm_fused("mm_d_out", [(df, w_o.T)], [(gates, 0), (gates, 1), (pr, 0), (pa, 0)], [bg8],
                                       [(BF16, 1), (BF16, 1), (BF16, 2)], [(8, D)], e_dmerge, 512, D)
    du_rnn = _mm("mm_d_proj_rnn", [(dpr, w_pr.T)], F32, 512, DR)
    du_attn = _mm("mm_d_proj_attn", [(dpa, w_pa.T)], BF16, 512, D)
    d_w_pr = _mm_tn("mmt_proj_rnn", u_rnn, dpr, 1024, 512)
    d_w_pa = _mm_tn("mmt_proj_attn", u_attn, dpa, 1024, 512)

    drnn, d_wa, d_wi, vec = _rnn_bwd(proj_rnn, hs, du_rnn, cw, cb, wa_b, wat_b, ba, wi_b, wit_b, bi, sp, dsp_dlam)

    dparts = [_attn_bwd(qkv, du_attn, u_attn, lse, d) for d in DILATIONS]

    def b4(ri, fi, ro, ao):
        for col in range(3):
            ro[0][:, col * D:(col + 1) * D] = (ri[col][...].astype(F32) + ri[3 + col][...].astype(F32)
                                               + ri[6 + col][...].astype(F32)).astype(BF16)

    (dqkv,) = _rowwise(b4, "d_qkv_sum", s_len, tm, [(t, D, 0) for grp in dparts for t in grp], [], [(3 * D, BF16)], [])

    dh1 = _mm("mm_d_in", [(drnn, w_rnn.T), (dqkv, w_qkv.T), (dgates, w_gates.T)], F32, 256, 512)
    d_w_in = jnp.concatenate([_mm_tn("mmt_in_rnn", h1, drnn, 1024, DR), _mm_tn("mmt_in_qkv", h1, dqkv, 1024, D),
                              _mm_tn("mmt_in_gates", h1, dgates, 1024, D)], axis=1)

    def b5(ri, fi, ro, ao):
        md = fi[0]
        n0, rstd0 = _ln(ri[0][...])
        dh = ri[1][...]
        ro[0][...] = ri[2][...] + _ln_bwd(dh * (1.0 + md[1:2, :]), n0, rstd0)
        ao[0][0:1, :] += _rowsum(dh * n0)
        ao[0][1:2, :] += _rowsum(dh)

    grad_x, acc1 = _rowwise(b5, "d_ln_mod1", s_len, tm, [(xs, D, 0), (dh1, D, 0), (dxp, D, 0)], [mod8], [(D, F32)],
                            [((8, D), F32)])

    g_big = jnp.concatenate([
        _cols_to_slots(d_w_in, 960), _cols_to_slots(d_w_g, 352), _cols_to_slots(d_w_u, 352),
        d_w_pr.reshape(N_DEV, 160, D), d_w_pa.reshape(N_DEV, 128, D), d_w_o.reshape(N_DEV, 128, D),
        d_w_dn.reshape(N_DEV, 352, D)], axis=1).astype(BF16)
    dmod = jnp.concatenate([acc1[1:2], acc1[0:1], acc2[4:5], acc2[1:2], acc2[0:1], acc5[3:4]], axis=0)
    dmod_slots = lax.dynamic_update_slice(jnp.zeros((N_DEV, 6, D), F32), dmod[None], (me, 0, 0)).reshape(N_DEV * 6, D)
    g_small = jnp.concatenate([
        dmod_slots,
        _rows(vec[4], 2), d_wa.reshape(160, D), _rows(vec[5], 2), d_wi.reshape(160, D), _rows(vec[6], 2), _rows(vec[7], 2),
        acc2[2:4], acc5[1:3],
        _rows(vec[0:4], 5), acc3[0:2], acc5[0:1], jnp.zeros((4, D), F32)], axis=0)
    recv_big, recv_small = _exchange_grads(g_big, g_small)
    red = _sum_slots(recv_small)
    dmod_all = red[0:48].reshape(N_DEV, 6 * D)
    loss = jnp.sum(red[387])

    packs = [_pack_shards(dict(w_in=a, w_ffn_gate=b, w_ffn_up=cc, w_proj_rnn=d_, w_proj_attn=e, w_out=ff, w_ffn_down=gg))
             for a, b, cc, d_, e, ff, gg in ((w_in, w_ffn_gate, w_ffn_up, w_proj_rnn, w_proj_attn, w_out, w_ffn_down),
                                            (m_w_in, m_w_ffn_gate, m_w_ffn_up, m_w_proj_rnn, m_w_proj_attn, m_w_out, m_w_ffn_down),
                                            (v_w_in, v_w_ffn_gate, v_w_ffn_up, v_w_proj_rnn, v_w_proj_attn, v_w_out, v_w_ffn_down))]
    big_out = _adam_packed("adam_big", recv_big, packs[0], packs[1], packs[2], 128)
    ada_out = _adam_w_ada(cond_all.T, lax.dynamic_slice(dmod_all, (0, me * 768), (N_DEV, 768)),
                          w_ada[0], m_w_ada[0], v_w_ada[0])

    small_spec = (("b_ada", 6), ("conv_b", 2), ("lru_wa", 160), ("lru_ba", 2), ("lru_wi", 160), ("lru_bi", 2),
                  ("lru_lambda", 2), ("ln1_g", 1), ("ln1_b", 1), ("ln2_g", 1), ("ln2_b", 1), ("conv_w", 1), ("b_gate", 1))
    small_rows = 344

    def pack_small(ps):
        return jnp.concatenate([_rows(ps[name], r) for name, r in small_spec] + [jnp.zeros((4, D), F32)], axis=0)

    names = [n for n, _ in small_spec]
    smalls = [pack_small(dict(zip(names, t))) for t in (
        (b_ada, conv_b, lru_wa, lru_ba, lru_wi, lru_bi, lru_lambda, ln1_g, ln1_b, ln2_g, ln2_b, conv_w, b_gate),
        (m_b_ada, m_conv_b, m_lru_wa, m_lru_ba, m_lru_wi, m_lru_bi, m_lru_lambda, m_ln1_g, m_ln1_b, m_ln2_g, m_ln2_b,
         m_conv_w, m_b_gate),
        (v_b_ada, v_conv_b, v_lru_wa, v_lru_ba, v_lru_wi, v_lru_bi, v_lru_lambda, v_ln1_g, v_ln1_b, v_ln2_g, v_ln2_b,
         v_conv_w, v_b_gate))]
    g_conv_w = lax.dynamic_slice(red[380:385].reshape(-1)[:CONV_W * DR].reshape(CONV_W, DR), (0, me * 160), (CONV_W, 160))
    g_b_gate = lax.dynamic_slice(red[385:387], (0, me * 128), (2, 128))
    g_small_pack = jnp.concatenate([jnp.sum(dmod_all, axis=0).reshape(6, D), red[48:380], _rows(g_conv_w, 1),
                                    _rows(g_b_gate, 1), jnp.zeros((4, D), F32)], axis=0)
    assert g_small_pack.shape == (small_rows, D) and smalls[0].shape == (small_rows, D)
    small_out = _adam_packed("adam_small", g_small_pack, smalls[0], smalls[1], smalls[2], small_rows)

    shapes = dict(w_ada=w_ada.shape, b_ada=b_ada.shape, w_in=w_in.shape, conv_w=conv_w.shape, conv_b=conv_b.shape,
                  lru_wa=lru_wa.shape, lru_ba=lru_ba.shape, lru_wi=lru_wi.shape, lru_bi=lru_bi.shape,
                  lru_lambda=lru_lambda.shape, w_proj_rnn=w_proj_rnn.shape, w_proj_attn=w_proj_attn.shape,
                  b_gate=b_gate.shape, w_out=w_out.shape, ln1_g=ln1_g.shape, ln1_b=ln1_b.shape,
                  w_ffn_gate=w_ffn_gate.shape, w_ffn_up=w_ffn_up.shape, w_ffn_down=w_ffn_down.shape,
                  ln2_g=ln2_g.shape, ln2_b=ln2_b.shape)

    def unpack(kind):
        out = {"w_ada": ada_out[kind].reshape(shapes["w_ada"])}
        o = 0
        for name, r in PACK_ROWS:
            out[name] = big_out[kind][o:o + r].reshape(shapes[name])
            o += r
        o = 0
        for name, r in small_spec:
            size = 1
            for dim in shapes[name]:
                size *= dim
            out[name] = small_out[kind][o:o + r].reshape(-1)[:size].reshape(shapes[name])
            o += r
        return [out[name] for name in shapes]

    return (loss, grad_x[None], *unpack(0), *unpack(1), *unpack(2), *unpack(3))
```

```python
import functools

import jax
import jax.numpy as jnp
from jax import lax
from jax.experimental import pallas as pl
from jax.experimental.pallas import tpu as pltpu

F32 = jnp.float32
BF16 = jnp.bfloat16
MESH = pl.DeviceIdType.MESH

N_DEV = 8
D = 1024
DR = 1280
N_BLK = 10
BW = 128
HEAD_PAIR = 128
N_PAIR = 8
QB = 128
DFF = 2816
CONV_W = 4
DILATIONS = (1, 4, 16)
ALPHA = 2.0 ** 0.25
LN_EPS = 1e-5
LRU_C = 8.0
SCALE = 0.125
NEG = -1e30
ADAM_LR, ADAM_B1, ADAM_B2, ADAM_EPS, ADAM_WD, ADAM_STEP = 0.001, 0.9, 0.999, 1e-08, 0.01, 10

VMEM_LIMIT = 48 * 1024 * 1024
ROW_TILE = 256
RNN_CHUNK = 256

PACK_ROWS = (("w_in", 960), ("w_ffn_gate", 352), ("w_ffn_up", 352), ("w_proj_rnn", 160),
             ("w_proj_attn", 128), ("w_out", 128), ("w_ffn_down", 352))
LATE_ROWS = PACK_ROWS[1:]
FFN_ROWS = (("w_ffn_gate", 352), ("w_ffn_up", 352), ("w_ffn_down", 352))
PROJ_ROWS = (("w_proj_rnn", 160), ("w_proj_attn", 128), ("w_out", 128))


def _cparams(sem):
    return pltpu.CompilerParams(dimension_semantics=sem, vmem_limit_bytes=VMEM_LIMIT)


def _ln(z):
    mu = jnp.mean(z, axis=-1, keepdims=True)
    zc = z - mu
    var = jnp.mean(zc * zc, axis=-1, keepdims=True)
    rstd = lax.rsqrt(var + LN_EPS)
    return zc * rstd, rstd


def _ln_bwd(dn, n, rstd):
    return rstd * (dn - jnp.mean(dn, axis=-1, keepdims=True) - n * jnp.mean(dn * n, axis=-1, keepdims=True))


def _sigmoid(x):
    return 1.0 / (1.0 + jnp.exp(-x))


_GELU_K = 0.7978845608028654
_GELU_C = 0.044715


def _gelu(y):
    t = jnp.tanh(_GELU_K * (y + _GELU_C * y * y * y))
    return 0.5 * y * (1.0 + t), t


def _gelu_grad(y, t):
    return 0.5 * (1.0 + t) + 0.5 * y * (1.0 - t * t) * _GELU_K * (1.0 + 3.0 * _GELU_C * y * y)


def _rowsum(v):
    return jnp.sum(v, axis=0, keepdims=True)


def _rowwise(body, name, n_rows, tm, row_ins, full_ins, row_outs, acc_outs):
    nri, nfi, nro = len(row_ins), len(full_ins), len(row_outs)

    def kern(*refs):
        ri, fi = refs[:nri], refs[nri:nri + nfi]
        ro, ao = refs[nri + nfi:nri + nfi + nro], refs[nri + nfi + nro:]
        if ao:
            @pl.when(pl.program_id(0) == 0)
            def _():
                for a in ao:
                    a[...] = jnp.zeros(a.shape, a.dtype)
        body(ri, fi, ro, ao)

    in_specs = [pl.BlockSpec((tm, w), functools.partial(lambda i, cb: (i, cb), cb=cb)) for _, w, cb in row_ins]
    in_specs += [pl.BlockSpec(a.shape, lambda i: (0, 0)) for a in full_ins]
    out_specs = [pl.BlockSpec((tm, w), lambda i: (i, 0)) for w, _ in row_outs]
    out_specs += [pl.BlockSpec(s, lambda i: (0, 0)) for s, _ in acc_outs]
    out_shape = [jax.ShapeDtypeStruct((n_rows, w), dt) for w, dt in row_outs]
    out_shape += [jax.ShapeDtypeStruct(s, dt) for s, dt in acc_outs]
    return pl.pallas_call(
        kern, name=name, grid=(n_rows // tm,), in_specs=in_specs, out_specs=out_specs, out_shape=out_shape,
        compiler_params=_cparams(("arbitrary",)),
    )(*[a for a, _, _ in row_ins], *full_ins)


def _mm(name, pairs, out_dtype, tm, tn, carry=None):
    m_rows, n_cols = pairs[0][0].shape[0], pairs[0][1].shape[1]
    n_pairs = len(pairs)
    n_c = len(carry.arrays) if carry else 0
    grid = (n_cols // tn, m_rows // tm)

    def kern(*refs):
        o_ref = refs[2 * n_pairs + n_c]
        if carry:
            step = pl.program_id(0) * grid[1] + pl.program_id(1)
            _carry(carry, refs[2 * n_pairs:2 * n_pairs + n_c], refs[2 * n_pairs + n_c + 1:2 * n_pairs + 2 * n_c + 1],
                   refs[-1], step == 0, step == grid[0] * grid[1] - 1)
        acc = None
        for p in range(n_pairs):
            t = jnp.dot(refs[2 * p][...], refs[2 * p + 1][...], preferred_element_type=F32)
            acc = t if acc is None else acc + t
        o_ref[...] = acc.astype(o_ref.dtype)

    in_specs, flat = [], []
    for a, w in pairs:
        k = a.shape[1]
        in_specs += [pl.BlockSpec((tm, k), lambda j, i: (i, 0)), pl.BlockSpec((k, tn), lambda j, i: (0, j))]
        flat += [a, w]
    out_specs = [pl.BlockSpec((tm, tn), lambda j, i: (i, j))]
    out_shape = [jax.ShapeDtypeStruct((m_rows, n_cols), out_dtype)]
    if carry:
        res = pl.pallas_call(
            kern, name=name, grid=grid, in_specs=in_specs + carry.in_specs, out_specs=out_specs + carry.out_specs,
            out_shape=out_shape + carry.out_shape, scratch_shapes=carry.scratch,
            compiler_params=_cparams(("arbitrary", "arbitrary")),
        )(*flat, *carry.arrays)
        return res[0], res[1:]
    return pl.pallas_call(
        kern, name=name, grid=grid, in_specs=in_specs, out_specs=out_specs[0], out_shape=out_shape[0],
        compiler_params=_cparams(("parallel", "parallel")),
    )(*flat)


def _mm_tn(name, a, g, tm, tn):
    m_rows, k = a.shape
    n_cols = g.shape[1]

    def kern(a_ref, g_ref, o_ref):
        @pl.when(pl.program_id(1) == 0)
        def _():
            o_ref[...] = jnp.zeros(o_ref.shape, F32)
        o_ref[...] += lax.dot_general(a_ref[...], g_ref[...], (((0,), (0,)), ((), ())), preferred_element_type=F32)

    return pl.pallas_call(
        kern, name=name, grid=(n_cols // tn, m_rows // tm),
        in_specs=[pl.BlockSpec((tm, k), lambda j, m: (m, 0)), pl.BlockSpec((tm, tn), lambda j, m: (m, j))],
        out_specs=pl.BlockSpec((k, tn), lambda j, m: (0, j)),
        out_shape=jax.ShapeDtypeStruct((k, n_cols), F32),
        compiler_params=_cparams(("parallel", "arbitrary")),
    )(a, g)


def _mm_fused(name, pairs, tile_ins, full_ins, outs, accs, epilogue, tm, tn):
    m_rows, n_cols = pairs[0][0].shape[0], pairs[0][1].shape[1]
    n_p, n_t, n_f, n_o = len(pairs), len(tile_ins), len(full_ins), len(outs)

    def kern(*refs):
        dots = [jnp.dot(refs[2 * p][...], refs[2 * p + 1][...], preferred_element_type=F32) for p in range(n_p)]
        base = 2 * n_p
        t_refs, f_refs = refs[base:base + n_t], refs[base + n_t:base + n_t + n_f]
        o_refs, a_refs = refs[base + n_t + n_f:base + n_t + n_f + n_o], refs[base + n_t + n_f + n_o:]
        if a_refs:
            @pl.when((pl.program_id(0) == 0) & (pl.program_id(1) == 0))
            def _():
                for a in a_refs:
                    a[...] = jnp.zeros(a.shape, F32)
        epilogue(dots, t_refs, f_refs, o_refs, a_refs)

    in_specs, flat = [], []
    for a, w in pairs:
        k = a.shape[1]
        in_specs += [pl.BlockSpec((tm, k), lambda j, i: (i, 0)), pl.BlockSpec((k, tn), lambda j, i: (0, j))]
        flat += [a, w]
    for arr, cb in tile_ins:
        in_specs.append(pl.BlockSpec((tm, tn), functools.partial(lambda j, i, cb: (i, cb + j), cb=cb)))
        flat.append(arr)
    for arr in full_ins:
        in_specs.append(pl.BlockSpec(arr.shape, lambda j, i: (0, 0)))
        flat.append(arr)
    out_specs = [pl.BlockSpec((tm, f * tn), lambda j, i: (i, j)) for _, f in outs]
    out_specs += [pl.BlockSpec(s, lambda j, i: (0, 0)) for s in accs]
    out_shape = [jax.ShapeDtypeStruct((m_rows, f * n_cols), dt) for dt, f in outs]
    out_shape += [jax.ShapeDtypeStruct(s, F32) for s in accs]
    sem = ("arbitrary", "arbitrary") if accs else ("parallel", "parallel")
    return pl.pallas_call(
        kern, name=name, grid=(n_cols // tn, m_rows // tm), in_specs=in_specs, out_specs=out_specs,
        out_shape=out_shape, compiler_params=_cparams(sem),
    )(*flat)


def _peers():
    x, y, c = lax.axis_index("x"), lax.axis_index("y"), lax.axis_index("c")
    me = 4 * x + 2 * y + c
    peers = []
    for k in range(1, N_DEV):
        px = 1 - x if (k >> 2) & 1 else x
        py = 1 - y if (k >> 1) & 1 else y
        pc = 1 - c if k & 1 else c
        peers.append(((px, py, pc), 4 * px + 2 * py + pc))
    return me, peers


def _exchange(me, peers, items, send_sems, recv_sems, t0=0):
    started = []
    for t, (src_of, dst_of) in enumerate(items, start=t0):
        for k, (pid, plin) in enumerate(peers):
            cp = pltpu.make_async_remote_copy(
                src_ref=src_of(plin), dst_ref=dst_of(me), send_sem=send_sems.at[t, k], recv_sem=recv_sems.at[t, k],
                device_id=pid, device_id_type=MESH)
            cp.start()
            started.append(cp)
    for t, (src_of, dst_of) in enumerate(items, start=t0):
        for k, (pid, plin) in enumerate(peers):
            pltpu.make_async_remote_copy(
                src_ref=src_of(plin), dst_ref=dst_of(plin), send_sem=send_sems.at[t, k], recv_sem=recv_sems.at[t, k],
                device_id=pid, device_id_type=MESH).wait_recv()
    for cp in started:
        cp.wait_send()


def _hbm_spec():
    return pl.BlockSpec(memory_space=pltpu.HBM)


def _gather_shards(shard):
    def body(src, out, send_sems, recv_sems, own_sem):
        x, y, c = lax.axis_index("x"), lax.axis_index("y"), lax.axis_index("c")
        lin = lambda px, py, pc: 4 * px + 2 * py + pc
        sibling = (x, y, 1 - c)
        chips = [(1 - x, y), (x, 1 - y), (1 - x, 1 - y)]

        def copy(k, slot, to, from_src):
            return pltpu.make_async_remote_copy(
                src_ref=src if from_src else out.at[slot], dst_ref=out.at[slot], send_sem=send_sems.at[k],
                recv_sem=recv_sems.at[k], device_id=to, device_id_type=MESH)

        own = pltpu.make_async_copy(src, out.at[lin(x, y, c)], own_sem)
        own.start()
        first = [copy(0, lin(x, y, c), sibling, True)]
        first += [copy(1 + j, lin(x, y, c), (px, py, c), True) for j, (px, py) in enumerate(chips)]
        for cp in first:
            cp.start()
        passed = [copy(4 + j, lin(px, py, c), sibling, False) for j, (px, py) in enumerate(chips)]
        for j, (px, py) in enumerate(chips):
            copy(1 + j, lin(px, py, c), sibling, True).wait_recv()
            passed[j].start()
        copy(0, lin(x, y, 1 - c), sibling, True).wait_recv()
        for j, (px, py) in enumerate(chips):
            copy(4 + j, lin(px, py, 1 - c), sibling, False).wait_recv()
        for cp in first + passed:
            cp.wait_send()
        own.wait()

    return pl.pallas_call(
        body, name="gather_weights", in_specs=[_hbm_spec()], out_specs=_hbm_spec(),
        out_shape=jax.ShapeDtypeStruct((N_DEV,) + shard.shape, shard.dtype),
        scratch_shapes=[pltpu.SemaphoreType.DMA((N_DEV - 1,)), pltpu.SemaphoreType.DMA((N_DEV - 1,)),
                        pltpu.SemaphoreType.DMA(())],
    )(shard)


SEMS_PER_ITEM = 2 * (N_DEV - 1) + 1


class _Exchange:
    def __init__(self, items):
        self.arrays = [a for a, _ in items]
        self.scatter = [s for _, s in items]
        self.out_shape = [jax.ShapeDtypeStruct(a.shape if s else (N_DEV,) + a.shape, a.dtype) for a, s in items]
        self.in_specs = [_hbm_spec() for _ in items]
        self.out_specs = [_hbm_spec() for _ in items]
        self.scratch = [pltpu.SemaphoreType.DMA((SEMS_PER_ITEM * len(items),))]

    def _copies(self, ins, outs, sems, landing):
        me, peers = _peers()
        cps = []
        for t, scatter in enumerate(self.scatter):
            base = t * SEMS_PER_ITEM
            for k, (pid, plin) in enumerate(peers):
                src = ins[t].at[plin] if scatter else ins[t]
                dst = outs[t].at[plin if landing else me]
                cps.append(pltpu.make_async_remote_copy(
                    src_ref=src, dst_ref=dst, send_sem=sems.at[base + k], recv_sem=sems.at[base + N_DEV - 1 + k],
                    device_id=pid, device_id_type=MESH))
        return cps

    def _own(self, ins, outs, sems):
        me, _ = _peers()
        return [pltpu.make_async_copy(ins[t].at[me] if scatter else ins[t], outs[t].at[me],
                                      sems.at[t * SEMS_PER_ITEM + 2 * (N_DEV - 1)])
                for t, scatter in enumerate(self.scatter)]

    def start(self, ins, outs, sems):
        for cp in self._own(ins, outs, sems) + self._copies(ins, outs, sems, False):
            cp.start()

    def wait(self, ins, outs, sems):
        for cp in self._copies(ins, outs, sems, True):
            cp.wait_recv()
        for cp in self._copies(ins, outs, sems, False):
            cp.wait_send()
        for cp in self._own(ins, outs, sems):
            cp.wait()


def _carry(exchange, ins, outs, sems, first, last):
    @pl.when(first)
    def _():
        exchange.start(ins, outs, sems)

    @pl.when(last)
    def _():
        exchange.wait(ins, outs, sems)


def _exchange_alone(name, exchange):
    n = len(exchange.arrays)

    def body(*refs):
        exchange.start(refs[:n], refs[n:2 * n], refs[2 * n])
        exchange.wait(refs[:n], refs[n:2 * n], refs[2 * n])

    return pl.pallas_call(body, name=name, in_specs=exchange.in_specs, out_specs=exchange.out_specs,
                          out_shape=exchange.out_shape, scratch_shapes=exchange.scratch)(*exchange.arrays)


def _ada_modulation(c8, w_ada, b_ada_cols):
    wcols = w_ada.shape[1]

    def body(c_ref, w_ref, b_ref, call_ref, cond_ref, mod_ref, res, send_sems, recv_sems):
        me, peers = _peers()
        call_ref[me] = c_ref[...]
        _exchange(me, peers, [(lambda p: c_ref, lambda s: call_ref.at[s])], send_sems, recv_sems, t0=0)
        for dev in range(N_DEV):
            cv = call_ref[dev]
            cond = cv * _sigmoid(cv)
            cond_ref[dev] = cond
            res[dev] = jnp.dot(cond, w_ref[...], preferred_element_type=F32,
                               precision=lax.Precision.HIGHEST) + b_ref[...]
        mod_ref[me] = res[me]
        _exchange(me, peers, [(lambda p: res.at[p], lambda s: mod_ref.at[s])], send_sems, recv_sems, t0=1)

    vm = pl.BlockSpec(memory_space=pltpu.VMEM)
    return pl.pallas_call(
        body, name="ada_modulation", in_specs=[vm, vm, vm], out_specs=[vm, vm, vm],
        out_shape=[jax.ShapeDtypeStruct((N_DEV, 8, D), F32), jax.ShapeDtypeStruct((N_DEV, 8, D), F32),
                   jax.ShapeDtypeStruct((N_DEV, 8, wcols), F32)],
        scratch_shapes=[pltpu.VMEM((N_DEV, 8, wcols), F32),
                        pltpu.SemaphoreType.DMA((2, N_DEV - 1)), pltpu.SemaphoreType.DMA((2, N_DEV - 1))],
        compiler_params=pltpu.CompilerParams(vmem_limit_bytes=VMEM_LIMIT),
    )(c8, w_ada, b_ada_cols)


def _lru_gates(xc, wa_ref, ba_ref, wi_ref, bi_ref, sp_ref, row0):
    xcb = xc.astype(BF16)
    pre_r, pre_i = [], []
    for n in range(N_BLK):
        xb = xcb[:, n * BW:(n + 1) * BW]
        pre_r.append(jnp.dot(xb, wa_ref[n], preferred_element_type=F32))
        pre_i.append(jnp.dot(xb, wi_ref[n], preferred_element_type=F32))
    r = _sigmoid(jnp.concatenate(pre_r, axis=1) + ba_ref[...])
    ig = _sigmoid(jnp.concatenate(pre_i, axis=1) + bi_ref[...])
    log_a = (-LRU_C) * r * sp_ref[...]
    a = jnp.exp(log_a)
    e2 = jnp.exp(2.0 * log_a)
    mult_raw = jnp.sqrt(1.0 - e2)
    rows = row0 + lax.broadcasted_iota(jnp.int32, xc.shape, 0)
    start = rows == 0
    mult = jnp.where(start, 1.0, mult_raw)
    return xcb, r, ig, a, e2, mult_raw, mult, start


def _conv(xpad, cw_ref, cb_ref, tc):
    out = cb_ref[...]
    for tap in range(CONV_W):
        out = out + cw_ref[tap:tap + 1, :] * xpad[pl.ds(8 - (CONV_W - 1) + tap, tc), :]
    return out


def _rnn_fwd(proj_rnn, cw, cb, wa, ba, wi, bi, sp, carry):
    s_len = proj_rnn.shape[0]
    tc = RNN_CHUNK
    n_c = len(carry.arrays)

    def kern(*refs):
        y_ref, xr_ref, cw_ref, cb_ref, wa_ref, ba_ref, wi_ref, bi_ref, sp_ref = refs[:9]
        hs_ref, ur_ref = refs[9 + n_c:11 + n_c]
        xpad, a_scr, u_scr, h_scr = refs[11 + 2 * n_c:15 + 2 * n_c]
        i = pl.program_id(0)
        _carry(carry, refs[9:9 + n_c], refs[11 + n_c:11 + 2 * n_c], refs[-1], i == 0, i == s_len // tc - 1)

        @pl.when(i == 0)
        def _():
            xpad[0:8, :] = jnp.zeros((8, DR), F32)
            h_scr[...] = jnp.zeros((1, DR), F32)

        xpad[8:8 + tc, :] = xr_ref[...]
        xc = _conv(xpad, cw_ref, cb_ref, tc)
        xpad[0:8, :] = xpad[tc:tc + 8, :]
        _, r, ig, a, e2, mult_raw, mult, start = _lru_gates(xc, wa_ref, ba_ref, wi_ref, bi_ref, sp_ref, i * tc)
        a_scr[...] = a
        u_scr[...] = mult * (ig * xc)

        def step(t, h):
            h = a_scr[pl.ds(t, 1), :] * h + u_scr[pl.ds(t, 1), :]
            hs_ref[pl.ds(t, 1), :] = h
            return h

        h_scr[...] = lax.fori_loop(0, tc, step, h_scr[...], unroll=8)
        gy, _ = _gelu(y_ref[...])
        ur_ref[...] = (gy * hs_ref[...]).astype(BF16)

    full = lambda a: pl.BlockSpec(a.shape, lambda i: (0,) * a.ndim)
    res = pl.pallas_call(
        kern, name="rnn_fwd", grid=(s_len // tc,),
        in_specs=[pl.BlockSpec((tc, DR), lambda i: (i, 0)), pl.BlockSpec((tc, DR), lambda i: (i, 1)),
                  full(cw), full(cb), full(wa), full(ba), full(wi), full(bi), full(sp)] + carry.in_specs,
        out_specs=[pl.BlockSpec((tc, DR), lambda i: (i, 0)), pl.BlockSpec((tc, DR), lambda i: (i, 0))] + carry.out_specs,
        out_shape=[jax.ShapeDtypeStruct((s_len, DR), F32), jax.ShapeDtypeStruct((s_len, DR), BF16)] + carry.out_shape,
        scratch_shapes=[pltpu.VMEM((tc + 8, DR), F32), pltpu.VMEM((tc, DR), F32), pltpu.VMEM((tc, DR), F32),
                        pltpu.VMEM((1, DR), F32)] + carry.scratch,
        compiler_params=_cparams(("arbitrary",)),
    )(proj_rnn, proj_rnn, cw, cb, wa, ba, wi, bi, sp, *carry.arrays)
    return res[0], res[1], res[2:]


def _rnn_bwd(proj_rnn, hs, du, cw, cb, wa, wat, ba, wi, wit, bi, sp, dsp_dlam):
    s_len = proj_rnn.shape[0]
    tc = RNN_CHUNK
    nch = s_len // tc

    def kern(y_ref, xr_ref, xrp_ref, hs_ref, hsp_ref, du_ref, cw_ref, cb_ref, wa_ref, wat_ref, ba_ref, wi_ref,
             wit_ref, bi_ref, sp_ref, dspl_ref, drnn_ref, dwa_ref, dwi_ref, vec_ref,
             xpad, hpad, a_scr, d_scr, g_scr, dxcpad, ag_scr):
        j = pl.program_id(0)
        n = nch - 1 - j

        @pl.when(j == 0)
        def _():
            dwa_ref[...] = jnp.zeros(dwa_ref.shape, F32)
            dwi_ref[...] = jnp.zeros(dwi_ref.shape, F32)
            vec_ref[...] = jnp.zeros(vec_ref.shape, F32)
            ag_scr[...] = jnp.zeros((1, DR), F32)
            dxcpad[tc:tc + 8, :] = jnp.zeros((8, DR), F32)

        has_prev = n > 0
        xpad[0:8, :] = jnp.where(has_prev, xrp_ref[...], 0.0)
        xpad[8:8 + tc, :] = xr_ref[...]
        hpad[0:8, :] = jnp.where(has_prev, hsp_ref[...], 0.0)
        hpad[8:8 + tc, :] = hs_ref[...]
        xc = _conv(xpad, cw_ref, cb_ref, tc)
        xcb, r, ig, a, e2, mult_raw, mult, start = _lru_gates(xc, wa_ref, ba_ref, wi_ref, bi_ref, sp_ref, n * tc)

        y = y_ref[...]
        gy, th = _gelu(y)
        duv = du_ref[...].astype(F32)
        drnn_ref[:, 0:DR] = (duv * hs_ref[...] * _gelu_grad(y, th)).astype(BF16)
        a_scr[...] = a
        d_scr[...] = duv * gy

        def step(tt, ag):
            t = tc - 1 - tt
            g = d_scr[pl.ds(t, 1), :] + ag
            g_scr[pl.ds(t, 1), :] = g
            return a_scr[pl.ds(t, 1), :] * g

        ag_scr[...] = lax.fori_loop(0, tc, step, ag_scr[...], unroll=8)
        g = g_scr[...]
        da = g * hpad[pl.ds(7, tc), :]
        gx = g * xc
        dmult = jnp.where(start, 0.0, gx * ig)
        di = gx * mult
        dxc = g * mult * ig
        dlog_a = da * a - jnp.where(start, 0.0, dmult * e2 / mult_raw)
        dr = dlog_a * ((-LRU_C) * sp_ref[...])
        vec_ref[7:8, :] += _rowsum(dlog_a * ((-LRU_C) * r))
        dpr = dr * r * (1.0 - r)
        dpi = di * ig * (1.0 - ig)
        vec_ref[5:6, :] += _rowsum(dpr)
        vec_ref[6:7, :] += _rowsum(dpi)
        dprb, dpib = dpr.astype(BF16), dpi.astype(BF16)
        extra = []
        for b in range(N_BLK):
            sl = slice(b * BW, (b + 1) * BW)
            extra.append(jnp.dot(dprb[:, sl], wat_ref[b], preferred_element_type=F32)
                         + jnp.dot(dpib[:, sl], wit_ref[b], preferred_element_type=F32))
            dn = (((0,), (0,)), ((), ()))
            dwa_ref[b] += lax.dot_general(xcb[:, sl], dprb[:, sl], dn, preferred_element_type=F32)
            dwi_ref[b] += lax.dot_general(xcb[:, sl], dpib[:, sl], dn, preferred_element_type=F32)
        dxc = dxc + jnp.concatenate(extra, axis=1)
        vec_ref[4:5, :] += _rowsum(dxc)
        dxcpad[0:tc, :] = dxc
        dxr = jnp.zeros((tc, DR), F32)
        for tap in range(CONV_W):
            shift = CONV_W - 1 - tap
            dxr = dxr + cw_ref[tap:tap + 1, :] * dxcpad[pl.ds(shift, tc), :]
            vec_ref[tap:tap + 1, :] += _rowsum(dxc * xpad[pl.ds(8 - shift, tc), :])
        dxcpad[tc:tc + 8, :] = dxcpad[0:8, :]
        drnn_ref[:, DR:2 * DR] = dxr.astype(BF16)

        @pl.when(j == nch - 1)
        def _():
            vec_ref[7:8, :] = vec_ref[7:8, :] * dspl_ref[...]

    full = lambda a: pl.BlockSpec(a.shape, lambda j: (0,) * a.ndim)
    rev = lambda j: nch - 1 - j
    prev8 = lambda j: jnp.maximum((nch - 1 - j) * (tc // 8) - 1, 0)
    return pl.pallas_call(
        kern, name="rnn_bwd", grid=(nch,),
        in_specs=[pl.BlockSpec((tc, DR), lambda j: (rev(j), 0)), pl.BlockSpec((tc, DR), lambda j: (rev(j), 1)),
                  pl.BlockSpec((8, DR), lambda j: (prev8(j), 1)),
                  pl.BlockSpec((tc, DR), lambda j: (rev(j), 0)), pl.BlockSpec((8, DR), lambda j: (prev8(j), 0)),
                  pl.BlockSpec((tc, DR), lambda j: (rev(j), 0)),
                  full(cw), full(cb), full(wa), full(wat), full(ba), full(wi), full(wit), full(bi), full(sp),
                  full(dsp_dlam)],
        out_specs=[pl.BlockSpec((tc, 2 * DR), lambda j: (rev(j), 0)),
                   pl.BlockSpec((N_BLK, BW, BW), lambda j: (0, 0, 0)), pl.BlockSpec((N_BLK, BW, BW), lambda j: (0, 0, 0)),
                   pl.BlockSpec((8, DR), lambda j: (0, 0))],
        out_shape=[jax.ShapeDtypeStruct((s_len, 2 * DR), BF16), jax.ShapeDtypeStruct((N_BLK, BW, BW), F32),
                   jax.ShapeDtypeStruct((N_BLK, BW, BW), F32), jax.ShapeDtypeStruct((8, DR), F32)],
        scratch_shapes=[pltpu.VMEM((tc + 8, DR), F32), pltpu.VMEM((tc + 8, DR), F32), pltpu.VMEM((tc, DR), F32),
                        pltpu.VMEM((tc, DR), F32), pltpu.VMEM((tc, DR), F32), pltpu.VMEM((tc + 8, DR), F32),
                        pltpu.VMEM((1, DR), F32)],
        compiler_params=_cparams(("arbitrary",)),
    )(proj_rnn, proj_rnn, proj_rnn, hs, hs, du, cw, cb, wa, wat, ba, wi, wit, bi, sp, dsp_dlam)


def _attn_masks(n):
    qi = lax.broadcasted_iota(jnp.int32, (QB, 2 * QB), 0)
    ki = lax.broadcasted_iota(jnp.int32, (QB, 2 * QB), 1)
    dist = QB + qi - ki
    valid = (dist >= 0) & (dist <= QB) & ((n > 0) | (ki >= QB))
    lane = lax.broadcasted_iota(jnp.int32, (1, HEAD_PAIR), 1)
    return valid, lane


def _qkv_specs(d, n_of):
    prev = lambda r, j: jnp.maximum(n_of(j) - 1, 0)
    blk = (QB, D)
    return [pl.BlockSpec(blk, lambda r, j: (n_of(j), 3 * r)),
            pl.BlockSpec(blk, lambda r, j: (prev(r, j), 3 * r + 1)), pl.BlockSpec(blk, lambda r, j: (n_of(j), 3 * r + 1)),
            pl.BlockSpec(blk, lambda r, j: (prev(r, j), 3 * r + 2)), pl.BlockSpec(blk, lambda r, j: (n_of(j), 3 * r + 2))]


def _attn_fwd(qkv, d):
    s_len = qkv.shape[0]
    n_l = s_len // d
    nb = n_l // QB
    qv = qkv.reshape(n_l, d * 3 * D)

    def kern(q_ref, kp_ref, kc_ref, vp_ref, vc_ref, o_ref, lse_ref):
        valid, lane = _attn_masks(pl.program_id(1))
        first = lane < 64
        lse_blk = jnp.zeros((QB, HEAD_PAIR), F32)
        for hp in range(N_PAIR):
            sl = slice(hp * HEAD_PAIR, (hp + 1) * HEAD_PAIR)
            q2 = q_ref[:, sl]
            k2 = jnp.concatenate([kp_ref[:, sl], kc_ref[:, sl]], axis=0)
            v2 = jnp.concatenate([vp_ref[:, sl], vc_ref[:, sl]], axis=0)
            outs = []
            for half in range(2):
                hm = first if half == 0 else jnp.logical_not(first)
                qh = jnp.where(hm, q2, jnp.zeros_like(q2))
                s = lax.dot_general(qh, k2, (((1,), (1,)), ((), ())), preferred_element_type=F32) * SCALE
                s = jnp.where(valid, s, NEG)
                mx = jnp.max(s, axis=-1, keepdims=True)
                p = jnp.exp(s - mx)
                den = jnp.sum(p, axis=-1, keepdims=True)
                outs.append(jnp.dot(p.astype(BF16), v2, preferred_element_type=F32) / den)
                lse_blk = jnp.where(lane == 2 * hp + half, mx + jnp.log(den), lse_blk)
            o_ref[:, sl] = jnp.where(first, outs[0], outs[1]).astype(BF16)
        lse_ref[...] = lse_blk

    o, lse = pl.pallas_call(
        kern, name=f"attn_fwd_d{d}", grid=(d, nb), in_specs=_qkv_specs(d, lambda j: j),
        out_specs=[pl.BlockSpec((QB, D), lambda r, j: (j, r)), pl.BlockSpec((QB, HEAD_PAIR), lambda r, j: (j, r))],
        out_shape=[jax.ShapeDtypeStruct((n_l, d * D), BF16), jax.ShapeDtypeStruct((n_l, d * HEAD_PAIR), F32)],
        compiler_params=_cparams(("arbitrary", "arbitrary")),
    )(qv, qv, qv, qv, qv)
    return o.reshape(s_len, D), lse.reshape(s_len, HEAD_PAIR)


def _attn_bwd(qkv, do, o, lse, d, carry=None):
    s_len = qkv.shape[0]
    n_l = s_len // d
    nb = n_l // QB
    qv = qkv.reshape(n_l, d * 3 * D)
    view = lambda t: t.reshape(n_l, d * D)
    n_c = len(carry.arrays) if carry else 0

    def kern(*refs):
        q_ref, kp_ref, kc_ref, vp_ref, vc_ref, do_ref, o_ref, lse_ref = refs[:8]
        dq_ref, dk_ref, dv_ref = refs[8 + n_c:11 + n_c]
        dk_scr, dv_scr = refs[11 + 2 * n_c:13 + 2 * n_c]
        j = pl.program_id(1)
        if carry:
            r = pl.program_id(0)
            _carry(carry, refs[8:8 + n_c], refs[11 + n_c:11 + 2 * n_c], refs[-1], (r == 0) & (j == 0),
                   (r == d - 1) & (j == nb - 1))

        @pl.when(j == 0)
        def _():
            dk_scr[...] = jnp.zeros((QB, D), F32)
            dv_scr[...] = jnp.zeros((QB, D), F32)

        valid, lane = _attn_masks(nb - 1 - j)
        first = lane < 64
        lse_blk = lse_ref[...]
        nt = (((1,), (1,)), ((), ()))
        tn = (((0,), (0,)), ((), ()))
        for hp in range(N_PAIR):
            sl = slice(hp * HEAD_PAIR, (hp + 1) * HEAD_PAIR)
            q2 = q_ref[:, sl]
            k2 = jnp.concatenate([kp_ref[:, sl], kc_ref[:, sl]], axis=0)
            v2 = jnp.concatenate([vp_ref[:, sl], vc_ref[:, sl]], axis=0)
            do2 = do_ref[:, sl]
            prod = do2.astype(F32) * o_ref[:, sl].astype(F32)
            dqs = []
            dk2 = jnp.zeros((2 * QB, HEAD_PAIR), F32)
            dv2 = jnp.zeros((2 * QB, HEAD_PAIR), F32)
            for half in range(2):
                hm = first if half == 0 else jnp.logical_not(first)
                qh = jnp.where(hm, q2, jnp.zeros_like(q2))
                doh = jnp.where(hm, do2, jnp.zeros_like(do2))
                dsum = jnp.sum(jnp.where(hm, prod, 0.0), axis=-1, keepdims=True)
                lseh = jnp.sum(jnp.where(lane == 2 * hp + half, lse_blk, 0.0), axis=-1, keepdims=True)
                s = lax.dot_general(qh, k2, nt, preferred_element_type=F32) * SCALE
                p = jnp.exp(jnp.where(valid, s, NEG) - lseh)
                dp = lax.dot_general(doh, v2, nt, preferred_element_type=F32)
                ds = (p * (dp - dsum) * SCALE).astype(BF16)
                dqs.append(jnp.dot(ds, k2, preferred_element_type=F32))
                dk2 = dk2 + lax.dot_general(ds, qh, tn, preferred_element_type=F32)
                dv2 = dv2 + lax.dot_general(p.astype(BF16), doh, tn, preferred_element_type=F32)
            dq_ref[:, sl] = jnp.where(first, dqs[0], dqs[1]).astype(BF16)
            dk_ref[:, sl] = (dk2[QB:, :] + dk_scr[:, sl]).astype(BF16)
            dv_ref[:, sl] = (dv2[QB:, :] + dv_scr[:, sl]).astype(BF16)
            dk_scr[:, sl] = dk2[:QB, :]
            dv_scr[:, sl] = dv2[:QB, :]

    rev = lambda j: nb - 1 - j
    row = pl.BlockSpec((QB, D), lambda r, j: (rev(j), r))
    res = pl.pallas_call(
        kern, name=f"attn_bwd_d{d}", grid=(d, nb),
        in_specs=_qkv_specs(d, rev) + [row, row, pl.BlockSpec((QB, HEAD_PAIR), lambda r, j: (rev(j), r))]
        + (carry.in_specs if carry else []),
        out_specs=[row, row, row] + (carry.out_specs if carry else []),
        out_shape=[jax.ShapeDtypeStruct((n_l, d * D), BF16)] * 3 + (carry.out_shape if carry else []),
        scratch_shapes=[pltpu.VMEM((QB, D), F32), pltpu.VMEM((QB, D), F32)] + (carry.scratch if carry else []),
        compiler_params=_cparams(("arbitrary", "arbitrary")),
    )(qv, qv, qv, qv, qv, view(do), view(o), lse.reshape(n_l, d * HEAD_PAIR), *(carry.arrays if carry else []))
    dq, dk, dv = res[:3]
    return (dq.reshape(s_len, D), dk.reshape(s_len, D), dv.reshape(s_len, D)), res[3:]


def _adamw(w, g, m, v):
    m = ADAM_B1 * m + (1.0 - ADAM_B1) * g
    v = ADAM_B2 * v + (1.0 - ADAM_B2) * (g * g)
    m_hat = m / (1.0 - ADAM_B1 ** ADAM_STEP)
    v_hat = v / (1.0 - ADAM_B2 ** ADAM_STEP)
    delta = -ADAM_LR * (m_hat / (jnp.sqrt(v_hat) + ADAM_EPS) + ADAM_WD * w)
    return delta, m, v


def _adam_packed(name, recv, w, m, v, tm):
    n_rows, width = w.shape
    summed = recv.ndim == 3

    def kern(r_ref, w_ref, m_ref, v_ref, g_out, d_out, m_out, v_out):
        if summed:
            g = r_ref[0].astype(F32)
            for s in range(1, N_DEV):
                g = g + r_ref[s].astype(F32)
        else:
            g = r_ref[...]
        delta, mn, vn = _adamw(w_ref[...], g, m_ref[...], v_ref[...])
        g_out[...] = g
        d_out[...] = delta
        m_out[...] = mn
        v_out[...] = vn

    tile = pl.BlockSpec((tm, width), lambda i: (i, 0))
    rspec = pl.BlockSpec((N_DEV, tm, width), lambda i: (0, i, 0)) if summed else tile
    return pl.pallas_call(
        kern, name=name, grid=(n_rows // tm,), in_specs=[rspec, tile, tile, tile], out_specs=[tile] * 4,
        out_shape=[jax.ShapeDtypeStruct((n_rows, width), F32)] * 4, compiler_params=_cparams(("parallel",)),
    )(recv, w, m, v)


def _sum_slots(recv):
    _, n_rows, width = recv.shape

    def kern(r_ref, o_ref):
        g = r_ref[0]
        for s in range(1, N_DEV):
            g = g + r_ref[s]
        o_ref[...] = g

    return pl.pallas_call(
        kern, name="sum_small", grid=(1,), in_specs=[pl.BlockSpec(recv.shape, lambda i: (0, 0, 0))],
        out_specs=pl.BlockSpec((n_rows, width), lambda i: (0, 0)),
        out_shape=jax.ShapeDtypeStruct((n_rows, width), F32), compiler_params=_cparams(("arbitrary",)),
    )(recv)


def _adam_w_ada(cond_t, dmod_cols, w, m, v):
    n_rows, width = w.shape
    tm = ROW_TILE

    def kern(c_ref, d_ref, w_ref, m_ref, v_ref, g_out, d_out, m_out, v_out):
        g = c_ref[:, 0:1] * d_ref[0:1, :]
        for b in range(1, N_DEV):
            g = g + c_ref[:, b:b + 1] * d_ref[b:b + 1, :]
        delta, mn, vn = _adamw(w_ref[...], g, m_ref[...], v_ref[...])
        g_out[...] = g
        d_out[...] = delta
        m_out[...] = mn
        v_out[...] = vn

    tile = pl.BlockSpec((tm, width), lambda i: (i, 0))
    return pl.pallas_call(
        kern, name="adam_w_ada", grid=(n_rows // tm,),
        in_specs=[pl.BlockSpec((tm, N_DEV), lambda i: (i, 0)), pl.BlockSpec((N_DEV, width), lambda i: (0, 0)),
                  tile, tile, tile],
        out_specs=[tile] * 4, out_shape=[jax.ShapeDtypeStruct((n_rows, width), F32)] * 4,
        compiler_params=_cparams(("parallel",)),
    )(cond_t, dmod_cols, w, m, v)


def _rows(a, n_rows=None):
    flat = a.reshape(-1)
    need = (n_rows if n_rows is not None else -(-flat.shape[0] // D)) * D
    if need != flat.shape[0]:
        flat = jnp.concatenate([flat, jnp.zeros((need - flat.shape[0],), flat.dtype)])
    return flat.reshape(-1, D)


def _pack_shards(ws):
    return jnp.concatenate([ws[name][0].reshape(r, D) for name, r in PACK_ROWS], axis=0)


def _cols_to_slots(g, cols):
    k = g.shape[0]
    return g.reshape(k, N_DEV, cols).transpose(1, 0, 2).reshape(N_DEV, k * cols // D, D)


def _slots_to_cols(s, k, cols):
    return s.reshape(N_DEV, k, cols).transpose(1, 0, 2).reshape(k, N_DEV * cols)


def kernel(x, c, w_ada, b_ada, w_in, conv_w, conv_b, lru_wa, lru_ba, lru_wi, lru_bi, lru_lambda, w_proj_rnn, w_proj_attn, b_gate, w_out, ln1_g, ln1_b, w_ffn_gate, w_ffn_up, w_ffn_down, ln2_g, ln2_b, loss_target, m_w_ada, m_b_ada, m_w_in, m_conv_w, m_conv_b, m_lru_wa, m_lru_ba, m_lru_wi, m_lru_bi, m_lru_lambda, m_w_proj_rnn, m_w_proj_attn, m_b_gate, m_w_out, m_ln1_g, m_ln1_b, m_w_ffn_gate, m_w_ffn_up, m_w_ffn_down, m_ln2_g, m_ln2_b, v_w_ada, v_b_ada, v_w_in, v_conv_w, v_conv_b, v_lru_wa, v_lru_ba, v_lru_wi, v_lru_bi, v_lru_lambda, v_w_proj_rnn, v_w_proj_attn, v_b_gate, v_w_out, v_ln1_g, v_ln1_b, v_w_ffn_gate, v_w_ffn_up, v_w_ffn_down, v_ln2_g, v_ln2_b):
    s_len = x.shape[1]
    me = 4 * lax.axis_index("x") + 2 * lax.axis_index("y") + lax.axis_index("c")
    xs = x[0]
    tgt = loss_target[0]
    tm = ROW_TILE

    b_ada_cols = lax.dynamic_slice(b_ada, (0, me * 768), (1, 768))
    pad_cols = lambda a: jnp.concatenate([a, jnp.zeros((a.shape[0], D - a.shape[1]), F32)], axis=1)
    c8 = jnp.concatenate([c, pad_cols(conv_w[0]), pad_cols(b_gate[0]), jnp.zeros((1, D), F32)], axis=0)
    c_all, cond_blocks, mod_parts = _ada_modulation(c8, w_ada[0], b_ada_cols)
    cond_all = cond_blocks[:, 0, :]
    mod = mod_parts[:, 0, :].reshape(6, D)
    mod8 = jnp.concatenate([mod, jnp.zeros((2, D), F32)], axis=0)
    cw = c_all[:, 1:1 + CONV_W, :DR // N_DEV].transpose(1, 0, 2).reshape(CONV_W, DR)
    bg = c_all[:, 1 + CONV_W:3 + CONV_W, :D // N_DEV].transpose(1, 0, 2).reshape(2, D)
    bg8 = jnp.concatenate([bg, jnp.zeros((6, D), F32)], axis=0)

    w_in_f = _slots_to_cols(_gather_shards(w_in[0].reshape(960, D).astype(BF16)), D, 960)
    w_rnn, w_qkv, w_gates = w_in_f[:, :2 * DR], w_in_f[:, 2 * DR:2 * DR + 3 * D], w_in_f[:, 2 * DR + 3 * D:]
    late = dict(w_ffn_gate=w_ffn_gate, w_ffn_up=w_ffn_up, w_proj_rnn=w_proj_rnn, w_proj_attn=w_proj_attn,
                w_out=w_out, w_ffn_down=w_ffn_down)
    late_shard = jnp.concatenate([late[name][0].reshape(r, D) for name, r in LATE_ROWS], axis=0).astype(BF16)

    cb = conv_b
    wa_b, wi_b = lru_wa[0].astype(BF16), lru_wi[0].astype(BF16)
    wat_b, wit_b = jnp.swapaxes(wa_b, 1, 2), jnp.swapaxes(wi_b, 1, 2)
    ba, bi = lru_ba.reshape(1, DR), lru_bi.reshape(1, DR)
    sp = jax.nn.softplus(-lru_lambda)
    dsp_dlam = -jax.nn.sigmoid(-lru_lambda)
    ln1 = jnp.concatenate([ln1_g, ln1_b, jnp.zeros((6, D), F32)], axis=0)
    ln2 = jnp.concatenate([ln2_g, ln2_b, jnp.zeros((6, D), F32)], axis=0)

    def f1(ri, fi, ro, ao):
        n, _ = _ln(ri[0][...])
        ro[0][...] = (n * (1.0 + fi[0][1:2, :]) + fi[0][0:1, :]).astype(BF16)

    (h1,) = _rowwise(f1, "ln_mod1", s_len, tm, [(xs, D, 0)], [mod8], [(D, BF16)], [])
    proj_rnn = _mm("mm_in_rnn", [(h1, w_rnn)], F32, 512, DR)
    qkv = _mm("mm_in_qkv", [(h1, w_qkv)], BF16, 512, 1536)
    gates = _mm("mm_in_gates", [(h1, w_gates)], F32, 512, D)

    hs, u_rnn, (late_all,) = _rnn_fwd(proj_rnn, cw, cb, wa_b, ba, wi_b, bi, sp, _Exchange([(late_shard, False)]))
    offs, o = {}, 0
    for name, r in LATE_ROWS:
        offs[name] = (o, o + r)
        o += r
    part = lambda name: late_all[:, offs[name][0]:offs[name][1], :]
    w_g, w_u = _slots_to_cols(part("w_ffn_gate"), D, 352), _slots_to_cols(part("w_ffn_up"), D, 352)
    w_pr = part("w_proj_rnn").reshape(DR, D)
    w_pa = part("w_proj_attn").reshape(D, D)
    w_o = part("w_out").reshape(D, D)
    w_dn = part("w_ffn_down").reshape(DFF, D)

    group = [_attn_fwd(qkv, d) for d in DILATIONS]

    expand = (lax.broadcasted_iota(jnp.int32, (HEAD_PAIR, D), 1) // 64
              == lax.broadcasted_iota(jnp.int32, (HEAD_PAIR, D), 0)).astype(BF16)

    def f_mix(ri, fi, ro, ao):
        ls = [ri[3 + g][...] for g in range(3)]
        mx = jnp.maximum(jnp.maximum(ls[0], ls[1]), ls[2])
        es = [jnp.exp(l - mx) for l in ls]
        tot = es[0] + es[1] + es[2]
        acc = None
        for g in range(3):
            wgt = es[g] / tot
            hi = wgt.astype(BF16)
            lo = (wgt - hi.astype(F32)).astype(BF16)
            wide = (jnp.dot(hi, fi[0][...], preferred_element_type=F32)
                    + jnp.dot(lo, fi[0][...], preferred_element_type=F32))
            t = wide * ri[g][...].astype(F32)
            acc = t if acc is None else acc + t
        ro[0][...] = acc.astype(BF16)
        ro[1][...] = mx + jnp.log(tot)

    u_attn, lse = _rowwise(f_mix, "attn_mix", s_len, tm,
                           [(o_g, D, 0) for o_g, _ in group] + [(l_g, HEAD_PAIR, 0) for _, l_g in group],
                           [expand], [(D, BF16), (HEAD_PAIR, F32)], [])

    def e_merge(dots, ti, fi, ro, ao):
        g_r = _sigmoid(ti[0][...] + fi[0][0:1, :])
        g_a = _sigmoid(ti[1][...] + fi[0][1:2, :])
        ro[0][...] = dots[0].astype(BF16)
        ro[1][...] = dots[1].astype(BF16)
        ro[2][...] = (g_r * dots[0] + g_a * dots[1]).astype(BF16)

    pr, pa, merged = _mm_fused("mm_proj_merge", [(u_rnn, w_pr), (u_attn, w_pa)], [(gates, 0), (gates, 1)], [bg8],
                               [(BF16, 1), (BF16, 1), (BF16, 1)], [], e_merge, 512, D)
    f = _mm("mm_out", [(merged, w_o)], F32, 512, D)

    def f3(ri, fi, ro, ao):
        md, l1 = fi[0], fi[1]
        n1, _ = _ln(ALPHA * ri[0][...] + md[2:3, :] * ri[1][...])
        x1 = n1 * l1[0:1, :] + l1[1:2, :]
        n0, _ = _ln(x1)
        ro[0][...] = x1
        ro[1][...] = (n0 * (1.0 + md[4:5, :]) + md[3:4, :]).astype(BF16)

    x1, h2 = _rowwise(f3, "post1", s_len, tm, [(xs, D, 0), (f, D, 0)], [mod8, ln1], [(D, F32), (D, BF16)], [])
    def e_swiglu(dots, ti, fi, ro, ao):
        gp, up = dots
        ro[0][...] = gp.astype(BF16)
        ro[1][...] = up.astype(BF16)
        ro[2][...] = (gp * _sigmoid(gp) * up).astype(BF16)

    gpre, upre, act = _mm_fused("mm_ffn_in", [(h2, w_g), (h2, w_u)], [], [], [(BF16, 1)] * 3, [], e_swiglu, 512, DFF // 2)
    f2o = _mm("mm_ffn_out", [(act, w_dn)], F32, 512, D)

    def f5(ri, fi, ro, ao):
        md, l2 = fi[0], fi[1]
        f2v = ri[1][...]
        n2, rstd = _ln(ALPHA * ri[0][...] + md[5:6, :] * f2v)
        err = n2 * l2[0:1, :] + l2[1:2, :] - ri[2][...]
        dx2 = err * (1.0 / D)
        dz2 = _ln_bwd(dx2 * l2[0:1, :], n2, rstd)
        ro[0][...] = dz2
        ro[1][...] = (md[5:6, :] * dz2).astype(BF16)
        acc = ao[0]
        acc[0:1, :] += _rowsum(err * err) * (0.5 / D)
        acc[1:2, :] += _rowsum(dx2 * n2)
        acc[2:3, :] += _rowsum(dx2)
        acc[3:4, :] += _rowsum(dz2 * f2v)

    dz2, df2, acc5 = _rowwise(f5, "loss_post2", s_len, tm, [(x1, D, 0), (f2o, D, 0), (tgt, D, 0)], [mod8, ln2],
                              [(D, F32), (D, BF16)], [((8, D), F32)])

    d_w_dn = _mm_tn("mmt_ffn_down", act, df2, 1024, 512)

    def e_dswiglu(dots, ti, fi, ro, ao):
        da = dots[0]
        gp, up = ti[0][...].astype(F32), ti[1][...].astype(F32)
        sg = _sigmoid(gp)
        ro[0][...] = (da * up * sg * (1.0 + gp * (1.0 - sg))).astype(BF16)
        ro[1][...] = (da * gp * sg).astype(BF16)

    dgp, dup = _mm_fused("mm_d_ffn_out", [(df2, w_dn.T)], [(gpre, 0), (upre, 0)], [], [(BF16, 1)] * 2, [], e_dswiglu,
                         512, DFF // 2)
    dh2 = _mm("mm_d_ffn_in", [(dgp, w_g.T), (dup, w_u.T)], F32, 256, D)
    d_w_g = _mm_tn("mmt_ffn_gate", h2, dgp, 1024, DFF // 2)
    d_w_u = _mm_tn("mmt_ffn_up", h2, dup, 1024, DFF // 2)

    def b2(ri, fi, ro, ao):
        md, l1 = fi[0], fi[1]
        fv, dh2v = ri[1][...], ri[2][...]
        n1, rstd1 = _ln(ALPHA * ri[0][...] + md[2:3, :] * fv)
        n0, rstd0 = _ln(n1 * l1[0:1, :] + l1[1:2, :])
        dx1 = ALPHA * ri[3][...] + _ln_bwd(dh2v * (1.0 + md[4:5, :]), n0, rstd0)
        dz1 = _ln_bwd(dx1 * l1[0:1, :], n1, rstd1)
        ro[0][...] = ALPHA * dz1
        ro[1][...] = (md[2:3, :] * dz1).astype(BF16)
        acc = ao[0]
        acc[0:1, :] += _rowsum(dh2v * n0)
        acc[1:2, :] += _rowsum(dh2v)
        acc[2:3, :] += _rowsum(dx1 * n1)
        acc[3:4, :] += _rowsum(dx1)
        acc[4:5, :] += _rowsum(dz1 * fv)

    dxp, df, acc2 = _rowwise(b2, "d_post1", s_len, tm, [(xs, D, 0), (f, D, 0), (dh2, D, 0), (dz2, D, 0)], [mod8, ln1],
                             [(D, F32), (D, BF16)], [((8, D), F32)])
    d_w_o = _mm_tn("mmt_out", merged, df, 1024, 512)

    def e_dmerge(dots, ti, fi, ro, ao):
        dm = dots[0]
        g_r = _sigmoid(ti[0][...] + fi[0][0:1, :])
        g_a = _sigmoid(ti[1][...] + fi[0][1:2, :])
        ro[0][...] = (dm * g_r).astype(BF16)
        ro[1][...] = (dm * g_a).astype(BF16)
        dga = dm * ti[2][...].astype(F32) * g_r * (1.0 - g_r)
        dgb = dm * ti[3][...].astype(F32) * g_a * (1.0 - g_a)
        ro[2][:, 0:D] = dga.astype(BF16)
        ro[2][:, D:2 * D] = dgb.astype(BF16)
        ao[0][0:1, :] += _rowsum(dga)
        ao[0][1:2, :] += _rowsum(dgb)

    dpr, dpa, dgates, acc3 = _mm_fused("mm_d_out", [(df, w_o.T)], [(gates, 0), (gates, 1), (pr, 0), (pa, 0)], [bg8],
                                       [(BF16, 1), (BF16, 1), (BF16, 2)], [(8, D)], e_dmerge, 512, D)
    du_rnn = _mm("mm_d_proj_rnn", [(dpr, w_pr.T)], F32, 512, DR)
    du_attn = _mm("mm_d_proj_attn", [(dpa, w_pa.T)], BF16, 512, D)
    d_w_pr = _mm_tn("mmt_proj_rnn", u_rnn, dpr, 1024, 512)
    d_w_pa = _mm_tn("mmt_proj_attn", u_attn, dpa, 1024, 512)

    drnn, d_wa, d_wi, vec = _rnn_bwd(proj_rnn, hs, du_rnn, cw, cb, wa_b, wat_b, ba, wi_b, wit_b, bi, sp, dsp_dlam)

    g_ffn = jnp.concatenate([_cols_to_slots(d_w_g, 352), _cols_to_slots(d_w_u, 352), d_w_dn.reshape(N_DEV, 352, D)],
                            axis=1).astype(BF16)
    g_proj = jnp.concatenate([d_w_pr.reshape(N_DEV, 160, D), d_w_pa.reshape(N_DEV, 128, D),
                              d_w_o.reshape(N_DEV, 128, D)], axis=1).astype(BF16)
    g_small = jnp.concatenate([
        _rows(vec[4], 2), d_wa.reshape(160, D), _rows(vec[5], 2), d_wi.reshape(160, D), _rows(vec[6], 2), _rows(vec[7], 2),
        acc2[2:4], acc5[1:3],
        _rows(vec[0:4], 5), acc3[0:2], acc5[0:1], jnp.zeros((4, D), F32)], axis=0)
    part1, (recv_ffn,) = _attn_bwd(qkv, du_attn, u_attn, lse, DILATIONS[0], _Exchange([(g_ffn, True)]))
    part2, (recv_proj, recv_small) = _attn_bwd(qkv, du_attn, u_attn, lse, DILATIONS[1],
                                               _Exchange([(g_proj, True), (g_small, False)]))
    part3, _ = _attn_bwd(qkv, du_attn, u_attn, lse, DILATIONS[2])
    dparts = [part1, part2, part3]

    def b4(ri, fi, ro, ao):
        for col in range(3):
            ro[0][:, col * D:(col + 1) * D] = (ri[col][...].astype(F32) + ri[3 + col][...].astype(F32)
                                               + ri[6 + col][...].astype(F32)).astype(BF16)

    (dqkv,) = _rowwise(b4, "d_qkv_sum", s_len, tm, [(t, D, 0) for grp in dparts for t in grp], [], [(3 * D, BF16)], [])

    d_w_in = jnp.concatenate([_mm_tn("mmt_in_rnn", h1, drnn, 1024, DR), _mm_tn("mmt_in_qkv", h1, dqkv, 1024, D),
                              _mm_tn("mmt_in_gates", h1, dgates, 1024, D)], axis=1)
    g_in = _cols_to_slots(d_w_in, 960).astype(BF16)
    dh1, (recv_in,) = _mm("mm_d_in", [(drnn, w_rnn.T), (dqkv, w_qkv.T), (dgates, w_gates.T)], F32, 256, 512,
                          carry=_Exchange([(g_in, True)]))

    def b5(ri, fi, ro, ao):
        md = fi[0]
        n0, rstd0 = _ln(ri[0][...])
        dh = ri[1][...]
        ro[0][...] = ri[2][...] + _ln_bwd(dh * (1.0 + md[1:2, :]), n0, rstd0)
        ao[0][0:1, :] += _rowsum(dh * n0)
        ao[0][1:2, :] += _rowsum(dh)

    grad_x, acc1 = _rowwise(b5, "d_ln_mod1", s_len, tm, [(xs, D, 0), (dh1, D, 0), (dxp, D, 0)], [mod8], [(D, F32)],
                            [((8, D), F32)])

    dmod = jnp.concatenate([acc1[1:2], acc1[0:1], acc2[4:5], acc2[1:2], acc2[0:1], acc5[3:4], jnp.zeros((2, D), F32)],
                           axis=0)
    (recv_dmod,) = _exchange_alone("gather_dmod", _Exchange([(dmod, False)]))
    dmod_all = recv_dmod[:, 0:6, :].reshape(N_DEV, 6 * D)
    red = _sum_slots(recv_small)
    loss = jnp.sum(red[339])

    wmv = (dict(w_in=w_in, w_ffn_gate=w_ffn_gate, w_ffn_up=w_ffn_up, w_proj_rnn=w_proj_rnn, w_proj_attn=w_proj_attn,
                w_out=w_out, w_ffn_down=w_ffn_down),
           dict(w_in=m_w_in, w_ffn_gate=m_w_ffn_gate, w_ffn_up=m_w_ffn_up, w_proj_rnn=m_w_proj_rnn,
                w_proj_attn=m_w_proj_attn, w_out=m_w_out, w_ffn_down=m_w_ffn_down),
           dict(w_in=v_w_in, w_ffn_gate=v_w_ffn_gate, w_ffn_up=v_w_ffn_up, w_proj_rnn=v_w_proj_rnn,
                w_proj_attn=v_w_proj_attn, w_out=v_w_out, w_ffn_down=v_w_ffn_down))
    pack = lambda ps, rows: jnp.concatenate([ps[name][0].reshape(r, D) for name, r in rows], axis=0)
    big_groups = ((PACK_ROWS[:1], recv_in, "adam_in", 96), (FFN_ROWS, recv_ffn, "adam_ffn", 96),
                  (PROJ_ROWS, recv_proj, "adam_proj", 208))
    big_out = {}
    for rows, recv, name, tile in big_groups:
        res = _adam_packed(name, recv, pack(wmv[0], rows), pack(wmv[1], rows), pack(wmv[2], rows), tile)
        o = 0
        for wname, r in rows:
            big_out[wname] = [t[o:o + r] for t in res]
            o += r
    ada_out = _adam_w_ada(cond_all.T, lax.dynamic_slice(dmod_all, (0, me * 768), (N_DEV, 768)),
                          w_ada[0], m_w_ada[0], v_w_ada[0])

    small_spec = (("b_ada", 6), ("conv_b", 2), ("lru_wa", 160), ("lru_ba", 2), ("lru_wi", 160), ("lru_bi", 2),
                  ("lru_lambda", 2), ("ln1_g", 1), ("ln1_b", 1), ("ln2_g", 1), ("ln2_b", 1), ("conv_w", 1), ("b_gate", 1))
    small_rows = 344

    def pack_small(ps):
        return jnp.concatenate([_rows(ps[name], r) for name, r in small_spec] + [jnp.zeros((4, D), F32)], axis=0)

    names = [n for n, _ in small_spec]
    smalls = [pack_small(dict(zip(names, t))) for t in (
        (b_ada, conv_b, lru_wa, lru_ba, lru_wi, lru_bi, lru_lambda, ln1_g, ln1_b, ln2_g, ln2_b, conv_w, b_gate),
        (m_b_ada, m_conv_b, m_lru_wa, m_lru_ba, m_lru_wi, m_lru_bi, m_lru_lambda, m_ln1_g, m_ln1_b, m_ln2_g, m_ln2_b,
         m_conv_w, m_b_gate),
        (v_b_ada, v_conv_b, v_lru_wa, v_lru_ba, v_lru_wi, v_lru_bi, v_lru_lambda, v_ln1_g, v_ln1_b, v_ln2_g, v_ln2_b,
         v_conv_w, v_b_gate))]
    g_conv_w = lax.dynamic_slice(red[332:337].reshape(-1)[:CONV_W * DR].reshape(CONV_W, DR), (0, me * 160), (CONV_W, 160))
    g_b_gate = lax.dynamic_slice(red[337:339], (0, me * 128), (2, 128))
    g_small_pack = jnp.concatenate([jnp.sum(dmod_all, axis=0).reshape(6, D), red[0:332], _rows(g_conv_w, 1),
                                    _rows(g_b_gate, 1), jnp.zeros((4, D), F32)], axis=0)
    assert g_small_pack.shape == (small_rows, D) and smalls[0].shape == (small_rows, D)
    small_out = _adam_packed("adam_small", g_small_pack, smalls[0], smalls[1], smalls[2], small_rows)

    shapes = dict(w_ada=w_ada.shape, b_ada=b_ada.shape, w_in=w_in.shape, conv_w=conv_w.shape, conv_b=conv_b.shape,
                  lru_wa=lru_wa.shape, lru_ba=lru_ba.shape, lru_wi=lru_wi.shape, lru_bi=lru_bi.shape,
                  lru_lambda=lru_lambda.shape, w_proj_rnn=w_proj_rnn.shape, w_proj_attn=w_proj_attn.shape,
                  b_gate=b_gate.shape, w_out=w_out.shape, ln1_g=ln1_g.shape, ln1_b=ln1_b.shape,
                  w_ffn_gate=w_ffn_gate.shape, w_ffn_up=w_ffn_up.shape, w_ffn_down=w_ffn_down.shape,
                  ln2_g=ln2_g.shape, ln2_b=ln2_b.shape)

    def unpack(kind):
        out = {"w_ada": ada_out[kind].reshape(shapes["w_ada"])}
        for name, _ in PACK_ROWS:
            out[name] = big_out[name][kind].reshape(shapes[name])
        o = 0
        for name, r in small_spec:
            size = 1
            for dim in shapes[name]:
                size *= dim
            out[name] = small_out[kind][o:o + r].reshape(-1)[:size].reshape(shapes[name])
            o += r
        return [out[name] for name in shapes]

    return (loss, grad_x[None], *unpack(0), *unpack(1), *unpack(2), *unpack(3))
```

```python
import functools

import jax
import jax.numpy as jnp
from jax import lax
from jax.experimental import pallas as pl
from jax.experimental.pallas import tpu as pltpu

F32 = jnp.float32
BF16 = jnp.bfloat16
MESH = pl.DeviceIdType.MESH

N_DEV = 8
D = 1024
DR = 1280
N_BLK = 10
BW = 128
HEAD_PAIR = 128
N_PAIR = 8
QB = 128
DFF = 2816
CONV_W = 4
DILATIONS = (1, 4, 16)
ALPHA = 2.0 ** 0.25
LN_EPS = 1e-5
LRU_C = 8.0
SCALE = 0.125
NEG = -1e30
ADAM_LR, ADAM_B1, ADAM_B2, ADAM_EPS, ADAM_WD, ADAM_STEP = 0.001, 0.9, 0.999, 1e-08, 0.01, 10

VMEM_LIMIT = 48 * 1024 * 1024
ROW_TILE = 256
RNN_CHUNK = 256

PACK_ROWS = (("w_in", 960), ("w_ffn_gate", 352), ("w_ffn_up", 352), ("w_proj_rnn", 160),
             ("w_proj_attn", 128), ("w_out", 128), ("w_ffn_down", 352))
LATE_ROWS = PACK_ROWS[1:]
FFN_ROWS = (("w_ffn_gate", 352), ("w_ffn_up", 352), ("w_ffn_down", 352))
PROJ_ROWS = (("w_proj_rnn", 160), ("w_proj_attn", 128), ("w_out", 128))


def _cparams(sem):
    return pltpu.CompilerParams(dimension_semantics=sem, vmem_limit_bytes=VMEM_LIMIT)


def _ln(z):
    mu = jnp.mean(z, axis=-1, keepdims=True)
    zc = z - mu
    var = jnp.mean(zc * zc, axis=-1, keepdims=True)
    rstd = lax.rsqrt(var + LN_EPS)
    return zc * rstd, rstd


def _ln_bwd(dn, n, rstd):
    return rstd * (dn - jnp.mean(dn, axis=-1, keepdims=True) - n * jnp.mean(dn * n, axis=-1, keepdims=True))


def _sigmoid(x):
    return 1.0 / (1.0 + jnp.exp(-x))


_GELU_K = 0.7978845608028654
_GELU_C = 0.044715


def _gelu(y):
    t = jnp.tanh(_GELU_K * (y + _GELU_C * y * y * y))
    return 0.5 * y * (1.0 + t), t


def _gelu_grad(y, t):
    return 0.5 * (1.0 + t) + 0.5 * y * (1.0 - t * t) * _GELU_K * (1.0 + 3.0 * _GELU_C * y * y)


def _rowsum(v):
    return jnp.sum(v, axis=0, keepdims=True)


def _rowwise(body, name, n_rows, tm, row_ins, full_ins, row_outs, acc_outs):
    nri, nfi, nro = len(row_ins), len(full_ins), len(row_outs)

    def kern(*refs):
        ri, fi = refs[:nri], refs[nri:nri + nfi]
        ro, ao = refs[nri + nfi:nri + nfi + nro], refs[nri + nfi + nro:]
        if ao:
            @pl.when(pl.program_id(0) == 0)
            def _():
                for a in ao:
                    a[...] = jnp.zeros(a.shape, a.dtype)
        body(ri, fi, ro, ao)

    in_specs = [pl.BlockSpec((tm, w), functools.partial(lambda i, cb: (i, cb), cb=cb)) for _, w, cb in row_ins]
    in_specs += [pl.BlockSpec(a.shape, lambda i: (0, 0)) for a in full_ins]
    out_specs = [pl.BlockSpec((tm, w), lambda i: (i, 0)) for w, _ in row_outs]
    out_specs += [pl.BlockSpec(s, lambda i: (0, 0)) for s, _ in acc_outs]
    out_shape = [jax.ShapeDtypeStruct((n_rows, w), dt) for w, dt in row_outs]
    out_shape += [jax.ShapeDtypeStruct(s, dt) for s, dt in acc_outs]
    return pl.pallas_call(
        kern, name=name, grid=(n_rows // tm,), in_specs=in_specs, out_specs=out_specs, out_shape=out_shape,
        compiler_params=_cparams(("arbitrary",)),
    )(*[a for a, _, _ in row_ins], *full_ins)


def _mm(name, pairs, out_dtype, tm, tn, carry=None):
    m_rows, n_cols = pairs[0][0].shape[0], pairs[0][1].shape[1]
    n_pairs = len(pairs)
    n_c = len(carry.arrays) if carry else 0
    grid = (n_cols // tn, m_rows // tm)

    def kern(*refs):
        o_ref = refs[2 * n_pairs + n_c]
        if carry:
            step = pl.program_id(0) * grid[1] + pl.program_id(1)
            _carry(carry, refs[2 * n_pairs:2 * n_pairs + n_c], refs[2 * n_pairs + n_c + 1:2 * n_pairs + 2 * n_c + 1],
                   refs[-1], step == 0, step == grid[0] * grid[1] - 1)
        acc = None
        for p in range(n_pairs):
            t = jnp.dot(refs[2 * p][...], refs[2 * p + 1][...], preferred_element_type=F32)
            acc = t if acc is None else acc + t
        o_ref[...] = acc.astype(o_ref.dtype)

    in_specs, flat = [], []
    for a, w in pairs:
        k = a.shape[1]
        in_specs += [pl.BlockSpec((tm, k), lambda j, i: (i, 0)), pl.BlockSpec((k, tn), lambda j, i: (0, j))]
        flat += [a, w]
    out_specs = [pl.BlockSpec((tm, tn), lambda j, i: (i, j))]
    out_shape = [jax.ShapeDtypeStruct((m_rows, n_cols), out_dtype)]
    if carry:
        res = pl.pallas_call(
            kern, name=name, grid=grid, in_specs=in_specs + carry.in_specs, out_specs=out_specs + carry.out_specs,
            out_shape=out_shape + carry.out_shape, scratch_shapes=carry.scratch,
            compiler_params=_cparams(("arbitrary", "arbitrary")),
        )(*flat, *carry.arrays)
        return res[0], res[1:]
    return pl.pallas_call(
        kern, name=name, grid=grid, in_specs=in_specs, out_specs=out_specs[0], out_shape=out_shape[0],
        compiler_params=_cparams(("parallel", "parallel")),
    )(*flat)


def _mm_tn(name, a, g, tm, tn):
    m_rows, k = a.shape
    n_cols = g.shape[1]

    def kern(a_ref, g_ref, o_ref):
        @pl.when(pl.program_id(1) == 0)
        def _():
            o_ref[...] = jnp.zeros(o_ref.shape, F32)
        o_ref[...] += lax.dot_general(a_ref[...], g_ref[...], (((0,), (0,)), ((), ())), preferred_element_type=F32)

    return pl.pallas_call(
        kern, name=name, grid=(n_cols // tn, m_rows // tm),
        in_specs=[pl.BlockSpec((tm, k), lambda j, m: (m, 0)), pl.BlockSpec((tm, tn), lambda j, m: (m, j))],
        out_specs=pl.BlockSpec((k, tn), lambda j, m: (0, j)),
        out_shape=jax.ShapeDtypeStruct((k, n_cols), F32),
        compiler_params=_cparams(("parallel", "arbitrary")),
    )(a, g)


def _mm_fused(name, pairs, tile_ins, full_ins, outs, accs, epilogue, tm, tn):
    m_rows, n_cols = pairs[0][0].shape[0], pairs[0][1].shape[1]
    n_p, n_t, n_f, n_o = len(pairs), len(tile_ins), len(full_ins), len(outs)

    def kern(*refs):
        dots = [jnp.dot(refs[2 * p][...], refs[2 * p + 1][...], preferred_element_type=F32) for p in range(n_p)]
        base = 2 * n_p
        t_refs, f_refs = refs[base:base + n_t], refs[base + n_t:base + n_t + n_f]
        o_refs, a_refs = refs[base + n_t + n_f:base + n_t + n_f + n_o], refs[base + n_t + n_f + n_o:]
        if a_refs:
            @pl.when((pl.program_id(0) == 0) & (pl.program_id(1) == 0))
            def _():
                for a in a_refs:
                    a[...] = jnp.zeros(a.shape, F32)
        epilogue(dots, t_refs, f_refs, o_refs, a_refs)

    in_specs, flat = [], []
    for a, w in pairs:
        k = a.shape[1]
        in_specs += [pl.BlockSpec((tm, k), lambda j, i: (i, 0)), pl.BlockSpec((k, tn), lambda j, i: (0, j))]
        flat += [a, w]
    for arr, cb in tile_ins:
        in_specs.append(pl.BlockSpec((tm, tn), functools.partial(lambda j, i, cb: (i, cb + j), cb=cb)))
        flat.append(arr)
    for arr in full_ins:
        in_specs.append(pl.BlockSpec(arr.shape, lambda j, i: (0, 0)))
        flat.append(arr)
    out_specs = [pl.BlockSpec((tm, f * tn), lambda j, i: (i, j)) for _, f in outs]
    out_specs += [pl.BlockSpec(s, lambda j, i: (0, 0)) for s in accs]
    out_shape = [jax.ShapeDtypeStruct((m_rows, f * n_cols), dt) for dt, f in outs]
    out_shape += [jax.ShapeDtypeStruct(s, F32) for s in accs]
    sem = ("arbitrary", "arbitrary") if accs else ("parallel", "parallel")
    return pl.pallas_call(
        kern, name=name, grid=(n_cols // tn, m_rows // tm), in_specs=in_specs, out_specs=out_specs,
        out_shape=out_shape, compiler_params=_cparams(sem),
    )(*flat)


def _peers():
    x, y, c = lax.axis_index("x"), lax.axis_index("y"), lax.axis_index("c")
    me = 4 * x + 2 * y + c
    peers = []
    for k in range(1, N_DEV):
        px = 1 - x if (k >> 2) & 1 else x
        py = 1 - y if (k >> 1) & 1 else y
        pc = 1 - c if k & 1 else c
        peers.append(((px, py, pc), 4 * px + 2 * py + pc))
    return me, peers


def _exchange(me, peers, items, send_sems, recv_sems, t0=0):
    started = []
    for t, (src_of, dst_of) in enumerate(items, start=t0):
        for k, (pid, plin) in enumerate(peers):
            cp = pltpu.make_async_remote_copy(
                src_ref=src_of(plin), dst_ref=dst_of(me), send_sem=send_sems.at[t, k], recv_sem=recv_sems.at[t, k],
                device_id=pid, device_id_type=MESH)
            cp.start()
            started.append(cp)
    for t, (src_of, dst_of) in enumerate(items, start=t0):
        for k, (pid, plin) in enumerate(peers):
            pltpu.make_async_remote_copy(
                src_ref=src_of(plin), dst_ref=dst_of(plin), send_sem=send_sems.at[t, k], recv_sem=recv_sems.at[t, k],
                device_id=pid, device_id_type=MESH).wait_recv()
    for cp in started:
        cp.wait_send()


def _hbm_spec():
    return pl.BlockSpec(memory_space=pltpu.HBM)


def _gather_shards(shard):
    def body(src, out, send_sems, recv_sems, own_sem):
        x, y, c = lax.axis_index("x"), lax.axis_index("y"), lax.axis_index("c")
        lin = lambda px, py, pc: 4 * px + 2 * py + pc
        sibling = (x, y, 1 - c)
        chips = [(1 - x, y), (x, 1 - y), (1 - x, 1 - y)]

        def copy(k, slot, to, from_src):
            return pltpu.make_async_remote_copy(
                src_ref=src if from_src else out.at[slot], dst_ref=out.at[slot], send_sem=send_sems.at[k],
                recv_sem=recv_sems.at[k], device_id=to, device_id_type=MESH)

        own = pltpu.make_async_copy(src, out.at[lin(x, y, c)], own_sem)
        own.start()
        first = [copy(0, lin(x, y, c), sibling, True)]
        first += [copy(1 + j, lin(x, y, c), (px, py, c), True) for j, (px, py) in enumerate(chips)]
        for cp in first:
            cp.start()
        passed = [copy(4 + j, lin(px, py, c), sibling, False) for j, (px, py) in enumerate(chips)]
        for j, (px, py) in enumerate(chips):
            copy(1 + j, lin(px, py, c), sibling, True).wait_recv()
            passed[j].start()
        copy(0, lin(x, y, 1 - c), sibling, True).wait_recv()
        for j, (px, py) in enumerate(chips):
            copy(4 + j, lin(px, py, 1 - c), sibling, False).wait_recv()
        for cp in first + passed:
            cp.wait_send()
        own.wait()

    return pl.pallas_call(
        body, name="gather_weights", in_specs=[_hbm_spec()], out_specs=_hbm_spec(),
        out_shape=jax.ShapeDtypeStruct((N_DEV,) + shard.shape, shard.dtype),
        scratch_shapes=[pltpu.SemaphoreType.DMA((N_DEV - 1,)), pltpu.SemaphoreType.DMA((N_DEV - 1,)),
                        pltpu.SemaphoreType.DMA(())],
    )(shard)


SEMS_PER_ITEM = 2 * (N_DEV - 1) + 1


class _Exchange:
    def __init__(self, items):
        self.arrays = [a for a, _ in items]
        self.scatter = [s for _, s in items]
        self.out_shape = [jax.ShapeDtypeStruct(a.shape if s else (N_DEV,) + a.shape, a.dtype) for a, s in items]
        self.in_specs = [_hbm_spec() for _ in items]
        self.out_specs = [_hbm_spec() for _ in items]
        self.scratch = [pltpu.SemaphoreType.DMA((SEMS_PER_ITEM * len(items),))]

    def _copies(self, ins, outs, sems, landing):
        me, peers = _peers()
        cps = []
        for t, scatter in enumerate(self.scatter):
            base = t * SEMS_PER_ITEM
            for k, (pid, plin) in enumerate(peers):
                src = ins[t].at[plin] if scatter else ins[t]
                dst = outs[t].at[plin if landing else me]
                cps.append(pltpu.make_async_remote_copy(
                    src_ref=src, dst_ref=dst, send_sem=sems.at[base + k], recv_sem=sems.at[base + N_DEV - 1 + k],
                    device_id=pid, device_id_type=MESH))
        return cps

    def _own(self, ins, outs, sems):
        me, _ = _peers()
        return [pltpu.make_async_copy(ins[t].at[me] if scatter else ins[t], outs[t].at[me],
                                      sems.at[t * SEMS_PER_ITEM + 2 * (N_DEV - 1)])
                for t, scatter in enumerate(self.scatter)]

    def start(self, ins, outs, sems):
        for cp in self._own(ins, outs, sems) + self._copies(ins, outs, sems, False):
            cp.start()

    def wait(self, ins, outs, sems):
        for cp in self._copies(ins, outs, sems, True):
            cp.wait_recv()
        for cp in self._copies(ins, outs, sems, False):
            cp.wait_send()
        for cp in self._own(ins, outs, sems):
            cp.wait()


def _carry(exchange, ins, outs, sems, first, last):
    @pl.when(first)
    def _():
        exchange.start(ins, outs, sems)

    @pl.when(last)
    def _():
        exchange.wait(ins, outs, sems)


def _exchange_alone(name, exchange):
    n = len(exchange.arrays)

    def body(*refs):
        exchange.start(refs[:n], refs[n:2 * n], refs[2 * n])
        exchange.wait(refs[:n], refs[n:2 * n], refs[2 * n])

    return pl.pallas_call(body, name=name, in_specs=exchange.in_specs, out_specs=exchange.out_specs,
                          out_shape=exchange.out_shape, scratch_shapes=exchange.scratch)(*exchange.arrays)


def _ada_modulation(c8, w_ada, b_ada_cols):
    wcols = w_ada.shape[1]

    def body(c_ref, w_ref, b_ref, call_ref, cond_ref, mod_ref, res, send_sems, recv_sems):
        me, peers = _peers()
        call_ref[me] = c_ref[...]
        _exchange(me, peers, [(lambda p: c_ref, lambda s: call_ref.at[s])], send_sems, recv_sems, t0=0)
        for dev in range(N_DEV):
            cv = call_ref[dev]
            cond = cv * _sigmoid(cv)
            cond_ref[dev] = cond
            res[dev] = jnp.dot(cond, w_ref[...], preferred_element_type=F32,
                               precision=lax.Precision.HIGHEST) + b_ref[...]
        mod_ref[me] = res[me]
        _exchange(me, peers, [(lambda p: res.at[p], lambda s: mod_ref.at[s])], send_sems, recv_sems, t0=1)

    vm = pl.BlockSpec(memory_space=pltpu.VMEM)
    return pl.pallas_call(
        body, name="ada_modulation", in_specs=[vm, vm, vm], out_specs=[vm, vm, vm],
        out_shape=[jax.ShapeDtypeStruct((N_DEV, 8, D), F32), jax.ShapeDtypeStruct((N_DEV, 8, D), F32),
                   jax.ShapeDtypeStruct((N_DEV, 8, wcols), F32)],
        scratch_shapes=[pltpu.VMEM((N_DEV, 8, wcols), F32),
                        pltpu.SemaphoreType.DMA((2, N_DEV - 1)), pltpu.SemaphoreType.DMA((2, N_DEV - 1))],
        compiler_params=pltpu.CompilerParams(vmem_limit_bytes=VMEM_LIMIT),
    )(c8, w_ada, b_ada_cols)


def _lru_gates(xc, wa_ref, ba_ref, wi_ref, bi_ref, sp_ref, row0):
    xcb = xc.astype(BF16)
    pre_r, pre_i = [], []
    for n in range(N_BLK):
        xb = xcb[:, n * BW:(n + 1) * BW]
        pre_r.append(jnp.dot(xb, wa_ref[n], preferred_element_type=F32))
        pre_i.append(jnp.dot(xb, wi_ref[n], preferred_element_type=F32))
    r = _sigmoid(jnp.concatenate(pre_r, axis=1) + ba_ref[...])
    ig = _sigmoid(jnp.concatenate(pre_i, axis=1) + bi_ref[...])
    log_a = (-LRU_C) * r * sp_ref[...]
    a = jnp.exp(log_a)
    e2 = jnp.exp(2.0 * log_a)
    mult_raw = jnp.sqrt(1.0 - e2)
    rows = row0 + lax.broadcasted_iota(jnp.int32, xc.shape, 0)
    start = rows == 0
    mult = jnp.where(start, 1.0, mult_raw)
    return xcb, r, ig, a, e2, mult_raw, mult, start


def _conv(xpad, cw_ref, cb_ref, tc):
    out = cb_ref[...]
    for tap in range(CONV_W):
        out = out + cw_ref[tap:tap + 1, :] * xpad[pl.ds(8 - (CONV_W - 1) + tap, tc), :]
    return out


def _rnn_fwd(proj_rnn, cw, cb, wa, ba, wi, bi, sp, carry):
    s_len = proj_rnn.shape[0]
    tc = RNN_CHUNK
    n_c = len(carry.arrays)

    def kern(*refs):
        y_ref, xr_ref, cw_ref, cb_ref, wa_ref, ba_ref, wi_ref, bi_ref, sp_ref = refs[:9]
        hs_ref, ur_ref = refs[9 + n_c:11 + n_c]
        xpad, a_scr, u_scr, h_scr = refs[11 + 2 * n_c:15 + 2 * n_c]
        i = pl.program_id(0)
        _carry(carry, refs[9:9 + n_c], refs[11 + n_c:11 + 2 * n_c], refs[-1], i == 0, i == s_len // tc - 1)

        @pl.when(i == 0)
        def _():
            xpad[0:8, :] = jnp.zeros((8, DR), F32)
            h_scr[...] = jnp.zeros((1, DR), F32)

        xpad[8:8 + tc, :] = xr_ref[...]
        xc = _conv(xpad, cw_ref, cb_ref, tc)
        xpad[0:8, :] = xpad[tc:tc + 8, :]
        _, r, ig, a, e2, mult_raw, mult, start = _lru_gates(xc, wa_ref, ba_ref, wi_ref, bi_ref, sp_ref, i * tc)
        a_scr[...] = a
        u_scr[...] = mult * (ig * xc)

        def step(t, h):
            h = a_scr[pl.ds(t, 1), :] * h + u_scr[pl.ds(t, 1), :]
            hs_ref[pl.ds(t, 1), :] = h
            return h

        h_scr[...] = lax.fori_loop(0, tc, step, h_scr[...], unroll=8)
        gy, _ = _gelu(y_ref[...])
        ur_ref[...] = (gy * hs_ref[...]).astype(BF16)

    full = lambda a: pl.BlockSpec(a.shape, lambda i: (0,) * a.ndim)
    res = pl.pallas_call(
        kern, name="rnn_fwd", grid=(s_len // tc,),
        in_specs=[pl.BlockSpec((tc, DR), lambda i: (i, 0)), pl.BlockSpec((tc, DR), lambda i: (i, 1)),
                  full(cw), full(cb), full(wa), full(ba), full(wi), full(bi), full(sp)] + carry.in_specs,
        out_specs=[pl.BlockSpec((tc, DR), lambda i: (i, 0)), pl.BlockSpec((tc, DR), lambda i: (i, 0))] + carry.out_specs,
        out_shape=[jax.ShapeDtypeStruct((s_len, DR), F32), jax.ShapeDtypeStruct((s_len, DR), BF16)] + carry.out_shape,
        scratch_shapes=[pltpu.VMEM((tc + 8, DR), F32), pltpu.VMEM((tc, DR), F32), pltpu.VMEM((tc, DR), F32),
                        pltpu.VMEM((1, DR), F32)] + carry.scratch,
        compiler_params=_cparams(("arbitrary",)),
    )(proj_rnn, proj_rnn, cw, cb, wa, ba, wi, bi, sp, *carry.arrays)
    return res[0], res[1], res[2:]


def _rnn_bwd(proj_rnn, hs, du, cw, cb, wa, wat, ba, wi, wit, bi, sp, dsp_dlam):
    s_len = proj_rnn.shape[0]
    tc = RNN_CHUNK
    nch = s_len // tc

    def kern(y_ref, xr_ref, xrp_ref, hs_ref, hsp_ref, du_ref, cw_ref, cb_ref, wa_ref, wat_ref, ba_ref, wi_ref,
             wit_ref, bi_ref, sp_ref, dspl_ref, drnn_ref, dwa_ref, dwi_ref, vec_ref,
             xpad, hpad, a_scr, d_scr, g_scr, dxcpad, ag_scr):
        j = pl.program_id(0)
        n = nch - 1 - j

        @pl.when(j == 0)
        def _():
            dwa_ref[...] = jnp.zeros(dwa_ref.shape, F32)
            dwi_ref[...] = jnp.zeros(dwi_ref.shape, F32)
            vec_ref[...] = jnp.zeros(vec_ref.shape, F32)
            ag_scr[...] = jnp.zeros((1, DR), F32)
            dxcpad[tc:tc + 8, :] = jnp.zeros((8, DR), F32)

        has_prev = n > 0
        xpad[0:8, :] = jnp.where(has_prev, xrp_ref[...], 0.0)
        xpad[8:8 + tc, :] = xr_ref[...]
        hpad[0:8, :] = jnp.where(has_prev, hsp_ref[...], 0.0)
        hpad[8:8 + tc, :] = hs_ref[...]
        xc = _conv(xpad, cw_ref, cb_ref, tc)
        xcb, r, ig, a, e2, mult_raw, mult, start = _lru_gates(xc, wa_ref, ba_ref, wi_ref, bi_ref, sp_ref, n * tc)

        y = y_ref[...]
        gy, th = _gelu(y)
        duv = du_ref[...].astype(F32)
        drnn_ref[:, 0:DR] = (duv * hs_ref[...] * _gelu_grad(y, th)).astype(BF16)
        a_scr[...] = a
        d_scr[...] = duv * gy

        def step(tt, ag):
            t = tc - 1 - tt
            g = d_scr[pl.ds(t, 1), :] + ag
            g_scr[pl.ds(t, 1), :] = g
            return a_scr[pl.ds(t, 1), :] * g

        ag_scr[...] = lax.fori_loop(0, tc, step, ag_scr[...], unroll=8)
        g = g_scr[...]
        da = g * hpad[pl.ds(7, tc), :]
        gx = g * xc
        dmult = jnp.where(start, 0.0, gx * ig)
        di = gx * mult
        dxc = g * mult * ig
        dlog_a = da * a - jnp.where(start, 0.0, dmult * e2 / mult_raw)
        dr = dlog_a * ((-LRU_C) * sp_ref[...])
        vec_ref[7:8, :] += _rowsum(dlog_a * ((-LRU_C) * r))
        dpr = dr * r * (1.0 - r)
        dpi = di * ig * (1.0 - ig)
        vec_ref[5:6, :] += _rowsum(dpr)
        vec_ref[6:7, :] += _rowsum(dpi)
        dprb, dpib = dpr.astype(BF16), dpi.astype(BF16)
        extra = []
        for b in range(N_BLK):
            sl = slice(b * BW, (b + 1) * BW)
            extra.append(jnp.dot(dprb[:, sl], wat_ref[b], preferred_element_type=F32)
                         + jnp.dot(dpib[:, sl], wit_ref[b], preferred_element_type=F32))
            dn = (((0,), (0,)), ((), ()))
            dwa_ref[b] += lax.dot_general(xcb[:, sl], dprb[:, sl], dn, preferred_element_type=F32)
            dwi_ref[b] += lax.dot_general(xcb[:, sl], dpib[:, sl], dn, preferred_element_type=F32)
        dxc = dxc + jnp.concatenate(extra, axis=1)
        vec_ref[4:5, :] += _rowsum(dxc)
        dxcpad[0:tc, :] = dxc
        dxr = jnp.zeros((tc, DR), F32)
        for tap in range(CONV_W):
            shift = CONV_W - 1 - tap
            dxr = dxr + cw_ref[tap:tap + 1, :] * dxcpad[pl.ds(shift, tc), :]
            vec_ref[tap:tap + 1, :] += _rowsum(dxc * xpad[pl.ds(8 - shift, tc), :])
        dxcpad[tc:tc + 8, :] = dxcpad[0:8, :]
        drnn_ref[:, DR:2 * DR] = dxr.astype(BF16)

        @pl.when(j == nch - 1)
        def _():
            vec_ref[7:8, :] = vec_ref[7:8, :] * dspl_ref[...]

    full = lambda a: pl.BlockSpec(a.shape, lambda j: (0,) * a.ndim)
    rev = lambda j: nch - 1 - j
    prev8 = lambda j: jnp.maximum((nch - 1 - j) * (tc // 8) - 1, 0)
    return pl.pallas_call(
        kern, name="rnn_bwd", grid=(nch,),
        in_specs=[pl.BlockSpec((tc, DR), lambda j: (rev(j), 0)), pl.BlockSpec((tc, DR), lambda j: (rev(j), 1)),
                  pl.BlockSpec((8, DR), lambda j: (prev8(j), 1)),
                  pl.BlockSpec((tc, DR), lambda j: (rev(j), 0)), pl.BlockSpec((8, DR), lambda j: (prev8(j), 0)),
                  pl.BlockSpec((tc, DR), lambda j: (rev(j), 0)),
                  full(cw), full(cb), full(wa), full(wat), full(ba), full(wi), full(wit), full(bi), full(sp),
                  full(dsp_dlam)],
        out_specs=[pl.BlockSpec((tc, 2 * DR), lambda j: (rev(j), 0)),
                   pl.BlockSpec((N_BLK, BW, BW), lambda j: (0, 0, 0)), pl.BlockSpec((N_BLK, BW, BW), lambda j: (0, 0, 0)),
                   pl.BlockSpec((8, DR), lambda j: (0, 0))],
        out_shape=[jax.ShapeDtypeStruct((s_len, 2 * DR), BF16), jax.ShapeDtypeStruct((N_BLK, BW, BW), F32),
                   jax.ShapeDtypeStruct((N_BLK, BW, BW), F32), jax.ShapeDtypeStruct((8, DR), F32)],
        scratch_shapes=[pltpu.VMEM((tc + 8, DR), F32), pltpu.VMEM((tc + 8, DR), F32), pltpu.VMEM((tc, DR), F32),
                        pltpu.VMEM((tc, DR), F32), pltpu.VMEM((tc, DR), F32), pltpu.VMEM((tc + 8, DR), F32),
                        pltpu.VMEM((1, DR), F32)],
        compiler_params=_cparams(("arbitrary",)),
    )(proj_rnn, proj_rnn, proj_rnn, hs, hs, du, cw, cb, wa, wat, ba, wi, wit, bi, sp, dsp_dlam)


def _attn_masks(n):
    qi = lax.broadcasted_iota(jnp.int32, (QB, 2 * QB), 0)
    ki = lax.broadcasted_iota(jnp.int32, (QB, 2 * QB), 1)
    dist = QB + qi - ki
    valid = (dist >= 0) & (dist <= QB) & ((n > 0) | (ki >= QB))
    lane = lax.broadcasted_iota(jnp.int32, (1, HEAD_PAIR), 1)
    return valid, lane


def _qkv_specs(d, n_of):
    prev = lambda r, j: jnp.maximum(n_of(j) - 1, 0)
    blk = (QB, D)
    return [pl.BlockSpec(blk, lambda r, j: (n_of(j), 3 * r)),
            pl.BlockSpec(blk, lambda r, j: (prev(r, j), 3 * r + 1)), pl.BlockSpec(blk, lambda r, j: (n_of(j), 3 * r + 1)),
            pl.BlockSpec(blk, lambda r, j: (prev(r, j), 3 * r + 2)), pl.BlockSpec(blk, lambda r, j: (n_of(j), 3 * r + 2))]


def _attn_fwd(qkv, d):
    s_len = qkv.shape[0]
    n_l = s_len // d
    nb = n_l // QB
    qv = qkv.reshape(n_l, d * 3 * D)

    def kern(q_ref, kp_ref, kc_ref, vp_ref, vc_ref, o_ref, lse_ref):
        valid, lane = _attn_masks(pl.program_id(1))
        first = lane < 64
        lse_blk = jnp.zeros((QB, HEAD_PAIR), F32)
        for hp in range(N_PAIR):
            sl = slice(hp * HEAD_PAIR, (hp + 1) * HEAD_PAIR)
            q2 = q_ref[:, sl]
            k2 = jnp.concatenate([kp_ref[:, sl], kc_ref[:, sl]], axis=0)
            v2 = jnp.concatenate([vp_ref[:, sl], vc_ref[:, sl]], axis=0)
            outs = []
            for half in range(2):
                hm = first if half == 0 else jnp.logical_not(first)
                qh = jnp.where(hm, q2, jnp.zeros_like(q2))
                s = lax.dot_general(qh, k2, (((1,), (1,)), ((), ())), preferred_element_type=F32) * SCALE
                s = jnp.where(valid, s, NEG)
                mx = jnp.max(s, axis=-1, keepdims=True)
                p = jnp.exp(s - mx)
                den = jnp.sum(p, axis=-1, keepdims=True)
                outs.append(jnp.dot(p.astype(BF16), v2, preferred_element_type=F32) / den)
                lse_blk = jnp.where(lane == 2 * hp + half, mx + jnp.log(den), lse_blk)
            o_ref[:, sl] = jnp.where(first, outs[0], outs[1]).astype(BF16)
        lse_ref[...] = lse_blk

    o, lse = pl.pallas_call(
        kern, name=f"attn_fwd_d{d}", grid=(d, nb), in_specs=_qkv_specs(d, lambda j: j),
        out_specs=[pl.BlockSpec((QB, D), lambda r, j: (j, r)), pl.BlockSpec((QB, HEAD_PAIR), lambda r, j: (j, r))],
        out_shape=[jax.ShapeDtypeStruct((n_l, d * D), BF16), jax.ShapeDtypeStruct((n_l, d * HEAD_PAIR), F32)],
        compiler_params=_cparams(("arbitrary", "arbitrary")),
    )(qv, qv, qv, qv, qv)
    return o.reshape(s_len, D), lse.reshape(s_len, HEAD_PAIR)


def _attn_bwd(qkv, do, o, lse, d, carry=None):
    s_len = qkv.shape[0]
    n_l = s_len // d
    nb = n_l // QB
    qv = qkv.reshape(n_l, d * 3 * D)
    view = lambda t: t.reshape(n_l, d * D)
    n_c = len(carry.arrays) if carry else 0

    def kern(*refs):
        q_ref, kp_ref, kc_ref, vp_ref, vc_ref, do_ref, o_ref, lse_ref = refs[:8]
        dq_ref, dk_ref, dv_ref = refs[8 + n_c:11 + n_c]
        dk_scr, dv_scr = refs[11 + 2 * n_c:13 + 2 * n_c]
        j = pl.program_id(1)
        if carry:
            r = pl.program_id(0)
            _carry(carry, refs[8:8 + n_c], refs[11 + n_c:11 + 2 * n_c], refs[-1], (r == 0) & (j == 0),
                   (r == d - 1) & (j == nb - 1))

        @pl.when(j == 0)
        def _():
            dk_scr[...] = jnp.zeros((QB, D), F32)
            dv_scr[...] = jnp.zeros((QB, D), F32)

        valid, lane = _attn_masks(nb - 1 - j)
        first = lane < 64
        lse_blk = lse_ref[...]
        nt = (((1,), (1,)), ((), ()))
        tn = (((0,), (0,)), ((), ()))
        for hp in range(N_PAIR):
            sl = slice(hp * HEAD_PAIR, (hp + 1) * HEAD_PAIR)
            q2 = q_ref[:, sl]
            k2 = jnp.concatenate([kp_ref[:, sl], kc_ref[:, sl]], axis=0)
            v2 = jnp.concatenate([vp_ref[:, sl], vc_ref[:, sl]], axis=0)
            do2 = do_ref[:, sl]
            prod = do2.astype(F32) * o_ref[:, sl].astype(F32)
            dqs = []
            dk2 = jnp.zeros((2 * QB, HEAD_PAIR), F32)
            dv2 = jnp.zeros((2 * QB, HEAD_PAIR), F32)
            for half in range(2):
                hm = first if half == 0 else jnp.logical_not(first)
                qh = jnp.where(hm, q2, jnp.zeros_like(q2))
                doh = jnp.where(hm, do2, jnp.zeros_like(do2))
                dsum = jnp.sum(jnp.where(hm, prod, 0.0), axis=-1, keepdims=True)
                lseh = jnp.sum(jnp.where(lane == 2 * hp + half, lse_blk, 0.0), axis=-1, keepdims=True)
                s = lax.dot_general(qh, k2, nt, preferred_element_type=F32) * SCALE
                p = jnp.exp(jnp.where(valid, s, NEG) - lseh)
                dp = lax.dot_general(doh, v2, nt, preferred_element_type=F32)
                ds = (p * (dp - dsum) * SCALE).astype(BF16)
                dqs.append(jnp.dot(ds, k2, preferred_element_type=F32))
                dk2 = dk2 + lax.dot_general(ds, qh, tn, preferred_element_type=F32)
                dv2 = dv2 + lax.dot_general(p.astype(BF16), doh, tn, preferred_element_type=F32)
            dq_ref[:, sl] = jnp.where(first, dqs[0], dqs[1]).astype(BF16)
            dk_ref[:, sl] = (dk2[QB:, :] + dk_scr[:, sl]).astype(BF16)
            dv_ref[:, sl] = (dv2[QB:, :] + dv_scr[:, sl]).astype(BF16)
            dk_scr[:, sl] = dk2[:QB, :]
            dv_scr[:, sl] = dv2[:QB, :]

    rev = lambda j: nb - 1 - j
    row = pl.BlockSpec((QB, D), lambda r, j: (rev(j), r))
    res = pl.pallas_call(
        kern, name=f"attn_bwd_d{d}", grid=(d, nb),
        in_specs=_qkv_specs(d, rev) + [row, row, pl.BlockSpec((QB, HEAD_PAIR), lambda r, j: (rev(j), r))]
        + (carry.in_specs if carry else []),
        out_specs=[row, row, row] + (carry.out_specs if carry else []),
        out_shape=[jax.ShapeDtypeStruct((n_l, d * D), BF16)] * 3 + (carry.out_shape if carry else []),
        scratch_shapes=[pltpu.VMEM((QB, D), F32), pltpu.VMEM((QB, D), F32)] + (carry.scratch if carry else []),
        compiler_params=_cparams(("arbitrary", "arbitrary")),
    )(qv, qv, qv, qv, qv, view(do), view(o), lse.reshape(n_l, d * HEAD_PAIR), *(carry.arrays if carry else []))
    dq, dk, dv = res[:3]
    return (dq.reshape(s_len, D), dk.reshape(s_len, D), dv.reshape(s_len, D)), res[3:]


ATT_BLK = 2048


def _attn_units():
    return [(g, d, b, b * QB * d + r)
            for g, d in enumerate(DILATIONS) for b in range(ATT_BLK // (QB * d)) for r in range(d)]


def _rows_at(ref, start, n, d):
    return ref[pl.ds(start, n, stride=d), :] if d > 1 else ref[start:start + n, :]


def _add_rows_at(ref, start, n, d, val):
    if d > 1:
        ref[pl.ds(start, n, stride=d), :] = ref[pl.ds(start, n, stride=d), :] + val
    else:
        ref[start:start + n, :] = ref[start:start + n, :] + val


def _stack_heads(x, first):
    zero = jnp.zeros_like(x)
    return jnp.concatenate([jnp.where(first, x, zero), jnp.where(first, zero, x)], axis=0)


def _unstack_heads(x, first):
    return jnp.where(first, x[:QB], x[QB:])


def _head_column(x, lane, half):
    return jnp.sum(jnp.where(lane == 64 * half, x, 0.0), axis=-1, keepdims=True)


def _band_masks():
    row = lax.broadcasted_iota(jnp.int32, (2 * QB, 2 * QB), 0) & (QB - 1)
    col = lax.broadcasted_iota(jnp.int32, (2 * QB, 2 * QB), 1)
    dist = QB + row - col
    lane = lax.broadcasted_iota(jnp.int32, (1, HEAD_PAIR), 1)
    return (dist >= 0) & (dist <= QB), col >= QB, lane


def _attention_specs(nblk, blk_of):
    cur = lambda off: pl.BlockSpec((ATT_BLK, HEAD_PAIR), lambda hp, j: (blk_of(j), off + hp))
    prev = lambda off: pl.BlockSpec((ATT_BLK, HEAD_PAIR), lambda hp, j: (jnp.maximum(blk_of(j) - 1, 0), off + hp))
    return cur, prev


def _attention_fwd(qkv):
    s_len = qkv.shape[0]
    nblk = s_len // ATT_BLK
    units = _attn_units()
    nt = (((1,), (1,)), ((), ()))

    def kern(q_ref, k_ref, v_ref, o_ref, lse_ref, kbuf, vbuf, og, lg):
        blk = pl.program_id(1)

        @pl.when(blk == 0)
        def _():
            kbuf[0:ATT_BLK, :] = jnp.zeros((ATT_BLK, HEAD_PAIR), F32)
            vbuf[0:ATT_BLK, :] = jnp.zeros((ATT_BLK, HEAD_PAIR), F32)

        kbuf[ATT_BLK:2 * ATT_BLK, :] = k_ref[...]
        vbuf[ATT_BLK:2 * ATT_BLK, :] = v_ref[...]
        band, later, lane = _band_masks()
        first = lane < 64
        band0 = band & (later | (blk > 0))
        for g, d, b, start in units:
            kstart = ATT_BLK + start - QB * d
            qs = _stack_heads(_rows_at(q_ref, start, QB, d).astype(BF16), first)
            k2 = _rows_at(kbuf, kstart, 2 * QB, d).astype(BF16)
            v2 = _rows_at(vbuf, kstart, 2 * QB, d).astype(BF16)
            s = lax.dot_general(qs, k2, nt, preferred_element_type=F32) * SCALE
            s = jnp.where(band if b > 0 else band0, s, NEG)
            mx = jnp.max(s, axis=-1, keepdims=True)
            p = jnp.exp(s - mx)
            den = jnp.sum(p, axis=-1, keepdims=True)
            out = jnp.dot(p.astype(BF16), v2, preferred_element_type=F32) / den
            lse2 = jnp.broadcast_to(mx + jnp.log(den), (2 * QB, HEAD_PAIR))
            if d > 1:
                og[g, pl.ds(start, QB, stride=d), :] = _unstack_heads(out, first)
                lg[g, pl.ds(start, QB, stride=d), :] = _unstack_heads(lse2, first)
            else:
                og[g, start:start + QB, :] = _unstack_heads(out, first)
                lg[g, start:start + QB, :] = _unstack_heads(lse2, first)
        kbuf[0:ATT_BLK, :] = kbuf[ATT_BLK:2 * ATT_BLK, :]
        vbuf[0:ATT_BLK, :] = vbuf[ATT_BLK:2 * ATT_BLK, :]
        mx = jnp.maximum(jnp.maximum(lg[0], lg[1]), lg[2])
        es = [jnp.exp(lg[g] - mx) for g in range(3)]
        tot = es[0] + es[1] + es[2]
        o_ref[...] = ((es[0] * og[0] + es[1] * og[1] + es[2] * og[2]) / tot).astype(BF16)
        lse_ref[...] = mx + jnp.log(tot)

    cur, _ = _attention_specs(nblk, lambda j: j)
    out_spec = pl.BlockSpec((ATT_BLK, HEAD_PAIR), lambda hp, j: (j, hp))
    return pl.pallas_call(
        kern, name="attention_fwd", grid=(N_PAIR, nblk), in_specs=[cur(0), cur(N_PAIR), cur(2 * N_PAIR)],
        out_specs=[out_spec, out_spec],
        out_shape=[jax.ShapeDtypeStruct((s_len, D), BF16), jax.ShapeDtypeStruct((s_len, D), F32)],
        scratch_shapes=[pltpu.VMEM((2 * ATT_BLK, HEAD_PAIR), F32), pltpu.VMEM((2 * ATT_BLK, HEAD_PAIR), F32),
                        pltpu.VMEM((3, ATT_BLK, HEAD_PAIR), F32), pltpu.VMEM((3, ATT_BLK, HEAD_PAIR), F32)],
        compiler_params=_cparams(("arbitrary", "arbitrary")),
    )(qkv, qkv, qkv)


def _attention_bwd(qkv, do, o, lse, carry):
    s_len = qkv.shape[0]
    nblk = s_len // ATT_BLK
    units = _attn_units()
    nt = (((1,), (1,)), ((), ()))
    tn = (((0,), (0,)), ((), ()))
    n_c = len(carry.arrays) if carry else 0
    c_in, c_out = (carry.in_specs, carry.out_specs) if carry else ([], [])
    c_shape, c_scratch, c_arrays = (carry.out_shape, carry.scratch, carry.arrays) if carry else ([], [], [])

    def kern(*refs):
        q_ref, k_ref, v_ref, kp_ref, vp_ref, do_ref, o_ref, lse_ref = refs[:8]
        dq_ref, dk_ref, dv_ref = refs[8 + n_c:11 + n_c]
        kbuf, vbuf, dkbuf, dvbuf, dq_scr, dsum = refs[11 + 2 * n_c:17 + 2 * n_c]
        hp, j = pl.program_id(0), pl.program_id(1)
        blk = nblk - 1 - j
        if carry:
            _carry(carry, refs[8:8 + n_c], refs[11 + n_c:11 + 2 * n_c], refs[-1], (hp == 0) & (j == 0),
                   (hp == N_PAIR - 1) & (j == nblk - 1))
        zeros = jnp.zeros((ATT_BLK, HEAD_PAIR), F32)

        @pl.when(j == 0)
        def _():
            dkbuf[ATT_BLK:2 * ATT_BLK, :] = zeros
            dvbuf[ATT_BLK:2 * ATT_BLK, :] = zeros

        @pl.when(j > 0)
        def _():
            dkbuf[ATT_BLK:2 * ATT_BLK, :] = dkbuf[0:ATT_BLK, :]
            dvbuf[ATT_BLK:2 * ATT_BLK, :] = dvbuf[0:ATT_BLK, :]

        dkbuf[0:ATT_BLK, :] = zeros
        dvbuf[0:ATT_BLK, :] = zeros
        dq_scr[...] = zeros
        kbuf[0:ATT_BLK, :] = kp_ref[...]
        kbuf[ATT_BLK:2 * ATT_BLK, :] = k_ref[...]
        vbuf[0:ATT_BLK, :] = vp_ref[...]
        vbuf[ATT_BLK:2 * ATT_BLK, :] = v_ref[...]
        band, later, lane = _band_masks()
        first = lane < 64
        band0 = band & (later | (blk > 0))
        prod = do_ref[...] * o_ref[...].astype(F32)
        dsum[...] = jnp.where(first, jnp.sum(jnp.where(first, prod, 0.0), axis=-1, keepdims=True),
                              jnp.sum(jnp.where(first, 0.0, prod), axis=-1, keepdims=True))
        for g, d, b, start in units:
            kstart = ATT_BLK + start - QB * d
            qs = _stack_heads(_rows_at(q_ref, start, QB, d).astype(BF16), first)
            dos = _stack_heads(_rows_at(do_ref, start, QB, d).astype(BF16), first)
            k2 = _rows_at(kbuf, kstart, 2 * QB, d).astype(BF16)
            v2 = _rows_at(vbuf, kstart, 2 * QB, d).astype(BF16)
            ds_rows = _rows_at(dsum, start, QB, d)
            lse_rows = _rows_at(lse_ref, start, QB, d)
            dcol = jnp.concatenate([_head_column(ds_rows, lane, 0), _head_column(ds_rows, lane, 1)], axis=0)
            lcol = jnp.concatenate([_head_column(lse_rows, lane, 0), _head_column(lse_rows, lane, 1)], axis=0)
            s = lax.dot_general(qs, k2, nt, preferred_element_type=F32) * SCALE
            p = jnp.exp(jnp.where(band if b > 0 else band0, s, NEG) - lcol)
            dp = lax.dot_general(dos, v2, nt, preferred_element_type=F32)
            ds = (p * (dp - dcol) * SCALE).astype(BF16)
            _add_rows_at(dq_scr, start, QB, d, _unstack_heads(jnp.dot(ds, k2, preferred_element_type=F32), first))
            _add_rows_at(dkbuf, kstart, 2 * QB, d, lax.dot_general(ds, qs, tn, preferred_element_type=F32))
            _add_rows_at(dvbuf, kstart, 2 * QB, d, lax.dot_general(p.astype(BF16), dos, tn, preferred_element_type=F32))
        dq_ref[...] = dq_scr[...].astype(BF16)
        dk_ref[...] = dkbuf[ATT_BLK:2 * ATT_BLK, :].astype(BF16)
        dv_ref[...] = dvbuf[ATT_BLK:2 * ATT_BLK, :].astype(BF16)

    rev = lambda j: nblk - 1 - j
    cur, prev = _attention_specs(nblk, rev)
    buf = lambda rows: pltpu.VMEM((rows, HEAD_PAIR), F32)
    res = pl.pallas_call(
        kern, name="attention_bwd", grid=(N_PAIR, nblk),
        in_specs=[cur(0), cur(N_PAIR), cur(2 * N_PAIR), prev(N_PAIR), prev(2 * N_PAIR), cur(0), cur(0), cur(0)] + c_in,
        out_specs=[cur(0)] * 3 + c_out,
        out_shape=[jax.ShapeDtypeStruct((s_len, D), BF16)] * 3 + c_shape,
        scratch_shapes=[buf(2 * ATT_BLK), buf(2 * ATT_BLK), buf(2 * ATT_BLK), buf(2 * ATT_BLK), buf(ATT_BLK),
                        buf(ATT_BLK)] + c_scratch,
        compiler_params=_cparams(("arbitrary", "arbitrary")),
    )(qkv, qkv, qkv, qkv, qkv, do, o, lse, *c_arrays)
    return res[0], res[1], res[2], res[3:]


def _adamw(w, g, m, v):
    m = ADAM_B1 * m + (1.0 - ADAM_B1) * g
    v = ADAM_B2 * v + (1.0 - ADAM_B2) * (g * g)
    m_hat = m / (1.0 - ADAM_B1 ** ADAM_STEP)
    v_hat = v / (1.0 - ADAM_B2 ** ADAM_STEP)
    delta = -ADAM_LR * (m_hat / (jnp.sqrt(v_hat) + ADAM_EPS) + ADAM_WD * w)
    return delta, m, v


def _adam_packed(name, recv, w, m, v, tm):
    n_rows, width = w.shape
    summed = recv.ndim == 3

    def kern(r_ref, w_ref, m_ref, v_ref, g_out, d_out, m_out, v_out):
        if summed:
            g = r_ref[0].astype(F32)
            for s in range(1, N_DEV):
                g = g + r_ref[s].astype(F32)
        else:
            g = r_ref[...]
        delta, mn, vn = _adamw(w_ref[...], g, m_ref[...], v_ref[...])
        g_out[...] = g
        d_out[...] = delta
        m_out[...] = mn
        v_out[...] = vn

    tile = pl.BlockSpec((tm, width), lambda i: (i, 0))
    rspec = pl.BlockSpec((N_DEV, tm, width), lambda i: (0, i, 0)) if summed else tile
    return pl.pallas_call(
        kern, name=name, grid=(n_rows // tm,), in_specs=[rspec, tile, tile, tile], out_specs=[tile] * 4,
        out_shape=[jax.ShapeDtypeStruct((n_rows, width), F32)] * 4, compiler_params=_cparams(("parallel",)),
    )(recv, w, m, v)


def _sum_slots(recv):
    _, n_rows, width = recv.shape

    def kern(r_ref, o_ref):
        g = r_ref[0]
        for s in range(1, N_DEV):
            g = g + r_ref[s]
        o_ref[...] = g

    return pl.pallas_call(
        kern, name="sum_small", grid=(1,), in_specs=[pl.BlockSpec(recv.shape, lambda i: (0, 0, 0))],
        out_specs=pl.BlockSpec((n_rows, width), lambda i: (0, 0)),
        out_shape=jax.ShapeDtypeStruct((n_rows, width), F32), compiler_params=_cparams(("arbitrary",)),
    )(recv)


def _adam_w_ada(cond_t, dmod_cols, w, m, v):
    n_rows, width = w.shape
    tm = ROW_TILE

    def kern(c_ref, d_ref, w_ref, m_ref, v_ref, g_out, d_out, m_out, v_out):
        g = c_ref[:, 0:1] * d_ref[0:1, :]
        for b in range(1, N_DEV):
            g = g + c_ref[:, b:b + 1] * d_ref[b:b + 1, :]
        delta, mn, vn = _adamw(w_ref[...], g, m_ref[...], v_ref[...])
        g_out[...] = g
        d_out[...] = delta
        m_out[...] = mn
        v_out[...] = vn

    tile = pl.BlockSpec((tm, width), lambda i: (i, 0))
    return pl.pallas_call(
        kern, name="adam_w_ada", grid=(n_rows // tm,),
        in_specs=[pl.BlockSpec((tm, N_DEV), lambda i: (i, 0)), pl.BlockSpec((N_DEV, width), lambda i: (0, 0)),
                  tile, tile, tile],
        out_specs=[tile] * 4, out_shape=[jax.ShapeDtypeStruct((n_rows, width), F32)] * 4,
        compiler_params=_cparams(("parallel",)),
    )(cond_t, dmod_cols, w, m, v)


def _rows(a, n_rows=None):
    flat = a.reshape(-1)
    need = (n_rows if n_rows is not None else -(-flat.shape[0] // D)) * D
    if need != flat.shape[0]:
        flat = jnp.concatenate([flat, jnp.zeros((need - flat.shape[0],), flat.dtype)])
    return flat.reshape(-1, D)


def _pack_shards(ws):
    return jnp.concatenate([ws[name][0].reshape(r, D) for name, r in PACK_ROWS], axis=0)


def _cols_to_slots(g, cols):
    k = g.shape[0]
    return g.reshape(k, N_DEV, cols).transpose(1, 0, 2).reshape(N_DEV, k * cols // D, D)


def _slots_to_cols(s, k, cols):
    return s.reshape(N_DEV, k, cols).transpose(1, 0, 2).reshape(k, N_DEV * cols)


def kernel(x, c, w_ada, b_ada, w_in, conv_w, conv_b, lru_wa, lru_ba, lru_wi, lru_bi, lru_lambda, w_proj_rnn, w_proj_attn, b_gate, w_out, ln1_g, ln1_b, w_ffn_gate, w_ffn_up, w_ffn_down, ln2_g, ln2_b, loss_target, m_w_ada, m_b_ada, m_w_in, m_conv_w, m_conv_b, m_lru_wa, m_lru_ba, m_lru_wi, m_lru_bi, m_lru_lambda, m_w_proj_rnn, m_w_proj_attn, m_b_gate, m_w_out, m_ln1_g, m_ln1_b, m_w_ffn_gate, m_w_ffn_up, m_w_ffn_down, m_ln2_g, m_ln2_b, v_w_ada, v_b_ada, v_w_in, v_conv_w, v_conv_b, v_lru_wa, v_lru_ba, v_lru_wi, v_lru_bi, v_lru_lambda, v_w_proj_rnn, v_w_proj_attn, v_b_gate, v_w_out, v_ln1_g, v_ln1_b, v_w_ffn_gate, v_w_ffn_up, v_w_ffn_down, v_ln2_g, v_ln2_b):
    s_len = x.shape[1]
    me = 4 * lax.axis_index("x") + 2 * lax.axis_index("y") + lax.axis_index("c")
    xs = x[0]
    tgt = loss_target[0]
    tm = ROW_TILE

    b_ada_cols = lax.dynamic_slice(b_ada, (0, me * 768), (1, 768))
    pad_cols = lambda a: jnp.concatenate([a, jnp.zeros((a.shape[0], D - a.shape[1]), F32)], axis=1)
    c8 = jnp.concatenate([c, pad_cols(conv_w[0]), pad_cols(b_gate[0]), jnp.zeros((1, D), F32)], axis=0)
    c_all, cond_blocks, mod_parts = _ada_modulation(c8, w_ada[0], b_ada_cols)
    cond_all = cond_blocks[:, 0, :]
    mod = mod_parts[:, 0, :].reshape(6, D)
    mod8 = jnp.concatenate([mod, jnp.zeros((2, D), F32)], axis=0)
    cw = c_all[:, 1:1 + CONV_W, :DR // N_DEV].transpose(1, 0, 2).reshape(CONV_W, DR)
    bg = c_all[:, 1 + CONV_W:3 + CONV_W, :D // N_DEV].transpose(1, 0, 2).reshape(2, D)
    bg8 = jnp.concatenate([bg, jnp.zeros((6, D), F32)], axis=0)

    w_in_f = _slots_to_cols(_gather_shards(w_in[0].reshape(960, D).astype(BF16)), D, 960)
    w_rnn, w_qkv, w_gates = w_in_f[:, :2 * DR], w_in_f[:, 2 * DR:2 * DR + 3 * D], w_in_f[:, 2 * DR + 3 * D:]
    late = dict(w_ffn_gate=w_ffn_gate, w_ffn_up=w_ffn_up, w_proj_rnn=w_proj_rnn, w_proj_attn=w_proj_attn,
                w_out=w_out, w_ffn_down=w_ffn_down)
    late_shard = jnp.concatenate([late[name][0].reshape(r, D) for name, r in LATE_ROWS], axis=0).astype(BF16)

    cb = conv_b
    wa_b, wi_b = lru_wa[0].astype(BF16), lru_wi[0].astype(BF16)
    wat_b, wit_b = jnp.swapaxes(wa_b, 1, 2), jnp.swapaxes(wi_b, 1, 2)
    ba, bi = lru_ba.reshape(1, DR), lru_bi.reshape(1, DR)
    sp = jax.nn.softplus(-lru_lambda)
    dsp_dlam = -jax.nn.sigmoid(-lru_lambda)
    ln1 = jnp.concatenate([ln1_g, ln1_b, jnp.zeros((6, D), F32)], axis=0)
    ln2 = jnp.concatenate([ln2_g, ln2_b, jnp.zeros((6, D), F32)], axis=0)

    def f1(ri, fi, ro, ao):
        n, _ = _ln(ri[0][...])
        ro[0][...] = (n * (1.0 + fi[0][1:2, :]) + fi[0][0:1, :]).astype(BF16)

    (h1,) = _rowwise(f1, "ln_mod1", s_len, tm, [(xs, D, 0)], [mod8], [(D, BF16)], [])
    proj_rnn = _mm("mm_in_rnn", [(h1, w_rnn)], F32, 512, DR)
    qkv = _mm("mm_in_qkv", [(h1, w_qkv)], F32, 512, D)
    gates = _mm("mm_in_gates", [(h1, w_gates)], F32, 512, D)

    hs, u_rnn, (late_all,) = _rnn_fwd(proj_rnn, cw, cb, wa_b, ba, wi_b, bi, sp, _Exchange([(late_shard, False)]))
    offs, o = {}, 0
    for name, r in LATE_ROWS:
        offs[name] = (o, o + r)
        o += r
    part = lambda name: late_all[:, offs[name][0]:offs[name][1], :]
    w_g, w_u = _slots_to_cols(part("w_ffn_gate"), D, 352), _slots_to_cols(part("w_ffn_up"), D, 352)
    w_pr = part("w_proj_rnn").reshape(DR, D)
    w_pa = part("w_proj_attn").reshape(D, D)
    w_o = part("w_out").reshape(D, D)
    w_dn = part("w_ffn_down").reshape(DFF, D)

    u_attn, lse = _attention_fwd(qkv)

    def e_merge(dots, ti, fi, ro, ao):
        g_r = _sigmoid(ti[0][...] + fi[0][0:1, :])
        g_a = _sigmoid(ti[1][...] + fi[0][1:2, :])
        ro[0][...] = dots[0].astype(BF16)
        ro[1][...] = dots[1].astype(BF16)
        ro[2][...] = (g_r * dots[0] + g_a * dots[1]).astype(BF16)

    pr, pa, merged = _mm_fused("mm_proj_merge", [(u_rnn, w_pr), (u_attn, w_pa)], [(gates, 0), (gates, 1)], [bg8],
                               [(BF16, 1), (BF16, 1), (BF16, 1)], [], e_merge, 512, D)
    f = _mm("mm_out", [(merged, w_o)], F32, 512, D)

    def f3(ri, fi, ro, ao):
        md, l1 = fi[0], fi[1]
        n1, _ = _ln(ALPHA * ri[0][...] + md[2:3, :] * ri[1][...])
        x1 = n1 * l1[0:1, :] + l1[1:2, :]
        n0, _ = _ln(x1)
        ro[0][...] = x1
        ro[1][...] = (n0 * (1.0 + md[4:5, :]) + md[3:4, :]).astype(BF16)

    x1, h2 = _rowwise(f3, "post1", s_len, tm, [(xs, D, 0), (f, D, 0)], [mod8, ln1], [(D, F32), (D, BF16)], [])
    def e_swiglu(dots, ti, fi, ro, ao):
        gp, up = dots
        ro[0][...] = gp.astype(BF16)
        ro[1][...] = up.astype(BF16)
        ro[2][...] = (gp * _sigmoid(gp) * up).astype(BF16)

    gpre, upre, act = _mm_fused("mm_ffn_in", [(h2, w_g), (h2, w_u)], [], [], [(BF16, 1)] * 3, [], e_swiglu, 512, DFF // 2)
    f2o = _mm("mm_ffn_out", [(act, w_dn)], F32, 512, D)

    def f5(ri, fi, ro, ao):
        md, l2 = fi[0], fi[1]
        f2v = ri[1][...]
        n2, rstd = _ln(ALPHA * ri[0][...] + md[5:6, :] * f2v)
        err = n2 * l2[0:1, :] + l2[1:2, :] - ri[2][...]
        dx2 = err * (1.0 / D)
        dz2 = _ln_bwd(dx2 * l2[0:1, :], n2, rstd)
        ro[0][...] = dz2
        ro[1][...] = (md[5:6, :] * dz2).astype(BF16)
        acc = ao[0]
        acc[0:1, :] += _rowsum(err * err) * (0.5 / D)
        acc[1:2, :] += _rowsum(dx2 * n2)
        acc[2:3, :] += _rowsum(dx2)
        acc[3:4, :] += _rowsum(dz2 * f2v)

    dz2, df2, acc5 = _rowwise(f5, "loss_post2", s_len, tm, [(x1, D, 0), (f2o, D, 0), (tgt, D, 0)], [mod8, ln2],
                              [(D, F32), (D, BF16)], [((8, D), F32)])

    d_w_dn = _mm_tn("mmt_ffn_down", act, df2, 1024, 512)

    def e_dswiglu(dots, ti, fi, ro, ao):
        da = dots[0]
        gp, up = ti[0][...].astype(F32), ti[1][...].astype(F32)
        sg = _sigmoid(gp)
        ro[0][...] = (da * up * sg * (1.0 + gp * (1.0 - sg))).astype(BF16)
        ro[1][...] = (da * gp * sg).astype(BF16)

    dgp, dup = _mm_fused("mm_d_ffn_out", [(df2, w_dn.T)], [(gpre, 0), (upre, 0)], [], [(BF16, 1)] * 2, [], e_dswiglu,
                         512, DFF // 2)
    dh2 = _mm("mm_d_ffn_in", [(dgp, w_g.T), (dup, w_u.T)], F32, 256, D)
    d_w_g = _mm_tn("mmt_ffn_gate", h2, dgp, 1024, DFF // 2)
    d_w_u = _mm_tn("mmt_ffn_up", h2, dup, 1024, DFF // 2)

    def b2(ri, fi, ro, ao):
        md, l1 = fi[0], fi[1]
        fv, dh2v = ri[1][...], ri[2][...]
        n1, rstd1 = _ln(ALPHA * ri[0][...] + md[2:3, :] * fv)
        n0, rstd0 = _ln(n1 * l1[0:1, :] + l1[1:2, :])
        dx1 = ALPHA * ri[3][...] + _ln_bwd(dh2v * (1.0 + md[4:5, :]), n0, rstd0)
        dz1 = _ln_bwd(dx1 * l1[0:1, :], n1, rstd1)
        ro[0][...] = ALPHA * dz1
        ro[1][...] = (md[2:3, :] * dz1).astype(BF16)
        acc = ao[0]
        acc[0:1, :] += _rowsum(dh2v * n0)
        acc[1:2, :] += _rowsum(dh2v)
        acc[2:3, :] += _rowsum(dx1 * n1)
        acc[3:4, :] += _rowsum(dx1)
        acc[4:5, :] += _rowsum(dz1 * fv)

    dxp, df, acc2 = _rowwise(b2, "d_post1", s_len, tm, [(xs, D, 0), (f, D, 0), (dh2, D, 0), (dz2, D, 0)], [mod8, ln1],
                             [(D, F32), (D, BF16)], [((8, D), F32)])
    d_w_o = _mm_tn("mmt_out", merged, df, 1024, 512)

    def e_dmerge(dots, ti, fi, ro, ao):
        dm = dots[0]
        g_r = _sigmoid(ti[0][...] + fi[0][0:1, :])
        g_a = _sigmoid(ti[1][...] + fi[0][1:2, :])
        ro[0][...] = (dm * g_r).astype(BF16)
        ro[1][...] = (dm * g_a).astype(BF16)
        dga = dm * ti[2][...].astype(F32) * g_r * (1.0 - g_r)
        dgb = dm * ti[3][...].astype(F32) * g_a * (1.0 - g_a)
        ro[2][:, 0:D] = dga.astype(BF16)
        ro[2][:, D:2 * D] = dgb.astype(BF16)
        ao[0][0:1, :] += _rowsum(dga)
        ao[0][1:2, :] += _rowsum(dgb)

    dpr, dpa, dgates, acc3 = _mm_fused("mm_d_out", [(df, w_o.T)], [(gates, 0), (gates, 1), (pr, 0), (pa, 0)], [bg8],
                                       [(BF16, 1), (BF16, 1), (BF16, 2)], [(8, D)], e_dmerge, 512, D)
    du_rnn = _mm("mm_d_proj_rnn", [(dpr, w_pr.T)], F32, 512, DR)
    du_attn = _mm("mm_d_proj_attn", [(dpa, w_pa.T)], F32, 512, D)
    d_w_pr = _mm_tn("mmt_proj_rnn", u_rnn, dpr, 1024, 512)
    d_w_pa = _mm_tn("mmt_proj_attn", u_attn, dpa, 1024, 512)

    drnn, d_wa, d_wi, vec = _rnn_bwd(proj_rnn, hs, du_rnn, cw, cb, wa_b, wat_b, ba, wi_b, wit_b, bi, sp, dsp_dlam)

    g_ffn = jnp.concatenate([_cols_to_slots(d_w_g, 352), _cols_to_slots(d_w_u, 352), d_w_dn.reshape(N_DEV, 352, D)],
                            axis=1).astype(BF16)
    g_proj = jnp.concatenate([d_w_pr.reshape(N_DEV, 160, D), d_w_pa.reshape(N_DEV, 128, D),
                              d_w_o.reshape(N_DEV, 128, D)], axis=1).astype(BF16)
    g_small = jnp.concatenate([
        _rows(vec[4], 2), d_wa.reshape(160, D), _rows(vec[5], 2), d_wi.reshape(160, D), _rows(vec[6], 2), _rows(vec[7], 2),
        acc2[2:4], acc5[1:3],
        _rows(vec[0:4], 5), acc3[0:2], acc5[0:1], jnp.zeros((4, D), F32)], axis=0)
    dq, dk, dv, (recv_ffn, recv_proj, recv_small) = _attention_bwd(
        qkv, du_attn, u_attn, lse, _Exchange([(g_ffn, True), (g_proj, True), (g_small, False)]))

    d_w_in = jnp.concatenate([_mm_tn("mmt_in_rnn", h1, drnn, 1024, DR), _mm_tn("mmt_in_q", h1, dq, 1024, D),
                              _mm_tn("mmt_in_k", h1, dk, 1024, D), _mm_tn("mmt_in_v", h1, dv, 1024, D),
                              _mm_tn("mmt_in_gates", h1, dgates, 1024, D)], axis=1)
    g_in = _cols_to_slots(d_w_in, 960).astype(BF16)
    w_qkv_t = w_qkv.T
    dh1, (recv_in,) = _mm("mm_d_in", [(drnn, w_rnn.T), (dq, w_qkv_t[:D]), (dk, w_qkv_t[D:2 * D]), (dv, w_qkv_t[2 * D:]),
                                      (dgates, w_gates.T)], F32, 256, 512, carry=_Exchange([(g_in, True)]))

    def b5(ri, fi, ro, ao):
        md = fi[0]
        n0, rstd0 = _ln(ri[0][...])
        dh = ri[1][...]
        ro[0][...] = ri[2][...] + _ln_bwd(dh * (1.0 + md[1:2, :]), n0, rstd0)
        ao[0][0:1, :] += _rowsum(dh * n0)
        ao[0][1:2, :] += _rowsum(dh)

    grad_x, acc1 = _rowwise(b5, "d_ln_mod1", s_len, tm, [(xs, D, 0), (dh1, D, 0), (dxp, D, 0)], [mod8], [(D, F32)],
                            [((8, D), F32)])

    dmod = jnp.concatenate([acc1[1:2], acc1[0:1], acc2[4:5], acc2[1:2], acc2[0:1], acc5[3:4], jnp.zeros((2, D), F32)],
                           axis=0)
    (recv_dmod,) = _exchange_alone("gather_dmod", _Exchange([(dmod, False)]))
    dmod_all = recv_dmod[:, 0:6, :].reshape(N_DEV, 6 * D)
    red = _sum_slots(recv_small)
    loss = jnp.sum(red[339])

    wmv = (dict(w_in=w_in, w_ffn_gate=w_ffn_gate, w_ffn_up=w_ffn_up, w_proj_rnn=w_proj_rnn, w_proj_attn=w_proj_attn,
                w_out=w_out, w_ffn_down=w_ffn_down),
           dict(w_in=m_w_in, w_ffn_gate=m_w_ffn_gate, w_ffn_up=m_w_ffn_up, w_proj_rnn=m_w_proj_rnn,
                w_proj_attn=m_w_proj_attn, w_out=m_w_out, w_ffn_down=m_w_ffn_down),
           dict(w_in=v_w_in, w_ffn_gate=v_w_ffn_gate, w_ffn_up=v_w_ffn_up, w_proj_rnn=v_w_proj_rnn,
                w_proj_attn=v_w_proj_attn, w_out=v_w_out, w_ffn_down=v_w_ffn_down))
    pack = lambda ps, rows: jnp.concatenate([ps[name][0].reshape(r, D) for name, r in rows], axis=0)
    big_groups = ((PACK_ROWS[:1], recv_in, "adam_in", 96), (FFN_ROWS, recv_ffn, "adam_ffn", 96),
                  (PROJ_ROWS, recv_proj, "adam_proj", 208))
    big_out = {}
    for rows, recv, name, tile in big_groups:
        res = _adam_packed(name, recv, pack(wmv[0], rows), pack(wmv[1], rows), pack(wmv[2], rows), tile)
        o = 0
        for wname, r in rows:
            big_out[wname] = [t[o:o + r] for t in res]
            o += r
    ada_out = _adam_w_ada(cond_all.T, lax.dynamic_slice(dmod_all, (0, me * 768), (N_DEV, 768)),
                          w_ada[0], m_w_ada[0], v_w_ada[0])

    small_spec = (("b_ada", 6), ("conv_b", 2), ("lru_wa", 160), ("lru_ba", 2), ("lru_wi", 160), ("lru_bi", 2),
                  ("lru_lambda", 2), ("ln1_g", 1), ("ln1_b", 1), ("ln2_g", 1), ("ln2_b", 1), ("conv_w", 1), ("b_gate", 1))
    small_rows = 344

    def pack_small(ps):
        return jnp.concatenate([_rows(ps[name], r) for name, r in small_spec] + [jnp.zeros((4, D), F32)], axis=0)

    names = [n for n, _ in small_spec]
    smalls = [pack_small(dict(zip(names, t))) for t in (
        (b_ada, conv_b, lru_wa, lru_ba, lru_wi, lru_bi, lru_lambda, ln1_g, ln1_b, ln2_g, ln2_b, conv_w, b_gate),
        (m_b_ada, m_conv_b, m_lru_wa, m_lru_ba, m_lru_wi, m_lru_bi, m_lru_lambda, m_ln1_g, m_ln1_b, m_ln2_g, m_ln2_b,
         m_conv_w, m_b_gate),
        (v_b_ada, v_conv_b, v_lru_wa, v_lru_ba, v_lru_wi, v_lru_bi, v_lru_lambda, v_ln1_g, v_ln1_b, v_ln2_g, v_ln2_b,
         v_conv_w, v_b_gate))]
    g_conv_w = lax.dynamic_slice(red[332:337].reshape(-1)[:CONV_W * DR].reshape(CONV_W, DR), (0, me * 160), (CONV_W, 160))
    g_b_gate = lax.dynamic_slice(red[337:339], (0, me * 128), (2, 128))
    g_small_pack = jnp.concatenate([jnp.sum(dmod_all, axis=0).reshape(6, D), red[0:332], _rows(g_conv_w, 1),
                                    _rows(g_b_gate, 1), jnp.zeros((4, D), F32)], axis=0)
    assert g_small_pack.shape == (small_rows, D) and smalls[0].shape == (small_rows, D)
    small_out = _adam_packed("adam_small", g_small_pack, smalls[0], smalls[1], smalls[2], small_rows)

    shapes = dict(w_ada=w_ada.shape, b_ada=b_ada.shape, w_in=w_in.shape, conv_w=conv_w.shape, conv_b=conv_b.shape,
                  lru_wa=lru_wa.shape, lru_ba=lru_ba.shape, lru_wi=lru_wi.shape, lru_bi=lru_bi.shape,
                  lru_lambda=lru_lambda.shape, w_proj_rnn=w_proj_rnn.shape, w_proj_attn=w_proj_attn.shape,
                  b_gate=b_gate.shape, w_out=w_out.shape, ln1_g=ln1_g.shape, ln1_b=ln1_b.shape,
                  w_ffn_gate=w_ffn_gate.shape, w_ffn_up=w_ffn_up.shape, w_ffn_down=w_ffn_down.shape,
                  ln2_g=ln2_g.shape, ln2_b=ln2_b.shape)

    def unpack(kind):
        out = {"w_ada": ada_out[kind].reshape(shapes["w_ada"])}
        for name, _ in PACK_ROWS:
            out[name] = big_out[name][kind].reshape(shapes[name])
        o = 0
        for name, r in small_spec:
            size = 1
            for dim in shapes[name]:
                size *= dim
            out[name] = small_out[kind][o:o + r].reshape(-1)[:size].reshape(shapes[name])
            o += r
        return [out[name] for name in shapes]

    return (loss, grad_x[None], *unpack(0), *unpack(1), *unpack(2), *unpack(3))
```

```python
import functools

import jax
import jax.numpy as jnp
from jax import lax
from jax.experimental import pallas as pl
from jax.experimental.pallas import tpu as pltpu

F32 = jnp.float32
BF16 = jnp.bfloat16
MESH = pl.DeviceIdType.MESH

N_DEV = 8
D = 1024
DR = 1280
N_BLK = 10
BW = 128
HEAD_PAIR = 128
N_PAIR = 8
QB = 128
DFF = 2816
CONV_W = 4
DILATIONS = (1, 4, 16)
ALPHA = 2.0 ** 0.25
LN_EPS = 1e-5
LRU_C = 8.0
SCALE = 0.125
NEG = -1e30
ADAM_LR, ADAM_B1, ADAM_B2, ADAM_EPS, ADAM_WD, ADAM_STEP = 0.001, 0.9, 0.999, 1e-08, 0.01, 10

VMEM_LIMIT = 48 * 1024 * 1024
ROW_TILE = 256
RNN_CHUNK = 256

PACK_ROWS = (("w_in", 960), ("w_ffn_gate", 352), ("w_ffn_up", 352), ("w_proj_rnn", 160),
             ("w_proj_attn", 128), ("w_out", 128), ("w_ffn_down", 352))
LATE_ROWS = PACK_ROWS[1:]
FFN_ROWS = (("w_ffn_gate", 352), ("w_ffn_up", 352), ("w_ffn_down", 352))
PROJ_ROWS = (("w_proj_rnn", 160), ("w_proj_attn", 128), ("w_out", 128))


def _cparams(sem):
    return pltpu.CompilerParams(dimension_semantics=sem, vmem_limit_bytes=VMEM_LIMIT)


def _ln(z):
    mu = jnp.mean(z, axis=-1, keepdims=True)
    zc = z - mu
    var = jnp.mean(zc * zc, axis=-1, keepdims=True)
    rstd = lax.rsqrt(var + LN_EPS)
    return zc * rstd, rstd


def _ln_bwd(dn, n, rstd):
    return rstd * (dn - jnp.mean(dn, axis=-1, keepdims=True) - n * jnp.mean(dn * n, axis=-1, keepdims=True))


def _sigmoid(x):
    return 1.0 / (1.0 + jnp.exp(-x))


_GELU_K = 0.7978845608028654
_GELU_C = 0.044715


def _gelu(y):
    t = jnp.tanh(_GELU_K * (y + _GELU_C * y * y * y))
    return 0.5 * y * (1.0 + t), t


def _gelu_grad(y, t):
    return 0.5 * (1.0 + t) + 0.5 * y * (1.0 - t * t) * _GELU_K * (1.0 + 3.0 * _GELU_C * y * y)


def _rowsum(v):
    return jnp.sum(v, axis=0, keepdims=True)


def _rowwise(body, name, n_rows, tm, row_ins, full_ins, row_outs, acc_outs, carry=None):
    nri, nfi, nro, nao = len(row_ins), len(full_ins), len(row_outs), len(acc_outs)
    n_c = len(carry.arrays) if carry else 0
    n_in = nri + nfi + n_c

    def kern(*refs):
        ri, fi = refs[:nri], refs[nri:nri + nfi]
        ro, ao = refs[n_in:n_in + nro], refs[n_in + nro:n_in + nro + nao]
        if carry:
            i = pl.program_id(0)
            _carry(carry, refs[nri + nfi:n_in], refs[n_in + nro + nao:n_in + nro + nao + n_c], refs[-1], i == 0,
                   i == n_rows // tm - 1)
        if ao:
            @pl.when(pl.program_id(0) == 0)
            def _():
                for a in ao:
                    a[...] = jnp.zeros(a.shape, a.dtype)
        body(ri, fi, ro, ao)

    in_specs = [pl.BlockSpec((tm, w), functools.partial(lambda i, cb: (i, cb), cb=cb)) for _, w, cb in row_ins]
    in_specs += [pl.BlockSpec(a.shape, lambda i: (0, 0)) for a in full_ins]
    out_specs = [pl.BlockSpec((tm, w), lambda i: (i, 0)) for w, _ in row_outs]
    out_specs += [pl.BlockSpec(s, lambda i: (0, 0)) for s, _ in acc_outs]
    out_shape = [jax.ShapeDtypeStruct((n_rows, w), dt) for w, dt in row_outs]
    out_shape += [jax.ShapeDtypeStruct(s, dt) for s, dt in acc_outs]
    operands = [a for a, _, _ in row_ins] + list(full_ins)
    scratch = []
    if carry:
        in_specs, out_specs, out_shape = in_specs + carry.in_specs, out_specs + carry.out_specs, out_shape + carry.out_shape
        operands, scratch = operands + carry.arrays, carry.scratch
    return pl.pallas_call(
        kern, name=name, grid=(n_rows // tm,), in_specs=in_specs, out_specs=out_specs, out_shape=out_shape,
        scratch_shapes=scratch, compiler_params=_cparams(("arbitrary",)),
    )(*operands)


NT_DIMS = (((1,), (1,)), ((), ()))


def _pair_cols(pair):
    return pair[1].shape[0] if len(pair) == 3 else pair[1].shape[1]


def _pair_specs(pairs, tm, tn):
    in_specs, flat = [], []
    for pair in pairs:
        a, w = pair[0], pair[1]
        k = a.shape[1]
        in_specs.append(pl.BlockSpec((tm, k), lambda j, i: (i, 0)))
        if len(pair) == 3:
            in_specs.append(pl.BlockSpec((tn, k), lambda j, i: (j, 0)))
        else:
            in_specs.append(pl.BlockSpec((k, tn), lambda j, i: (0, j)))
        flat += [a, w]
    return in_specs, flat


def _pair_dot(pair, a_ref, w_ref):
    if len(pair) == 3:
        return lax.dot_general(a_ref[...], w_ref[...], NT_DIMS, preferred_element_type=F32)
    return jnp.dot(a_ref[...], w_ref[...], preferred_element_type=F32)


def _mm(name, pairs, out_dtype, tm, tn, carry=None):
    m_rows, n_cols = pairs[0][0].shape[0], _pair_cols(pairs[0])
    n_pairs = len(pairs)
    n_c = len(carry.arrays) if carry else 0
    grid = (n_cols // tn, m_rows // tm)

    def kern(*refs):
        o_ref = refs[2 * n_pairs + n_c]
        if carry:
            step = pl.program_id(0) * grid[1] + pl.program_id(1)
            _carry(carry, refs[2 * n_pairs:2 * n_pairs + n_c], refs[2 * n_pairs + n_c + 1:2 * n_pairs + 2 * n_c + 1],
                   refs[-1], step == 0, step == grid[0] * grid[1] - 1)
        acc = None
        for p in range(n_pairs):
            t = _pair_dot(pairs[p], refs[2 * p], refs[2 * p + 1])
            acc = t if acc is None else acc + t
        o_ref[...] = acc.astype(o_ref.dtype)

    in_specs, flat = _pair_specs(pairs, tm, tn)
    out_specs = [pl.BlockSpec((tm, tn), lambda j, i: (i, j))]
    out_shape = [jax.ShapeDtypeStruct((m_rows, n_cols), out_dtype)]
    if carry:
        res = pl.pallas_call(
            kern, name=name, grid=grid, in_specs=in_specs + carry.in_specs, out_specs=out_specs + carry.out_specs,
            out_shape=out_shape + carry.out_shape, scratch_shapes=carry.scratch,
            compiler_params=_cparams(("arbitrary", "arbitrary")),
        )(*flat, *carry.arrays)
        return res[0], res[1:]
    return pl.pallas_call(
        kern, name=name, grid=grid, in_specs=in_specs, out_specs=out_specs[0], out_shape=out_shape[0],
        compiler_params=_cparams(("parallel", "parallel")),
    )(*flat)


def _mm_tn(name, a, g, tm, tn):
    m_rows, k = a.shape
    n_cols = g.shape[1]

    def kern(a_ref, g_ref, o_ref):
        @pl.when(pl.program_id(1) == 0)
        def _():
            o_ref[...] = jnp.zeros(o_ref.shape, F32)
        o_ref[...] += lax.dot_general(a_ref[...], g_ref[...], (((0,), (0,)), ((), ())), preferred_element_type=F32)

    return pl.pallas_call(
        kern, name=name, grid=(n_cols // tn, m_rows // tm),
        in_specs=[pl.BlockSpec((tm, k), lambda j, m: (m, 0)), pl.BlockSpec((tm, tn), lambda j, m: (m, j))],
        out_specs=pl.BlockSpec((k, tn), lambda j, m: (0, j)),
        out_shape=jax.ShapeDtypeStruct((k, n_cols), F32),
        compiler_params=_cparams(("parallel", "arbitrary")),
    )(a, g)


def _mm_fused(name, pairs, tile_ins, full_ins, outs, accs, epilogue, tm, tn):
    m_rows, n_cols = pairs[0][0].shape[0], _pair_cols(pairs[0])
    n_p, n_t, n_f, n_o = len(pairs), len(tile_ins), len(full_ins), len(outs)

    def kern(*refs):
        dots = [_pair_dot(pairs[p], refs[2 * p], refs[2 * p + 1]) for p in range(n_p)]
        base = 2 * n_p
        t_refs, f_refs = refs[base:base + n_t], refs[base + n_t:base + n_t + n_f]
        o_refs, a_refs = refs[base + n_t + n_f:base + n_t + n_f + n_o], refs[base + n_t + n_f + n_o:]
        if a_refs:
            @pl.when((pl.program_id(0) == 0) & (pl.program_id(1) == 0))
            def _():
                for a in a_refs:
                    a[...] = jnp.zeros(a.shape, F32)
        epilogue(dots, t_refs, f_refs, o_refs, a_refs)

    in_specs, flat = _pair_specs(pairs, tm, tn)
    for arr, cb in tile_ins:
        in_specs.append(pl.BlockSpec((tm, tn), functools.partial(lambda j, i, cb: (i, cb + j), cb=cb)))
        flat.append(arr)
    for arr in full_ins:
        in_specs.append(pl.BlockSpec(arr.shape, lambda j, i: (0, 0)))
        flat.append(arr)
    out_specs = [pl.BlockSpec((tm, f * tn), lambda j, i: (i, j)) for _, f in outs]
    out_specs += [pl.BlockSpec(s, lambda j, i: (0, 0)) for s in accs]
    out_shape = [jax.ShapeDtypeStruct((m_rows, f * n_cols), dt) for dt, f in outs]
    out_shape += [jax.ShapeDtypeStruct(s, F32) for s in accs]
    sem = ("arbitrary", "arbitrary") if accs else ("parallel", "parallel")
    return pl.pallas_call(
        kern, name=name, grid=(n_cols // tn, m_rows // tm), in_specs=in_specs, out_specs=out_specs,
        out_shape=out_shape, compiler_params=_cparams(sem),
    )(*flat)


def _peers():
    x, y, c = lax.axis_index("x"), lax.axis_index("y"), lax.axis_index("c")
    me = 4 * x + 2 * y + c
    peers = []
    for k in range(1, N_DEV):
        px = 1 - x if (k >> 2) & 1 else x
        py = 1 - y if (k >> 1) & 1 else y
        pc = 1 - c if k & 1 else c
        peers.append(((px, py, pc), 4 * px + 2 * py + pc))
    return me, peers


def _exchange(me, peers, items, send_sems, recv_sems, t0=0):
    started = []
    for t, (src_of, dst_of) in enumerate(items, start=t0):
        for k, (pid, plin) in enumerate(peers):
            cp = pltpu.make_async_remote_copy(
                src_ref=src_of(plin), dst_ref=dst_of(me), send_sem=send_sems.at[t, k], recv_sem=recv_sems.at[t, k],
                device_id=pid, device_id_type=MESH)
            cp.start()
            started.append(cp)
    for t, (src_of, dst_of) in enumerate(items, start=t0):
        for k, (pid, plin) in enumerate(peers):
            pltpu.make_async_remote_copy(
                src_ref=src_of(plin), dst_ref=dst_of(plin), send_sem=send_sems.at[t, k], recv_sem=recv_sems.at[t, k],
                device_id=pid, device_id_type=MESH).wait_recv()
    for cp in started:
        cp.wait_send()


def _hbm_spec():
    return pl.BlockSpec(memory_space=pltpu.HBM)


class _ChipGather:
    def __init__(self, shard):
        self.arrays = [shard]
        self.out_shape = [jax.ShapeDtypeStruct((N_DEV,) + shard.shape, shard.dtype)]
        self.in_specs, self.out_specs = [_hbm_spec()], [_hbm_spec()]
        self.scratch = [pltpu.SemaphoreType.DMA((SEMS_PER_ITEM,))]

    def _parts(self, ins, outs, sems):
        src, out = ins[0], outs[0]
        x, y, c = lax.axis_index("x"), lax.axis_index("y"), lax.axis_index("c")
        lin = lambda px, py, pc: 4 * px + 2 * py + pc
        sibling = (x, y, 1 - c)
        chips = [(1 - x, y), (x, 1 - y), (1 - x, 1 - y)]

        def copy(k, slot, to, from_src):
            return pltpu.make_async_remote_copy(
                src_ref=src if from_src else out.at[slot], dst_ref=out.at[slot], send_sem=sems.at[k],
                recv_sem=sems.at[N_DEV - 1 + k], device_id=to, device_id_type=MESH)

        own = pltpu.make_async_copy(src, out.at[lin(x, y, c)], sems.at[2 * (N_DEV - 1)])
        first = [copy(0, lin(x, y, c), sibling, True)]
        first += [copy(1 + j, lin(x, y, c), (px, py, c), True) for j, (px, py) in enumerate(chips)]
        passed = [copy(4 + j, lin(px, py, c), sibling, False) for j, (px, py) in enumerate(chips)]
        landed = [copy(1 + j, lin(px, py, c), sibling, True) for j, (px, py) in enumerate(chips)]
        from_sibling = [copy(0, lin(x, y, 1 - c), sibling, True)]
        from_sibling += [copy(4 + j, lin(px, py, 1 - c), sibling, False) for j, (px, py) in enumerate(chips)]
        return own, first, passed, landed, from_sibling

    def start(self, ins, outs, sems):
        own, first, _, _, _ = self._parts(ins, outs, sems)
        own.start()
        for cp in first:
            cp.start()

    def wait(self, ins, outs, sems):
        own, first, passed, landed, from_sibling = self._parts(ins, outs, sems)
        for arrival, forward in zip(landed, passed):
            arrival.wait_recv()
            forward.start()
        for cp in from_sibling:
            cp.wait_recv()
        for cp in first + passed:
            cp.wait_send()
        own.wait()


SEMS_PER_ITEM = 2 * (N_DEV - 1) + 1


class _Exchange:
    def __init__(self, items):
        self.arrays = [a for a, _ in items]
        self.scatter = [s for _, s in items]
        self.out_shape = [jax.ShapeDtypeStruct(a.shape if s else (N_DEV,) + a.shape, a.dtype) for a, s in items]
        self.in_specs = [_hbm_spec() for _ in items]
        self.out_specs = [_hbm_spec() for _ in items]
        self.scratch = [pltpu.SemaphoreType.DMA((SEMS_PER_ITEM * len(items),))]

    def _copies(self, ins, outs, sems, landing):
        me, peers = _peers()
        cps = []
        for t, scatter in enumerate(self.scatter):
            base = t * SEMS_PER_ITEM
            for k, (pid, plin) in enumerate(peers):
                src = ins[t].at[plin] if scatter else ins[t]
                dst = outs[t].at[plin if landing else me]
                cps.append(pltpu.make_async_remote_copy(
                    src_ref=src, dst_ref=dst, send_sem=sems.at[base + k], recv_sem=sems.at[base + N_DEV - 1 + k],
                    device_id=pid, device_id_type=MESH))
        return cps

    def _own(self, ins, outs, sems):
        me, _ = _peers()
        return [pltpu.make_async_copy(ins[t].at[me] if scatter else ins[t], outs[t].at[me],
                                      sems.at[t * SEMS_PER_ITEM + 2 * (N_DEV - 1)])
                for t, scatter in enumerate(self.scatter)]

    def start(self, ins, outs, sems):
        for cp in self._own(ins, outs, sems) + self._copies(ins, outs, sems, False):
            cp.start()

    def wait(self, ins, outs, sems):
        for cp in self._copies(ins, outs, sems, True):
            cp.wait_recv()
        for cp in self._copies(ins, outs, sems, False):
            cp.wait_send()
        for cp in self._own(ins, outs, sems):
            cp.wait()


def _carry(exchange, ins, outs, sems, first, last):
    @pl.when(first)
    def _():
        exchange.start(ins, outs, sems)

    @pl.when(last)
    def _():
        exchange.wait(ins, outs, sems)


def _exchange_alone(name, exchange):
    n = len(exchange.arrays)

    def body(*refs):
        exchange.start(refs[:n], refs[n:2 * n], refs[2 * n])
        exchange.wait(refs[:n], refs[n:2 * n], refs[2 * n])

    return pl.pallas_call(body, name=name, in_specs=exchange.in_specs, out_specs=exchange.out_specs,
                          out_shape=exchange.out_shape, scratch_shapes=exchange.scratch)(*exchange.arrays)


def _ada_modulation(c8, w_ada, b_ada_cols):
    wcols = w_ada.shape[1]

    def body(c_ref, w_ref, b_ref, call_ref, cond_ref, mod_ref, res, send_sems, recv_sems):
        me, peers = _peers()
        call_ref[me] = c_ref[...]
        _exchange(me, peers, [(lambda p: c_ref, lambda s: call_ref.at[s])], send_sems, recv_sems, t0=0)
        for dev in range(N_DEV):
            cv = call_ref[dev]
            cond = cv * _sigmoid(cv)
            cond_ref[dev] = cond
            res[dev] = jnp.dot(cond, w_ref[...], preferred_element_type=F32,
                               precision=lax.Precision.HIGHEST) + b_ref[...]
        mod_ref[me] = res[me]
        _exchange(me, peers, [(lambda p: res.at[p], lambda s: mod_ref.at[s])], send_sems, recv_sems, t0=1)

    vm = pl.BlockSpec(memory_space=pltpu.VMEM)
    return pl.pallas_call(
        body, name="ada_modulation", in_specs=[vm, vm, vm], out_specs=[vm, vm, vm],
        out_shape=[jax.ShapeDtypeStruct((N_DEV, 8, D), F32), jax.ShapeDtypeStruct((N_DEV, 8, D), F32),
                   jax.ShapeDtypeStruct((N_DEV, 8, wcols), F32)],
        scratch_shapes=[pltpu.VMEM((N_DEV, 8, wcols), F32),
                        pltpu.SemaphoreType.DMA((2, N_DEV - 1)), pltpu.SemaphoreType.DMA((2, N_DEV - 1))],
        compiler_params=pltpu.CompilerParams(vmem_limit_bytes=VMEM_LIMIT),
    )(c8, w_ada, b_ada_cols)


def _lru_gates(xc, wa_ref, ba_ref, wi_ref, bi_ref, sp_ref, row0):
    xcb = xc.astype(BF16)
    pre_r, pre_i = [], []
    for n in range(N_BLK):
        xb = xcb[:, n * BW:(n + 1) * BW]
        pre_r.append(jnp.dot(xb, wa_ref[n], preferred_element_type=F32))
        pre_i.append(jnp.dot(xb, wi_ref[n], preferred_element_type=F32))
    r = _sigmoid(jnp.concatenate(pre_r, axis=1) + ba_ref[...])
    ig = _sigmoid(jnp.concatenate(pre_i, axis=1) + bi_ref[...])
    log_a = (-LRU_C) * r * sp_ref[...]
    a = jnp.exp(log_a)
    e2 = jnp.exp(2.0 * log_a)
    mult_raw = jnp.sqrt(1.0 - e2)
    rows = row0 + lax.broadcasted_iota(jnp.int32, xc.shape, 0)
    start = rows == 0
    mult = jnp.where(start, 1.0, mult_raw)
    return xcb, r, ig, a, e2, mult_raw, mult, start


def _conv(xpad, cw_ref, cb_ref, tc):
    out = cb_ref[...]
    for tap in range(CONV_W):
        out = out + cw_ref[tap:tap + 1, :] * xpad[pl.ds(8 - (CONV_W - 1) + tap, tc), :]
    return out


def _rnn_fwd(proj_rnn, cw, cb, wa, ba, wi, bi, sp, carry):
    s_len = proj_rnn.shape[0]
    tc = RNN_CHUNK
    n_c = len(carry.arrays)

    def kern(*refs):
        y_ref, xr_ref, cw_ref, cb_ref, wa_ref, ba_ref, wi_ref, bi_ref, sp_ref = refs[:9]
        hs_ref, ur_ref = refs[9 + n_c:11 + n_c]
        xpad, a_scr, u_scr, h_scr = refs[11 + 2 * n_c:15 + 2 * n_c]
        i = pl.program_id(0)
        _carry(carry, refs[9:9 + n_c], refs[11 + n_c:11 + 2 * n_c], refs[-1], i == 0, i == s_len // tc - 1)

        @pl.when(i == 0)
        def _():
            xpad[0:8, :] = jnp.zeros((8, DR), F32)
            h_scr[...] = jnp.zeros((1, DR), F32)

        xpad[8:8 + tc, :] = xr_ref[...]
        xc = _conv(xpad, cw_ref, cb_ref, tc)
        xpad[0:8, :] = xpad[tc:tc + 8, :]
        _, r, ig, a, e2, mult_raw, mult, start = _lru_gates(xc, wa_ref, ba_ref, wi_ref, bi_ref, sp_ref, i * tc)
        a_scr[...] = a
        u_scr[...] = mult * (ig * xc)

        def step(t, h):
            h = a_scr[pl.ds(t, 1), :] * h + u_scr[pl.ds(t, 1), :]
            hs_ref[pl.ds(t, 1), :] = h
            return h

        h_scr[...] = lax.fori_loop(0, tc, step, h_scr[...], unroll=8)
        gy, _ = _gelu(y_ref[...])
        ur_ref[...] = (gy * hs_ref[...]).astype(BF16)

    full = lambda a: pl.BlockSpec(a.shape, lambda i: (0,) * a.ndim)
    res = pl.pallas_call(
        kern, name="rnn_fwd", grid=(s_len // tc,),
        in_specs=[pl.BlockSpec((tc, DR), lambda i: (i, 0)), pl.BlockSpec((tc, DR), lambda i: (i, 1)),
                  full(cw), full(cb), full(wa), full(ba), full(wi), full(bi), full(sp)] + carry.in_specs,
        out_specs=[pl.BlockSpec((tc, DR), lambda i: (i, 0)), pl.BlockSpec((tc, DR), lambda i: (i, 0))] + carry.out_specs,
        out_shape=[jax.ShapeDtypeStruct((s_len, DR), F32), jax.ShapeDtypeStruct((s_len, DR), BF16)] + carry.out_shape,
        scratch_shapes=[pltpu.VMEM((tc + 8, DR), F32), pltpu.VMEM((tc, DR), F32), pltpu.VMEM((tc, DR), F32),
                        pltpu.VMEM((1, DR), F32)] + carry.scratch,
        compiler_params=_cparams(("arbitrary",)),
    )(proj_rnn, proj_rnn, cw, cb, wa, ba, wi, bi, sp, *carry.arrays)
    return res[0], res[1], res[2:]


def _rnn_bwd(proj_rnn, hs, du, cw, cb, wa, wat, ba, wi, wit, bi, sp, dsp_dlam):
    s_len = proj_rnn.shape[0]
    tc = RNN_CHUNK
    nch = s_len // tc

    def kern(y_ref, xr_ref, xrp_ref, hs_ref, hsp_ref, du_ref, cw_ref, cb_ref, wa_ref, wat_ref, ba_ref, wi_ref,
             wit_ref, bi_ref, sp_ref, dspl_ref, drnn_ref, dwa_ref, dwi_ref, vec_ref,
             xpad, hpad, a_scr, d_scr, g_scr, dxcpad, ag_scr):
        j = pl.program_id(0)
        n = nch - 1 - j

        @pl.when(j == 0)
        def _():
            dwa_ref[...] = jnp.zeros(dwa_ref.shape, F32)
            dwi_ref[...] = jnp.zeros(dwi_ref.shape, F32)
            vec_ref[...] = jnp.zeros(vec_ref.shape, F32)
            ag_scr[...] = jnp.zeros((1, DR), F32)
            dxcpad[tc:tc + 8, :] = jnp.zeros((8, DR), F32)

        has_prev = n > 0
        xpad[0:8, :] = jnp.where(has_prev, xrp_ref[...], 0.0)
        xpad[8:8 + tc, :] = xr_ref[...]
        hpad[0:8, :] = jnp.where(has_prev, hsp_ref[...], 0.0)
        hpad[8:8 + tc, :] = hs_ref[...]
        xc = _conv(xpad, cw_ref, cb_ref, tc)
        xcb, r, ig, a, e2, mult_raw, mult, start = _lru_gates(xc, wa_ref, ba_ref, wi_ref, bi_ref, sp_ref, n * tc)

        y = y_ref[...]
        gy, th = _gelu(y)
        duv = du_ref[...].astype(F32)
        drnn_ref[:, 0:DR] = (duv * hs_ref[...] * _gelu_grad(y, th)).astype(BF16)
        a_scr[...] = a
        d_scr[...] = duv * gy

        def step(tt, ag):
            t = tc - 1 - tt
            g = d_scr[pl.ds(t, 1), :] + ag
            g_scr[pl.ds(t, 1), :] = g
            return a_scr[pl.ds(t, 1), :] * g

        ag_scr[...] = lax.fori_loop(0, tc, step, ag_scr[...], unroll=8)
        g = g_scr[...]
        da = g * hpad[pl.ds(7, tc), :]
        gx = g * xc
        dmult = jnp.where(start, 0.0, gx * ig)
        di = gx * mult
        dxc = g * mult * ig
        dlog_a = da * a - jnp.where(start, 0.0, dmult * e2 / mult_raw)
        dr = dlog_a * ((-LRU_C) * sp_ref[...])
        vec_ref[7:8, :] += _rowsum(dlog_a * ((-LRU_C) * r))
        dpr = dr * r * (1.0 - r)
        dpi = di * ig * (1.0 - ig)
        vec_ref[5:6, :] += _rowsum(dpr)
        vec_ref[6:7, :] += _rowsum(dpi)
        dprb, dpib = dpr.astype(BF16), dpi.astype(BF16)
        extra = []
        for b in range(N_BLK):
            sl = slice(b * BW, (b + 1) * BW)
            extra.append(jnp.dot(dprb[:, sl], wat_ref[b], preferred_element_type=F32)
                         + jnp.dot(dpib[:, sl], wit_ref[b], preferred_element_type=F32))
            dn = (((0,), (0,)), ((), ()))
            dwa_ref[b] += lax.dot_general(xcb[:, sl], dprb[:, sl], dn, preferred_element_type=F32)
            dwi_ref[b] += lax.dot_general(xcb[:, sl], dpib[:, sl], dn, preferred_element_type=F32)
        dxc = dxc + jnp.concatenate(extra, axis=1)
        vec_ref[4:5, :] += _rowsum(dxc)
        dxcpad[0:tc, :] = dxc
        dxr = jnp.zeros((tc, DR), F32)
        for tap in range(CONV_W):
            shift = CONV_W - 1 - tap
            dxr = dxr + cw_ref[tap:tap + 1, :] * dxcpad[pl.ds(shift, tc), :]
            vec_ref[tap:tap + 1, :] += _rowsum(dxc * xpad[pl.ds(8 - shift, tc), :])
        dxcpad[tc:tc + 8, :] = dxcpad[0:8, :]
        drnn_ref[:, DR:2 * DR] = dxr.astype(BF16)

        @pl.when(j == nch - 1)
        def _():
            vec_ref[7:8, :] = vec_ref[7:8, :] * dspl_ref[...]

    full = lambda a: pl.BlockSpec(a.shape, lambda j: (0,) * a.ndim)
    rev = lambda j: nch - 1 - j
    prev8 = lambda j: jnp.maximum((nch - 1 - j) * (tc // 8) - 1, 0)
    return pl.pallas_call(
        kern, name="rnn_bwd", grid=(nch,),
        in_specs=[pl.BlockSpec((tc, DR), lambda j: (rev(j), 0)), pl.BlockSpec((tc, DR), lambda j: (rev(j), 1)),
                  pl.BlockSpec((8, DR), lambda j: (prev8(j), 1)),
                  pl.BlockSpec((tc, DR), lambda j: (rev(j), 0)), pl.BlockSpec((8, DR), lambda j: (prev8(j), 0)),
                  pl.BlockSpec((tc, DR), lambda j: (rev(j), 0)),
                  full(cw), full(cb), full(wa), full(wat), full(ba), full(wi), full(wit), full(bi), full(sp),
                  full(dsp_dlam)],
        out_specs=[pl.BlockSpec((tc, 2 * DR), lambda j: (rev(j), 0)),
                   pl.BlockSpec((N_BLK, BW, BW), lambda j: (0, 0, 0)), pl.BlockSpec((N_BLK, BW, BW), lambda j: (0, 0, 0)),
                   pl.BlockSpec((8, DR), lambda j: (0, 0))],
        out_shape=[jax.ShapeDtypeStruct((s_len, 2 * DR), BF16), jax.ShapeDtypeStruct((N_BLK, BW, BW), F32),
                   jax.ShapeDtypeStruct((N_BLK, BW, BW), F32), jax.ShapeDtypeStruct((8, DR), F32)],
        scratch_shapes=[pltpu.VMEM((tc + 8, DR), F32), pltpu.VMEM((tc + 8, DR), F32), pltpu.VMEM((tc, DR), F32),
                        pltpu.VMEM((tc, DR), F32), pltpu.VMEM((tc, DR), F32), pltpu.VMEM((tc + 8, DR), F32),
                        pltpu.VMEM((1, DR), F32)],
        compiler_params=_cparams(("arbitrary",)),
    )(proj_rnn, proj_rnn, proj_rnn, hs, hs, du, cw, cb, wa, wat, ba, wi, wit, bi, sp, dsp_dlam)


def _attn_masks(n):
    qi = lax.broadcasted_iota(jnp.int32, (QB, 2 * QB), 0)
    ki = lax.broadcasted_iota(jnp.int32, (QB, 2 * QB), 1)
    dist = QB + qi - ki
    valid = (dist >= 0) & (dist <= QB) & ((n > 0) | (ki >= QB))
    lane = lax.broadcasted_iota(jnp.int32, (1, HEAD_PAIR), 1)
    return valid, lane


def _qkv_specs(d, n_of):
    prev = lambda r, j: jnp.maximum(n_of(j) - 1, 0)
    blk = (QB, D)
    return [pl.BlockSpec(blk, lambda r, j: (n_of(j), 3 * r)),
            pl.BlockSpec(blk, lambda r, j: (prev(r, j), 3 * r + 1)), pl.BlockSpec(blk, lambda r, j: (n_of(j), 3 * r + 1)),
            pl.BlockSpec(blk, lambda r, j: (prev(r, j), 3 * r + 2)), pl.BlockSpec(blk, lambda r, j: (n_of(j), 3 * r + 2))]


def _attn_fwd(qkv, d):
    s_len = qkv.shape[0]
    n_l = s_len // d
    nb = n_l // QB
    qv = qkv.reshape(n_l, d * 3 * D)

    def kern(q_ref, kp_ref, kc_ref, vp_ref, vc_ref, o_ref, lse_ref):
        valid, lane = _attn_masks(pl.program_id(1))
        first = lane < 64
        lse_blk = jnp.zeros((QB, HEAD_PAIR), F32)
        for hp in range(N_PAIR):
            sl = slice(hp * HEAD_PAIR, (hp + 1) * HEAD_PAIR)
            q2 = q_ref[:, sl]
            k2 = jnp.concatenate([kp_ref[:, sl], kc_ref[:, sl]], axis=0)
            v2 = jnp.concatenate([vp_ref[:, sl], vc_ref[:, sl]], axis=0)
            outs = []
            for half in range(2):
                hm = first if half == 0 else jnp.logical_not(first)
                qh = jnp.where(hm, q2, jnp.zeros_like(q2))
                s = lax.dot_general(qh, k2, (((1,), (1,)), ((), ())), preferred_element_type=F32) * SCALE
                s = jnp.where(valid, s, NEG)
                mx = jnp.max(s, axis=-1, keepdims=True)
                p = jnp.exp(s - mx)
                den = jnp.sum(p, axis=-1, keepdims=True)
                outs.append(jnp.dot(p.astype(BF16), v2, preferred_element_type=F32) / den)
                lse_blk = jnp.where(lane == 2 * hp + half, mx + jnp.log(den), lse_blk)
            o_ref[:, sl] = jnp.where(first, outs[0], outs[1]).astype(BF16)
        lse_ref[...] = lse_blk

    o, lse = pl.pallas_call(
        kern, name=f"attn_fwd_d{d}", grid=(d, nb), in_specs=_qkv_specs(d, lambda j: j),
        out_specs=[pl.BlockSpec((QB, D), lambda r, j: (j, r)), pl.BlockSpec((QB, HEAD_PAIR), lambda r, j: (j, r))],
        out_shape=[jax.ShapeDtypeStruct((n_l, d * D), BF16), jax.ShapeDtypeStruct((n_l, d * HEAD_PAIR), F32)],
        compiler_params=_cparams(("arbitrary", "arbitrary")),
    )(qv, qv, qv, qv, qv)
    return o.reshape(s_len, D), lse.reshape(s_len, HEAD_PAIR)


def _attn_bwd(qkv, do, o, lse, d, carry=None):
    s_len = qkv.shape[0]
    n_l = s_len // d
    nb = n_l // QB
    qv = qkv.reshape(n_l, d * 3 * D)
    view = lambda t: t.reshape(n_l, d * D)
    n_c = len(carry.arrays) if carry else 0

    def kern(*refs):
        q_ref, kp_ref, kc_ref, vp_ref, vc_ref, do_ref, o_ref, lse_ref = refs[:8]
        dq_ref, dk_ref, dv_ref = refs[8 + n_c:11 + n_c]
        dk_scr, dv_scr = refs[11 + 2 * n_c:13 + 2 * n_c]
        j = pl.program_id(1)
        if carry:
            r = pl.program_id(0)
            _carry(carry, refs[8:8 + n_c], refs[11 + n_c:11 + 2 * n_c], refs[-1], (r == 0) & (j == 0),
                   (r == d - 1) & (j == nb - 1))

        @pl.when(j == 0)
        def _():
            dk_scr[...] = jnp.zeros((QB, D), F32)
            dv_scr[...] = jnp.zeros((QB, D), F32)

        valid, lane = _attn_masks(nb - 1 - j)
        first = lane < 64
        lse_blk = lse_ref[...]
        nt = (((1,), (1,)), ((), ()))
        tn = (((0,), (0,)), ((), ()))
        for hp in range(N_PAIR):
            sl = slice(hp * HEAD_PAIR, (hp + 1) * HEAD_PAIR)
            q2 = q_ref[:, sl]
            k2 = jnp.concatenate([kp_ref[:, sl], kc_ref[:, sl]], axis=0)
            v2 = jnp.concatenate([vp_ref[:, sl], vc_ref[:, sl]], axis=0)
            do2 = do_ref[:, sl]
            prod = do2.astype(F32) * o_ref[:, sl].astype(F32)
            dqs = []
            dk2 = jnp.zeros((2 * QB, HEAD_PAIR), F32)
            dv2 = jnp.zeros((2 * QB, HEAD_PAIR), F32)
            for half in range(2):
                hm = first if half == 0 else jnp.logical_not(first)
                qh = jnp.where(hm, q2, jnp.zeros_like(q2))
                doh = jnp.where(hm, do2, jnp.zeros_like(do2))
                dsum = jnp.sum(jnp.where(hm, prod, 0.0), axis=-1, keepdims=True)
                lseh = jnp.sum(jnp.where(lane == 2 * hp + half, lse_blk, 0.0), axis=-1, keepdims=True)
                s = lax.dot_general(qh, k2, nt, preferred_element_type=F32) * SCALE
                p = jnp.exp(jnp.where(valid, s, NEG) - lseh)
                dp = lax.dot_general(doh, v2, nt, preferred_element_type=F32)
                ds = (p * (dp - dsum) * SCALE).astype(BF16)
                dqs.append(jnp.dot(ds, k2, preferred_element_type=F32))
                dk2 = dk2 + lax.dot_general(ds, qh, tn, preferred_element_type=F32)
                dv2 = dv2 + lax.dot_general(p.astype(BF16), doh, tn, preferred_element_type=F32)
            dq_ref[:, sl] = jnp.where(first, dqs[0], dqs[1]).astype(BF16)
            dk_ref[:, sl] = (dk2[QB:, :] + dk_scr[:, sl]).astype(BF16)
            dv_ref[:, sl] = (dv2[QB:, :] + dv_scr[:, sl]).astype(BF16)
            dk_scr[:, sl] = dk2[:QB, :]
            dv_scr[:, sl] = dv2[:QB, :]

    rev = lambda j: nb - 1 - j
    row = pl.BlockSpec((QB, D), lambda r, j: (rev(j), r))
    res = pl.pallas_call(
        kern, name=f"attn_bwd_d{d}", grid=(d, nb),
        in_specs=_qkv_specs(d, rev) + [row, row, pl.BlockSpec((QB, HEAD_PAIR), lambda r, j: (rev(j), r))]
        + (carry.in_specs if carry else []),
        out_specs=[row, row, row] + (carry.out_specs if carry else []),
        out_shape=[jax.ShapeDtypeStruct((n_l, d * D), BF16)] * 3 + (carry.out_shape if carry else []),
        scratch_shapes=[pltpu.VMEM((QB, D), F32), pltpu.VMEM((QB, D), F32)] + (carry.scratch if carry else []),
        compiler_params=_cparams(("arbitrary", "arbitrary")),
    )(qv, qv, qv, qv, qv, view(do), view(o), lse.reshape(n_l, d * HEAD_PAIR), *(carry.arrays if carry else []))
    dq, dk, dv = res[:3]
    return (dq.reshape(s_len, D), dk.reshape(s_len, D), dv.reshape(s_len, D)), res[3:]


ATT_BLK = 2048


def _attn_units():
    return [(g, d, b, b * QB * d + r)
            for g, d in enumerate(DILATIONS) for b in range(ATT_BLK // (QB * d)) for r in range(d)]


def _rows_at(ref, start, n, d):
    return ref[pl.ds(start, n, stride=d), :] if d > 1 else ref[start:start + n, :]


def _add_rows_at(ref, start, n, d, val):
    if d > 1:
        ref[pl.ds(start, n, stride=d), :] = ref[pl.ds(start, n, stride=d), :] + val
    else:
        ref[start:start + n, :] = ref[start:start + n, :] + val


def _stack_heads(x, first):
    zero = jnp.zeros_like(x)
    return jnp.concatenate([jnp.where(first, x, zero), jnp.where(first, zero, x)], axis=0)


def _unstack_heads(x, first):
    return jnp.where(first, x[:QB], x[QB:])


def _head_column(x, lane, half):
    return jnp.sum(jnp.where(lane == 64 * half, x, 0.0), axis=-1, keepdims=True)


def _band_masks():
    row = lax.broadcasted_iota(jnp.int32, (2 * QB, 2 * QB), 0) & (QB - 1)
    col = lax.broadcasted_iota(jnp.int32, (2 * QB, 2 * QB), 1)
    dist = QB + row - col
    lane = lax.broadcasted_iota(jnp.int32, (1, HEAD_PAIR), 1)
    return (dist >= 0) & (dist <= QB), col >= QB, lane


def _attention_specs(nblk, blk_of):
    cur = lambda off: pl.BlockSpec((ATT_BLK, HEAD_PAIR), lambda hp, j: (blk_of(j), off + hp))
    prev = lambda off: pl.BlockSpec((ATT_BLK, HEAD_PAIR), lambda hp, j: (jnp.maximum(blk_of(j) - 1, 0), off + hp))
    return cur, prev


def _attention_fwd(qkv):
    s_len = qkv.shape[0]
    nblk = s_len // ATT_BLK
    units = _attn_units()
    nt = (((1,), (1,)), ((), ()))

    def kern(q_ref, k_ref, v_ref, o_ref, lse_ref, kbuf, vbuf, og, lg):
        blk = pl.program_id(1)

        @pl.when(blk == 0)
        def _():
            kbuf[0:ATT_BLK, :] = jnp.zeros((ATT_BLK, HEAD_PAIR), F32)
            vbuf[0:ATT_BLK, :] = jnp.zeros((ATT_BLK, HEAD_PAIR), F32)

        kbuf[ATT_BLK:2 * ATT_BLK, :] = k_ref[...]
        vbuf[ATT_BLK:2 * ATT_BLK, :] = v_ref[...]
        band, later, lane = _band_masks()
        first = lane < 64
        band0 = band & (later | (blk > 0))
        for g, d, b, start in units:
            kstart = ATT_BLK + start - QB * d
            qs = _stack_heads(_rows_at(q_ref, start, QB, d).astype(BF16), first)
            k2 = _rows_at(kbuf, kstart, 2 * QB, d).astype(BF16)
            v2 = _rows_at(vbuf, kstart, 2 * QB, d).astype(BF16)
            s = lax.dot_general(qs, k2, nt, preferred_element_type=F32) * SCALE
            s = jnp.where(band if b > 0 else band0, s, NEG)
            mx = jnp.max(s, axis=-1, keepdims=True)
            p = jnp.exp(s - mx)
            den = jnp.sum(p, axis=-1, keepdims=True)
            out = jnp.dot(p.astype(BF16), v2, preferred_element_type=F32) / den
            lse2 = jnp.broadcast_to(mx + jnp.log(den), (2 * QB, HEAD_PAIR))
            if d > 1:
                og[g, pl.ds(start, QB, stride=d), :] = _unstack_heads(out, first)
                lg[g, pl.ds(start, QB, stride=d), :] = _unstack_heads(lse2, first)
            else:
                og[g, start:start + QB, :] = _unstack_heads(out, first)
                lg[g, start:start + QB, :] = _unstack_heads(lse2, first)
        kbuf[0:ATT_BLK, :] = kbuf[ATT_BLK:2 * ATT_BLK, :]
        vbuf[0:ATT_BLK, :] = vbuf[ATT_BLK:2 * ATT_BLK, :]
        mx = jnp.maximum(jnp.maximum(lg[0], lg[1]), lg[2])
        es = [jnp.exp(lg[g] - mx) for g in range(3)]
        tot = es[0] + es[1] + es[2]
        o_ref[...] = ((es[0] * og[0] + es[1] * og[1] + es[2] * og[2]) / tot).astype(BF16)
        lse_ref[...] = mx + jnp.log(tot)

    cur, _ = _attention_specs(nblk, lambda j: j)
    out_spec = pl.BlockSpec((ATT_BLK, HEAD_PAIR), lambda hp, j: (j, hp))
    return pl.pallas_call(
        kern, name="attention_fwd", grid=(N_PAIR, nblk), in_specs=[cur(0), cur(N_PAIR), cur(2 * N_PAIR)],
        out_specs=[out_spec, out_spec],
        out_shape=[jax.ShapeDtypeStruct((s_len, D), BF16), jax.ShapeDtypeStruct((s_len, D), F32)],
        scratch_shapes=[pltpu.VMEM((2 * ATT_BLK, HEAD_PAIR), F32), pltpu.VMEM((2 * ATT_BLK, HEAD_PAIR), F32),
                        pltpu.VMEM((3, ATT_BLK, HEAD_PAIR), F32), pltpu.VMEM((3, ATT_BLK, HEAD_PAIR), F32)],
        compiler_params=_cparams(("arbitrary", "arbitrary")),
    )(qkv, qkv, qkv)


def _attention_bwd(qkv, do, o, lse, carry):
    s_len = qkv.shape[0]
    nblk = s_len // ATT_BLK
    units = _attn_units()
    nt = (((1,), (1,)), ((), ()))
    tn = (((0,), (0,)), ((), ()))
    n_c = len(carry.arrays) if carry else 0
    c_in, c_out = (carry.in_specs, carry.out_specs) if carry else ([], [])
    c_shape, c_scratch, c_arrays = (carry.out_shape, carry.scratch, carry.arrays) if carry else ([], [], [])

    def kern(*refs):
        q_ref, k_ref, v_ref, kp_ref, vp_ref, do_ref, o_ref, lse_ref = refs[:8]
        dq_ref, dk_ref, dv_ref = refs[8 + n_c:11 + n_c]
        kbuf, vbuf, dkbuf, dvbuf, dq_scr, dsum = refs[11 + 2 * n_c:17 + 2 * n_c]
        hp, j = pl.program_id(0), pl.program_id(1)
        blk = nblk - 1 - j
        if carry:
            _carry(carry, refs[8:8 + n_c], refs[11 + n_c:11 + 2 * n_c], refs[-1], (hp == 0) & (j == 0),
                   (hp == N_PAIR - 1) & (j == nblk - 1))
        zeros = jnp.zeros((ATT_BLK, HEAD_PAIR), F32)

        @pl.when(j == 0)
        def _():
            dkbuf[ATT_BLK:2 * ATT_BLK, :] = zeros
            dvbuf[ATT_BLK:2 * ATT_BLK, :] = zeros

        @pl.when(j > 0)
        def _():
            dkbuf[ATT_BLK:2 * ATT_BLK, :] = dkbuf[0:ATT_BLK, :]
            dvbuf[ATT_BLK:2 * ATT_BLK, :] = dvbuf[0:ATT_BLK, :]

        dkbuf[0:ATT_BLK, :] = zeros
        dvbuf[0:ATT_BLK, :] = zeros
        dq_scr[...] = zeros
        kbuf[0:ATT_BLK, :] = kp_ref[...]
        kbuf[ATT_BLK:2 * ATT_BLK, :] = k_ref[...]
        vbuf[0:ATT_BLK, :] = vp_ref[...]
        vbuf[ATT_BLK:2 * ATT_BLK, :] = v_ref[...]
        band, later, lane = _band_masks()
        first = lane < 64
        band0 = band & (later | (blk > 0))
        prod = do_ref[...] * o_ref[...].astype(F32)
        dsum[...] = jnp.where(first, jnp.sum(jnp.where(first, prod, 0.0), axis=-1, keepdims=True),
                              jnp.sum(jnp.where(first, 0.0, prod), axis=-1, keepdims=True))
        for g, d, b, start in units:
            kstart = ATT_BLK + start - QB * d
            qs = _stack_heads(_rows_at(q_ref, start, QB, d).astype(BF16), first)
            dos = _stack_heads(_rows_at(do_ref, start, QB, d).astype(BF16), first)
            k2 = _rows_at(kbuf, kstart, 2 * QB, d).astype(BF16)
            v2 = _rows_at(vbuf, kstart, 2 * QB, d).astype(BF16)
            ds_rows = _rows_at(dsum, start, QB, d)
            lse_rows = _rows_at(lse_ref, start, QB, d)
            dcol = jnp.concatenate([_head_column(ds_rows, lane, 0), _head_column(ds_rows, lane, 1)], axis=0)
            lcol = jnp.concatenate([_head_column(lse_rows, lane, 0), _head_column(lse_rows, lane, 1)], axis=0)
            s = lax.dot_general(qs, k2, nt, preferred_element_type=F32) * SCALE
            p = jnp.exp(jnp.where(band if b > 0 else band0, s, NEG) - lcol)
            dp = lax.dot_general(dos, v2, nt, preferred_element_type=F32)
            ds = (p * (dp - dcol) * SCALE).astype(BF16)
            _add_rows_at(dq_scr, start, QB, d, _unstack_heads(jnp.dot(ds, k2, preferred_element_type=F32), first))
            _add_rows_at(dkbuf, kstart, 2 * QB, d, lax.dot_general(ds, qs, tn, preferred_element_type=F32))
            _add_rows_at(dvbuf, kstart, 2 * QB, d, lax.dot_general(p.astype(BF16), dos, tn, preferred_element_type=F32))
        dq_ref[...] = dq_scr[...].astype(BF16)
        dk_ref[...] = dkbuf[ATT_BLK:2 * ATT_BLK, :].astype(BF16)
        dv_ref[...] = dvbuf[ATT_BLK:2 * ATT_BLK, :].astype(BF16)

    rev = lambda j: nblk - 1 - j
    cur, prev = _attention_specs(nblk, rev)
    buf = lambda rows: pltpu.VMEM((rows, HEAD_PAIR), F32)
    res = pl.pallas_call(
        kern, name="attention_bwd", grid=(N_PAIR, nblk),
        in_specs=[cur(0), cur(N_PAIR), cur(2 * N_PAIR), prev(N_PAIR), prev(2 * N_PAIR), cur(0), cur(0), cur(0)] + c_in,
        out_specs=[cur(0)] * 3 + c_out,
        out_shape=[jax.ShapeDtypeStruct((s_len, D), BF16)] * 3 + c_shape,
        scratch_shapes=[buf(2 * ATT_BLK), buf(2 * ATT_BLK), buf(2 * ATT_BLK), buf(2 * ATT_BLK), buf(ATT_BLK),
                        buf(ATT_BLK)] + c_scratch,
        compiler_params=_cparams(("arbitrary", "arbitrary")),
    )(qkv, qkv, qkv, qkv, qkv, do, o, lse, *c_arrays)
    return res[0], res[1], res[2], res[3:]


def _adamw(w, g, m, v):
    m = ADAM_B1 * m + (1.0 - ADAM_B1) * g
    v = ADAM_B2 * v + (1.0 - ADAM_B2) * (g * g)
    m_hat = m / (1.0 - ADAM_B1 ** ADAM_STEP)
    v_hat = v / (1.0 - ADAM_B2 ** ADAM_STEP)
    delta = -ADAM_LR * (m_hat / (jnp.sqrt(v_hat) + ADAM_EPS) + ADAM_WD * w)
    return delta, m, v


def _adam_packed(name, recv, w, m, v, tm):
    n_rows, width = w.shape
    summed = recv.ndim == 3

    def kern(r_ref, w_ref, m_ref, v_ref, g_out, d_out, m_out, v_out):
        if summed:
            g = r_ref[0].astype(F32)
            for s in range(1, N_DEV):
                g = g + r_ref[s].astype(F32)
        else:
            g = r_ref[...]
        delta, mn, vn = _adamw(w_ref[...], g, m_ref[...], v_ref[...])
        g_out[...] = g
        d_out[...] = delta
        m_out[...] = mn
        v_out[...] = vn

    tile = pl.BlockSpec((tm, width), lambda i: (i, 0))
    rspec = pl.BlockSpec((N_DEV, tm, width), lambda i: (0, i, 0)) if summed else tile
    return pl.pallas_call(
        kern, name=name, grid=(n_rows // tm,), in_specs=[rspec, tile, tile, tile], out_specs=[tile] * 4,
        out_shape=[jax.ShapeDtypeStruct((n_rows, width), F32)] * 4, compiler_params=_cparams(("parallel",)),
    )(recv, w, m, v)


def _sum_slots(recv):
    _, n_rows, width = recv.shape

    def kern(r_ref, o_ref):
        g = r_ref[0]
        for s in range(1, N_DEV):
            g = g + r_ref[s]
        o_ref[...] = g

    return pl.pallas_call(
        kern, name="sum_small", grid=(1,), in_specs=[pl.BlockSpec(recv.shape, lambda i: (0, 0, 0))],
        out_specs=pl.BlockSpec((n_rows, width), lambda i: (0, 0)),
        out_shape=jax.ShapeDtypeStruct((n_rows, width), F32), compiler_params=_cparams(("arbitrary",)),
    )(recv)


def _adam_w_ada(cond_t, dmod_cols, w, m, v):
    n_rows, width = w.shape
    tm = ROW_TILE

    def kern(c_ref, d_ref, w_ref, m_ref, v_ref, g_out, d_out, m_out, v_out):
        g = c_ref[:, 0:1] * d_ref[0:1, :]
        for b in range(1, N_DEV):
            g = g + c_ref[:, b:b + 1] * d_ref[b:b + 1, :]
        delta, mn, vn = _adamw(w_ref[...], g, m_ref[...], v_ref[...])
        g_out[...] = g
        d_out[...] = delta
        m_out[...] = mn
        v_out[...] = vn

    tile = pl.BlockSpec((tm, width), lambda i: (i, 0))
    return pl.pallas_call(
        kern, name="adam_w_ada", grid=(n_rows // tm,),
        in_specs=[pl.BlockSpec((tm, N_DEV), lambda i: (i, 0)), pl.BlockSpec((N_DEV, width), lambda i: (0, 0)),
                  tile, tile, tile],
        out_specs=[tile] * 4, out_shape=[jax.ShapeDtypeStruct((n_rows, width), F32)] * 4,
        compiler_params=_cparams(("parallel",)),
    )(cond_t, dmod_cols, w, m, v)


def _rows(a, n_rows=None):
    flat = a.reshape(-1)
    need = (n_rows if n_rows is not None else -(-flat.shape[0] // D)) * D
    if need != flat.shape[0]:
        flat = jnp.concatenate([flat, jnp.zeros((need - flat.shape[0],), flat.dtype)])
    return flat.reshape(-1, D)


def _pack_shards(ws):
    return jnp.concatenate([ws[name][0].reshape(r, D) for name, r in PACK_ROWS], axis=0)


def _cols_to_slots(g, cols):
    k = g.shape[0]
    return g.reshape(k, N_DEV, cols).transpose(1, 0, 2).reshape(N_DEV, k * cols // D, D)


def _slots_to_cols(s, k, cols):
    return s.reshape(N_DEV, k, cols).transpose(1, 0, 2).reshape(k, N_DEV * cols)


def kernel(x, c, w_ada, b_ada, w_in, conv_w, conv_b, lru_wa, lru_ba, lru_wi, lru_bi, lru_lambda, w_proj_rnn, w_proj_attn, b_gate, w_out, ln1_g, ln1_b, w_ffn_gate, w_ffn_up, w_ffn_down, ln2_g, ln2_b, loss_target, m_w_ada, m_b_ada, m_w_in, m_conv_w, m_conv_b, m_lru_wa, m_lru_ba, m_lru_wi, m_lru_bi, m_lru_lambda, m_w_proj_rnn, m_w_proj_attn, m_b_gate, m_w_out, m_ln1_g, m_ln1_b, m_w_ffn_gate, m_w_ffn_up, m_w_ffn_down, m_ln2_g, m_ln2_b, v_w_ada, v_b_ada, v_w_in, v_conv_w, v_conv_b, v_lru_wa, v_lru_ba, v_lru_wi, v_lru_bi, v_lru_lambda, v_w_proj_rnn, v_w_proj_attn, v_b_gate, v_w_out, v_ln1_g, v_ln1_b, v_w_ffn_gate, v_w_ffn_up, v_w_ffn_down, v_ln2_g, v_ln2_b):
    s_len = x.shape[1]
    me = 4 * lax.axis_index("x") + 2 * lax.axis_index("y") + lax.axis_index("c")
    xs = x[0]
    tgt = loss_target[0]
    tm = ROW_TILE

    b_ada_cols = lax.dynamic_slice(b_ada, (0, me * 768), (1, 768))
    pad_cols = lambda a: jnp.concatenate([a, jnp.zeros((a.shape[0], D - a.shape[1]), F32)], axis=1)
    c8 = jnp.concatenate([c, pad_cols(conv_w[0]), pad_cols(b_gate[0]), jnp.zeros((1, D), F32)], axis=0)
    c_all, cond_blocks, mod_parts = _ada_modulation(c8, w_ada[0], b_ada_cols)
    cond_all = cond_blocks[:, 0, :]
    mod = mod_parts[:, 0, :].reshape(6, D)
    mod8 = jnp.concatenate([mod, jnp.zeros((2, D), F32)], axis=0)
    cw = c_all[:, 1:1 + CONV_W, :DR // N_DEV].transpose(1, 0, 2).reshape(CONV_W, DR)
    bg = c_all[:, 1 + CONV_W:3 + CONV_W, :D // N_DEV].transpose(1, 0, 2).reshape(2, D)
    bg8 = jnp.concatenate([bg, jnp.zeros((6, D), F32)], axis=0)

    late = dict(w_ffn_gate=w_ffn_gate, w_ffn_up=w_ffn_up, w_proj_rnn=w_proj_rnn, w_proj_attn=w_proj_attn,
                w_out=w_out, w_ffn_down=w_ffn_down)
    late_shard = jnp.concatenate([late[name][0].reshape(r, D) for name, r in LATE_ROWS], axis=0).astype(BF16)

    cb = conv_b
    wa_b, wi_b = lru_wa[0].astype(BF16), lru_wi[0].astype(BF16)
    wat_b, wit_b = jnp.swapaxes(wa_b, 1, 2), jnp.swapaxes(wi_b, 1, 2)
    ba, bi = lru_ba.reshape(1, DR), lru_bi.reshape(1, DR)
    sp = jax.nn.softplus(-lru_lambda)
    dsp_dlam = -jax.nn.sigmoid(-lru_lambda)
    ln1 = jnp.concatenate([ln1_g, ln1_b, jnp.zeros((6, D), F32)], axis=0)
    ln2 = jnp.concatenate([ln2_g, ln2_b, jnp.zeros((6, D), F32)], axis=0)

    def f1(ri, fi, ro, ao):
        n, _ = _ln(ri[0][...])
        ro[0][...] = (n * (1.0 + fi[0][1:2, :]) + fi[0][0:1, :]).astype(BF16)

    h1, w_in_slots = _rowwise(f1, "ln_mod1", s_len, tm, [(xs, D, 0)], [mod8], [(D, BF16)], [],
                              carry=_ChipGather(w_in[0].astype(BF16)))
    w_in_f = w_in_slots.transpose(1, 0, 2).reshape(D, N_DEV * 960)
    w_rnn, w_qkv, w_gates = w_in_f[:, :2 * DR], w_in_f[:, 2 * DR:2 * DR + 3 * D], w_in_f[:, 2 * DR + 3 * D:]
    proj_rnn = _mm("mm_in_rnn", [(h1, w_rnn)], F32, 512, DR)
    qkv = _mm("mm_in_qkv", [(h1, w_qkv)], F32, 512, D)
    gates = _mm("mm_in_gates", [(h1, w_gates)], F32, 512, D)

    hs, u_rnn, (late_all,) = _rnn_fwd(proj_rnn, cw, cb, wa_b, ba, wi_b, bi, sp, _Exchange([(late_shard, False)]))
    offs, o = {}, 0
    for name, r in LATE_ROWS:
        offs[name] = (o, o + r)
        o += r
    part = lambda name: late_all[:, offs[name][0]:offs[name][1], :]
    w_g, w_u = _slots_to_cols(part("w_ffn_gate"), D, 352), _slots_to_cols(part("w_ffn_up"), D, 352)
    w_pr = part("w_proj_rnn").reshape(DR, D)
    w_pa = part("w_proj_attn").reshape(D, D)
    w_o = part("w_out").reshape(D, D)
    w_dn = part("w_ffn_down").reshape(DFF, D)

    u_attn, lse = _attention_fwd(qkv)

    def e_merge(dots, ti, fi, ro, ao):
        g_r = _sigmoid(ti[0][...] + fi[0][0:1, :])
        g_a = _sigmoid(ti[1][...] + fi[0][1:2, :])
        ro[0][...] = dots[0].astype(BF16)
        ro[1][...] = dots[1].astype(BF16)
        ro[2][...] = (g_r * dots[0] + g_a * dots[1]).astype(BF16)

    pr, pa, merged = _mm_fused("mm_proj_merge", [(u_rnn, w_pr), (u_attn, w_pa)], [(gates, 0), (gates, 1)], [bg8],
                               [(BF16, 1), (BF16, 1), (BF16, 1)], [], e_merge, 512, D)
    def e_post1(dots, ti, fi, ro, ao):
        md, l1 = fi[0], fi[1]
        fv = dots[0]
        n1, _ = _ln(ALPHA * ti[0][...] + md[2:3, :] * fv)
        x1 = n1 * l1[0:1, :] + l1[1:2, :]
        n0, _ = _ln(x1)
        ro[0][...] = fv
        ro[1][...] = x1
        ro[2][...] = (n0 * (1.0 + md[4:5, :]) + md[3:4, :]).astype(BF16)

    f, x1, h2 = _mm_fused("mm_out_post1", [(merged, w_o)], [(xs, 0)], [mod8, ln1], [(F32, 1), (F32, 1), (BF16, 1)], [],
                          e_post1, 256, D)

    def e_swiglu(dots, ti, fi, ro, ao):
        gp, up = dots
        ro[0][...] = gp.astype(BF16)
        ro[1][...] = up.astype(BF16)
        ro[2][...] = (gp * _sigmoid(gp) * up).astype(BF16)

    gpre, upre, act = _mm_fused("mm_ffn_in", [(h2, w_g), (h2, w_u)], [], [], [(BF16, 1)] * 3, [], e_swiglu, 512, DFF // 2)
    def e_loss(dots, ti, fi, ro, ao):
        md, l2 = fi[0], fi[1]
        f2v = dots[0]
        n2, rstd = _ln(ALPHA * ti[0][...] + md[5:6, :] * f2v)
        err = n2 * l2[0:1, :] + l2[1:2, :] - ti[1][...]
        dx2 = err * (1.0 / D)
        dz2 = _ln_bwd(dx2 * l2[0:1, :], n2, rstd)
        ro[0][...] = dz2
        ro[1][...] = (md[5:6, :] * dz2).astype(BF16)
        acc = ao[0]
        acc[0:1, :] += _rowsum(err * err) * (0.5 / D)
        acc[1:2, :] += _rowsum(dx2 * n2)
        acc[2:3, :] += _rowsum(dx2)
        acc[3:4, :] += _rowsum(dz2 * f2v)

    dz2, df2, acc5 = _mm_fused("mm_ffn_out_loss", [(act, w_dn)], [(x1, 0), (tgt, 0)], [mod8, ln2],
                               [(F32, 1), (BF16, 1)], [(8, D)], e_loss, 256, D)

    d_w_dn = _mm_tn("mmt_ffn_down", act, df2, 1024, 512)

    def e_dswiglu(dots, ti, fi, ro, ao):
        da = dots[0]
        gp, up = ti[0][...].astype(F32), ti[1][...].astype(F32)
        sg = _sigmoid(gp)
        ro[0][...] = (da * up * sg * (1.0 + gp * (1.0 - sg))).astype(BF16)
        ro[1][...] = (da * gp * sg).astype(BF16)

    dgp, dup = _mm_fused("mm_d_ffn_out", [(df2, w_dn, True)], [(gpre, 0), (upre, 0)], [], [(BF16, 1)] * 2, [],
                         e_dswiglu, 512, DFF // 2)
    d_w_g = _mm_tn("mmt_ffn_gate", h2, dgp, 1024, DFF // 2)
    d_w_u = _mm_tn("mmt_ffn_up", h2, dup, 1024, DFF // 2)

    def e_dpost1(dots, ti, fi, ro, ao):
        md, l1 = fi[0], fi[1]
        dh2v = dots[0] + dots[1]
        fv = ti[1][...]
        n1, rstd1 = _ln(ALPHA * ti[0][...] + md[2:3, :] * fv)
        n0, rstd0 = _ln(n1 * l1[0:1, :] + l1[1:2, :])
        dx1 = ALPHA * ti[2][...] + _ln_bwd(dh2v * (1.0 + md[4:5, :]), n0, rstd0)
        dz1 = _ln_bwd(dx1 * l1[0:1, :], n1, rstd1)
        ro[0][...] = ALPHA * dz1
        ro[1][...] = (md[2:3, :] * dz1).astype(BF16)
        acc = ao[0]
        acc[0:1, :] += _rowsum(dh2v * n0)
        acc[1:2, :] += _rowsum(dh2v)
        acc[2:3, :] += _rowsum(dx1 * n1)
        acc[3:4, :] += _rowsum(dx1)
        acc[4:5, :] += _rowsum(dz1 * fv)

    dxp, df, acc2 = _mm_fused("mm_d_ffn_in_post1", [(dgp, w_g, True), (dup, w_u, True)], [(xs, 0), (f, 0), (dz2, 0)],
                              [mod8, ln1], [(F32, 1), (BF16, 1)], [(8, D)], e_dpost1, 256, D)
    d_w_o = _mm_tn("mmt_out", merged, df, 1024, 512)

    def e_dmerge(dots, ti, fi, ro, ao):
        dm = dots[0]
        g_r = _sigmoid(ti[0][...] + fi[0][0:1, :])
        g_a = _sigmoid(ti[1][...] + fi[0][1:2, :])
        ro[0][...] = (dm * g_r).astype(BF16)
        ro[1][...] = (dm * g_a).astype(BF16)
        dga = dm * ti[2][...].astype(F32) * g_r * (1.0 - g_r)
        dgb = dm * ti[3][...].astype(F32) * g_a * (1.0 - g_a)
        ro[2][:, 0:D] = dga.astype(BF16)
        ro[2][:, D:2 * D] = dgb.astype(BF16)
        ao[0][0:1, :] += _rowsum(dga)
        ao[0][1:2, :] += _rowsum(dgb)

    dpr, dpa, dgates, acc3 = _mm_fused("mm_d_out", [(df, w_o, True)], [(gates, 0), (gates, 1), (pr, 0), (pa, 0)], [bg8],
                                       [(BF16, 1), (BF16, 1), (BF16, 2)], [(8, D)], e_dmerge, 512, D)
    du_rnn = _mm("mm_d_proj_rnn", [(dpr, w_pr, True)], F32, 512, DR)
    du_attn = _mm("mm_d_proj_attn", [(dpa, w_pa, True)], F32, 512, D)
    d_w_pr = _mm_tn("mmt_proj_rnn", u_rnn, dpr, 1024, 512)
    d_w_pa = _mm_tn("mmt_proj_attn", u_attn, dpa, 1024, 512)

    drnn, d_wa, d_wi, vec = _rnn_bwd(proj_rnn, hs, du_rnn, cw, cb, wa_b, wat_b, ba, wi_b, wit_b, bi, sp, dsp_dlam)

    col_slots = lambda g, cols: g.reshape(g.shape[0], N_DEV, cols).transpose(1, 0, 2).astype(BF16)
    row_slots = lambda g: g.reshape(N_DEV, g.shape[0] // N_DEV, g.shape[1]).astype(BF16)
    early = [("w_ffn_gate", col_slots(d_w_g, 352)), ("w_ffn_up", col_slots(d_w_u, 352)), ("w_ffn_down", row_slots(d_w_dn)),
             ("w_proj_rnn", row_slots(d_w_pr)), ("w_proj_attn", row_slots(d_w_pa)), ("w_out", row_slots(d_w_o))]
    g_small = jnp.concatenate([
        _rows(vec[4], 2), d_wa.reshape(160, D), _rows(vec[5], 2), d_wi.reshape(160, D), _rows(vec[6], 2), _rows(vec[7], 2),
        acc2[2:4], acc5[1:3],
        _rows(vec[0:4], 5), acc3[0:2], acc5[0:1], jnp.zeros((4, D), F32)], axis=0)
    dq, dk, dv, recvd = _attention_bwd(qkv, du_attn, u_attn, lse,
                                       _Exchange([(g, True) for _, g in early] + [(g_small, False)]))
    recv = {name: r for (name, _), r in zip(early, recvd)}
    recv_small = recvd[-1]

    d_w_in = jnp.concatenate([_mm_tn("mmt_in_rnn", h1, drnn, 1024, DR), _mm_tn("mmt_in_q", h1, dq, 1024, D),
                              _mm_tn("mmt_in_k", h1, dk, 1024, D), _mm_tn("mmt_in_v", h1, dv, 1024, D),
                              _mm_tn("mmt_in_gates", h1, dgates, 1024, D)], axis=1)
    g_in = col_slots(d_w_in, 960)
    dh1, (recv_in,) = _mm("mm_d_in", [(drnn, w_rnn, True), (dq, w_qkv[:, :D], True), (dk, w_qkv[:, D:2 * D], True),
                                      (dv, w_qkv[:, 2 * D:], True), (dgates, w_gates, True)], F32, 256, 512,
                          carry=_Exchange([(g_in, True)]))
    recv["w_in"] = recv_in

    def b5(ri, fi, ro, ao):
        md = fi[0]
        n0, rstd0 = _ln(ri[0][...])
        dh = ri[1][...]
        ro[0][...] = ri[2][...] + _ln_bwd(dh * (1.0 + md[1:2, :]), n0, rstd0)
        ao[0][0:1, :] += _rowsum(dh * n0)
        ao[0][1:2, :] += _rowsum(dh)

    grad_x, acc1 = _rowwise(b5, "d_ln_mod1", s_len, tm, [(xs, D, 0), (dh1, D, 0), (dxp, D, 0)], [mod8], [(D, F32)],
                            [((8, D), F32)])

    dmod = jnp.concatenate([acc1[1:2], acc1[0:1], acc2[4:5], acc2[1:2], acc2[0:1], acc5[3:4], jnp.zeros((2, D), F32)],
                           axis=0)
    (recv_dmod,) = _exchange_alone("gather_dmod", _Exchange([(dmod, False)]))
    dmod_all = recv_dmod[:, 0:6, :].reshape(N_DEV, 6 * D)
    red = _sum_slots(recv_small)
    loss = jnp.sum(red[339])

    wmv = (dict(w_in=w_in, w_ffn_gate=w_ffn_gate, w_ffn_up=w_ffn_up, w_proj_rnn=w_proj_rnn, w_proj_attn=w_proj_attn,
                w_out=w_out, w_ffn_down=w_ffn_down),
           dict(w_in=m_w_in, w_ffn_gate=m_w_ffn_gate, w_ffn_up=m_w_ffn_up, w_proj_rnn=m_w_proj_rnn,
                w_proj_attn=m_w_proj_attn, w_out=m_w_out, w_ffn_down=m_w_ffn_down),
           dict(w_in=v_w_in, w_ffn_gate=v_w_ffn_gate, w_ffn_up=v_w_ffn_up, w_proj_rnn=v_w_proj_rnn,
                w_proj_attn=v_w_proj_attn, w_out=v_w_out, w_ffn_down=v_w_ffn_down))
    adam_tile = dict(w_in=128, w_ffn_gate=256, w_ffn_up=256, w_ffn_down=176, w_proj_rnn=160, w_proj_attn=128, w_out=128)
    big_out = {name: _adam_packed("adam_" + name, recv[name], wmv[0][name][0], wmv[1][name][0], wmv[2][name][0], tile)
               for name, tile in adam_tile.items()}
    ada_out = _adam_w_ada(cond_all.T, lax.dynamic_slice(dmod_all, (0, me * 768), (N_DEV, 768)),
                          w_ada[0], m_w_ada[0], v_w_ada[0])

    small_spec = (("b_ada", 6), ("conv_b", 2), ("lru_wa", 160), ("lru_ba", 2), ("lru_wi", 160), ("lru_bi", 2),
                  ("lru_lambda", 2), ("ln1_g", 1), ("ln1_b", 1), ("ln2_g", 1), ("ln2_b", 1), ("conv_w", 1), ("b_gate", 1))
    small_rows = 344

    def pack_small(ps):
        return jnp.concatenate([_rows(ps[name], r) for name, r in small_spec] + [jnp.zeros((4, D), F32)], axis=0)

    names = [n for n, _ in small_spec]
    smalls = [pack_small(dict(zip(names, t))) for t in (
        (b_ada, conv_b, lru_wa, lru_ba, lru_wi, lru_bi, lru_lambda, ln1_g, ln1_b, ln2_g, ln2_b, conv_w, b_gate),
        (m_b_ada, m_conv_b, m_lru_wa, m_lru_ba, m_lru_wi, m_lru_bi, m_lru_lambda, m_ln1_g, m_ln1_b, m_ln2_g, m_ln2_b,
         m_conv_w, m_b_gate),
        (v_b_ada, v_conv_b, v_lru_wa, v_lru_ba, v_lru_wi, v_lru_bi, v_lru_lambda, v_ln1_g, v_ln1_b, v_ln2_g, v_ln2_b,
         v_conv_w, v_b_gate))]
    g_conv_w = lax.dynamic_slice(red[332:337].reshape(-1)[:CONV_W * DR].reshape(CONV_W, DR), (0, me * 160), (CONV_W, 160))
    g_b_gate = lax.dynamic_slice(red[337:339], (0, me * 128), (2, 128))
    g_small_pack = jnp.concatenate([jnp.sum(dmod_all, axis=0).reshape(6, D), red[0:332], _rows(g_conv_w, 1),
                                    _rows(g_b_gate, 1), jnp.zeros((4, D), F32)], axis=0)
    assert g_small_pack.shape == (small_rows, D) and smalls[0].shape == (small_rows, D)
    small_out = _adam_packed("adam_small", g_small_pack, smalls[0], smalls[1], smalls[2], small_rows)

    shapes = dict(w_ada=w_ada.shape, b_ada=b_ada.shape, w_in=w_in.shape, conv_w=conv_w.shape, conv_b=conv_b.shape,
                  lru_wa=lru_wa.shape, lru_ba=lru_ba.shape, lru_wi=lru_wi.shape, lru_bi=lru_bi.shape,
                  lru_lambda=lru_lambda.shape, w_proj_rnn=w_proj_rnn.shape, w_proj_attn=w_proj_attn.shape,
                  b_gate=b_gate.shape, w_out=w_out.shape, ln1_g=ln1_g.shape, ln1_b=ln1_b.shape,
                  w_ffn_gate=w_ffn_gate.shape, w_ffn_up=w_ffn_up.shape, w_ffn_down=w_ffn_down.shape,
                  ln2_g=ln2_g.shape, ln2_b=ln2_b.shape)

    def unpack(kind):
        out = {"w_ada": ada_out[kind].reshape(shapes["w_ada"])}
        for name, _ in PACK_ROWS:
            out[name] = big_out[name][kind].reshape(shapes[name])
        o = 0
        for name, r in small_spec:
            size = 1
            for dim in shapes[name]:
                size *= dim
            out[name] = small_out[kind][o:o + r].reshape(-1)[:size].reshape(shapes[name])
            o += r
        return [out[name] for name in shapes]

    return (loss, grad_x[None], *unpack(0), *unpack(1), *unpack(2), *unpack(3))
```

```python
import functools

import jax
import jax.numpy as jnp
from jax import lax
from jax.experimental import pallas as pl
from jax.experimental.pallas import tpu as pltpu

F32 = jnp.float32
BF16 = jnp.bfloat16
MESH = pl.DeviceIdType.MESH

N_DEV = 8
D = 1024
DR = 1280
N_BLK = 10
BW = 128
HEAD_PAIR = 128
N_PAIR = 8
QB = 128
DFF = 2816
CONV_W = 4
DILATIONS = (1, 4, 16)
ALPHA = 2.0 ** 0.25
LN_EPS = 1e-5
LRU_C = 8.0
SCALE = 0.125
NEG = -1e30
ADAM_LR, ADAM_B1, ADAM_B2, ADAM_EPS, ADAM_WD, ADAM_STEP = 0.001, 0.9, 0.999, 1e-08, 0.01, 10

VMEM_LIMIT = 48 * 1024 * 1024
ROW_TILE = 256
RNN_CHUNK = 256

PACK_ROWS = (("w_in", 960), ("w_ffn_gate", 352), ("w_ffn_up", 352), ("w_proj_rnn", 160),
             ("w_proj_attn", 128), ("w_out", 128), ("w_ffn_down", 352))
LATE_ROWS = PACK_ROWS[1:]


def _cparams(sem):
    return pltpu.CompilerParams(dimension_semantics=sem, vmem_limit_bytes=VMEM_LIMIT)


def _ln(z):
    mu = jnp.mean(z, axis=-1, keepdims=True)
    zc = z - mu
    var = jnp.mean(zc * zc, axis=-1, keepdims=True)
    rstd = lax.rsqrt(var + LN_EPS)
    return zc * rstd, rstd


def _ln_bwd(dn, n, rstd):
    return rstd * (dn - jnp.mean(dn, axis=-1, keepdims=True) - n * jnp.mean(dn * n, axis=-1, keepdims=True))


def _sigmoid(x):
    return 1.0 / (1.0 + jnp.exp(-x))


_GELU_K = 0.7978845608028654
_GELU_C = 0.044715


def _gelu(y):
    t = jnp.tanh(_GELU_K * (y + _GELU_C * y * y * y))
    return 0.5 * y * (1.0 + t), t


def _gelu_grad(y, t):
    return 0.5 * (1.0 + t) + 0.5 * y * (1.0 - t * t) * _GELU_K * (1.0 + 3.0 * _GELU_C * y * y)


def _rowsum(v):
    return jnp.sum(v, axis=0, keepdims=True)


def _rowwise(body, name, n_rows, tm, row_ins, full_ins, row_outs, acc_outs, carry=None):
    nri, nfi, nro, nao = len(row_ins), len(full_ins), len(row_outs), len(acc_outs)
    n_c = len(carry.arrays) if carry else 0
    n_in = nri + nfi + n_c

    def kern(*refs):
        ri, fi = refs[:nri], refs[nri:nri + nfi]
        ro, ao = refs[n_in:n_in + nro], refs[n_in + nro:n_in + nro + nao]
        if carry:
            i = pl.program_id(0)
            _carry(carry, refs[nri + nfi:n_in], refs[n_in + nro + nao:n_in + nro + nao + n_c], refs[-1], i == 0,
                   i == n_rows // tm - 1)
        if ao:
            @pl.when(pl.program_id(0) == 0)
            def _():
                for a in ao:
                    a[...] = jnp.zeros(a.shape, a.dtype)
        body(ri, fi, ro, ao)

    in_specs = [pl.BlockSpec((tm, w), functools.partial(lambda i, cb: (i, cb), cb=cb)) for _, w, cb in row_ins]
    in_specs += [pl.BlockSpec(a.shape, lambda i: (0, 0)) for a in full_ins]
    out_specs = [pl.BlockSpec((tm, w), lambda i: (i, 0)) for w, _ in row_outs]
    out_specs += [pl.BlockSpec(s, lambda i: (0, 0)) for s, _ in acc_outs]
    out_shape = [jax.ShapeDtypeStruct((n_rows, w), dt) for w, dt in row_outs]
    out_shape += [jax.ShapeDtypeStruct(s, dt) for s, dt in acc_outs]
    operands = [a for a, _, _ in row_ins] + list(full_ins)
    scratch = []
    if carry:
        in_specs, out_specs, out_shape = in_specs + carry.in_specs, out_specs + carry.out_specs, out_shape + carry.out_shape
        operands, scratch = operands + carry.arrays, carry.scratch
    return pl.pallas_call(
        kern, name=name, grid=(n_rows // tm,), in_specs=in_specs, out_specs=out_specs, out_shape=out_shape,
        scratch_shapes=scratch, compiler_params=_cparams(("arbitrary",)),
    )(*operands)


NT_DIMS = (((1,), (1,)), ((), ()))


def _pair_cols(pair):
    return pair[1].shape[0] if len(pair) == 3 else pair[1].shape[1]


def _pair_specs(pairs, tm, tn):
    in_specs, flat = [], []
    for pair in pairs:
        a, w = pair[0], pair[1]
        k = a.shape[1]
        in_specs.append(pl.BlockSpec((tm, k), lambda j, i: (i, 0)))
        if len(pair) == 3:
            in_specs.append(pl.BlockSpec((tn, k), lambda j, i: (j, 0)))
        else:
            in_specs.append(pl.BlockSpec((k, tn), lambda j, i: (0, j)))
        flat += [a, w]
    return in_specs, flat


def _pair_dot(pair, a_ref, w_ref):
    if len(pair) == 3:
        return lax.dot_general(a_ref[...], w_ref[...], NT_DIMS, preferred_element_type=F32)
    return jnp.dot(a_ref[...], w_ref[...], preferred_element_type=F32)


def _mm(name, pairs, out_dtype, tm, tn, carry=None):
    m_rows, n_cols = pairs[0][0].shape[0], _pair_cols(pairs[0])
    n_pairs = len(pairs)
    n_c = len(carry.arrays) if carry else 0
    grid = (n_cols // tn, m_rows // tm)

    def kern(*refs):
        o_ref = refs[2 * n_pairs + n_c]
        if carry:
            step = pl.program_id(0) * grid[1] + pl.program_id(1)
            _carry(carry, refs[2 * n_pairs:2 * n_pairs + n_c], refs[2 * n_pairs + n_c + 1:2 * n_pairs + 2 * n_c + 1],
                   refs[-1], step == 0, step == grid[0] * grid[1] - 1)
        acc = None
        for p in range(n_pairs):
            t = _pair_dot(pairs[p], refs[2 * p], refs[2 * p + 1])
            acc = t if acc is None else acc + t
        o_ref[...] = acc.astype(o_ref.dtype)

    in_specs, flat = _pair_specs(pairs, tm, tn)
    out_specs = [pl.BlockSpec((tm, tn), lambda j, i: (i, j))]
    out_shape = [jax.ShapeDtypeStruct((m_rows, n_cols), out_dtype)]
    if carry:
        res = pl.pallas_call(
            kern, name=name, grid=grid, in_specs=in_specs + carry.in_specs, out_specs=out_specs + carry.out_specs,
            out_shape=out_shape + carry.out_shape, scratch_shapes=carry.scratch,
            compiler_params=_cparams(("arbitrary", "arbitrary")),
        )(*flat, *carry.arrays)
        return res[0], res[1:]
    return pl.pallas_call(
        kern, name=name, grid=grid, in_specs=in_specs, out_specs=out_specs[0], out_shape=out_shape[0],
        compiler_params=_cparams(("parallel", "parallel")),
    )(*flat)


def _mm_tn(name, a, g, tm, tn):
    m_rows, k = a.shape
    n_cols = g.shape[1]

    def kern(a_ref, g_ref, o_ref):
        @pl.when(pl.program_id(1) == 0)
        def _():
            o_ref[...] = jnp.zeros(o_ref.shape, F32)
        o_ref[...] += lax.dot_general(a_ref[...], g_ref[...], (((0,), (0,)), ((), ())), preferred_element_type=F32)

    return pl.pallas_call(
        kern, name=name, grid=(n_cols // tn, m_rows // tm),
        in_specs=[pl.BlockSpec((tm, k), lambda j, m: (m, 0)), pl.BlockSpec((tm, tn), lambda j, m: (m, j))],
        out_specs=pl.BlockSpec((k, tn), lambda j, m: (0, j)),
        out_shape=jax.ShapeDtypeStruct((k, n_cols), F32),
        compiler_params=_cparams(("parallel", "arbitrary")),
    )(a, g)


def _mm_fused(name, pairs, tile_ins, full_ins, outs, accs, epilogue, tm, tn, sub=None):
    m_rows, n_cols = pairs[0][0].shape[0], _pair_cols(pairs[0])
    n_p, n_t, n_f, n_o = len(pairs), len(tile_ins), len(full_ins), len(outs)
    sub = sub or tm

    def kern(*refs):
        base = 2 * n_p
        t_refs, f_refs = refs[base:base + n_t], refs[base + n_t:base + n_t + n_f]
        o_refs, a_refs = refs[base + n_t + n_f:base + n_t + n_f + n_o], refs[base + n_t + n_f + n_o:]
        if a_refs:
            @pl.when((pl.program_id(0) == 0) & (pl.program_id(1) == 0))
            def _():
                for a in a_refs:
                    a[...] = jnp.zeros(a.shape, F32)
        for h in range(tm // sub):
            rows = pl.ds(h * sub, sub)
            dots = [_pair_dot(pairs[p], refs[2 * p].at[rows, :], refs[2 * p + 1]) for p in range(n_p)]
            epilogue(dots, [t.at[rows, :] for t in t_refs], f_refs, [o.at[rows, :] for o in o_refs], a_refs)

    in_specs, flat = _pair_specs(pairs, tm, tn)
    for arr, cb in tile_ins:
        in_specs.append(pl.BlockSpec((tm, tn), functools.partial(lambda j, i, cb: (i, cb + j), cb=cb)))
        flat.append(arr)
    for arr in full_ins:
        in_specs.append(pl.BlockSpec(arr.shape, lambda j, i: (0, 0)))
        flat.append(arr)
    out_specs = [pl.BlockSpec((tm, f * tn), lambda j, i: (i, j)) for _, f in outs]
    out_specs += [pl.BlockSpec(s, lambda j, i: (0, 0)) for s in accs]
    out_shape = [jax.ShapeDtypeStruct((m_rows, f * n_cols), dt) for dt, f in outs]
    out_shape += [jax.ShapeDtypeStruct(s, F32) for s in accs]
    sem = ("arbitrary", "arbitrary") if accs else ("parallel", "parallel")
    return pl.pallas_call(
        kern, name=name, grid=(n_cols // tn, m_rows // tm), in_specs=in_specs, out_specs=out_specs,
        out_shape=out_shape, compiler_params=_cparams(sem),
    )(*flat)


def _peers():
    x, y, c = lax.axis_index("x"), lax.axis_index("y"), lax.axis_index("c")
    me = 4 * x + 2 * y + c
    peers = []
    for k in range(1, N_DEV):
        px = 1 - x if (k >> 2) & 1 else x
        py = 1 - y if (k >> 1) & 1 else y
        pc = 1 - c if k & 1 else c
        peers.append(((px, py, pc), 4 * px + 2 * py + pc))
    return me, peers


def _exchange(me, peers, items, send_sems, recv_sems, t0=0):
    started = []
    for t, (src_of, dst_of) in enumerate(items, start=t0):
        for k, (pid, plin) in enumerate(peers):
            cp = pltpu.make_async_remote_copy(
                src_ref=src_of(plin), dst_ref=dst_of(me), send_sem=send_sems.at[t, k], recv_sem=recv_sems.at[t, k],
                device_id=pid, device_id_type=MESH)
            cp.start()
            started.append(cp)
    for t, (src_of, dst_of) in enumerate(items, start=t0):
        for k, (pid, plin) in enumerate(peers):
            pltpu.make_async_remote_copy(
                src_ref=src_of(plin), dst_ref=dst_of(plin), send_sem=send_sems.at[t, k], recv_sem=recv_sems.at[t, k],
                device_id=pid, device_id_type=MESH).wait_recv()
    for cp in started:
        cp.wait_send()


def _hbm_spec():
    return pl.BlockSpec(memory_space=pltpu.HBM)


class _ChipGather:
    def __init__(self, shard):
        self.arrays = [shard]
        self.out_shape = [jax.ShapeDtypeStruct((N_DEV,) + shard.shape, shard.dtype)]
        self.in_specs, self.out_specs = [_hbm_spec()], [_hbm_spec()]
        self.scratch = [pltpu.SemaphoreType.DMA((SEMS_PER_ITEM,))]

    def _parts(self, ins, outs, sems):
        src, out = ins[0], outs[0]
        x, y, c = lax.axis_index("x"), lax.axis_index("y"), lax.axis_index("c")
        lin = lambda px, py, pc: 4 * px + 2 * py + pc
        sibling = (x, y, 1 - c)
        chips = [(1 - x, y), (x, 1 - y), (1 - x, 1 - y)]

        def copy(k, slot, to, from_src):
            return pltpu.make_async_remote_copy(
                src_ref=src if from_src else out.at[slot], dst_ref=out.at[slot], send_sem=sems.at[k],
                recv_sem=sems.at[N_DEV - 1 + k], device_id=to, device_id_type=MESH)

        own = pltpu.make_async_copy(src, out.at[lin(x, y, c)], sems.at[2 * (N_DEV - 1)])
        first = [copy(0, lin(x, y, c), sibling, True)]
        first += [copy(1 + j, lin(x, y, c), (px, py, c), True) for j, (px, py) in enumerate(chips)]
        passed = [copy(4 + j, lin(px, py, c), sibling, False) for j, (px, py) in enumerate(chips)]
        landed = [copy(1 + j, lin(px, py, c), sibling, True) for j, (px, py) in enumerate(chips)]
        from_sibling = [copy(0, lin(x, y, 1 - c), sibling, True)]
        from_sibling += [copy(4 + j, lin(px, py, 1 - c), sibling, False) for j, (px, py) in enumerate(chips)]
        return own, first, passed, landed, from_sibling

    def start(self, ins, outs, sems):
        own, first, _, _, _ = self._parts(ins, outs, sems)
        own.start()
        for cp in first:
            cp.start()

    def wait(self, ins, outs, sems):
        own, first, passed, landed, from_sibling = self._parts(ins, outs, sems)
        for arrival, forward in zip(landed, passed):
            arrival.wait_recv()
            forward.start()
        for cp in from_sibling:
            cp.wait_recv()
        for cp in first + passed:
            cp.wait_send()
        own.wait()


SEMS_PER_ITEM = 2 * (N_DEV - 1) + 1


class _Exchange:
    def __init__(self, items):
        self.arrays = [a for a, _ in items]
        self.scatter = [s for _, s in items]
        self.out_shape = [jax.ShapeDtypeStruct(a.shape if s else (N_DEV,) + a.shape, a.dtype) for a, s in items]
        self.in_specs = [_hbm_spec() for _ in items]
        self.out_specs = [_hbm_spec() for _ in items]
        self.scratch = [pltpu.SemaphoreType.DMA((SEMS_PER_ITEM * len(items),))]

    def _copies(self, ins, outs, sems, landing):
        me, peers = _peers()
        cps = []
        for t, scatter in enumerate(self.scatter):
            base = t * SEMS_PER_ITEM
            for k, (pid, plin) in enumerate(peers):
                src = ins[t].at[plin] if scatter else ins[t]
                dst = outs[t].at[plin if landing else me]
                cps.append(pltpu.make_async_remote_copy(
                    src_ref=src, dst_ref=dst, send_sem=sems.at[base + k], recv_sem=sems.at[base + N_DEV - 1 + k],
                    device_id=pid, device_id_type=MESH))
        return cps

    def _own(self, ins, outs, sems):
        me, _ = _peers()
        return [pltpu.make_async_copy(ins[t].at[me] if scatter else ins[t], outs[t].at[me],
                                      sems.at[t * SEMS_PER_ITEM + 2 * (N_DEV - 1)])
                for t, scatter in enumerate(self.scatter)]

    def start(self, ins, outs, sems):
        for cp in self._own(ins, outs, sems) + self._copies(ins, outs, sems, False):
            cp.start()

    def wait(self, ins, outs, sems):
        for cp in self._copies(ins, outs, sems, True):
            cp.wait_recv()
        for cp in self._copies(ins, outs, sems, False):
            cp.wait_send()
        for cp in self._own(ins, outs, sems):
            cp.wait()


def _carry(exchange, ins, outs, sems, first, last):
    @pl.when(first)
    def _():
        exchange.start(ins, outs, sems)

    @pl.when(last)
    def _():
        exchange.wait(ins, outs, sems)


def _exchange_alone(name, exchange):
    n = len(exchange.arrays)

    def body(*refs):
        exchange.start(refs[:n], refs[n:2 * n], refs[2 * n])
        exchange.wait(refs[:n], refs[n:2 * n], refs[2 * n])

    return pl.pallas_call(body, name=name, in_specs=exchange.in_specs, out_specs=exchange.out_specs,
                          out_shape=exchange.out_shape, scratch_shapes=exchange.scratch)(*exchange.arrays)


def _ada_modulation(c8, w_ada, b_ada_cols):
    wcols = w_ada.shape[1]

    def body(c_ref, w_ref, b_ref, call_ref, cond_ref, mod_ref, res, send_sems, recv_sems):
        me, peers = _peers()
        call_ref[me] = c_ref[...]
        _exchange(me, peers, [(lambda p: c_ref, lambda s: call_ref.at[s])], send_sems, recv_sems, t0=0)
        for dev in range(N_DEV):
            cv = call_ref[dev]
            cond = cv * _sigmoid(cv)
            cond_ref[dev] = cond
            res[dev] = jnp.dot(cond, w_ref[...], preferred_element_type=F32,
                               precision=lax.Precision.HIGHEST) + b_ref[...]
        mod_ref[me] = res[me]
        _exchange(me, peers, [(lambda p: res.at[p], lambda s: mod_ref.at[s])], send_sems, recv_sems, t0=1)

    vm = pl.BlockSpec(memory_space=pltpu.VMEM)
    return pl.pallas_call(
        body, name="ada_modulation", in_specs=[vm, vm, vm], out_specs=[vm, vm, vm],
        out_shape=[jax.ShapeDtypeStruct((N_DEV, 8, D), F32), jax.ShapeDtypeStruct((N_DEV, 8, D), F32),
                   jax.ShapeDtypeStruct((N_DEV, 8, wcols), F32)],
        scratch_shapes=[pltpu.VMEM((N_DEV, 8, wcols), F32),
                        pltpu.SemaphoreType.DMA((2, N_DEV - 1)), pltpu.SemaphoreType.DMA((2, N_DEV - 1))],
        compiler_params=pltpu.CompilerParams(vmem_limit_bytes=VMEM_LIMIT),
    )(c8, w_ada, b_ada_cols)


def _lru_gates(xc, wa_ref, ba_ref, wi_ref, bi_ref, sp_ref, row0):
    xcb = xc.astype(BF16)
    pre_r, pre_i = [], []
    for n in range(N_BLK):
        xb = xcb[:, n * BW:(n + 1) * BW]
        pre_r.append(jnp.dot(xb, wa_ref[n], preferred_element_type=F32))
        pre_i.append(jnp.dot(xb, wi_ref[n], preferred_element_type=F32))
    r = _sigmoid(jnp.concatenate(pre_r, axis=1) + ba_ref[...])
    ig = _sigmoid(jnp.concatenate(pre_i, axis=1) + bi_ref[...])
    log_a = (-LRU_C) * r * sp_ref[...]
    a = jnp.exp(log_a)
    e2 = jnp.exp(2.0 * log_a)
    mult_raw = jnp.sqrt(1.0 - e2)
    rows = row0 + lax.broadcasted_iota(jnp.int32, xc.shape, 0)
    start = rows == 0
    mult = jnp.where(start, 1.0, mult_raw)
    return xcb, r, ig, a, e2, mult_raw, mult, start


def _conv(xpad, cw_ref, cb_ref, tc):
    out = cb_ref[...]
    for tap in range(CONV_W):
        out = out + cw_ref[tap:tap + 1, :] * xpad[pl.ds(8 - (CONV_W - 1) + tap, tc), :]
    return out


def _rnn_fwd(proj_rnn, cw, cb, wa, ba, wi, bi, sp, carry):
    s_len = proj_rnn.shape[0]
    tc = RNN_CHUNK
    n_c = len(carry.arrays)

    def kern(*refs):
        y_ref, xr_ref, cw_ref, cb_ref, wa_ref, ba_ref, wi_ref, bi_ref, sp_ref = refs[:9]
        hs_ref, ur_ref = refs[9 + n_c:11 + n_c]
        xpad, a_scr, u_scr, h_scr = refs[11 + 2 * n_c:15 + 2 * n_c]
        i = pl.program_id(0)
        _carry(carry, refs[9:9 + n_c], refs[11 + n_c:11 + 2 * n_c], refs[-1], i == 0, i == s_len // tc - 1)

        @pl.when(i == 0)
        def _():
            xpad[0:8, :] = jnp.zeros((8, DR), F32)
            h_scr[...] = jnp.zeros((1, DR), F32)

        xpad[8:8 + tc, :] = xr_ref[...]
        xc = _conv(xpad, cw_ref, cb_ref, tc)
        xpad[0:8, :] = xpad[tc:tc + 8, :]
        _, r, ig, a, e2, mult_raw, mult, start = _lru_gates(xc, wa_ref, ba_ref, wi_ref, bi_ref, sp_ref, i * tc)
        a_scr[...] = a
        u_scr[...] = mult * (ig * xc)

        def step(t, h):
            h = a_scr[pl.ds(t, 1), :] * h + u_scr[pl.ds(t, 1), :]
            hs_ref[pl.ds(t, 1), :] = h
            return h

        h_scr[...] = lax.fori_loop(0, tc, step, h_scr[...], unroll=8)
        gy, _ = _gelu(y_ref[...])
        ur_ref[...] = (gy * hs_ref[...]).astype(BF16)

    full = lambda a: pl.BlockSpec(a.shape, lambda i: (0,) * a.ndim)
    res = pl.pallas_call(
        kern, name="rnn_fwd", grid=(s_len // tc,),
        in_specs=[pl.BlockSpec((tc, DR), lambda i: (i, 0)), pl.BlockSpec((tc, DR), lambda i: (i, 1)),
                  full(cw), full(cb), full(wa), full(ba), full(wi), full(bi), full(sp)] + carry.in_specs,
        out_specs=[pl.BlockSpec((tc, DR), lambda i: (i, 0)), pl.BlockSpec((tc, DR), lambda i: (i, 0))] + carry.out_specs,
        out_shape=[jax.ShapeDtypeStruct((s_len, DR), F32), jax.ShapeDtypeStruct((s_len, DR), BF16)] + carry.out_shape,
        scratch_shapes=[pltpu.VMEM((tc + 8, DR), F32), pltpu.VMEM((tc, DR), F32), pltpu.VMEM((tc, DR), F32),
                        pltpu.VMEM((1, DR), F32)] + carry.scratch,
        compiler_params=_cparams(("arbitrary",)),
    )(proj_rnn, proj_rnn, cw, cb, wa, ba, wi, bi, sp, *carry.arrays)
    return res[0], res[1], res[2:]


def _rnn_bwd(proj_rnn, hs, du, cw, cb, wa, wat, ba, wi, wit, bi, sp, dsp_dlam):
    s_len = proj_rnn.shape[0]
    tc = RNN_CHUNK
    nch = s_len // tc

    def kern(y_ref, xr_ref, xrp_ref, hs_ref, hsp_ref, du_ref, cw_ref, cb_ref, wa_ref, wat_ref, ba_ref, wi_ref,
             wit_ref, bi_ref, sp_ref, dspl_ref, drnn_ref, dwa_ref, dwi_ref, vec_ref,
             xpad, hpad, a_scr, d_scr, g_scr, dxcpad, ag_scr):
        j = pl.program_id(0)
        n = nch - 1 - j

        @pl.when(j == 0)
        def _():
            dwa_ref[...] = jnp.zeros(dwa_ref.shape, F32)
            dwi_ref[...] = jnp.zeros(dwi_ref.shape, F32)
            vec_ref[...] = jnp.zeros(vec_ref.shape, F32)
            ag_scr[...] = jnp.zeros((1, DR), F32)
            dxcpad[tc:tc + 8, :] = jnp.zeros((8, DR), F32)

        has_prev = n > 0
        xpad[0:8, :] = jnp.where(has_prev, xrp_ref[...], 0.0)
        xpad[8:8 + tc, :] = xr_ref[...]
        hpad[0:8, :] = jnp.where(has_prev, hsp_ref[...], 0.0)
        hpad[8:8 + tc, :] = hs_ref[...]
        xc = _conv(xpad, cw_ref, cb_ref, tc)
        xcb, r, ig, a, e2, mult_raw, mult, start = _lru_gates(xc, wa_ref, ba_ref, wi_ref, bi_ref, sp_ref, n * tc)

        y = y_ref[...]
        gy, th = _gelu(y)
        duv = du_ref[...].astype(F32)
        drnn_ref[:, 0:DR] = (duv * hs_ref[...] * _gelu_grad(y, th)).astype(BF16)
        a_scr[...] = a
        d_scr[...] = duv * gy

        def step(tt, ag):
            t = tc - 1 - tt
            g = d_scr[pl.ds(t, 1), :] + ag
            g_scr[pl.ds(t, 1), :] = g
            return a_scr[pl.ds(t, 1), :] * g

        ag_scr[...] = lax.fori_loop(0, tc, step, ag_scr[...], unroll=8)
        g = g_scr[...]
        da = g * hpad[pl.ds(7, tc), :]
        gx = g * xc
        dmult = jnp.where(start, 0.0, gx * ig)
        di = gx * mult
        dxc = g * mult * ig
        dlog_a = da * a - jnp.where(start, 0.0, dmult * e2 / mult_raw)
        dr = dlog_a * ((-LRU_C) * sp_ref[...])
        vec_ref[7:8, :] += _rowsum(dlog_a * ((-LRU_C) * r))
        dpr = dr * r * (1.0 - r)
        dpi = di * ig * (1.0 - ig)
        vec_ref[5:6, :] += _rowsum(dpr)
        vec_ref[6:7, :] += _rowsum(dpi)
        dprb, dpib = dpr.astype(BF16), dpi.astype(BF16)
        extra = []
        for b in range(N_BLK):
            sl = slice(b * BW, (b + 1) * BW)
            extra.append(jnp.dot(dprb[:, sl], wat_ref[b], preferred_element_type=F32)
                         + jnp.dot(dpib[:, sl], wit_ref[b], preferred_element_type=F32))
            dn = (((0,), (0,)), ((), ()))
            dwa_ref[b] += lax.dot_general(xcb[:, sl], dprb[:, sl], dn, preferred_element_type=F32)
            dwi_ref[b] += lax.dot_general(xcb[:, sl], dpib[:, sl], dn, preferred_element_type=F32)
        dxc = dxc + jnp.concatenate(extra, axis=1)
        vec_ref[4:5, :] += _rowsum(dxc)
        dxcpad[0:tc, :] = dxc
        dxr = jnp.zeros((tc, DR), F32)
        for tap in range(CONV_W):
            shift = CONV_W - 1 - tap
            dxr = dxr + cw_ref[tap:tap + 1, :] * dxcpad[pl.ds(shift, tc), :]
            vec_ref[tap:tap + 1, :] += _rowsum(dxc * xpad[pl.ds(8 - shift, tc), :])
        dxcpad[tc:tc + 8, :] = dxcpad[0:8, :]
        drnn_ref[:, DR:2 * DR] = dxr.astype(BF16)

        @pl.when(j == nch - 1)
        def _():
            vec_ref[7:8, :] = vec_ref[7:8, :] * dspl_ref[...]

    full = lambda a: pl.BlockSpec(a.shape, lambda j: (0,) * a.ndim)
    rev = lambda j: nch - 1 - j
    prev8 = lambda j: jnp.maximum((nch - 1 - j) * (tc // 8) - 1, 0)
    return pl.pallas_call(
        kern, name="rnn_bwd", grid=(nch,),
        in_specs=[pl.BlockSpec((tc, DR), lambda j: (rev(j), 0)), pl.BlockSpec((tc, DR), lambda j: (rev(j), 1)),
                  pl.BlockSpec((8, DR), lambda j: (prev8(j), 1)),
                  pl.BlockSpec((tc, DR), lambda j: (rev(j), 0)), pl.BlockSpec((8, DR), lambda j: (prev8(j), 0)),
                  pl.BlockSpec((tc, DR), lambda j: (rev(j), 0)),
                  full(cw), full(cb), full(wa), full(wat), full(ba), full(wi), full(wit), full(bi), full(sp),
                  full(dsp_dlam)],
        out_specs=[pl.BlockSpec((tc, 2 * DR), lambda j: (rev(j), 0)),
                   pl.BlockSpec((N_BLK, BW, BW), lambda j: (0, 0, 0)), pl.BlockSpec((N_BLK, BW, BW), lambda j: (0, 0, 0)),
                   pl.BlockSpec((8, DR), lambda j: (0, 0))],
        out_shape=[jax.ShapeDtypeStruct((s_len, 2 * DR), BF16), jax.ShapeDtypeStruct((N_BLK, BW, BW), F32),
                   jax.ShapeDtypeStruct((N_BLK, BW, BW), F32), jax.ShapeDtypeStruct((8, DR), F32)],
        scratch_shapes=[pltpu.VMEM((tc + 8, DR), F32), pltpu.VMEM((tc + 8, DR), F32), pltpu.VMEM((tc, DR), F32),
                        pltpu.VMEM((tc, DR), F32), pltpu.VMEM((tc, DR), F32), pltpu.VMEM((tc + 8, DR), F32),
                        pltpu.VMEM((1, DR), F32)],
        compiler_params=_cparams(("arbitrary",)),
    )(proj_rnn, proj_rnn, proj_rnn, hs, hs, du, cw, cb, wa, wat, ba, wi, wit, bi, sp, dsp_dlam)


ATT_BLK = 2048


def _attn_units():
    return [(g, d, b, b * QB * d + r)
            for g, d in enumerate(DILATIONS) for b in range(ATT_BLK // (QB * d)) for r in range(d)]


def _rows_at(ref, start, n, d):
    return ref[pl.ds(start, n, stride=d), :] if d > 1 else ref[start:start + n, :]


def _add_rows_at(ref, start, n, d, val):
    if d > 1:
        ref[pl.ds(start, n, stride=d), :] = ref[pl.ds(start, n, stride=d), :] + val
    else:
        ref[start:start + n, :] = ref[start:start + n, :] + val


def _stack_heads(x, first):
    zero = jnp.zeros_like(x)
    return jnp.concatenate([jnp.where(first, x, zero), jnp.where(first, zero, x)], axis=0)


def _unstack_heads(x, first):
    return jnp.where(first, x[:QB], x[QB:])


def _head_column(x, lane, half):
    return jnp.sum(jnp.where(lane == 64 * half, x, 0.0), axis=-1, keepdims=True)


def _band_masks():
    row = lax.broadcasted_iota(jnp.int32, (2 * QB, 2 * QB), 0) & (QB - 1)
    col = lax.broadcasted_iota(jnp.int32, (2 * QB, 2 * QB), 1)
    dist = QB + row - col
    lane = lax.broadcasted_iota(jnp.int32, (1, HEAD_PAIR), 1)
    return (dist >= 0) & (dist <= QB), col >= QB, lane


def _attention_specs(nblk, blk_of):
    cur = lambda off: pl.BlockSpec((ATT_BLK, HEAD_PAIR), lambda hp, j: (blk_of(j), off + hp))
    prev = lambda off: pl.BlockSpec((ATT_BLK, HEAD_PAIR), lambda hp, j: (jnp.maximum(blk_of(j) - 1, 0), off + hp))
    return cur, prev


def _attention_fwd(qkv):
    s_len = qkv.shape[0]
    nblk = s_len // ATT_BLK
    units = _attn_units()
    nt = (((1,), (1,)), ((), ()))

    def kern(q_ref, k_ref, v_ref, o_ref, lse_ref, kbuf, vbuf, og, lg):
        blk = pl.program_id(1)

        @pl.when(blk == 0)
        def _():
            kbuf[0:ATT_BLK, :] = jnp.zeros((ATT_BLK, HEAD_PAIR), F32)
            vbuf[0:ATT_BLK, :] = jnp.zeros((ATT_BLK, HEAD_PAIR), F32)

        kbuf[ATT_BLK:2 * ATT_BLK, :] = k_ref[...]
        vbuf[ATT_BLK:2 * ATT_BLK, :] = v_ref[...]
        band, later, lane = _band_masks()
        first = lane < 64
        band0 = band & (later | (blk > 0))
        for g, d, b, start in units:
            kstart = ATT_BLK + start - QB * d
            qs = _stack_heads(_rows_at(q_ref, start, QB, d).astype(BF16), first)
            k2 = _rows_at(kbuf, kstart, 2 * QB, d).astype(BF16)
            v2 = _rows_at(vbuf, kstart, 2 * QB, d).astype(BF16)
            s = lax.dot_general(qs, k2, nt, preferred_element_type=F32) * SCALE
            s = jnp.where(band if b > 0 else band0, s, NEG)
            mx = jnp.max(s, axis=-1, keepdims=True)
            p = jnp.exp(s - mx)
            den = jnp.sum(p, axis=-1, keepdims=True)
            out = jnp.dot(p.astype(BF16), v2, preferred_element_type=F32) / den
            lse2 = jnp.broadcast_to(mx + jnp.log(den), (2 * QB, HEAD_PAIR))
            if d > 1:
                og[g, pl.ds(start, QB, stride=d), :] = _unstack_heads(out, first)
                lg[g, pl.ds(start, QB, stride=d), :] = _unstack_heads(lse2, first)
            else:
                og[g, start:start + QB, :] = _unstack_heads(out, first)
                lg[g, start:start + QB, :] = _unstack_heads(lse2, first)
        kbuf[0:ATT_BLK, :] = kbuf[ATT_BLK:2 * ATT_BLK, :]
        vbuf[0:ATT_BLK, :] = vbuf[ATT_BLK:2 * ATT_BLK, :]
        mx = jnp.maximum(jnp.maximum(lg[0], lg[1]), lg[2])
        es = [jnp.exp(lg[g] - mx) for g in range(3)]
        tot = es[0] + es[1] + es[2]
        o_ref[...] = ((es[0] * og[0] + es[1] * og[1] + es[2] * og[2]) / tot).astype(BF16)
        lse_ref[...] = mx + jnp.log(tot)

    cur, _ = _attention_specs(nblk, lambda j: j)
    out_spec = pl.BlockSpec((ATT_BLK, HEAD_PAIR), lambda hp, j: (j, hp))
    return pl.pallas_call(
        kern, name="attention_fwd", grid=(N_PAIR, nblk), in_specs=[cur(0), cur(N_PAIR), cur(2 * N_PAIR)],
        out_specs=[out_spec, out_spec],
        out_shape=[jax.ShapeDtypeStruct((s_len, D), BF16), jax.ShapeDtypeStruct((s_len, D), F32)],
        scratch_shapes=[pltpu.VMEM((2 * ATT_BLK, HEAD_PAIR), F32), pltpu.VMEM((2 * ATT_BLK, HEAD_PAIR), F32),
                        pltpu.VMEM((3, ATT_BLK, HEAD_PAIR), F32), pltpu.VMEM((3, ATT_BLK, HEAD_PAIR), F32)],
        compiler_params=_cparams(("arbitrary", "arbitrary")),
    )(qkv, qkv, qkv)


def _attention_bwd(qkv, do, o, lse, carry):
    s_len = qkv.shape[0]
    nblk = s_len // ATT_BLK
    units = _attn_units()
    nt = (((1,), (1,)), ((), ()))
    tn = (((0,), (0,)), ((), ()))
    n_c = len(carry.arrays) if carry else 0
    c_in, c_out = (carry.in_specs, carry.out_specs) if carry else ([], [])
    c_shape, c_scratch, c_arrays = (carry.out_shape, carry.scratch, carry.arrays) if carry else ([], [], [])

    def kern(*refs):
        q_ref, k_ref, v_ref, kp_ref, vp_ref, do_ref, o_ref, lse_ref = refs[:8]
        dq_ref, dk_ref, dv_ref = refs[8 + n_c:11 + n_c]
        kbuf, vbuf, dkbuf, dvbuf, dq_scr, dsum = refs[11 + 2 * n_c:17 + 2 * n_c]
        hp, j = pl.program_id(0), pl.program_id(1)
        blk = nblk - 1 - j
        if carry:
            _carry(carry, refs[8:8 + n_c], refs[11 + n_c:11 + 2 * n_c], refs[-1], (hp == 0) & (j == 0),
                   (hp == N_PAIR - 1) & (j == nblk - 1))
        zeros = jnp.zeros((ATT_BLK, HEAD_PAIR), F32)

        @pl.when(j == 0)
        def _():
            dkbuf[ATT_BLK:2 * ATT_BLK, :] = zeros
            dvbuf[ATT_BLK:2 * ATT_BLK, :] = zeros

        @pl.when(j > 0)
        def _():
            dkbuf[ATT_BLK:2 * ATT_BLK, :] = dkbuf[0:ATT_BLK, :]
            dvbuf[ATT_BLK:2 * ATT_BLK, :] = dvbuf[0:ATT_BLK, :]

        dkbuf[0:ATT_BLK, :] = zeros
        dvbuf[0:ATT_BLK, :] = zeros
        dq_scr[...] = zeros
        kbuf[0:ATT_BLK, :] = kp_ref[...]
        kbuf[ATT_BLK:2 * ATT_BLK, :] = k_ref[...]
        vbuf[0:ATT_BLK, :] = vp_ref[...]
        vbuf[ATT_BLK:2 * ATT_BLK, :] = v_ref[...]
        band, later, lane = _band_masks()
        first = lane < 64
        band0 = band & (later | (blk > 0))
        prod = do_ref[...] * o_ref[...].astype(F32)
        dsum[...] = jnp.where(first, jnp.sum(jnp.where(first, prod, 0.0), axis=-1, keepdims=True),
                              jnp.sum(jnp.where(first, 0.0, prod), axis=-1, keepdims=True))
        for g, d, b, start in units:
            kstart = ATT_BLK + start - QB * d
            qs = _stack_heads(_rows_at(q_ref, start, QB, d).astype(BF16), first)
            dos = _stack_heads(_rows_at(do_ref, start, QB, d).astype(BF16), first)
            k2 = _rows_at(kbuf, kstart, 2 * QB, d).astype(BF16)
            v2 = _rows_at(vbuf, kstart, 2 * QB, d).astype(BF16)
            ds_rows = _rows_at(dsum, start, QB, d)
            lse_rows = _rows_at(lse_ref, start, QB, d)
            dcol = jnp.concatenate([_head_column(ds_rows, lane, 0), _head_column(ds_rows, lane, 1)], axis=0)
            lcol = jnp.concatenate([_head_column(lse_rows, lane, 0), _head_column(lse_rows, lane, 1)], axis=0)
            s = lax.dot_general(qs, k2, nt, preferred_element_type=F32) * SCALE
            p = jnp.exp(jnp.where(band if b > 0 else band0, s, NEG) - lcol)
            dp = lax.dot_general(dos, v2, nt, preferred_element_type=F32)
            ds = (p * (dp - dcol) * SCALE).astype(BF16)
            _add_rows_at(dq_scr, start, QB, d, _unstack_heads(jnp.dot(ds, k2, preferred_element_type=F32), first))
            _add_rows_at(dkbuf, kstart, 2 * QB, d, lax.dot_general(ds, qs, tn, preferred_element_type=F32))
            _add_rows_at(dvbuf, kstart, 2 * QB, d, lax.dot_general(p.astype(BF16), dos, tn, preferred_element_type=F32))
        dq_ref[...] = dq_scr[...].astype(BF16)
        dk_ref[...] = dkbuf[ATT_BLK:2 * ATT_BLK, :].astype(BF16)
        dv_ref[...] = dvbuf[ATT_BLK:2 * ATT_BLK, :].astype(BF16)

    rev = lambda j: nblk - 1 - j
    cur, prev = _attention_specs(nblk, rev)
    buf = lambda rows: pltpu.VMEM((rows, HEAD_PAIR), F32)
    res = pl.pallas_call(
        kern, name="attention_bwd", grid=(N_PAIR, nblk),
        in_specs=[cur(0), cur(N_PAIR), cur(2 * N_PAIR), prev(N_PAIR), prev(2 * N_PAIR), cur(0), cur(0), cur(0)] + c_in,
        out_specs=[cur(0)] * 3 + c_out,
        out_shape=[jax.ShapeDtypeStruct((s_len, D), BF16)] * 3 + c_shape,
        scratch_shapes=[buf(2 * ATT_BLK), buf(2 * ATT_BLK), buf(2 * ATT_BLK), buf(2 * ATT_BLK), buf(ATT_BLK),
                        buf(ATT_BLK)] + c_scratch,
        compiler_params=_cparams(("arbitrary", "arbitrary")),
    )(qkv, qkv, qkv, qkv, qkv, do, o, lse, *c_arrays)
    return res[0], res[1], res[2], res[3:]


def _adamw(w, g, m, v):
    m = ADAM_B1 * m + (1.0 - ADAM_B1) * g
    v = ADAM_B2 * v + (1.0 - ADAM_B2) * (g * g)
    m_hat = m / (1.0 - ADAM_B1 ** ADAM_STEP)
    v_hat = v / (1.0 - ADAM_B2 ** ADAM_STEP)
    delta = -ADAM_LR * (m_hat / (jnp.sqrt(v_hat) + ADAM_EPS) + ADAM_WD * w)
    return delta, m, v


def _adam_packed(name, recv, w, m, v, tm):
    n_rows, width = w.shape
    summed = recv.ndim == 3

    def kern(r_ref, w_ref, m_ref, v_ref, g_out, d_out, m_out, v_out):
        if summed:
            g = r_ref[0].astype(F32)
            for s in range(1, N_DEV):
                g = g + r_ref[s].astype(F32)
        else:
            g = r_ref[...]
        delta, mn, vn = _adamw(w_ref[...], g, m_ref[...], v_ref[...])
        g_out[...] = g
        d_out[...] = delta
        m_out[...] = mn
        v_out[...] = vn

    tile = pl.BlockSpec((tm, width), lambda i: (i, 0))
    rspec = pl.BlockSpec((N_DEV, tm, width), lambda i: (0, i, 0)) if summed else tile
    return pl.pallas_call(
        kern, name=name, grid=(n_rows // tm,), in_specs=[rspec, tile, tile, tile], out_specs=[tile] * 4,
        out_shape=[jax.ShapeDtypeStruct((n_rows, width), F32)] * 4, compiler_params=_cparams(("parallel",)),
    )(recv, w, m, v)


def _sum_slots(recv):
    _, n_rows, width = recv.shape

    def kern(r_ref, o_ref):
        g = r_ref[0]
        for s in range(1, N_DEV):
            g = g + r_ref[s]
        o_ref[...] = g

    return pl.pallas_call(
        kern, name="sum_small", grid=(1,), in_specs=[pl.BlockSpec(recv.shape, lambda i: (0, 0, 0))],
        out_specs=pl.BlockSpec((n_rows, width), lambda i: (0, 0)),
        out_shape=jax.ShapeDtypeStruct((n_rows, width), F32), compiler_params=_cparams(("arbitrary",)),
    )(recv)


def _adam_w_ada(cond_t, dmod_cols, w, m, v):
    n_rows, width = w.shape
    tm = ROW_TILE

    def kern(c_ref, d_ref, w_ref, m_ref, v_ref, g_out, d_out, m_out, v_out):
        g = c_ref[:, 0:1] * d_ref[0:1, :]
        for b in range(1, N_DEV):
            g = g + c_ref[:, b:b + 1] * d_ref[b:b + 1, :]
        delta, mn, vn = _adamw(w_ref[...], g, m_ref[...], v_ref[...])
        g_out[...] = g
        d_out[...] = delta
        m_out[...] = mn
        v_out[...] = vn

    tile = pl.BlockSpec((tm, width), lambda i: (i, 0))
    return pl.pallas_call(
        kern, name="adam_w_ada", grid=(n_rows // tm,),
        in_specs=[pl.BlockSpec((tm, N_DEV), lambda i: (i, 0)), pl.BlockSpec((N_DEV, width), lambda i: (0, 0)),
                  tile, tile, tile],
        out_specs=[tile] * 4, out_shape=[jax.ShapeDtypeStruct((n_rows, width), F32)] * 4,
        compiler_params=_cparams(("parallel",)),
    )(cond_t, dmod_cols, w, m, v)


def _rows(a, n_rows=None):
    flat = a.reshape(-1)
    need = (n_rows if n_rows is not None else -(-flat.shape[0] // D)) * D
    if need != flat.shape[0]:
        flat = jnp.concatenate([flat, jnp.zeros((need - flat.shape[0],), flat.dtype)])
    return flat.reshape(-1, D)


def _slots_to_cols(s, k, cols):
    return s.reshape(N_DEV, k, cols).transpose(1, 0, 2).reshape(k, N_DEV * cols)


def kernel(x, c, w_ada, b_ada, w_in, conv_w, conv_b, lru_wa, lru_ba, lru_wi, lru_bi, lru_lambda, w_proj_rnn, w_proj_attn, b_gate, w_out, ln1_g, ln1_b, w_ffn_gate, w_ffn_up, w_ffn_down, ln2_g, ln2_b, loss_target, m_w_ada, m_b_ada, m_w_in, m_conv_w, m_conv_b, m_lru_wa, m_lru_ba, m_lru_wi, m_lru_bi, m_lru_lambda, m_w_proj_rnn, m_w_proj_attn, m_b_gate, m_w_out, m_ln1_g, m_ln1_b, m_w_ffn_gate, m_w_ffn_up, m_w_ffn_down, m_ln2_g, m_ln2_b, v_w_ada, v_b_ada, v_w_in, v_conv_w, v_conv_b, v_lru_wa, v_lru_ba, v_lru_wi, v_lru_bi, v_lru_lambda, v_w_proj_rnn, v_w_proj_attn, v_b_gate, v_w_out, v_ln1_g, v_ln1_b, v_w_ffn_gate, v_w_ffn_up, v_w_ffn_down, v_ln2_g, v_ln2_b):
    s_len = x.shape[1]
    me = 4 * lax.axis_index("x") + 2 * lax.axis_index("y") + lax.axis_index("c")
    xs = x[0]
    tgt = loss_target[0]
    tm = ROW_TILE

    b_ada_cols = lax.dynamic_slice(b_ada, (0, me * 768), (1, 768))
    pad_cols = lambda a: jnp.concatenate([a, jnp.zeros((a.shape[0], D - a.shape[1]), F32)], axis=1)
    c8 = jnp.concatenate([c, pad_cols(conv_w[0]), pad_cols(b_gate[0]), jnp.zeros((1, D), F32)], axis=0)
    c_all, cond_blocks, mod_parts = _ada_modulation(c8, w_ada[0], b_ada_cols)
    cond_all = cond_blocks[:, 0, :]
    mod = mod_parts[:, 0, :].reshape(6, D)
    mod8 = jnp.concatenate([mod, jnp.zeros((2, D), F32)], axis=0)
    cw = c_all[:, 1:1 + CONV_W, :DR // N_DEV].transpose(1, 0, 2).reshape(CONV_W, DR)
    bg = c_all[:, 1 + CONV_W:3 + CONV_W, :D // N_DEV].transpose(1, 0, 2).reshape(2, D)
    bg8 = jnp.concatenate([bg, jnp.zeros((6, D), F32)], axis=0)

    late = dict(w_ffn_gate=w_ffn_gate, w_ffn_up=w_ffn_up, w_proj_rnn=w_proj_rnn, w_proj_attn=w_proj_attn,
                w_out=w_out, w_ffn_down=w_ffn_down)
    late_shard = jnp.concatenate([late[name][0].reshape(r, D) for name, r in LATE_ROWS], axis=0).astype(BF16)

    cb = conv_b
    wa_b, wi_b = lru_wa[0].astype(BF16), lru_wi[0].astype(BF16)
    wat_b, wit_b = jnp.swapaxes(wa_b, 1, 2), jnp.swapaxes(wi_b, 1, 2)
    ba, bi = lru_ba.reshape(1, DR), lru_bi.reshape(1, DR)
    sp = jax.nn.softplus(-lru_lambda)
    dsp_dlam = -jax.nn.sigmoid(-lru_lambda)
    ln1 = jnp.concatenate([ln1_g, ln1_b, jnp.zeros((6, D), F32)], axis=0)
    ln2 = jnp.concatenate([ln2_g, ln2_b, jnp.zeros((6, D), F32)], axis=0)

    def f1(ri, fi, ro, ao):
        n, _ = _ln(ri[0][...])
        ro[0][...] = (n * (1.0 + fi[0][1:2, :]) + fi[0][0:1, :]).astype(BF16)

    h1, w_in_slots = _rowwise(f1, "ln_mod1", s_len, tm, [(xs, D, 0)], [mod8], [(D, BF16)], [],
                              carry=_ChipGather(w_in[0].astype(BF16)))
    w_in_f = w_in_slots.transpose(1, 0, 2).reshape(D, N_DEV * 960)
    w_rnn, w_qkv, w_gates = w_in_f[:, :2 * DR], w_in_f[:, 2 * DR:2 * DR + 3 * D], w_in_f[:, 2 * DR + 3 * D:]
    proj_rnn = _mm("mm_in_rnn", [(h1, w_rnn)], F32, 512, DR)
    qkv = _mm("mm_in_qkv", [(h1, w_qkv)], F32, 512, D)
    gates = _mm("mm_in_gates", [(h1, w_gates)], F32, 512, D)

    hs, u_rnn, (late_all,) = _rnn_fwd(proj_rnn, cw, cb, wa_b, ba, wi_b, bi, sp, _Exchange([(late_shard, False)]))
    offs, o = {}, 0
    for name, r in LATE_ROWS:
        offs[name] = (o, o + r)
        o += r
    part = lambda name: late_all[:, offs[name][0]:offs[name][1], :]
    w_g, w_u = _slots_to_cols(part("w_ffn_gate"), D, 352), _slots_to_cols(part("w_ffn_up"), D, 352)
    w_pr = part("w_proj_rnn").reshape(DR, D)
    w_pa = part("w_proj_attn").reshape(D, D)
    w_o = part("w_out").reshape(D, D)
    w_dn = part("w_ffn_down").reshape(DFF, D)

    u_attn, lse = _attention_fwd(qkv)

    def e_merge(dots, ti, fi, ro, ao):
        g_r = _sigmoid(ti[0][...] + fi[0][0:1, :])
        g_a = _sigmoid(ti[1][...] + fi[0][1:2, :])
        ro[0][...] = dots[0].astype(BF16)
        ro[1][...] = dots[1].astype(BF16)
        ro[2][...] = (g_r * dots[0] + g_a * dots[1]).astype(BF16)

    pr, pa, merged = _mm_fused("mm_proj_merge", [(u_rnn, w_pr), (u_attn, w_pa)], [(gates, 0), (gates, 1)], [bg8],
                               [(BF16, 1), (BF16, 1), (BF16, 1)], [], e_merge, 512, D, sub=256)
    def e_post1(dots, ti, fi, ro, ao):
        md, l1 = fi[0], fi[1]
        fv = dots[0]
        n1, _ = _ln(ALPHA * ti[0][...] + md[2:3, :] * fv)
        x1 = n1 * l1[0:1, :] + l1[1:2, :]
        n0, _ = _ln(x1)
        ro[0][...] = fv
        ro[1][...] = x1
        ro[2][...] = (n0 * (1.0 + md[4:5, :]) + md[3:4, :]).astype(BF16)

    f, x1, h2 = _mm_fused("mm_out_post1", [(merged, w_o)], [(xs, 0)], [mod8, ln1], [(F32, 1), (F32, 1), (BF16, 1)], [],
                          e_post1, 512, D, sub=256)

    def e_swiglu(dots, ti, fi, ro, ao):
        gp, up = dots
        ro[0][...] = gp.astype(BF16)
        ro[1][...] = up.astype(BF16)
        ro[2][...] = (gp * _sigmoid(gp) * up).astype(BF16)

    gpre, upre, act = _mm_fused("mm_ffn_in", [(h2, w_g), (h2, w_u)], [], [], [(BF16, 1)] * 3, [], e_swiglu, 512, DFF // 2)
    def e_loss(dots, ti, fi, ro, ao):
        md, l2 = fi[0], fi[1]
        f2v = dots[0]
        n2, rstd = _ln(ALPHA * ti[0][...] + md[5:6, :] * f2v)
        err = n2 * l2[0:1, :] + l2[1:2, :] - ti[1][...]
        dx2 = err * (1.0 / D)
        dz2 = _ln_bwd(dx2 * l2[0:1, :], n2, rstd)
        ro[0][...] = dz2
        ro[1][...] = (md[5:6, :] * dz2).astype(BF16)
        acc = ao[0]
        acc[0:1, :] += _rowsum(err * err) * (0.5 / D)
        acc[1:2, :] += _rowsum(dx2 * n2)
        acc[2:3, :] += _rowsum(dx2)
        acc[3:4, :] += _rowsum(dz2 * f2v)

    dz2, df2, acc5 = _mm_fused("mm_ffn_out_loss", [(act, w_dn)], [(x1, 0), (tgt, 0)], [mod8, ln2],
                               [(F32, 1), (BF16, 1)], [(8, D)], e_loss, 512, D, sub=256)

    d_w_dn = _mm_tn("mmt_ffn_down", act, df2, 1024, 512)

    def e_dswiglu(dots, ti, fi, ro, ao):
        da = dots[0]
        gp, up = ti[0][...].astype(F32), ti[1][...].astype(F32)
        sg = _sigmoid(gp)
        ro[0][...] = (da * up * sg * (1.0 + gp * (1.0 - sg))).astype(BF16)
        ro[1][...] = (da * gp * sg).astype(BF16)

    dgp, dup = _mm_fused("mm_d_ffn_out", [(df2, w_dn, True)], [(gpre, 0), (upre, 0)], [], [(BF16, 1)] * 2, [],
                         e_dswiglu, 512, DFF // 2, sub=256)
    d_w_g = _mm_tn("mmt_ffn_gate", h2, dgp, 2048, DFF // 2)
    d_w_u = _mm_tn("mmt_ffn_up", h2, dup, 2048, DFF // 2)

    def e_dpost1(dots, ti, fi, ro, ao):
        md, l1 = fi[0], fi[1]
        dh2v = dots[0] + dots[1]
        fv = ti[1][...]
        n1, rstd1 = _ln(ALPHA * ti[0][...] + md[2:3, :] * fv)
        n0, rstd0 = _ln(n1 * l1[0:1, :] + l1[1:2, :])
        dx1 = ALPHA * ti[2][...] + _ln_bwd(dh2v * (1.0 + md[4:5, :]), n0, rstd0)
        dz1 = _ln_bwd(dx1 * l1[0:1, :], n1, rstd1)
        ro[0][...] = ALPHA * dz1
        ro[1][...] = (md[2:3, :] * dz1).astype(BF16)
        acc = ao[0]
        acc[0:1, :] += _rowsum(dh2v * n0)
        acc[1:2, :] += _rowsum(dh2v)
        acc[2:3, :] += _rowsum(dx1 * n1)
        acc[3:4, :] += _rowsum(dx1)
        acc[4:5, :] += _rowsum(dz1 * fv)

    dxp, df, acc2 = _mm_fused("mm_d_ffn_in_post1", [(dgp, w_g, True), (dup, w_u, True)], [(xs, 0), (f, 0), (dz2, 0)],
                              [mod8, ln1], [(F32, 1), (BF16, 1)], [(8, D)], e_dpost1, 256, D)
    d_w_o = _mm_tn("mmt_out", merged, df, 2048, D)

    def e_dmerge(dots, ti, fi, ro, ao):
        dm = dots[0]
        g_r = _sigmoid(ti[0][...] + fi[0][0:1, :])
        g_a = _sigmoid(ti[1][...] + fi[0][1:2, :])
        ro[0][...] = (dm * g_r).astype(BF16)
        ro[1][...] = (dm * g_a).astype(BF16)
        dga = dm * ti[2][...].astype(F32) * g_r * (1.0 - g_r)
        dgb = dm * ti[3][...].astype(F32) * g_a * (1.0 - g_a)
        ro[2][:, 0:D] = dga.astype(BF16)
        ro[2][:, D:2 * D] = dgb.astype(BF16)
        ao[0][0:1, :] += _rowsum(dga)
        ao[0][1:2, :] += _rowsum(dgb)

    dpr, dpa, dgates, acc3 = _mm_fused("mm_d_out", [(df, w_o, True)], [(gates, 0), (gates, 1), (pr, 0), (pa, 0)], [bg8],
                                       [(BF16, 1), (BF16, 1), (BF16, 2)], [(8, D)], e_dmerge, 512, D, sub=256)
    du_rnn = _mm("mm_d_proj_rnn", [(dpr, w_pr, True)], F32, 512, DR)
    du_attn = _mm("mm_d_proj_attn", [(dpa, w_pa, True)], F32, 512, D)
    d_w_pr = _mm_tn("mmt_proj_rnn", u_rnn, dpr, 2048, D)
    d_w_pa = _mm_tn("mmt_proj_attn", u_attn, dpa, 2048, D)

    drnn, d_wa, d_wi, vec = _rnn_bwd(proj_rnn, hs, du_rnn, cw, cb, wa_b, wat_b, ba, wi_b, wit_b, bi, sp, dsp_dlam)

    col_slots = lambda g, cols: g.reshape(g.shape[0], N_DEV, cols).transpose(1, 0, 2).astype(BF16)
    row_slots = lambda g: g.reshape(N_DEV, g.shape[0] // N_DEV, g.shape[1]).astype(BF16)
    early = [("w_ffn_gate", col_slots(d_w_g, 352)), ("w_ffn_up", col_slots(d_w_u, 352)), ("w_ffn_down", row_slots(d_w_dn)),
             ("w_proj_rnn", row_slots(d_w_pr)), ("w_proj_attn", row_slots(d_w_pa)), ("w_out", row_slots(d_w_o))]
    g_small = jnp.concatenate([
        _rows(vec[4], 2), d_wa.reshape(160, D), _rows(vec[5], 2), d_wi.reshape(160, D), _rows(vec[6], 2), _rows(vec[7], 2),
        acc2[2:4], acc5[1:3],
        _rows(vec[0:4], 5), acc3[0:2], acc5[0:1], jnp.zeros((4, D), F32)], axis=0)
    dq, dk, dv, recvd = _attention_bwd(qkv, du_attn, u_attn, lse,
                                       _Exchange([(g, True) for _, g in early] + [(g_small, False)]))
    recv = {name: r for (name, _), r in zip(early, recvd)}
    recv_small = recvd[-1]

    d_w_in = jnp.concatenate([_mm_tn("mmt_in_rnn", h1, drnn, 2048, DR), _mm_tn("mmt_in_q", h1, dq, 2048, D),
                              _mm_tn("mmt_in_k", h1, dk, 2048, D), _mm_tn("mmt_in_v", h1, dv, 2048, D),
                              _mm_tn("mmt_in_gates", h1, dgates, 2048, D)], axis=1)
    g_in = col_slots(d_w_in, 960)
    dh1, (recv_in,) = _mm("mm_d_in", [(drnn, w_rnn, True), (dq, w_qkv[:, :D], True), (dk, w_qkv[:, D:2 * D], True),
                                      (dv, w_qkv[:, 2 * D:], True), (dgates, w_gates, True)], F32, 256, 512,
                          carry=_Exchange([(g_in, True)]))
    recv["w_in"] = recv_in

    def b5(ri, fi, ro, ao):
        md = fi[0]
        n0, rstd0 = _ln(ri[0][...])
        dh = ri[1][...]
        ro[0][...] = ri[2][...] + _ln_bwd(dh * (1.0 + md[1:2, :]), n0, rstd0)
        ao[0][0:1, :] += _rowsum(dh * n0)
        ao[0][1:2, :] += _rowsum(dh)

    grad_x, acc1 = _rowwise(b5, "d_ln_mod1", s_len, tm, [(xs, D, 0), (dh1, D, 0), (dxp, D, 0)], [mod8], [(D, F32)],
                            [((8, D), F32)])

    dmod = jnp.concatenate([acc1[1:2], acc1[0:1], acc2[4:5], acc2[1:2], acc2[0:1], acc5[3:4], jnp.zeros((2, D), F32)],
                           axis=0)
    (recv_dmod,) = _exchange_alone("gather_dmod", _Exchange([(dmod, False)]))
    dmod_all = recv_dmod[:, 0:6, :].reshape(N_DEV, 6 * D)
    red = _sum_slots(recv_small)
    loss = jnp.sum(red[339])

    wmv = (dict(w_in=w_in, w_ffn_gate=w_ffn_gate, w_ffn_up=w_ffn_up, w_proj_rnn=w_proj_rnn, w_proj_attn=w_proj_attn,
                w_out=w_out, w_ffn_down=w_ffn_down),
           dict(w_in=m_w_in, w_ffn_gate=m_w_ffn_gate, w_ffn_up=m_w_ffn_up, w_proj_rnn=m_w_proj_rnn,
                w_proj_attn=m_w_proj_attn, w_out=m_w_out, w_ffn_down=m_w_ffn_down),
           dict(w_in=v_w_in, w_ffn_gate=v_w_ffn_gate, w_ffn_up=v_w_ffn_up, w_proj_rnn=v_w_proj_rnn,
                w_proj_attn=v_w_proj_attn, w_out=v_w_out, w_ffn_down=v_w_ffn_down))
    adam_tile = dict(w_in=128, w_ffn_gate=256, w_ffn_up=256, w_ffn_down=176, w_proj_rnn=160, w_proj_attn=128, w_out=128)
    big_out = {name: _adam_packed("adam_" + name, recv[name], wmv[0][name][0], wmv[1][name][0], wmv[2][name][0], tile)
               for name, tile in adam_tile.items()}
    ada_out = _adam_w_ada(cond_all.T, lax.dynamic_slice(dmod_all, (0, me * 768), (N_DEV, 768)),
                          w_ada[0], m_w_ada[0], v_w_ada[0])

    small_spec = (("b_ada", 6), ("conv_b", 2), ("lru_wa", 160), ("lru_ba", 2), ("lru_wi", 160), ("lru_bi", 2),
                  ("lru_lambda", 2), ("ln1_g", 1), ("ln1_b", 1), ("ln2_g", 1), ("ln2_b", 1), ("conv_w", 1), ("b_gate", 1))
    small_rows = 344

    def pack_small(ps):
        return jnp.concatenate([_rows(ps[name], r) for name, r in small_spec] + [jnp.zeros((4, D), F32)], axis=0)

    names = [n for n, _ in small_spec]
    smalls = [pack_small(dict(zip(names, t))) for t in (
        (b_ada, conv_b, lru_wa, lru_ba, lru_wi, lru_bi, lru_lambda, ln1_g, ln1_b, ln2_g, ln2_b, conv_w, b_gate),
        (m_b_ada, m_conv_b, m_lru_wa, m_lru_ba, m_lru_wi, m_lru_bi, m_lru_lambda, m_ln1_g, m_ln1_b, m_ln2_g, m_ln2_b,
         m_conv_w, m_b_gate),
        (v_b_ada, v_conv_b, v_lru_wa, v_lru_ba, v_lru_wi, v_lru_bi, v_lru_lambda, v_ln1_g, v_ln1_b, v_ln2_g, v_ln2_b,
         v_conv_w, v_b_gate))]
    g_conv_w = lax.dynamic_slice(red[332:337].reshape(-1)[:CONV_W * DR].reshape(CONV_W, DR), (0, me * 160), (CONV_W, 160))
    g_b_gate = lax.dynamic_slice(red[337:339], (0, me * 128), (2, 128))
    g_small_pack = jnp.concatenate([jnp.sum(dmod_all, axis=0).reshape(6, D), red[0:332], _rows(g_conv_w, 1),
                                    _rows(g_b_gate, 1), jnp.zeros((4, D), F32)], axis=0)
    assert g_small_pack.shape == (small_rows, D) and smalls[0].shape == (small_rows, D)
    small_out = _adam_packed("adam_small", g_small_pack, smalls[0], smalls[1], smalls[2], small_rows)

    shapes = dict(w_ada=w_ada.shape, b_ada=b_ada.shape, w_in=w_in.shape, conv_w=conv_w.shape, conv_b=conv_b.shape,
                  lru_wa=lru_wa.shape, lru_ba=lru_ba.shape, lru_wi=lru_wi.shape, lru_bi=lru_bi.shape,
                  lru_lambda=lru_lambda.shape, w_proj_rnn=w_proj_rnn.shape, w_proj_attn=w_proj_attn.shape,
                  b_gate=b_gate.shape, w_out=w_out.shape, ln1_g=ln1_g.shape, ln1_b=ln1_b.shape,
                  w_ffn_gate=w_ffn_gate.shape, w_ffn_up=w_ffn_up.shape, w_ffn_down=w_ffn_down.shape,
                  ln2_g=ln2_g.shape, ln2_b=ln2_b.shape)

    def unpack(kind):
        out = {"w_ada": ada_out[kind].reshape(shapes["w_ada"])}
        for name, _ in PACK_ROWS:
            out[name] = big_out[name][kind].reshape(shapes[name])
        o = 0
        for name, r in small_spec:
            size = 1
            for dim in shapes[name]:
                size *= dim
            out[name] = small_out[kind][o:o + r].reshape(-1)[:size].reshape(shapes[name])
            o += r
        return [out[name] for name in shapes]

    return (loss, grad_x[None], *unpack(0), *unpack(1), *unpack(2), *unpack(3))
```

```python
import functools

import jax
import jax.numpy as jnp
from jax import lax
from jax.experimental import pallas as pl
from jax.experimental.pallas import tpu as pltpu

F32 = jnp.float32
BF16 = jnp.bfloat16
MESH = pl.DeviceIdType.MESH

N_DEV = 8
D = 1024
DR = 1280
N_BLK = 10
BW = 128
HEAD_PAIR = 128
N_PAIR = 8
QB = 128
DFF = 2816
CONV_W = 4
DILATIONS = (1, 4, 16)
ALPHA = 2.0 ** 0.25
LN_EPS = 1e-5
LRU_C = 8.0
SCALE = 0.125
NEG = -1e30
ADAM_LR, ADAM_B1, ADAM_B2, ADAM_EPS, ADAM_WD, ADAM_STEP = 0.001, 0.9, 0.999, 1e-08, 0.01, 10

VMEM_LIMIT = 48 * 1024 * 1024
ROW_TILE = 256
RNN_CHUNK = 256

PACK_ROWS = (("w_in", 960), ("w_ffn_gate", 352), ("w_ffn_up", 352), ("w_proj_rnn", 160),
             ("w_proj_attn", 128), ("w_out", 128), ("w_ffn_down", 352))
LATE_ROWS = PACK_ROWS[1:]


def _cparams(sem):
    return pltpu.CompilerParams(dimension_semantics=sem, vmem_limit_bytes=VMEM_LIMIT)


def _ln(z):
    mu = jnp.mean(z, axis=-1, keepdims=True)
    zc = z - mu
    var = jnp.mean(zc * zc, axis=-1, keepdims=True)
    rstd = lax.rsqrt(var + LN_EPS)
    return zc * rstd, rstd


def _ln_bwd(dn, n, rstd):
    return rstd * (dn - jnp.mean(dn, axis=-1, keepdims=True) - n * jnp.mean(dn * n, axis=-1, keepdims=True))


def _sigmoid(x):
    return 1.0 / (1.0 + jnp.exp(-x))


_GELU_K = 0.7978845608028654
_GELU_C = 0.044715


def _gelu(y):
    t = jnp.tanh(_GELU_K * (y + _GELU_C * y * y * y))
    return 0.5 * y * (1.0 + t), t


def _gelu_grad(y, t):
    return 0.5 * (1.0 + t) + 0.5 * y * (1.0 - t * t) * _GELU_K * (1.0 + 3.0 * _GELU_C * y * y)


def _rowsum(v):
    return jnp.sum(v, axis=0, keepdims=True)


def _rowwise(body, name, n_rows, tm, row_ins, full_ins, row_outs, acc_outs, carry=None):
    nri, nfi, nro, nao = len(row_ins), len(full_ins), len(row_outs), len(acc_outs)
    n_c = len(carry.arrays) if carry else 0
    n_in = nri + nfi + n_c

    def kern(*refs):
        ri, fi = refs[:nri], refs[nri:nri + nfi]
        ro, ao = refs[n_in:n_in + nro], refs[n_in + nro:n_in + nro + nao]
        if carry:
            i = pl.program_id(0)
            _carry(carry, refs[nri + nfi:n_in], refs[n_in + nro + nao:n_in + nro + nao + n_c], refs[-1], i == 0,
                   i == n_rows // tm - 1)
        if ao:
            @pl.when(pl.program_id(0) == 0)
            def _():
                for a in ao:
                    a[...] = jnp.zeros(a.shape, a.dtype)
        body(ri, fi, ro, ao)

    in_specs = [pl.BlockSpec((tm, w), functools.partial(lambda i, cb: (i, cb), cb=cb)) for _, w, cb in row_ins]
    in_specs += [pl.BlockSpec(a.shape, lambda i: (0, 0)) for a in full_ins]
    out_specs = [pl.BlockSpec((tm, w), lambda i: (i, 0)) for w, _ in row_outs]
    out_specs += [pl.BlockSpec(s, lambda i: (0, 0)) for s, _ in acc_outs]
    out_shape = [jax.ShapeDtypeStruct((n_rows, w), dt) for w, dt in row_outs]
    out_shape += [jax.ShapeDtypeStruct(s, dt) for s, dt in acc_outs]
    operands = [a for a, _, _ in row_ins] + list(full_ins)
    scratch = []
    if carry:
        in_specs, out_specs, out_shape = in_specs + carry.in_specs, out_specs + carry.out_specs, out_shape + carry.out_shape
        operands, scratch = operands + carry.arrays, carry.scratch
    return pl.pallas_call(
        kern, name=name, grid=(n_rows // tm,), in_specs=in_specs, out_specs=out_specs, out_shape=out_shape,
        scratch_shapes=scratch, compiler_params=_cparams(("arbitrary",)),
    )(*operands)


NT_DIMS = (((1,), (1,)), ((), ()))


def _pair_cols(pair):
    return pair[1].shape[0] if len(pair) == 3 else pair[1].shape[1]


def _pair_specs(pairs, tm, tn):
    in_specs, flat = [], []
    for pair in pairs:
        a, w = pair[0], pair[1]
        k = a.shape[1]
        in_specs.append(pl.BlockSpec((tm, k), lambda j, i: (i, 0)))
        if len(pair) == 3:
            in_specs.append(pl.BlockSpec((tn, k), lambda j, i: (j, 0)))
        else:
            in_specs.append(pl.BlockSpec((k, tn), lambda j, i: (0, j)))
        flat += [a, w]
    return in_specs, flat


def _pair_dot(pair, a_ref, w_ref):
    if len(pair) == 3:
        return lax.dot_general(a_ref[...], w_ref[...], NT_DIMS, preferred_element_type=F32)
    return jnp.dot(a_ref[...], w_ref[...], preferred_element_type=F32)


def _mm(name, pairs, out_dtype, tm, tn, carry=None):
    m_rows, n_cols = pairs[0][0].shape[0], _pair_cols(pairs[0])
    n_pairs = len(pairs)
    n_c = len(carry.arrays) if carry else 0
    grid = (n_cols // tn, m_rows // tm)

    def kern(*refs):
        o_ref = refs[2 * n_pairs + n_c]
        if carry:
            step = pl.program_id(0) * grid[1] + pl.program_id(1)
            _carry(carry, refs[2 * n_pairs:2 * n_pairs + n_c], refs[2 * n_pairs + n_c + 1:2 * n_pairs + 2 * n_c + 1],
                   refs[-1], step == 0, step == grid[0] * grid[1] - 1)
        acc = None
        for p in range(n_pairs):
            t = _pair_dot(pairs[p], refs[2 * p], refs[2 * p + 1])
            acc = t if acc is None else acc + t
        o_ref[...] = acc.astype(o_ref.dtype)

    in_specs, flat = _pair_specs(pairs, tm, tn)
    out_specs = [pl.BlockSpec((tm, tn), lambda j, i: (i, j))]
    out_shape = [jax.ShapeDtypeStruct((m_rows, n_cols), out_dtype)]
    if carry:
        res = pl.pallas_call(
            kern, name=name, grid=grid, in_specs=in_specs + carry.in_specs, out_specs=out_specs + carry.out_specs,
            out_shape=out_shape + carry.out_shape, scratch_shapes=carry.scratch,
            compiler_params=_cparams(("arbitrary", "arbitrary")),
        )(*flat, *carry.arrays)
        return res[0], res[1:]
    return pl.pallas_call(
        kern, name=name, grid=grid, in_specs=in_specs, out_specs=out_specs[0], out_shape=out_shape[0],
        compiler_params=_cparams(("parallel", "parallel")),
    )(*flat)


def _mm_tn(name, a, g, tm, tn):
    m_rows, k = a.shape
    n_cols = g.shape[1]

    def kern(a_ref, g_ref, o_ref):
        @pl.when(pl.program_id(1) == 0)
        def _():
            o_ref[...] = jnp.zeros(o_ref.shape, F32)
        o_ref[...] += lax.dot_general(a_ref[...], g_ref[...], (((0,), (0,)), ((), ())), preferred_element_type=F32)

    return pl.pallas_call(
        kern, name=name, grid=(n_cols // tn, m_rows // tm),
        in_specs=[pl.BlockSpec((tm, k), lambda j, m: (m, 0)), pl.BlockSpec((tm, tn), lambda j, m: (m, j))],
        out_specs=pl.BlockSpec((k, tn), lambda j, m: (0, j)),
        out_shape=jax.ShapeDtypeStruct((k, n_cols), F32),
        compiler_params=_cparams(("parallel", "arbitrary")),
    )(a, g)


def _mm_fused(name, pairs, tile_ins, full_ins, outs, accs, epilogue, tm, tn, sub=None):
    m_rows, n_cols = pairs[0][0].shape[0], _pair_cols(pairs[0])
    n_p, n_t, n_f, n_o = len(pairs), len(tile_ins), len(full_ins), len(outs)
    sub = sub or tm

    def kern(*refs):
        base = 2 * n_p
        t_refs, f_refs = refs[base:base + n_t], refs[base + n_t:base + n_t + n_f]
        o_refs, a_refs = refs[base + n_t + n_f:base + n_t + n_f + n_o], refs[base + n_t + n_f + n_o:]
        if a_refs:
            @pl.when((pl.program_id(0) == 0) & (pl.program_id(1) == 0))
            def _():
                for a in a_refs:
                    a[...] = jnp.zeros(a.shape, F32)
        for h in range(tm // sub):
            rows = pl.ds(h * sub, sub)
            dots = [_pair_dot(pairs[p], refs[2 * p].at[rows, :], refs[2 * p + 1]) for p in range(n_p)]
            epilogue(dots, [t.at[rows, :] for t in t_refs], f_refs, [o.at[rows, :] for o in o_refs], a_refs)

    in_specs, flat = _pair_specs(pairs, tm, tn)
    for arr, cb in tile_ins:
        in_specs.append(pl.BlockSpec((tm, tn), functools.partial(lambda j, i, cb: (i, cb + j), cb=cb)))
        flat.append(arr)
    for arr in full_ins:
        in_specs.append(pl.BlockSpec(arr.shape, lambda j, i: (0, 0)))
        flat.append(arr)
    out_specs = [pl.BlockSpec((tm, f * tn), lambda j, i: (i, j)) for _, f in outs]
    out_specs += [pl.BlockSpec(s, lambda j, i: (0, 0)) for s in accs]
    out_shape = [jax.ShapeDtypeStruct((m_rows, f * n_cols), dt) for dt, f in outs]
    out_shape += [jax.ShapeDtypeStruct(s, F32) for s in accs]
    sem = ("arbitrary", "arbitrary") if accs else ("parallel", "parallel")
    return pl.pallas_call(
        kern, name=name, grid=(n_cols // tn, m_rows // tm), in_specs=in_specs, out_specs=out_specs,
        out_shape=out_shape, compiler_params=_cparams(sem),
    )(*flat)


def _peers():
    x, y, c = lax.axis_index("x"), lax.axis_index("y"), lax.axis_index("c")
    me = 4 * x + 2 * y + c
    peers = []
    for k in range(1, N_DEV):
        px = 1 - x if (k >> 2) & 1 else x
        py = 1 - y if (k >> 1) & 1 else y
        pc = 1 - c if k & 1 else c
        peers.append(((px, py, pc), 4 * px + 2 * py + pc))
    return me, peers


def _exchange(me, peers, items, send_sems, recv_sems, t0=0):
    started = []
    for t, (src_of, dst_of) in enumerate(items, start=t0):
        for k, (pid, plin) in enumerate(peers):
            cp = pltpu.make_async_remote_copy(
                src_ref=src_of(plin), dst_ref=dst_of(me), send_sem=send_sems.at[t, k], recv_sem=recv_sems.at[t, k],
                device_id=pid, device_id_type=MESH)
            cp.start()
            started.append(cp)
    for t, (src_of, dst_of) in enumerate(items, start=t0):
        for k, (pid, plin) in enumerate(peers):
            pltpu.make_async_remote_copy(
                src_ref=src_of(plin), dst_ref=dst_of(plin), send_sem=send_sems.at[t, k], recv_sem=recv_sems.at[t, k],
                device_id=pid, device_id_type=MESH).wait_recv()
    for cp in started:
        cp.wait_send()


def _hbm_spec():
    return pl.BlockSpec(memory_space=pltpu.HBM)


class _ChipGather:
    def __init__(self, shard):
        self.arrays = [shard]
        self.out_shape = [jax.ShapeDtypeStruct((N_DEV,) + shard.shape, shard.dtype)]
        self.in_specs, self.out_specs = [_hbm_spec()], [_hbm_spec()]
        self.scratch = [pltpu.SemaphoreType.DMA((SEMS_PER_ITEM,))]

    def _parts(self, ins, outs, sems):
        src, out = ins[0], outs[0]
        x, y, c = lax.axis_index("x"), lax.axis_index("y"), lax.axis_index("c")
        lin = lambda px, py, pc: 4 * px + 2 * py + pc
        sibling = (x, y, 1 - c)
        chips = [(1 - x, y), (x, 1 - y), (1 - x, 1 - y)]

        def copy(k, slot, to, from_src):
            return pltpu.make_async_remote_copy(
                src_ref=src if from_src else out.at[slot], dst_ref=out.at[slot], send_sem=sems.at[k],
                recv_sem=sems.at[N_DEV - 1 + k], device_id=to, device_id_type=MESH)

        own = pltpu.make_async_copy(src, out.at[lin(x, y, c)], sems.at[2 * (N_DEV - 1)])
        first = [copy(0, lin(x, y, c), sibling, True)]
        first += [copy(1 + j, lin(x, y, c), (px, py, c), True) for j, (px, py) in enumerate(chips)]
        passed = [copy(4 + j, lin(px, py, c), sibling, False) for j, (px, py) in enumerate(chips)]
        landed = [copy(1 + j, lin(px, py, c), sibling, True) for j, (px, py) in enumerate(chips)]
        from_sibling = [copy(0, lin(x, y, 1 - c), sibling, True)]
        from_sibling += [copy(4 + j, lin(px, py, 1 - c), sibling, False) for j, (px, py) in enumerate(chips)]
        return own, first, passed, landed, from_sibling

    def start(self, ins, outs, sems):
        own, first, _, _, _ = self._parts(ins, outs, sems)
        own.start()
        for cp in first:
            cp.start()

    def wait(self, ins, outs, sems):
        own, first, passed, landed, from_sibling = self._parts(ins, outs, sems)
        for arrival, forward in zip(landed, passed):
            arrival.wait_recv()
            forward.start()
        for cp in from_sibling:
            cp.wait_recv()
        for cp in first + passed:
            cp.wait_send()
        own.wait()


SEMS_PER_ITEM = 2 * (N_DEV - 1) + 1


class _Exchange:
    def __init__(self, items):
        self.arrays = [a for a, _ in items]
        self.scatter = [s for _, s in items]
        self.out_shape = [jax.ShapeDtypeStruct(a.shape if s else (N_DEV,) + a.shape, a.dtype) for a, s in items]
        self.in_specs = [_hbm_spec() for _ in items]
        self.out_specs = [_hbm_spec() for _ in items]
        self.scratch = [pltpu.SemaphoreType.DMA((SEMS_PER_ITEM * len(items),))]

    def _copies(self, ins, outs, sems, landing):
        me, peers = _peers()
        cps = []
        for t, scatter in enumerate(self.scatter):
            base = t * SEMS_PER_ITEM
            for k, (pid, plin) in enumerate(peers):
                src = ins[t].at[plin] if scatter else ins[t]
                dst = outs[t].at[plin if landing else me]
                cps.append(pltpu.make_async_remote_copy(
                    src_ref=src, dst_ref=dst, send_sem=sems.at[base + k], recv_sem=sems.at[base + N_DEV - 1 + k],
                    device_id=pid, device_id_type=MESH))
        return cps

    def _own(self, ins, outs, sems):
        me, _ = _peers()
        return [pltpu.make_async_copy(ins[t].at[me] if scatter else ins[t], outs[t].at[me],
                                      sems.at[t * SEMS_PER_ITEM + 2 * (N_DEV - 1)])
                for t, scatter in enumerate(self.scatter)]

    def start(self, ins, outs, sems):
        for cp in self._own(ins, outs, sems) + self._copies(ins, outs, sems, False):
            cp.start()

    def wait(self, ins, outs, sems):
        for cp in self._copies(ins, outs, sems, True):
            cp.wait_recv()
        for cp in self._copies(ins, outs, sems, False):
            cp.wait_send()
        for cp in self._own(ins, outs, sems):
            cp.wait()


def _carry(exchange, ins, outs, sems, first, last):
    @pl.when(first)
    def _():
        exchange.start(ins, outs, sems)

    @pl.when(last)
    def _():
        exchange.wait(ins, outs, sems)


def _exchange_alone(name, exchange):
    n = len(exchange.arrays)

    def body(*refs):
        exchange.start(refs[:n], refs[n:2 * n], refs[2 * n])
        exchange.wait(refs[:n], refs[n:2 * n], refs[2 * n])

    return pl.pallas_call(body, name=name, in_specs=exchange.in_specs, out_specs=exchange.out_specs,
                          out_shape=exchange.out_shape, scratch_shapes=exchange.scratch)(*exchange.arrays)


def _ada_modulation(c8, w_ada, b_ada_cols):
    wcols = w_ada.shape[1]

    def body(c_ref, w_ref, b_ref, call_ref, cond_ref, mod_ref, res, send_sems, recv_sems):
        me, peers = _peers()
        call_ref[me] = c_ref[...]
        _exchange(me, peers, [(lambda p: c_ref, lambda s: call_ref.at[s])], send_sems, recv_sems, t0=0)
        for dev in range(N_DEV):
            cv = call_ref[dev]
            cond = cv * _sigmoid(cv)
            cond_ref[dev] = cond
            res[dev] = jnp.dot(cond, w_ref[...], preferred_element_type=F32,
                               precision=lax.Precision.HIGHEST) + b_ref[...]
        mod_ref[me] = res[me]
        _exchange(me, peers, [(lambda p: res.at[p], lambda s: mod_ref.at[s])], send_sems, recv_sems, t0=1)

    vm = pl.BlockSpec(memory_space=pltpu.VMEM)
    return pl.pallas_call(
        body, name="ada_modulation", in_specs=[vm, vm, vm], out_specs=[vm, vm, vm],
        out_shape=[jax.ShapeDtypeStruct((N_DEV, 8, D), F32), jax.ShapeDtypeStruct((N_DEV, 8, D), F32),
                   jax.ShapeDtypeStruct((N_DEV, 8, wcols), F32)],
        scratch_shapes=[pltpu.VMEM((N_DEV, 8, wcols), F32),
                        pltpu.SemaphoreType.DMA((2, N_DEV - 1)), pltpu.SemaphoreType.DMA((2, N_DEV - 1))],
        compiler_params=pltpu.CompilerParams(vmem_limit_bytes=VMEM_LIMIT),
    )(c8, w_ada, b_ada_cols)


def _lru_gates(xc, wa_ref, ba_ref, wi_ref, bi_ref, sp_ref, row0):
    xcb = xc.astype(BF16)
    pre_r, pre_i = [], []
    for n in range(N_BLK):
        xb = xcb[:, n * BW:(n + 1) * BW]
        pre_r.append(jnp.dot(xb, wa_ref[n], preferred_element_type=F32))
        pre_i.append(jnp.dot(xb, wi_ref[n], preferred_element_type=F32))
    r = _sigmoid(jnp.concatenate(pre_r, axis=1) + ba_ref[...])
    ig = _sigmoid(jnp.concatenate(pre_i, axis=1) + bi_ref[...])
    log_a = (-LRU_C) * r * sp_ref[...]
    a = jnp.exp(log_a)
    e2 = jnp.exp(2.0 * log_a)
    mult_raw = jnp.sqrt(1.0 - e2)
    rows = row0 + lax.broadcasted_iota(jnp.int32, xc.shape, 0)
    start = rows == 0
    mult = jnp.where(start, 1.0, mult_raw)
    return xcb, r, ig, a, e2, mult_raw, mult, start


def _conv(xpad, cw_ref, cb_ref, tc):
    out = cb_ref[...]
    for tap in range(CONV_W):
        out = out + cw_ref[tap:tap + 1, :] * xpad[pl.ds(8 - (CONV_W - 1) + tap, tc), :]
    return out


def _rnn_fwd(proj_rnn, cw, cb, wa, ba, wi, bi, sp, carry):
    s_len = proj_rnn.shape[0]
    tc = RNN_CHUNK
    n_c = len(carry.arrays)
    n_out = 5

    def kern(*refs):
        y_ref, xr_ref, cw_ref, cb_ref, wa_ref, ba_ref, wi_ref, bi_ref, sp_ref = refs[:9]
        hs_ref, ur_ref, xc_ref, r_ref, ig_ref = refs[9 + n_c:9 + n_c + n_out]
        xpad, a_scr, u_scr, h_scr = refs[9 + n_out + 2 * n_c:13 + n_out + 2 * n_c]
        i = pl.program_id(0)
        _carry(carry, refs[9:9 + n_c], refs[9 + n_out + n_c:9 + n_out + 2 * n_c], refs[-1], i == 0,
               i == s_len // tc - 1)

        @pl.when(i == 0)
        def _():
            xpad[0:8, :] = jnp.zeros((8, DR), F32)
            h_scr[...] = jnp.zeros((1, DR), F32)

        xpad[8:8 + tc, :] = xr_ref[...]
        xc = _conv(xpad, cw_ref, cb_ref, tc)
        xpad[0:8, :] = xpad[tc:tc + 8, :]
        xcb, r, ig, a, e2, mult_raw, mult, start = _lru_gates(xc, wa_ref, ba_ref, wi_ref, bi_ref, sp_ref, i * tc)
        a_scr[...] = a
        u_scr[...] = mult * (ig * xc)
        xc_ref[...] = xcb
        r_ref[...] = r.astype(BF16)
        ig_ref[...] = ig.astype(BF16)

        def step(t, h):
            h = a_scr[pl.ds(t, 1), :] * h + u_scr[pl.ds(t, 1), :]
            hs_ref[pl.ds(t, 1), :] = h
            return h

        h_scr[...] = lax.fori_loop(0, tc, step, h_scr[...], unroll=8)
        gy, _ = _gelu(y_ref[...])
        ur_ref[...] = (gy * hs_ref[...]).astype(BF16)

    full = lambda a: pl.BlockSpec(a.shape, lambda i: (0,) * a.ndim)
    res = pl.pallas_call(
        kern, name="rnn_fwd", grid=(s_len // tc,),
        in_specs=[pl.BlockSpec((tc, DR), lambda i: (i, 0)), pl.BlockSpec((tc, DR), lambda i: (i, 1)),
                  full(cw), full(cb), full(wa), full(ba), full(wi), full(bi), full(sp)] + carry.in_specs,
        out_specs=[pl.BlockSpec((tc, DR), lambda i: (i, 0))] * n_out + carry.out_specs,
        out_shape=[jax.ShapeDtypeStruct((s_len, DR), F32)] + [jax.ShapeDtypeStruct((s_len, DR), BF16)] * (n_out - 1)
        + carry.out_shape,
        scratch_shapes=[pltpu.VMEM((tc + 8, DR), F32), pltpu.VMEM((tc, DR), F32), pltpu.VMEM((tc, DR), F32),
                        pltpu.VMEM((1, DR), F32)] + carry.scratch,
        compiler_params=_cparams(("arbitrary",)),
    )(proj_rnn, proj_rnn, cw, cb, wa, ba, wi, bi, sp, *carry.arrays)
    return res[0], res[1], res[2:n_out], res[n_out:]


def _rnn_bwd(proj_rnn, hs, du, xc_b, r_b, ig_b, cw, wat, wit, sp, dsp_dlam):
    s_len = proj_rnn.shape[0]
    tc = RNN_CHUNK
    nch = s_len // tc

    def kern(y_ref, xr_ref, xrp_ref, hs_ref, hsp_ref, du_ref, xc_ref, r_ref, ig_ref, cw_ref, wat_ref, wit_ref, sp_ref,
             dspl_ref, drnn_ref, dwa_ref, dwi_ref, vec_ref, xpad, hpad, a_scr, d_scr, g_scr, dxcpad, ag_scr):
        j = pl.program_id(0)
        n = nch - 1 - j

        @pl.when(j == 0)
        def _():
            dwa_ref[...] = jnp.zeros(dwa_ref.shape, F32)
            dwi_ref[...] = jnp.zeros(dwi_ref.shape, F32)
            vec_ref[...] = jnp.zeros(vec_ref.shape, F32)
            ag_scr[...] = jnp.zeros((1, DR), F32)
            dxcpad[tc:tc + 8, :] = jnp.zeros((8, DR), F32)

        has_prev = n > 0
        xpad[0:8, :] = jnp.where(has_prev, xrp_ref[...], 0.0)
        xpad[8:8 + tc, :] = xr_ref[...]
        hpad[0:8, :] = jnp.where(has_prev, hsp_ref[...], 0.0)
        hpad[8:8 + tc, :] = hs_ref[...]
        xcb = xc_ref[...]
        xc, r, ig = xcb.astype(F32), r_ref[...].astype(F32), ig_ref[...].astype(F32)
        log_a = (-LRU_C) * r * sp_ref[...]
        a = jnp.exp(log_a)
        e2 = jnp.exp(2.0 * log_a)
        mult_raw = jnp.sqrt(1.0 - e2)
        start = (n * tc + lax.broadcasted_iota(jnp.int32, (tc, DR), 0)) == 0
        mult = jnp.where(start, 1.0, mult_raw)

        y = y_ref[...]
        gy, th = _gelu(y)
        duv = du_ref[...].astype(F32)
        drnn_ref[:, 0:DR] = (duv * hs_ref[...] * _gelu_grad(y, th)).astype(BF16)
        a_scr[...] = a
        d_scr[...] = duv * gy

        def step(tt, ag):
            t = tc - 1 - tt
            g = d_scr[pl.ds(t, 1), :] + ag
            g_scr[pl.ds(t, 1), :] = g
            return a_scr[pl.ds(t, 1), :] * g

        ag_scr[...] = lax.fori_loop(0, tc, step, ag_scr[...], unroll=8)
        g = g_scr[...]
        da = g * hpad[pl.ds(7, tc), :]
        gx = g * xc
        dmult = jnp.where(start, 0.0, gx * ig)
        di = gx * mult
        dxc = g * mult * ig
        dlog_a = da * a - jnp.where(start, 0.0, dmult * e2 / mult_raw)
        dr = dlog_a * ((-LRU_C) * sp_ref[...])
        vec_ref[7:8, :] += _rowsum(dlog_a * ((-LRU_C) * r))
        dpr = dr * r * (1.0 - r)
        dpi = di * ig * (1.0 - ig)
        vec_ref[5:6, :] += _rowsum(dpr)
        vec_ref[6:7, :] += _rowsum(dpi)
        dprb, dpib = dpr.astype(BF16), dpi.astype(BF16)
        extra = []
        for b in range(N_BLK):
            sl = slice(b * BW, (b + 1) * BW)
            extra.append(jnp.dot(dprb[:, sl], wat_ref[b], preferred_element_type=F32)
                         + jnp.dot(dpib[:, sl], wit_ref[b], preferred_element_type=F32))
            dn = (((0,), (0,)), ((), ()))
            dwa_ref[b] += lax.dot_general(xcb[:, sl], dprb[:, sl], dn, preferred_element_type=F32)
            dwi_ref[b] += lax.dot_general(xcb[:, sl], dpib[:, sl], dn, preferred_element_type=F32)
        dxc = dxc + jnp.concatenate(extra, axis=1)
        vec_ref[4:5, :] += _rowsum(dxc)
        dxcpad[0:tc, :] = dxc
        dxr = jnp.zeros((tc, DR), F32)
        for tap in range(CONV_W):
            shift = CONV_W - 1 - tap
            dxr = dxr + cw_ref[tap:tap + 1, :] * dxcpad[pl.ds(shift, tc), :]
            vec_ref[tap:tap + 1, :] += _rowsum(dxc * xpad[pl.ds(8 - shift, tc), :])
        dxcpad[tc:tc + 8, :] = dxcpad[0:8, :]
        drnn_ref[:, DR:2 * DR] = dxr.astype(BF16)

        @pl.when(j == nch - 1)
        def _():
            vec_ref[7:8, :] = vec_ref[7:8, :] * dspl_ref[...]

    full = lambda a: pl.BlockSpec(a.shape, lambda j: (0,) * a.ndim)
    rev = lambda j: nch - 1 - j
    prev8 = lambda j: jnp.maximum((nch - 1 - j) * (tc // 8) - 1, 0)
    chunk = pl.BlockSpec((tc, DR), lambda j: (rev(j), 0))
    return pl.pallas_call(
        kern, name="rnn_bwd", grid=(nch,),
        in_specs=[chunk, pl.BlockSpec((tc, DR), lambda j: (rev(j), 1)), pl.BlockSpec((8, DR), lambda j: (prev8(j), 1)),
                  chunk, pl.BlockSpec((8, DR), lambda j: (prev8(j), 0)), chunk, chunk, chunk, chunk,
                  full(cw), full(wat), full(wit), full(sp), full(dsp_dlam)],
        out_specs=[pl.BlockSpec((tc, 2 * DR), lambda j: (rev(j), 0)),
                   pl.BlockSpec((N_BLK, BW, BW), lambda j: (0, 0, 0)), pl.BlockSpec((N_BLK, BW, BW), lambda j: (0, 0, 0)),
                   pl.BlockSpec((8, DR), lambda j: (0, 0))],
        out_shape=[jax.ShapeDtypeStruct((s_len, 2 * DR), BF16), jax.ShapeDtypeStruct((N_BLK, BW, BW), F32),
                   jax.ShapeDtypeStruct((N_BLK, BW, BW), F32), jax.ShapeDtypeStruct((8, DR), F32)],
        scratch_shapes=[pltpu.VMEM((tc + 8, DR), F32), pltpu.VMEM((tc + 8, DR), F32), pltpu.VMEM((tc, DR), F32),
                        pltpu.VMEM((tc, DR), F32), pltpu.VMEM((tc, DR), F32), pltpu.VMEM((tc + 8, DR), F32),
                        pltpu.VMEM((1, DR), F32)],
        compiler_params=_cparams(("arbitrary",)),
    )(proj_rnn, proj_rnn, proj_rnn, hs, hs, du, xc_b, r_b, ig_b, cw, wat, wit, sp, dsp_dlam)


ATT_BLK = 2048


def _attn_units():
    return [(g, d, b, b * QB * d + r)
            for g, d in enumerate(DILATIONS) for b in range(ATT_BLK // (QB * d)) for r in range(d)]


def _rows_at(ref, start, n, d):
    return ref[pl.ds(start, n, stride=d), :] if d > 1 else ref[start:start + n, :]


def _add_rows_at(ref, start, n, d, val):
    if d > 1:
        ref[pl.ds(start, n, stride=d), :] = ref[pl.ds(start, n, stride=d), :] + val
    else:
        ref[start:start + n, :] = ref[start:start + n, :] + val


def _stack_heads(x, first):
    zero = jnp.zeros_like(x)
    return jnp.concatenate([jnp.where(first, x, zero), jnp.where(first, zero, x)], axis=0)


def _unstack_heads(x, first):
    return jnp.where(first, x[:QB], x[QB:])


def _head_column(x, lane, half):
    return jnp.sum(jnp.where(lane == 64 * half, x, 0.0), axis=-1, keepdims=True)


def _band_masks():
    row = lax.broadcasted_iota(jnp.int32, (2 * QB, 2 * QB), 0) & (QB - 1)
    col = lax.broadcasted_iota(jnp.int32, (2 * QB, 2 * QB), 1)
    dist = QB + row - col
    lane = lax.broadcasted_iota(jnp.int32, (1, HEAD_PAIR), 1)
    return (dist >= 0) & (dist <= QB), col >= QB, lane


def _attention_specs(nblk, blk_of):
    cur = lambda off: pl.BlockSpec((ATT_BLK, HEAD_PAIR), lambda hp, j: (blk_of(j), off + hp))
    prev = lambda off: pl.BlockSpec((ATT_BLK, HEAD_PAIR), lambda hp, j: (jnp.maximum(blk_of(j) - 1, 0), off + hp))
    return cur, prev


def _attention_fwd(qkv):
    s_len = qkv.shape[0]
    nblk = s_len // ATT_BLK
    units = _attn_units()
    nt = (((1,), (1,)), ((), ()))

    def kern(q_ref, k_ref, v_ref, o_ref, lse_ref, kbuf, vbuf, og, lg):
        blk = pl.program_id(1)

        @pl.when(blk == 0)
        def _():
            kbuf[0:ATT_BLK, :] = jnp.zeros((ATT_BLK, HEAD_PAIR), F32)
            vbuf[0:ATT_BLK, :] = jnp.zeros((ATT_BLK, HEAD_PAIR), F32)

        kbuf[ATT_BLK:2 * ATT_BLK, :] = k_ref[...]
        vbuf[ATT_BLK:2 * ATT_BLK, :] = v_ref[...]
        band, later, lane = _band_masks()
        first = lane < 64
        band0 = band & (later | (blk > 0))
        for g, d, b, start in units:
            kstart = ATT_BLK + start - QB * d
            qs = _stack_heads(_rows_at(q_ref, start, QB, d).astype(BF16), first)
            k2 = _rows_at(kbuf, kstart, 2 * QB, d).astype(BF16)
            v2 = _rows_at(vbuf, kstart, 2 * QB, d).astype(BF16)
            s = lax.dot_general(qs, k2, nt, preferred_element_type=F32) * SCALE
            s = jnp.where(band if b > 0 else band0, s, NEG)
            mx = jnp.max(s, axis=-1, keepdims=True)
            p = jnp.exp(s - mx)
            den = jnp.sum(p, axis=-1, keepdims=True)
            out = jnp.dot(p.astype(BF16), v2, preferred_element_type=F32) / den
            lse2 = jnp.broadcast_to(mx + jnp.log(den), (2 * QB, HEAD_PAIR))
            if d > 1:
                og[g, pl.ds(start, QB, stride=d), :] = _unstack_heads(out, first)
                lg[g, pl.ds(start, QB, stride=d), :] = _unstack_heads(lse2, first)
            else:
                og[g, start:start + QB, :] = _unstack_heads(out, first)
                lg[g, start:start + QB, :] = _unstack_heads(lse2, first)
        kbuf[0:ATT_BLK, :] = kbuf[ATT_BLK:2 * ATT_BLK, :]
        vbuf[0:ATT_BLK, :] = vbuf[ATT_BLK:2 * ATT_BLK, :]
        mx = jnp.maximum(jnp.maximum(lg[0], lg[1]), lg[2])
        es = [jnp.exp(lg[g] - mx) for g in range(3)]
        tot = es[0] + es[1] + es[2]
        o_ref[...] = ((es[0] * og[0] + es[1] * og[1] + es[2] * og[2]) / tot).astype(BF16)
        lse_ref[...] = mx + jnp.log(tot)

    cur, _ = _attention_specs(nblk, lambda j: j)
    out_spec = pl.BlockSpec((ATT_BLK, HEAD_PAIR), lambda hp, j: (j, hp))
    return pl.pallas_call(
        kern, name="attention_fwd", grid=(N_PAIR, nblk), in_specs=[cur(0), cur(N_PAIR), cur(2 * N_PAIR)],
        out_specs=[out_spec, out_spec],
        out_shape=[jax.ShapeDtypeStruct((s_len, D), BF16), jax.ShapeDtypeStruct((s_len, D), F32)],
        scratch_shapes=[pltpu.VMEM((2 * ATT_BLK, HEAD_PAIR), F32), pltpu.VMEM((2 * ATT_BLK, HEAD_PAIR), F32),
                        pltpu.VMEM((3, ATT_BLK, HEAD_PAIR), F32), pltpu.VMEM((3, ATT_BLK, HEAD_PAIR), F32)],
        compiler_params=_cparams(("arbitrary", "arbitrary")),
    )(qkv, qkv, qkv)


def _attention_bwd(qkv, do, o, lse, carry):
    s_len = qkv.shape[0]
    nblk = s_len // ATT_BLK
    units = _attn_units()
    nt = (((1,), (1,)), ((), ()))
    tn = (((0,), (0,)), ((), ()))
    n_c = len(carry.arrays) if carry else 0
    c_in, c_out = (carry.in_specs, carry.out_specs) if carry else ([], [])
    c_shape, c_scratch, c_arrays = (carry.out_shape, carry.scratch, carry.arrays) if carry else ([], [], [])

    def kern(*refs):
        q_ref, k_ref, v_ref, kp_ref, vp_ref, do_ref, o_ref, lse_ref = refs[:8]
        dq_ref, dk_ref, dv_ref = refs[8 + n_c:11 + n_c]
        kbuf, vbuf, dkbuf, dvbuf, dq_scr, dsum = refs[11 + 2 * n_c:17 + 2 * n_c]
        hp, j = pl.program_id(0), pl.program_id(1)
        blk = nblk - 1 - j
        if carry:
            _carry(carry, refs[8:8 + n_c], refs[11 + n_c:11 + 2 * n_c], refs[-1], (hp == 0) & (j == 0),
                   (hp == N_PAIR - 1) & (j == nblk - 1))
        zeros = jnp.zeros((ATT_BLK, HEAD_PAIR), F32)

        @pl.when(j == 0)
        def _():
            dkbuf[ATT_BLK:2 * ATT_BLK, :] = zeros
            dvbuf[ATT_BLK:2 * ATT_BLK, :] = zeros

        @pl.when(j > 0)
        def _():
            dkbuf[ATT_BLK:2 * ATT_BLK, :] = dkbuf[0:ATT_BLK, :]
            dvbuf[ATT_BLK:2 * ATT_BLK, :] = dvbuf[0:ATT_BLK, :]

        dkbuf[0:ATT_BLK, :] = zeros
        dvbuf[0:ATT_BLK, :] = zeros
        dq_scr[...] = zeros
        kbuf[0:ATT_BLK, :] = kp_ref[...]
        kbuf[ATT_BLK:2 * ATT_BLK, :] = k_ref[...]
        vbuf[0:ATT_BLK, :] = vp_ref[...]
        vbuf[ATT_BLK:2 * ATT_BLK, :] = v_ref[...]
        band, later, lane = _band_masks()
        first = lane < 64
        band0 = band & (later | (blk > 0))
        prod = do_ref[...] * o_ref[...].astype(F32)
        dsum[...] = jnp.where(first, jnp.sum(jnp.where(first, prod, 0.0), axis=-1, keepdims=True),
                              jnp.sum(jnp.where(first, 0.0, prod), axis=-1, keepdims=True))
        for g, d, b, start in units:
            kstart = ATT_BLK + start - QB * d
            qs = _stack_heads(_rows_at(q_ref, start, QB, d).astype(BF16), first)
            dos = _stack_heads(_rows_at(do_ref, start, QB, d).astype(BF16), first)
            k2 = _rows_at(kbuf, kstart, 2 * QB, d).astype(BF16)
            v2 = _rows_at(vbuf, kstart, 2 * QB, d).astype(BF16)
            ds_rows = _rows_at(dsum, start, QB, d)
            lse_rows = _rows_at(lse_ref, start, QB, d)
            dcol = jnp.concatenate([_head_column(ds_rows, lane, 0), _head_column(ds_rows, lane, 1)], axis=0)
            lcol = jnp.concatenate([_head_column(lse_rows, lane, 0), _head_column(lse_rows, lane, 1)], axis=0)
            s = lax.dot_general(qs, k2, nt, preferred_element_type=F32) * SCALE
            p = jnp.exp(jnp.where(band if b > 0 else band0, s, NEG) - lcol)
            dp = lax.dot_general(dos, v2, nt, preferred_element_type=F32)
            ds = (p * (dp - dcol) * SCALE).astype(BF16)
            _add_rows_at(dq_scr, start, QB, d, _unstack_heads(jnp.dot(ds, k2, preferred_element_type=F32), first))
            _add_rows_at(dkbuf, kstart, 2 * QB, d, lax.dot_general(ds, qs, tn, preferred_element_type=F32))
            _add_rows_at(dvbuf, kstart, 2 * QB, d, lax.dot_general(p.astype(BF16), dos, tn, preferred_element_type=F32))
        dq_ref[...] = dq_scr[...].astype(BF16)
        dk_ref[...] = dkbuf[ATT_BLK:2 * ATT_BLK, :].astype(BF16)
        dv_ref[...] = dvbuf[ATT_BLK:2 * ATT_BLK, :].astype(BF16)

    rev = lambda j: nblk - 1 - j
    cur, prev = _attention_specs(nblk, rev)
    buf = lambda rows: pltpu.VMEM((rows, HEAD_PAIR), F32)
    res = pl.pallas_call(
        kern, name="attention_bwd", grid=(N_PAIR, nblk),
        in_specs=[cur(0), cur(N_PAIR), cur(2 * N_PAIR), prev(N_PAIR), prev(2 * N_PAIR), cur(0), cur(0), cur(0)] + c_in,
        out_specs=[cur(0)] * 3 + c_out,
        out_shape=[jax.ShapeDtypeStruct((s_len, D), BF16)] * 3 + c_shape,
        scratch_shapes=[buf(2 * ATT_BLK), buf(2 * ATT_BLK), buf(2 * ATT_BLK), buf(2 * ATT_BLK), buf(ATT_BLK),
                        buf(ATT_BLK)] + c_scratch,
        compiler_params=_cparams(("arbitrary", "arbitrary")),
    )(qkv, qkv, qkv, qkv, qkv, do, o, lse, *c_arrays)
    return res[0], res[1], res[2], res[3:]


def _adamw(w, g, m, v):
    m = ADAM_B1 * m + (1.0 - ADAM_B1) * g
    v = ADAM_B2 * v + (1.0 - ADAM_B2) * (g * g)
    m_hat = m / (1.0 - ADAM_B1 ** ADAM_STEP)
    v_hat = v / (1.0 - ADAM_B2 ** ADAM_STEP)
    delta = -ADAM_LR * (m_hat / (jnp.sqrt(v_hat) + ADAM_EPS) + ADAM_WD * w)
    return delta, m, v


def _adam_packed(name, recv, w, m, v, tm):
    n_rows, width = w.shape
    summed = recv.ndim == 3

    def kern(r_ref, w_ref, m_ref, v_ref, g_out, d_out, m_out, v_out):
        if summed:
            g = r_ref[0].astype(F32)
            for s in range(1, N_DEV):
                g = g + r_ref[s].astype(F32)
        else:
            g = r_ref[...]
        delta, mn, vn = _adamw(w_ref[...], g, m_ref[...], v_ref[...])
        g_out[...] = g
        d_out[...] = delta
        m_out[...] = mn
        v_out[...] = vn

    tile = pl.BlockSpec((tm, width), lambda i: (i, 0))
    rspec = pl.BlockSpec((N_DEV, tm, width), lambda i: (0, i, 0)) if summed else tile
    return pl.pallas_call(
        kern, name=name, grid=(n_rows // tm,), in_specs=[rspec, tile, tile, tile], out_specs=[tile] * 4,
        out_shape=[jax.ShapeDtypeStruct((n_rows, width), F32)] * 4, compiler_params=_cparams(("parallel",)),
    )(recv, w, m, v)


def _sum_slots(recv):
    _, n_rows, width = recv.shape

    def kern(r_ref, o_ref):
        g = r_ref[0]
        for s in range(1, N_DEV):
            g = g + r_ref[s]
        o_ref[...] = g

    return pl.pallas_call(
        kern, name="sum_small", grid=(1,), in_specs=[pl.BlockSpec(recv.shape, lambda i: (0, 0, 0))],
        out_specs=pl.BlockSpec((n_rows, width), lambda i: (0, 0)),
        out_shape=jax.ShapeDtypeStruct((n_rows, width), F32), compiler_params=_cparams(("arbitrary",)),
    )(recv)


def _adam_w_ada(cond_t, dmod_cols, w, m, v):
    n_rows, width = w.shape
    tm = ROW_TILE

    def kern(c_ref, d_ref, w_ref, m_ref, v_ref, g_out, d_out, m_out, v_out):
        g = c_ref[:, 0:1] * d_ref[0:1, :]
        for b in range(1, N_DEV):
            g = g + c_ref[:, b:b + 1] * d_ref[b:b + 1, :]
        delta, mn, vn = _adamw(w_ref[...], g, m_ref[...], v_ref[...])
        g_out[...] = g
        d_out[...] = delta
        m_out[...] = mn
        v_out[...] = vn

    tile = pl.BlockSpec((tm, width), lambda i: (i, 0))
    return pl.pallas_call(
        kern, name="adam_w_ada", grid=(n_rows // tm,),
        in_specs=[pl.BlockSpec((tm, N_DEV), lambda i: (i, 0)), pl.BlockSpec((N_DEV, width), lambda i: (0, 0)),
                  tile, tile, tile],
        out_specs=[tile] * 4, out_shape=[jax.ShapeDtypeStruct((n_rows, width), F32)] * 4,
        compiler_params=_cparams(("parallel",)),
    )(cond_t, dmod_cols, w, m, v)


def _rows(a, n_rows=None):
    flat = a.reshape(-1)
    need = (n_rows if n_rows is not None else -(-flat.shape[0] // D)) * D
    if need != flat.shape[0]:
        flat = jnp.concatenate([flat, jnp.zeros((need - flat.shape[0],), flat.dtype)])
    return flat.reshape(-1, D)


def _slots_to_cols(s, k, cols):
    return s.reshape(N_DEV, k, cols).transpose(1, 0, 2).reshape(k, N_DEV * cols)


def kernel(x, c, w_ada, b_ada, w_in, conv_w, conv_b, lru_wa, lru_ba, lru_wi, lru_bi, lru_lambda, w_proj_rnn, w_proj_attn, b_gate, w_out, ln1_g, ln1_b, w_ffn_gate, w_ffn_up, w_ffn_down, ln2_g, ln2_b, loss_target, m_w_ada, m_b_ada, m_w_in, m_conv_w, m_conv_b, m_lru_wa, m_lru_ba, m_lru_wi, m_lru_bi, m_lru_lambda, m_w_proj_rnn, m_w_proj_attn, m_b_gate, m_w_out, m_ln1_g, m_ln1_b, m_w_ffn_gate, m_w_ffn_up, m_w_ffn_down, m_ln2_g, m_ln2_b, v_w_ada, v_b_ada, v_w_in, v_conv_w, v_conv_b, v_lru_wa, v_lru_ba, v_lru_wi, v_lru_bi, v_lru_lambda, v_w_proj_rnn, v_w_proj_attn, v_b_gate, v_w_out, v_ln1_g, v_ln1_b, v_w_ffn_gate, v_w_ffn_up, v_w_ffn_down, v_ln2_g, v_ln2_b):
    s_len = x.shape[1]
    me = 4 * lax.axis_index("x") + 2 * lax.axis_index("y") + lax.axis_index("c")
    xs = x[0]
    tgt = loss_target[0]
    tm = ROW_TILE

    b_ada_cols = lax.dynamic_slice(b_ada, (0, me * 768), (1, 768))
    pad_cols = lambda a: jnp.concatenate([a, jnp.zeros((a.shape[0], D - a.shape[1]), F32)], axis=1)
    c8 = jnp.concatenate([c, pad_cols(conv_w[0]), pad_cols(b_gate[0]), jnp.zeros((1, D), F32)], axis=0)
    c_all, cond_blocks, mod_parts = _ada_modulation(c8, w_ada[0], b_ada_cols)
    cond_all = cond_blocks[:, 0, :]
    mod = mod_parts[:, 0, :].reshape(6, D)
    mod8 = jnp.concatenate([mod, jnp.zeros((2, D), F32)], axis=0)
    cw = c_all[:, 1:1 + CONV_W, :DR // N_DEV].transpose(1, 0, 2).reshape(CONV_W, DR)
    bg = c_all[:, 1 + CONV_W:3 + CONV_W, :D // N_DEV].transpose(1, 0, 2).reshape(2, D)
    bg8 = jnp.concatenate([bg, jnp.zeros((6, D), F32)], axis=0)

    late = dict(w_ffn_gate=w_ffn_gate, w_ffn_up=w_ffn_up, w_proj_rnn=w_proj_rnn, w_proj_attn=w_proj_attn,
                w_out=w_out, w_ffn_down=w_ffn_down)
    late_shard = jnp.concatenate([late[name][0].reshape(r, D) for name, r in LATE_ROWS], axis=0).astype(BF16)

    cb = conv_b
    wa_b, wi_b = lru_wa[0].astype(BF16), lru_wi[0].astype(BF16)
    wat_b, wit_b = jnp.swapaxes(wa_b, 1, 2), jnp.swapaxes(wi_b, 1, 2)
    ba, bi = lru_ba.reshape(1, DR), lru_bi.reshape(1, DR)
    sp = jax.nn.softplus(-lru_lambda)
    dsp_dlam = -jax.nn.sigmoid(-lru_lambda)
    ln1 = jnp.concatenate([ln1_g, ln1_b, jnp.zeros((6, D), F32)], axis=0)
    ln2 = jnp.concatenate([ln2_g, ln2_b, jnp.zeros((6, D), F32)], axis=0)

    def f1(ri, fi, ro, ao):
        n, _ = _ln(ri[0][...])
        ro[0][...] = (n * (1.0 + fi[0][1:2, :]) + fi[0][0:1, :]).astype(BF16)

    h1, w_in_slots = _rowwise(f1, "ln_mod1", s_len, tm, [(xs, D, 0)], [mod8], [(D, BF16)], [],
                              carry=_ChipGather(w_in[0].astype(BF16)))
    w_in_f = w_in_slots.transpose(1, 0, 2).reshape(D, N_DEV * 960)
    w_rnn, w_qkv, w_gates = w_in_f[:, :2 * DR], w_in_f[:, 2 * DR:2 * DR + 3 * D], w_in_f[:, 2 * DR + 3 * D:]
    proj_rnn = _mm("mm_in_rnn", [(h1, w_rnn)], F32, 512, DR)
    qkv = _mm("mm_in_qkv", [(h1, w_qkv)], F32, 512, D)
    gates = _mm("mm_in_gates", [(h1, w_gates)], F32, 512, D)

    hs, u_rnn, kept, (late_all,) = _rnn_fwd(proj_rnn, cw, cb, wa_b, ba, wi_b, bi, sp, _Exchange([(late_shard, False)]))
    offs, o = {}, 0
    for name, r in LATE_ROWS:
        offs[name] = (o, o + r)
        o += r
    part = lambda name: late_all[:, offs[name][0]:offs[name][1], :]
    w_g, w_u = _slots_to_cols(part("w_ffn_gate"), D, 352), _slots_to_cols(part("w_ffn_up"), D, 352)
    w_pr = part("w_proj_rnn").reshape(DR, D)
    w_pa = part("w_proj_attn").reshape(D, D)
    w_o = part("w_out").reshape(D, D)
    w_dn = part("w_ffn_down").reshape(DFF, D)

    u_attn, lse = _attention_fwd(qkv)

    def e_merge(dots, ti, fi, ro, ao):
        g_r = _sigmoid(ti[0][...] + fi[0][0:1, :])
        g_a = _sigmoid(ti[1][...] + fi[0][1:2, :])
        ro[0][...] = dots[0].astype(BF16)
        ro[1][...] = dots[1].astype(BF16)
        ro[2][...] = (g_r * dots[0] + g_a * dots[1]).astype(BF16)

    pr, pa, merged = _mm_fused("mm_proj_merge", [(u_rnn, w_pr), (u_attn, w_pa)], [(gates, 0), (gates, 1)], [bg8],
                               [(BF16, 1), (BF16, 1), (BF16, 1)], [], e_merge, 512, D, sub=256)
    def e_post1(dots, ti, fi, ro, ao):
        md, l1 = fi[0], fi[1]
        fv = dots[0]
        n1, _ = _ln(ALPHA * ti[0][...] + md[2:3, :] * fv)
        x1 = n1 * l1[0:1, :] + l1[1:2, :]
        n0, _ = _ln(x1)
        ro[0][...] = fv
        ro[1][...] = x1
        ro[2][...] = (n0 * (1.0 + md[4:5, :]) + md[3:4, :]).astype(BF16)

    f, x1, h2 = _mm_fused("mm_out_post1", [(merged, w_o)], [(xs, 0)], [mod8, ln1], [(F32, 1), (F32, 1), (BF16, 1)], [],
                          e_post1, 512, D, sub=256)

    def e_swiglu(dots, ti, fi, ro, ao):
        gp, up = dots
        ro[0][...] = gp.astype(BF16)
        ro[1][...] = up.astype(BF16)
        ro[2][...] = (gp * _sigmoid(gp) * up).astype(BF16)

    gpre, upre, act = _mm_fused("mm_ffn_in", [(h2, w_g), (h2, w_u)], [], [], [(BF16, 1)] * 3, [], e_swiglu, 512, DFF // 2)
    def e_loss(dots, ti, fi, ro, ao):
        md, l2 = fi[0], fi[1]
        f2v = dots[0]
        n2, rstd = _ln(ALPHA * ti[0][...] + md[5:6, :] * f2v)
        err = n2 * l2[0:1, :] + l2[1:2, :] - ti[1][...]
        dx2 = err * (1.0 / D)
        dz2 = _ln_bwd(dx2 * l2[0:1, :], n2, rstd)
        ro[0][...] = dz2
        ro[1][...] = (md[5:6, :] * dz2).astype(BF16)
        acc = ao[0]
        acc[0:1, :] += _rowsum(err * err) * (0.5 / D)
        acc[1:2, :] += _rowsum(dx2 * n2)
        acc[2:3, :] += _rowsum(dx2)
        acc[3:4, :] += _rowsum(dz2 * f2v)

    dz2, df2, acc5 = _mm_fused("mm_ffn_out_loss", [(act, w_dn)], [(x1, 0), (tgt, 0)], [mod8, ln2],
                               [(F32, 1), (BF16, 1)], [(8, D)], e_loss, 512, D, sub=256)

    d_w_dn = _mm_tn("mmt_ffn_down", act, df2, 1024, 512)

    def e_dswiglu(dots, ti, fi, ro, ao):
        da = dots[0]
        gp, up = ti[0][...].astype(F32), ti[1][...].astype(F32)
        sg = _sigmoid(gp)
        ro[0][...] = (da * up * sg * (1.0 + gp * (1.0 - sg))).astype(BF16)
        ro[1][...] = (da * gp * sg).astype(BF16)

    dgp, dup = _mm_fused("mm_d_ffn_out", [(df2, w_dn, True)], [(gpre, 0), (upre, 0)], [], [(BF16, 1)] * 2, [],
                         e_dswiglu, 512, DFF // 2, sub=256)
    d_w_g = _mm_tn("mmt_ffn_gate", h2, dgp, 2048, DFF // 2)
    d_w_u = _mm_tn("mmt_ffn_up", h2, dup, 2048, DFF // 2)

    def e_dpost1(dots, ti, fi, ro, ao):
        md, l1 = fi[0], fi[1]
        dh2v = dots[0] + dots[1]
        fv = ti[1][...]
        n1, rstd1 = _ln(ALPHA * ti[0][...] + md[2:3, :] * fv)
        n0, rstd0 = _ln(n1 * l1[0:1, :] + l1[1:2, :])
        dx1 = ALPHA * ti[2][...] + _ln_bwd(dh2v * (1.0 + md[4:5, :]), n0, rstd0)
        dz1 = _ln_bwd(dx1 * l1[0:1, :], n1, rstd1)
        ro[0][...] = ALPHA * dz1
        ro[1][...] = (md[2:3, :] * dz1).astype(BF16)
        acc = ao[0]
        acc[0:1, :] += _rowsum(dh2v * n0)
        acc[1:2, :] += _rowsum(dh2v)
        acc[2:3, :] += _rowsum(dx1 * n1)
        acc[3:4, :] += _rowsum(dx1)
        acc[4:5, :] += _rowsum(dz1 * fv)

    dxp, df, acc2 = _mm_fused("mm_d_ffn_in_post1", [(dgp, w_g, True), (dup, w_u, True)], [(xs, 0), (f, 0), (dz2, 0)],
                              [mod8, ln1], [(F32, 1), (BF16, 1)], [(8, D)], e_dpost1, 256, D)
    d_w_o = _mm_tn("mmt_out", merged, df, 2048, D)

    def e_dmerge(dots, ti, fi, ro, ao):
        dm = dots[0]
        g_r = _sigmoid(ti[0][...] + fi[0][0:1, :])
        g_a = _sigmoid(ti[1][...] + fi[0][1:2, :])
        ro[0][...] = (dm * g_r).astype(BF16)
        ro[1][...] = (dm * g_a).astype(BF16)
        dga = dm * ti[2][...].astype(F32) * g_r * (1.0 - g_r)
        dgb = dm * ti[3][...].astype(F32) * g_a * (1.0 - g_a)
        ro[2][:, 0:D] = dga.astype(BF16)
        ro[2][:, D:2 * D] = dgb.astype(BF16)
        ao[0][0:1, :] += _rowsum(dga)
        ao[0][1:2, :] += _rowsum(dgb)

    dpr, dpa, dgates, acc3 = _mm_fused("mm_d_out", [(df, w_o, True)], [(gates, 0), (gates, 1), (pr, 0), (pa, 0)], [bg8],
                                       [(BF16, 1), (BF16, 1), (BF16, 2)], [(8, D)], e_dmerge, 512, D, sub=256)
    du_rnn = _mm("mm_d_proj_rnn", [(dpr, w_pr, True)], F32, 512, DR)
    du_attn = _mm("mm_d_proj_attn", [(dpa, w_pa, True)], F32, 512, D)
    d_w_pr = _mm_tn("mmt_proj_rnn", u_rnn, dpr, 2048, D)
    d_w_pa = _mm_tn("mmt_proj_attn", u_attn, dpa, 2048, D)

    drnn, d_wa, d_wi, vec = _rnn_bwd(proj_rnn, hs, du_rnn, *kept, cw, wat_b, wit_b, sp, dsp_dlam)

    col_slots = lambda g, cols: g.reshape(g.shape[0], N_DEV, cols).transpose(1, 0, 2).astype(BF16)
    row_slots = lambda g: g.reshape(N_DEV, g.shape[0] // N_DEV, g.shape[1]).astype(BF16)
    early = [("w_ffn_gate", col_slots(d_w_g, 352)), ("w_ffn_up", col_slots(d_w_u, 352)), ("w_ffn_down", row_slots(d_w_dn)),
             ("w_proj_rnn", row_slots(d_w_pr)), ("w_proj_attn", row_slots(d_w_pa)), ("w_out", row_slots(d_w_o))]
    g_small = jnp.concatenate([
        _rows(vec[4], 2), d_wa.reshape(160, D), _rows(vec[5], 2), d_wi.reshape(160, D), _rows(vec[6], 2), _rows(vec[7], 2),
        acc2[2:4], acc5[1:3],
        _rows(vec[0:4], 5), acc3[0:2], acc5[0:1], jnp.zeros((4, D), F32)], axis=0)
    dq, dk, dv, recvd = _attention_bwd(qkv, du_attn, u_attn, lse,
                                       _Exchange([(g, True) for _, g in early] + [(g_small, False)]))
    recv = {name: r for (name, _), r in zip(early, recvd)}
    recv_small = recvd[-1]

    d_w_in = jnp.concatenate([_mm_tn("mmt_in_rnn", h1, drnn, 2048, DR), _mm_tn("mmt_in_q", h1, dq, 2048, D),
                              _mm_tn("mmt_in_k", h1, dk, 2048, D), _mm_tn("mmt_in_v", h1, dv, 2048, D),
                              _mm_tn("mmt_in_gates", h1, dgates, 2048, D)], axis=1)
    g_in = col_slots(d_w_in, 960)
    dh1, (recv_in,) = _mm("mm_d_in", [(drnn, w_rnn, True), (dq, w_qkv[:, :D], True), (dk, w_qkv[:, D:2 * D], True),
                                      (dv, w_qkv[:, 2 * D:], True), (dgates, w_gates, True)], F32, 512, 512,
                          carry=_Exchange([(g_in, True)]))
    recv["w_in"] = recv_in

    def b5(ri, fi, ro, ao):
        md = fi[0]
        n0, rstd0 = _ln(ri[0][...])
        dh = ri[1][...]
        ro[0][...] = ri[2][...] + _ln_bwd(dh * (1.0 + md[1:2, :]), n0, rstd0)
        ao[0][0:1, :] += _rowsum(dh * n0)
        ao[0][1:2, :] += _rowsum(dh)

    grad_x, acc1 = _rowwise(b5, "d_ln_mod1", s_len, tm, [(xs, D, 0), (dh1, D, 0), (dxp, D, 0)], [mod8], [(D, F32)],
                            [((8, D), F32)])

    dmod = jnp.concatenate([acc1[1:2], acc1[0:1], acc2[4:5], acc2[1:2], acc2[0:1], acc5[3:4], jnp.zeros((2, D), F32)],
                           axis=0)
    (recv_dmod,) = _exchange_alone("gather_dmod", _Exchange([(dmod, False)]))
    dmod_all = recv_dmod[:, 0:6, :].reshape(N_DEV, 6 * D)
    red = _sum_slots(recv_small)
    loss = jnp.sum(red[339])

    wmv = (dict(w_in=w_in, w_ffn_gate=w_ffn_gate, w_ffn_up=w_ffn_up, w_proj_rnn=w_proj_rnn, w_proj_attn=w_proj_attn,
                w_out=w_out, w_ffn_down=w_ffn_down),
           dict(w_in=m_w_in, w_ffn_gate=m_w_ffn_gate, w_ffn_up=m_w_ffn_up, w_proj_rnn=m_w_proj_rnn,
                w_proj_attn=m_w_proj_attn, w_out=m_w_out, w_ffn_down=m_w_ffn_down),
           dict(w_in=v_w_in, w_ffn_gate=v_w_ffn_gate, w_ffn_up=v_w_ffn_up, w_proj_rnn=v_w_proj_rnn,
                w_proj_attn=v_w_proj_attn, w_out=v_w_out, w_ffn_down=v_w_ffn_down))
    adam_tile = dict(w_in=128, w_ffn_gate=256, w_ffn_up=256, w_ffn_down=176, w_proj_rnn=160, w_proj_attn=128, w_out=128)
    big_out = {name: _adam_packed("adam_" + name, recv[name], wmv[0][name][0], wmv[1][name][0], wmv[2][name][0], tile)
               for name, tile in adam_tile.items()}
    ada_out = _adam_w_ada(cond_all.T, lax.dynamic_slice(dmod_all, (0, me * 768), (N_DEV, 768)),
                          w_ada[0], m_w_ada[0], v_w_ada[0])

    small_spec = (("b_ada", 6), ("conv_b", 2), ("lru_wa", 160), ("lru_ba", 2), ("lru_wi", 160), ("lru_bi", 2),
                  ("lru_lambda", 2), ("ln1_g", 1), ("ln1_b", 1), ("ln2_g", 1), ("ln2_b", 1), ("conv_w", 1), ("b_gate", 1))
    small_rows = 344

    def pack_small(ps):
        return jnp.concatenate([_rows(ps[name], r) for name, r in small_spec] + [jnp.zeros((4, D), F32)], axis=0)

    names = [n for n, _ in small_spec]
    smalls = [pack_small(dict(zip(names, t))) for t in (
        (b_ada, conv_b, lru_wa, lru_ba, lru_wi, lru_bi, lru_lambda, ln1_g, ln1_b, ln2_g, ln2_b, conv_w, b_gate),
        (m_b_ada, m_conv_b, m_lru_wa, m_lru_ba, m_lru_wi, m_lru_bi, m_lru_lambda, m_ln1_g, m_ln1_b, m_ln2_g, m_ln2_b,
         m_conv_w, m_b_gate),
        (v_b_ada, v_conv_b, v_lru_wa, v_lru_ba, v_lru_wi, v_lru_bi, v_lru_lambda, v_ln1_g, v_ln1_b, v_ln2_g, v_ln2_b,
         v_conv_w, v_b_gate))]
    g_conv_w = lax.dynamic_slice(red[332:337].reshape(-1)[:CONV_W * DR].reshape(CONV_W, DR), (0, me * 160), (CONV_W, 160))
    g_b_gate = lax.dynamic_slice(red[337:339], (0, me * 128), (2, 128))
    g_small_pack = jnp.concatenate([jnp.sum(dmod_all, axis=0).reshape(6, D), red[0:332], _rows(g_conv_w, 1),
                                    _rows(g_b_gate, 1), jnp.zeros((4, D), F32)], axis=0)
    assert g_small_pack.shape == (small_rows, D) and smalls[0].shape == (small_rows, D)
    small_out = _adam_packed("adam_small", g_small_pack, smalls[0], smalls[1], smalls[2], small_rows)

    shapes = dict(w_ada=w_ada.shape, b_ada=b_ada.shape, w_in=w_in.shape, conv_w=conv_w.shape, conv_b=conv_b.shape,
                  lru_wa=lru_wa.shape, lru_ba=lru_ba.shape, lru_wi=lru_wi.shape, lru_bi=lru_bi.shape,
                  lru_lambda=lru_lambda.shape, w_proj_rnn=w_proj_rnn.shape, w_proj_attn=w_proj_attn.shape,
                  b_gate=b_gate.shape, w_out=w_out.shape, ln1_g=ln1_g.shape, ln1_b=ln1_b.shape,
                  w_ffn_gate=w_ffn_gate.shape, w_ffn_up=w_ffn_up.shape, w_ffn_down=w_ffn_down.shape,
                  ln2_g=ln2_g.shape, ln2_b=ln2_b.shape)

    def unpack(kind):
        out = {"w_ada": ada_out[kind].reshape(shapes["w_ada"])}
        for name, _ in PACK_ROWS:
            out[name] = big_out[name][kind].reshape(shapes[name])
        o = 0
        for name, r in small_spec:
            size = 1
            for dim in shapes[name]:
                size *= dim
            out[name] = small_out[kind][o:o + r].reshape(-1)[:size].reshape(shapes[name])
            o += r
        return [out[name] for name in shapes]

    return (loss, grad_x[None], *unpack(0), *unpack(1), *unpack(2), *unpack(3))
```

```python
import functools

import jax
import jax.numpy as jnp
from jax import lax
from jax.experimental import pallas as pl
from jax.experimental.pallas import tpu as pltpu

F32 = jnp.float32
BF16 = jnp.bfloat16
MESH = pl.DeviceIdType.MESH

N_DEV = 8
D = 1024
DR = 1280
N_BLK = 10
BW = 128
HEAD_PAIR = 128
N_PAIR = 8
QB = 128
DFF = 2816
CONV_W = 4
DILATIONS = (1, 4, 16)
ALPHA = 2.0 ** 0.25
LN_EPS = 1e-5
LRU_C = 8.0
SCALE = 0.125
NEG = -1e30
ADAM_LR, ADAM_B1, ADAM_B2, ADAM_EPS, ADAM_WD, ADAM_STEP = 0.001, 0.9, 0.999, 1e-08, 0.01, 10

VMEM_LIMIT = 48 * 1024 * 1024
ROW_TILE = 256
RNN_CHUNK = 256

PACK_ROWS = (("w_in", 960), ("w_ffn_gate", 352), ("w_ffn_up", 352), ("w_proj_rnn", 160),
             ("w_proj_attn", 128), ("w_out", 128), ("w_ffn_down", 352))
LATE_ROWS = PACK_ROWS[1:]


def _cparams(sem):
    return pltpu.CompilerParams(dimension_semantics=sem, vmem_limit_bytes=VMEM_LIMIT)


def _ln(z):
    mu = jnp.mean(z, axis=-1, keepdims=True)
    zc = z - mu
    var = jnp.mean(zc * zc, axis=-1, keepdims=True)
    rstd = lax.rsqrt(var + LN_EPS)
    return zc * rstd, rstd


def _ln_bwd(dn, n, rstd):
    return rstd * (dn - jnp.mean(dn, axis=-1, keepdims=True) - n * jnp.mean(dn * n, axis=-1, keepdims=True))


def _sigmoid(x):
    return 1.0 / (1.0 + jnp.exp(-x))


_GELU_K = 0.7978845608028654
_GELU_C = 0.044715


def _gelu(y):
    t = jnp.tanh(_GELU_K * (y + _GELU_C * y * y * y))
    return 0.5 * y * (1.0 + t), t


def _gelu_grad(y, t):
    return 0.5 * (1.0 + t) + 0.5 * y * (1.0 - t * t) * _GELU_K * (1.0 + 3.0 * _GELU_C * y * y)


def _rowsum(v):
    return jnp.sum(v, axis=0, keepdims=True)


def _rowwise(body, name, n_rows, tm, row_ins, full_ins, row_outs, acc_outs, carry=None):
    nri, nfi, nro, nao = len(row_ins), len(full_ins), len(row_outs), len(acc_outs)
    n_c = len(carry.arrays) if carry else 0
    n_in = nri + nfi + n_c

    def kern(*refs):
        ri, fi = refs[:nri], refs[nri:nri + nfi]
        ro, ao = refs[n_in:n_in + nro], refs[n_in + nro:n_in + nro + nao]
        if carry:
            i = pl.program_id(0)
            _carry(carry, refs[nri + nfi:n_in], refs[n_in + nro + nao:n_in + nro + nao + n_c], refs[-1], i == 0,
                   i == n_rows // tm - 1)
        if ao:
            @pl.when(pl.program_id(0) == 0)
            def _():
                for a in ao:
                    a[...] = jnp.zeros(a.shape, a.dtype)
        body(ri, fi, ro, ao)

    in_specs = [pl.BlockSpec((tm, w), functools.partial(lambda i, cb: (i, cb), cb=cb)) for _, w, cb in row_ins]
    in_specs += [pl.BlockSpec(a.shape, lambda i: (0, 0)) for a in full_ins]
    out_specs = [pl.BlockSpec((tm, w), lambda i: (i, 0)) for w, _ in row_outs]
    out_specs += [pl.BlockSpec(s, lambda i: (0, 0)) for s, _ in acc_outs]
    out_shape = [jax.ShapeDtypeStruct((n_rows, w), dt) for w, dt in row_outs]
    out_shape += [jax.ShapeDtypeStruct(s, dt) for s, dt in acc_outs]
    operands = [a for a, _, _ in row_ins] + list(full_ins)
    scratch = []
    if carry:
        in_specs, out_specs, out_shape = in_specs + carry.in_specs, out_specs + carry.out_specs, out_shape + carry.out_shape
        operands, scratch = operands + carry.arrays, carry.scratch
    return pl.pallas_call(
        kern, name=name, grid=(n_rows // tm,), in_specs=in_specs, out_specs=out_specs, out_shape=out_shape,
        scratch_shapes=scratch, compiler_params=_cparams(("arbitrary",)),
    )(*operands)


NT_DIMS = (((1,), (1,)), ((), ()))


def _pair_cols(pair):
    return pair[1].shape[0] if len(pair) == 3 else pair[1].shape[1]


def _pair_specs(pairs, tm, tn):
    in_specs, flat = [], []
    for pair in pairs:
        a, w = pair[0], pair[1]
        k = a.shape[1]
        in_specs.append(pl.BlockSpec((tm, k), lambda j, i: (i, 0)))
        if len(pair) == 3:
            in_specs.append(pl.BlockSpec((tn, k), lambda j, i: (j, 0)))
        else:
            in_specs.append(pl.BlockSpec((k, tn), lambda j, i: (0, j)))
        flat += [a, w]
    return in_specs, flat


def _pair_dot(pair, a_ref, w_ref):
    if len(pair) == 3:
        return lax.dot_general(a_ref[...], w_ref[...], NT_DIMS, preferred_element_type=F32)
    return jnp.dot(a_ref[...], w_ref[...], preferred_element_type=F32)


def _mm(name, pairs, out_dtype, tm, tn, carry=None):
    m_rows, n_cols = pairs[0][0].shape[0], _pair_cols(pairs[0])
    n_pairs = len(pairs)
    n_c = len(carry.arrays) if carry else 0
    grid = (n_cols // tn, m_rows // tm)

    def kern(*refs):
        o_ref = refs[2 * n_pairs + n_c]
        if carry:
            step = pl.program_id(0) * grid[1] + pl.program_id(1)
            _carry(carry, refs[2 * n_pairs:2 * n_pairs + n_c], refs[2 * n_pairs + n_c + 1:2 * n_pairs + 2 * n_c + 1],
                   refs[-1], step == 0, step == grid[0] * grid[1] - 1)
        acc = None
        for p in range(n_pairs):
            t = _pair_dot(pairs[p], refs[2 * p], refs[2 * p + 1])
            acc = t if acc is None else acc + t
        o_ref[...] = acc.astype(o_ref.dtype)

    in_specs, flat = _pair_specs(pairs, tm, tn)
    out_specs = [pl.BlockSpec((tm, tn), lambda j, i: (i, j))]
    out_shape = [jax.ShapeDtypeStruct((m_rows, n_cols), out_dtype)]
    if carry:
        res = pl.pallas_call(
            kern, name=name, grid=grid, in_specs=in_specs + carry.in_specs, out_specs=out_specs + carry.out_specs,
            out_shape=out_shape + carry.out_shape, scratch_shapes=carry.scratch,
            compiler_params=_cparams(("arbitrary", "arbitrary")),
        )(*flat, *carry.arrays)
        return res[0], res[1:]
    return pl.pallas_call(
        kern, name=name, grid=grid, in_specs=in_specs, out_specs=out_specs[0], out_shape=out_shape[0],
        compiler_params=_cparams(("parallel", "parallel")),
    )(*flat)


def _mm_tn(name, a, g, tm, tn):
    m_rows, k = a.shape
    n_cols = g.shape[1]

    def kern(a_ref, g_ref, o_ref):
        @pl.when(pl.program_id(1) == 0)
        def _():
            o_ref[...] = jnp.zeros(o_ref.shape, F32)
        o_ref[...] += lax.dot_general(a_ref[...], g_ref[...], (((0,), (0,)), ((), ())), preferred_element_type=F32)

    return pl.pallas_call(
        kern, name=name, grid=(n_cols // tn, m_rows // tm),
        in_specs=[pl.BlockSpec((tm, k), lambda j, m: (m, 0)), pl.BlockSpec((tm, tn), lambda j, m: (m, j))],
        out_specs=pl.BlockSpec((k, tn), lambda j, m: (0, j)),
        out_shape=jax.ShapeDtypeStruct((k, n_cols), F32),
        compiler_params=_cparams(("parallel", "arbitrary")),
    )(a, g)


def _mm_fused(name, pairs, tile_ins, full_ins, outs, accs, epilogue, tm, tn, sub=None):
    m_rows, n_cols = pairs[0][0].shape[0], _pair_cols(pairs[0])
    n_p, n_t, n_f, n_o = len(pairs), len(tile_ins), len(full_ins), len(outs)
    sub = sub or tm

    def kern(*refs):
        base = 2 * n_p
        t_refs, f_refs = refs[base:base + n_t], refs[base + n_t:base + n_t + n_f]
        o_refs, a_refs = refs[base + n_t + n_f:base + n_t + n_f + n_o], refs[base + n_t + n_f + n_o:]
        if a_refs:
            @pl.when((pl.program_id(0) == 0) & (pl.program_id(1) == 0))
            def _():
                for a in a_refs:
                    a[...] = jnp.zeros(a.shape, F32)
        for h in range(tm // sub):
            rows = pl.ds(h * sub, sub)
            dots = [_pair_dot(pairs[p], refs[2 * p].at[rows, :], refs[2 * p + 1]) for p in range(n_p)]
            epilogue(dots, [t.at[rows, :] for t in t_refs], f_refs, [o.at[rows, :] for o in o_refs], a_refs)

    in_specs, flat = _pair_specs(pairs, tm, tn)
    for arr, cb in tile_ins:
        in_specs.append(pl.BlockSpec((tm, tn), functools.partial(lambda j, i, cb: (i, cb + j), cb=cb)))
        flat.append(arr)
    for arr in full_ins:
        in_specs.append(pl.BlockSpec(arr.shape, lambda j, i: (0, 0)))
        flat.append(arr)
    out_specs = [pl.BlockSpec((tm, f * tn), lambda j, i: (i, j)) for _, f in outs]
    out_specs += [pl.BlockSpec(s, lambda j, i: (0, 0)) for s in accs]
    out_shape = [jax.ShapeDtypeStruct((m_rows, f * n_cols), dt) for dt, f in outs]
    out_shape += [jax.ShapeDtypeStruct(s, F32) for s in accs]
    sem = ("arbitrary", "arbitrary") if accs else ("parallel", "parallel")
    return pl.pallas_call(
        kern, name=name, grid=(n_cols // tn, m_rows // tm), in_specs=in_specs, out_specs=out_specs,
        out_shape=out_shape, compiler_params=_cparams(sem),
    )(*flat)


def _peers():
    x, y, c = lax.axis_index("x"), lax.axis_index("y"), lax.axis_index("c")
    me = 4 * x + 2 * y + c
    peers = []
    for k in range(1, N_DEV):
        px = 1 - x if (k >> 2) & 1 else x
        py = 1 - y if (k >> 1) & 1 else y
        pc = 1 - c if k & 1 else c
        peers.append(((px, py, pc), 4 * px + 2 * py + pc))
    return me, peers


def _exchange(me, peers, items, send_sems, recv_sems, t0=0):
    started = []
    for t, (src_of, dst_of) in enumerate(items, start=t0):
        for k, (pid, plin) in enumerate(peers):
            cp = pltpu.make_async_remote_copy(
                src_ref=src_of(plin), dst_ref=dst_of(me), send_sem=send_sems.at[t, k], recv_sem=recv_sems.at[t, k],
                device_id=pid, device_id_type=MESH)
            cp.start()
            started.append(cp)
    for t, (src_of, dst_of) in enumerate(items, start=t0):
        for k, (pid, plin) in enumerate(peers):
            pltpu.make_async_remote_copy(
                src_ref=src_of(plin), dst_ref=dst_of(plin), send_sem=send_sems.at[t, k], recv_sem=recv_sems.at[t, k],
                device_id=pid, device_id_type=MESH).wait_recv()
    for cp in started:
        cp.wait_send()


def _hbm_spec():
    return pl.BlockSpec(memory_space=pltpu.HBM)


class _ChipGather:
    def __init__(self, shard):
        self.arrays = [shard]
        self.out_shape = [jax.ShapeDtypeStruct((N_DEV,) + shard.shape, shard.dtype)]
        self.in_specs, self.out_specs = [_hbm_spec()], [_hbm_spec()]
        self.scratch = [pltpu.SemaphoreType.DMA((SEMS_PER_ITEM,))]

    def _parts(self, ins, outs, sems):
        src, out = ins[0], outs[0]
        x, y, c = lax.axis_index("x"), lax.axis_index("y"), lax.axis_index("c")
        lin = lambda px, py, pc: 4 * px + 2 * py + pc
        sibling = (x, y, 1 - c)
        chips = [(1 - x, y), (x, 1 - y), (1 - x, 1 - y)]

        def copy(k, slot, to, from_src):
            return pltpu.make_async_remote_copy(
                src_ref=src if from_src else out.at[slot], dst_ref=out.at[slot], send_sem=sems.at[k],
                recv_sem=sems.at[N_DEV - 1 + k], device_id=to, device_id_type=MESH)

        own = pltpu.make_async_copy(src, out.at[lin(x, y, c)], sems.at[2 * (N_DEV - 1)])
        first = [copy(0, lin(x, y, c), sibling, True)]
        first += [copy(1 + j, lin(x, y, c), (px, py, c), True) for j, (px, py) in enumerate(chips)]
        passed = [copy(4 + j, lin(px, py, c), sibling, False) for j, (px, py) in enumerate(chips)]
        landed = [copy(1 + j, lin(px, py, c), sibling, True) for j, (px, py) in enumerate(chips)]
        from_sibling = [copy(0, lin(x, y, 1 - c), sibling, True)]
        from_sibling += [copy(4 + j, lin(px, py, 1 - c), sibling, False) for j, (px, py) in enumerate(chips)]
        return own, first, passed, landed, from_sibling

    def start(self, ins, outs, sems):
        own, first, _, _, _ = self._parts(ins, outs, sems)
        own.start()
        for cp in first:
            cp.start()

    def wait(self, ins, outs, sems):
        own, first, passed, landed, from_sibling = self._parts(ins, outs, sems)
        for arrival, forward in zip(landed, passed):
            arrival.wait_recv()
            forward.start()
        for cp in from_sibling:
            cp.wait_recv()
        for cp in first + passed:
            cp.wait_send()
        own.wait()


SEMS_PER_ITEM = 2 * (N_DEV - 1) + 1


class _Exchange:
    def __init__(self, items):
        self.arrays = [a for a, _ in items]
        self.scatter = [s for _, s in items]
        self.out_shape = [jax.ShapeDtypeStruct(a.shape if s else (N_DEV,) + a.shape, a.dtype) for a, s in items]
        self.in_specs = [_hbm_spec() for _ in items]
        self.out_specs = [_hbm_spec() for _ in items]
        self.scratch = [pltpu.SemaphoreType.DMA((SEMS_PER_ITEM * len(items),))]

    def _copies(self, ins, outs, sems, landing):
        me, peers = _peers()
        cps = []
        for t, scatter in enumerate(self.scatter):
            base = t * SEMS_PER_ITEM
            for k, (pid, plin) in enumerate(peers):
                src = ins[t].at[plin] if scatter else ins[t]
                dst = outs[t].at[plin if landing else me]
                cps.append(pltpu.make_async_remote_copy(
                    src_ref=src, dst_ref=dst, send_sem=sems.at[base + k], recv_sem=sems.at[base + N_DEV - 1 + k],
                    device_id=pid, device_id_type=MESH))
        return cps

    def _own(self, ins, outs, sems):
        me, _ = _peers()
        return [pltpu.make_async_copy(ins[t].at[me] if scatter else ins[t], outs[t].at[me],
                                      sems.at[t * SEMS_PER_ITEM + 2 * (N_DEV - 1)])
                for t, scatter in enumerate(self.scatter)]

    def start(self, ins, outs, sems):
        for cp in self._own(ins, outs, sems) + self._copies(ins, outs, sems, False):
            cp.start()

    def wait(self, ins, outs, sems):
        for cp in self._copies(ins, outs, sems, True):
            cp.wait_recv()
        for cp in self._copies(ins, outs, sems, False):
            cp.wait_send()
        for cp in self._own(ins, outs, sems):
            cp.wait()


def _carry(exchange, ins, outs, sems, first, last):
    @pl.when(first)
    def _():
        exchange.start(ins, outs, sems)

    @pl.when(last)
    def _():
        exchange.wait(ins, outs, sems)


def _exchange_alone(name, exchange):
    n = len(exchange.arrays)

    def body(*refs):
        exchange.start(refs[:n], refs[n:2 * n], refs[2 * n])
        exchange.wait(refs[:n], refs[n:2 * n], refs[2 * n])

    return pl.pallas_call(body, name=name, in_specs=exchange.in_specs, out_specs=exchange.out_specs,
                          out_shape=exchange.out_shape, scratch_shapes=exchange.scratch)(*exchange.arrays)


def _ada_modulation(c8, w_ada, b_ada_cols):
    wcols = w_ada.shape[1]

    def body(c_ref, w_ref, b_ref, call_ref, cond_ref, mod_ref, res, send_sems, recv_sems):
        me, peers = _peers()
        call_ref[me] = c_ref[...]
        _exchange(me, peers, [(lambda p: c_ref, lambda s: call_ref.at[s])], send_sems, recv_sems, t0=0)
        for dev in range(N_DEV):
            cv = call_ref[dev]
            cond = cv * _sigmoid(cv)
            cond_ref[dev] = cond
            res[dev] = jnp.dot(cond, w_ref[...], preferred_element_type=F32,
                               precision=lax.Precision.HIGHEST) + b_ref[...]
        mod_ref[me] = res[me]
        _exchange(me, peers, [(lambda p: res.at[p], lambda s: mod_ref.at[s])], send_sems, recv_sems, t0=1)

    vm = pl.BlockSpec(memory_space=pltpu.VMEM)
    return pl.pallas_call(
        body, name="ada_modulation", in_specs=[vm, vm, vm], out_specs=[vm, vm, vm],
        out_shape=[jax.ShapeDtypeStruct((N_DEV, 8, D), F32), jax.ShapeDtypeStruct((N_DEV, 8, D), F32),
                   jax.ShapeDtypeStruct((N_DEV, 8, wcols), F32)],
        scratch_shapes=[pltpu.VMEM((N_DEV, 8, wcols), F32),
                        pltpu.SemaphoreType.DMA((2, N_DEV - 1)), pltpu.SemaphoreType.DMA((2, N_DEV - 1))],
        compiler_params=pltpu.CompilerParams(vmem_limit_bytes=VMEM_LIMIT),
    )(c8, w_ada, b_ada_cols)


def _lru_gates(xc, wa_ref, ba_ref, wi_ref, bi_ref, sp_ref, row0):
    xcb = xc.astype(BF16)
    pre_r, pre_i = [], []
    for n in range(N_BLK):
        xb = xcb[:, n * BW:(n + 1) * BW]
        pre_r.append(jnp.dot(xb, wa_ref[n], preferred_element_type=F32))
        pre_i.append(jnp.dot(xb, wi_ref[n], preferred_element_type=F32))
    r = _sigmoid(jnp.concatenate(pre_r, axis=1) + ba_ref[...])
    ig = _sigmoid(jnp.concatenate(pre_i, axis=1) + bi_ref[...])
    log_a = (-LRU_C) * r * sp_ref[...]
    a = jnp.exp(log_a)
    e2 = jnp.exp(2.0 * log_a)
    mult_raw = jnp.sqrt(1.0 - e2)
    rows = row0 + lax.broadcasted_iota(jnp.int32, xc.shape, 0)
    start = rows == 0
    mult = jnp.where(start, 1.0, mult_raw)
    return xcb, r, ig, a, e2, mult_raw, mult, start


def _conv(xpad, cw_ref, cb_ref, tc):
    out = cb_ref[...]
    for tap in range(CONV_W):
        out = out + cw_ref[tap:tap + 1, :] * xpad[pl.ds(8 - (CONV_W - 1) + tap, tc), :]
    return out


def _rnn_fwd(proj_rnn, cw, cb, wa, ba, wi, bi, sp, carry):
    s_len = proj_rnn.shape[0]
    tc = RNN_CHUNK
    n_c = len(carry.arrays)
    n_out = 5

    def kern(*refs):
        y_ref, xr_ref, cw_ref, cb_ref, wa_ref, ba_ref, wi_ref, bi_ref, sp_ref = refs[:9]
        hs_ref, ur_ref, xc_ref, r_ref, ig_ref = refs[9 + n_c:9 + n_c + n_out]
        xpad, a_scr, u_scr, h_scr = refs[9 + n_out + 2 * n_c:13 + n_out + 2 * n_c]
        i = pl.program_id(0)
        _carry(carry, refs[9:9 + n_c], refs[9 + n_out + n_c:9 + n_out + 2 * n_c], refs[-1], i == 0,
               i == s_len // tc - 1)

        @pl.when(i == 0)
        def _():
            xpad[0:8, :] = jnp.zeros((8, DR), F32)
            h_scr[...] = jnp.zeros((1, DR), F32)

        xpad[8:8 + tc, :] = xr_ref[...].astype(F32)
        xc = _conv(xpad, cw_ref, cb_ref, tc)
        xpad[0:8, :] = xpad[tc:tc + 8, :]
        xcb, r, ig, a, e2, mult_raw, mult, start = _lru_gates(xc, wa_ref, ba_ref, wi_ref, bi_ref, sp_ref, i * tc)
        a_scr[...] = a
        u_scr[...] = mult * (ig * xc)
        xc_ref[...] = xcb
        r_ref[...] = r.astype(BF16)
        ig_ref[...] = ig.astype(BF16)

        def step(t, h):
            h = a_scr[pl.ds(t, 1), :] * h + u_scr[pl.ds(t, 1), :]
            hs_ref[pl.ds(t, 1), :] = h
            return h

        h_scr[...] = lax.fori_loop(0, tc, step, h_scr[...], unroll=8)
        gy, _ = _gelu(y_ref[...].astype(F32))
        ur_ref[...] = (gy * hs_ref[...]).astype(BF16)

    full = lambda a: pl.BlockSpec(a.shape, lambda i: (0,) * a.ndim)
    res = pl.pallas_call(
        kern, name="rnn_fwd", grid=(s_len // tc,),
        in_specs=[pl.BlockSpec((tc, DR), lambda i: (i, 0)), pl.BlockSpec((tc, DR), lambda i: (i, 1)),
                  full(cw), full(cb), full(wa), full(ba), full(wi), full(bi), full(sp)] + carry.in_specs,
        out_specs=[pl.BlockSpec((tc, DR), lambda i: (i, 0))] * n_out + carry.out_specs,
        out_shape=[jax.ShapeDtypeStruct((s_len, DR), F32)] + [jax.ShapeDtypeStruct((s_len, DR), BF16)] * (n_out - 1)
        + carry.out_shape,
        scratch_shapes=[pltpu.VMEM((tc + 8, DR), F32), pltpu.VMEM((tc, DR), F32), pltpu.VMEM((tc, DR), F32),
                        pltpu.VMEM((1, DR), F32)] + carry.scratch,
        compiler_params=_cparams(("arbitrary",)),
    )(proj_rnn, proj_rnn, cw, cb, wa, ba, wi, bi, sp, *carry.arrays)
    return res[0], res[1], res[2:n_out], res[n_out:]


def _rnn_bwd(proj_rnn, hs, du, xc_b, r_b, ig_b, cw, wat, wit, sp, dsp_dlam):
    s_len = proj_rnn.shape[0]
    tc = RNN_CHUNK
    nch = s_len // tc

    def kern(y_ref, xr_ref, xrp_ref, hs_ref, hsp_ref, du_ref, xc_ref, r_ref, ig_ref, cw_ref, wat_ref, wit_ref, sp_ref,
             dspl_ref, drnn_ref, dwa_ref, dwi_ref, vec_ref, xpad, hpad, a_scr, d_scr, g_scr, dxcpad, ag_scr):
        j = pl.program_id(0)
        n = nch - 1 - j

        @pl.when(j == 0)
        def _():
            dwa_ref[...] = jnp.zeros(dwa_ref.shape, F32)
            dwi_ref[...] = jnp.zeros(dwi_ref.shape, F32)
            vec_ref[...] = jnp.zeros(vec_ref.shape, F32)
            ag_scr[...] = jnp.zeros((1, DR), F32)
            dxcpad[tc:tc + 8, :] = jnp.zeros((8, DR), F32)

        has_prev = n > 0
        xpad[0:8, :] = jnp.where(has_prev, xrp_ref[8:16, :].astype(F32), 0.0)
        xpad[8:8 + tc, :] = xr_ref[...].astype(F32)
        hpad[0:8, :] = jnp.where(has_prev, hsp_ref[...], 0.0)
        hpad[8:8 + tc, :] = hs_ref[...]
        xcb = xc_ref[...]
        xc, r, ig = xcb.astype(F32), r_ref[...].astype(F32), ig_ref[...].astype(F32)
        log_a = (-LRU_C) * r * sp_ref[...]
        a = jnp.exp(log_a)
        e2 = jnp.exp(2.0 * log_a)
        mult_raw = jnp.sqrt(1.0 - e2)
        start = (n * tc + lax.broadcasted_iota(jnp.int32, (tc, DR), 0)) == 0
        mult = jnp.where(start, 1.0, mult_raw)

        y = y_ref[...].astype(F32)
        gy, th = _gelu(y)
        duv = du_ref[...].astype(F32)
        drnn_ref[:, 0:DR] = (duv * hs_ref[...] * _gelu_grad(y, th)).astype(BF16)
        a_scr[...] = a
        d_scr[...] = duv * gy

        def step(tt, ag):
            t = tc - 1 - tt
            g = d_scr[pl.ds(t, 1), :] + ag
            g_scr[pl.ds(t, 1), :] = g
            return a_scr[pl.ds(t, 1), :] * g

        ag_scr[...] = lax.fori_loop(0, tc, step, ag_scr[...], unroll=8)
        g = g_scr[...]
        da = g * hpad[pl.ds(7, tc), :]
        gx = g * xc
        dmult = jnp.where(start, 0.0, gx * ig)
        di = gx * mult
        dxc = g * mult * ig
        dlog_a = da * a - jnp.where(start, 0.0, dmult * e2 / mult_raw)
        dr = dlog_a * ((-LRU_C) * sp_ref[...])
        vec_ref[7:8, :] += _rowsum(dlog_a * ((-LRU_C) * r))
        dpr = dr * r * (1.0 - r)
        dpi = di * ig * (1.0 - ig)
        vec_ref[5:6, :] += _rowsum(dpr)
        vec_ref[6:7, :] += _rowsum(dpi)
        dprb, dpib = dpr.astype(BF16), dpi.astype(BF16)
        extra = []
        for b in range(N_BLK):
            sl = slice(b * BW, (b + 1) * BW)
            extra.append(jnp.dot(dprb[:, sl], wat_ref[b], preferred_element_type=F32)
                         + jnp.dot(dpib[:, sl], wit_ref[b], preferred_element_type=F32))
            dn = (((0,), (0,)), ((), ()))
            dwa_ref[b] += lax.dot_general(xcb[:, sl], dprb[:, sl], dn, preferred_element_type=F32)
            dwi_ref[b] += lax.dot_general(xcb[:, sl], dpib[:, sl], dn, preferred_element_type=F32)
        dxc = dxc + jnp.concatenate(extra, axis=1)
        vec_ref[4:5, :] += _rowsum(dxc)
        dxcpad[0:tc, :] = dxc
        dxr = jnp.zeros((tc, DR), F32)
        for tap in range(CONV_W):
            shift = CONV_W - 1 - tap
            dxr = dxr + cw_ref[tap:tap + 1, :] * dxcpad[pl.ds(shift, tc), :]
            vec_ref[tap:tap + 1, :] += _rowsum(dxc * xpad[pl.ds(8 - shift, tc), :])
        dxcpad[tc:tc + 8, :] = dxcpad[0:8, :]
        drnn_ref[:, DR:2 * DR] = dxr.astype(BF16)

        @pl.when(j == nch - 1)
        def _():
            vec_ref[7:8, :] = vec_ref[7:8, :] * dspl_ref[...]

    full = lambda a: pl.BlockSpec(a.shape, lambda j: (0,) * a.ndim)
    rev = lambda j: nch - 1 - j
    prev8 = lambda j: jnp.maximum((nch - 1 - j) * (tc // 8) - 1, 0)
    prev16 = lambda j: jnp.maximum((nch - 1 - j) * (tc // 16) - 1, 0)
    chunk = pl.BlockSpec((tc, DR), lambda j: (rev(j), 0))
    return pl.pallas_call(
        kern, name="rnn_bwd", grid=(nch,),
        in_specs=[chunk, pl.BlockSpec((tc, DR), lambda j: (rev(j), 1)), pl.BlockSpec((16, DR), lambda j: (prev16(j), 1)),
                  chunk, pl.BlockSpec((8, DR), lambda j: (prev8(j), 0)), chunk, chunk, chunk, chunk,
                  full(cw), full(wat), full(wit), full(sp), full(dsp_dlam)],
        out_specs=[pl.BlockSpec((tc, 2 * DR), lambda j: (rev(j), 0)),
                   pl.BlockSpec((N_BLK, BW, BW), lambda j: (0, 0, 0)), pl.BlockSpec((N_BLK, BW, BW), lambda j: (0, 0, 0)),
                   pl.BlockSpec((8, DR), lambda j: (0, 0))],
        out_shape=[jax.ShapeDtypeStruct((s_len, 2 * DR), BF16), jax.ShapeDtypeStruct((N_BLK, BW, BW), F32),
                   jax.ShapeDtypeStruct((N_BLK, BW, BW), F32), jax.ShapeDtypeStruct((8, DR), F32)],
        scratch_shapes=[pltpu.VMEM((tc + 8, DR), F32), pltpu.VMEM((tc + 8, DR), F32), pltpu.VMEM((tc, DR), F32),
                        pltpu.VMEM((tc, DR), F32), pltpu.VMEM((tc, DR), F32), pltpu.VMEM((tc + 8, DR), F32),
                        pltpu.VMEM((1, DR), F32)],
        compiler_params=_cparams(("arbitrary",)),
    )(proj_rnn, proj_rnn, proj_rnn, hs, hs, du, xc_b, r_b, ig_b, cw, wat, wit, sp, dsp_dlam)


ATT_BLK = 2048


def _attn_units():
    return [(g, d, b, b * QB * d + r)
            for g, d in enumerate(DILATIONS) for b in range(ATT_BLK // (QB * d)) for r in range(d)]


def _rows_at(ref, start, n, d):
    return ref[pl.ds(start, n, stride=d), :] if d > 1 else ref[start:start + n, :]


def _add_rows_at(ref, start, n, d, val):
    if d > 1:
        ref[pl.ds(start, n, stride=d), :] = ref[pl.ds(start, n, stride=d), :] + val
    else:
        ref[start:start + n, :] = ref[start:start + n, :] + val


def _stack_heads(x, first):
    zero = jnp.zeros_like(x)
    return jnp.concatenate([jnp.where(first, x, zero), jnp.where(first, zero, x)], axis=0)


def _unstack_heads(x, first):
    return jnp.where(first, x[:QB], x[QB:])


def _head_column(x, lane, half):
    return jnp.sum(jnp.where(lane == 64 * half, x, 0.0), axis=-1, keepdims=True)


def _band_masks():
    row = lax.broadcasted_iota(jnp.int32, (2 * QB, 2 * QB), 0) & (QB - 1)
    col = lax.broadcasted_iota(jnp.int32, (2 * QB, 2 * QB), 1)
    dist = QB + row - col
    lane = lax.broadcasted_iota(jnp.int32, (1, HEAD_PAIR), 1)
    return (dist >= 0) & (dist <= QB), col >= QB, lane


def _attention_specs(nblk, blk_of):
    cur = lambda off: pl.BlockSpec((ATT_BLK, HEAD_PAIR), lambda hp, j: (blk_of(j), off + hp))
    prev = lambda off: pl.BlockSpec((ATT_BLK, HEAD_PAIR), lambda hp, j: (jnp.maximum(blk_of(j) - 1, 0), off + hp))
    return cur, prev


def _attention_fwd(qkv):
    s_len = qkv.shape[0]
    nblk = s_len // ATT_BLK
    units = _attn_units()
    nt = (((1,), (1,)), ((), ()))

    def kern(q_ref, k_ref, v_ref, o_ref, lse_ref, kbuf, vbuf, og, lg):
        blk = pl.program_id(1)

        @pl.when(blk == 0)
        def _():
            kbuf[0:ATT_BLK, :] = jnp.zeros((ATT_BLK, HEAD_PAIR), F32)
            vbuf[0:ATT_BLK, :] = jnp.zeros((ATT_BLK, HEAD_PAIR), F32)

        kbuf[ATT_BLK:2 * ATT_BLK, :] = k_ref[...]
        vbuf[ATT_BLK:2 * ATT_BLK, :] = v_ref[...]
        band, later, lane = _band_masks()
        first = lane < 64
        band0 = band & (later | (blk > 0))
        for g, d, b, start in units:
            kstart = ATT_BLK + start - QB * d
            qs = _stack_heads((_rows_at(q_ref, start, QB, d) * SCALE).astype(BF16), first)
            k2 = _rows_at(kbuf, kstart, 2 * QB, d).astype(BF16)
            v2 = _rows_at(vbuf, kstart, 2 * QB, d).astype(BF16)
            s = lax.dot_general(qs, k2, nt, preferred_element_type=F32)
            s = jnp.where(band if b > 0 else band0, s, NEG)
            mx = jnp.max(s, axis=-1, keepdims=True)
            p = jnp.exp(s - mx)
            den = jnp.sum(p, axis=-1, keepdims=True)
            out = jnp.dot(p.astype(BF16), v2, preferred_element_type=F32) / den
            lse2 = jnp.broadcast_to(mx + jnp.log(den), (2 * QB, HEAD_PAIR))
            if d > 1:
                og[g, pl.ds(start, QB, stride=d), :] = _unstack_heads(out, first)
                lg[g, pl.ds(start, QB, stride=d), :] = _unstack_heads(lse2, first)
            else:
                og[g, start:start + QB, :] = _unstack_heads(out, first)
                lg[g, start:start + QB, :] = _unstack_heads(lse2, first)
        kbuf[0:ATT_BLK, :] = kbuf[ATT_BLK:2 * ATT_BLK, :]
        vbuf[0:ATT_BLK, :] = vbuf[ATT_BLK:2 * ATT_BLK, :]
        mx = jnp.maximum(jnp.maximum(lg[0], lg[1]), lg[2])
        es = [jnp.exp(lg[g] - mx) for g in range(3)]
        tot = es[0] + es[1] + es[2]
        o_ref[...] = ((es[0] * og[0] + es[1] * og[1] + es[2] * og[2]) / tot).astype(BF16)
        lse_ref[...] = mx + jnp.log(tot)

    cur, _ = _attention_specs(nblk, lambda j: j)
    out_spec = pl.BlockSpec((ATT_BLK, HEAD_PAIR), lambda hp, j: (j, hp))
    return pl.pallas_call(
        kern, name="attention_fwd", grid=(N_PAIR, nblk), in_specs=[cur(0), cur(N_PAIR), cur(2 * N_PAIR)],
        out_specs=[out_spec, out_spec],
        out_shape=[jax.ShapeDtypeStruct((s_len, D), BF16), jax.ShapeDtypeStruct((s_len, D), F32)],
        scratch_shapes=[pltpu.VMEM((2 * ATT_BLK, HEAD_PAIR), F32), pltpu.VMEM((2 * ATT_BLK, HEAD_PAIR), F32),
                        pltpu.VMEM((3, ATT_BLK, HEAD_PAIR), F32), pltpu.VMEM((3, ATT_BLK, HEAD_PAIR), F32)],
        compiler_params=_cparams(("arbitrary", "arbitrary")),
    )(qkv, qkv, qkv)


def _attention_bwd(qkv, do, o, lse, carry):
    s_len = qkv.shape[0]
    nblk = s_len // ATT_BLK
    units = _attn_units()
    nt = (((1,), (1,)), ((), ()))
    tn = (((0,), (0,)), ((), ()))
    n_c = len(carry.arrays) if carry else 0
    c_in, c_out = (carry.in_specs, carry.out_specs) if carry else ([], [])
    c_shape, c_scratch, c_arrays = (carry.out_shape, carry.scratch, carry.arrays) if carry else ([], [], [])

    def kern(*refs):
        q_ref, k_ref, v_ref, kp_ref, vp_ref, do_ref, o_ref, lse_ref = refs[:8]
        dq_ref, dk_ref, dv_ref = refs[8 + n_c:11 + n_c]
        kbuf, vbuf, dkbuf, dvbuf, dq_scr, dsum = refs[11 + 2 * n_c:17 + 2 * n_c]
        hp, j = pl.program_id(0), pl.program_id(1)
        blk = nblk - 1 - j
        if carry:
            _carry(carry, refs[8:8 + n_c], refs[11 + n_c:11 + 2 * n_c], refs[-1], (hp == 0) & (j == 0),
                   (hp == N_PAIR - 1) & (j == nblk - 1))
        zeros = jnp.zeros((ATT_BLK, HEAD_PAIR), F32)

        @pl.when(j == 0)
        def _():
            dkbuf[ATT_BLK:2 * ATT_BLK, :] = zeros
            dvbuf[ATT_BLK:2 * ATT_BLK, :] = zeros

        @pl.when(j > 0)
        def _():
            dkbuf[ATT_BLK:2 * ATT_BLK, :] = dkbuf[0:ATT_BLK, :]
            dvbuf[ATT_BLK:2 * ATT_BLK, :] = dvbuf[0:ATT_BLK, :]

        dkbuf[0:ATT_BLK, :] = zeros
        dvbuf[0:ATT_BLK, :] = zeros
        dq_scr[...] = zeros
        kbuf[0:ATT_BLK, :] = kp_ref[...]
        kbuf[ATT_BLK:2 * ATT_BLK, :] = k_ref[...]
        vbuf[0:ATT_BLK, :] = vp_ref[...]
        vbuf[ATT_BLK:2 * ATT_BLK, :] = v_ref[...]
        band, later, lane = _band_masks()
        first = lane < 64
        band0 = band & (later | (blk > 0))
        prod = do_ref[...] * o_ref[...].astype(F32)
        dsum[...] = jnp.where(first, jnp.sum(jnp.where(first, prod, 0.0), axis=-1, keepdims=True),
                              jnp.sum(jnp.where(first, 0.0, prod), axis=-1, keepdims=True))
        for g, d, b, start in units:
            kstart = ATT_BLK + start - QB * d
            qs = _stack_heads((_rows_at(q_ref, start, QB, d) * SCALE).astype(BF16), first)
            dos = _stack_heads(_rows_at(do_ref, start, QB, d).astype(BF16), first)
            k2 = _rows_at(kbuf, kstart, 2 * QB, d).astype(BF16)
            v2 = _rows_at(vbuf, kstart, 2 * QB, d).astype(BF16)
            ds_rows = _rows_at(dsum, start, QB, d)
            lse_rows = _rows_at(lse_ref, start, QB, d)
            dcol = jnp.concatenate([_head_column(ds_rows, lane, 0), _head_column(ds_rows, lane, 1)], axis=0)
            lcol = jnp.concatenate([_head_column(lse_rows, lane, 0), _head_column(lse_rows, lane, 1)], axis=0)
            s = lax.dot_general(qs, k2, nt, preferred_element_type=F32)
            p = jnp.exp(jnp.where(band if b > 0 else band0, s, NEG) - lcol)
            dp = lax.dot_general(dos, v2, nt, preferred_element_type=F32)
            ds = (p * (dp - dcol)).astype(BF16)
            _add_rows_at(dq_scr, start, QB, d,
                         _unstack_heads(jnp.dot(ds, k2, preferred_element_type=F32), first) * SCALE)
            _add_rows_at(dkbuf, kstart, 2 * QB, d, lax.dot_general(ds, qs, tn, preferred_element_type=F32))
            _add_rows_at(dvbuf, kstart, 2 * QB, d, lax.dot_general(p.astype(BF16), dos, tn, preferred_element_type=F32))
        dq_ref[...] = dq_scr[...].astype(BF16)
        dk_ref[...] = dkbuf[ATT_BLK:2 * ATT_BLK, :].astype(BF16)
        dv_ref[...] = dvbuf[ATT_BLK:2 * ATT_BLK, :].astype(BF16)

    rev = lambda j: nblk - 1 - j
    cur, prev = _attention_specs(nblk, rev)
    buf = lambda rows: pltpu.VMEM((rows, HEAD_PAIR), F32)
    res = pl.pallas_call(
        kern, name="attention_bwd", grid=(N_PAIR, nblk),
        in_specs=[cur(0), cur(N_PAIR), cur(2 * N_PAIR), prev(N_PAIR), prev(2 * N_PAIR), cur(0), cur(0), cur(0)] + c_in,
        out_specs=[cur(0)] * 3 + c_out,
        out_shape=[jax.ShapeDtypeStruct((s_len, D), BF16)] * 3 + c_shape,
        scratch_shapes=[buf(2 * ATT_BLK), buf(2 * ATT_BLK), buf(2 * ATT_BLK), buf(2 * ATT_BLK), buf(ATT_BLK),
                        buf(ATT_BLK)] + c_scratch,
        compiler_params=_cparams(("arbitrary", "arbitrary")),
    )(qkv, qkv, qkv, qkv, qkv, do, o, lse, *c_arrays)
    return res[0], res[1], res[2], res[3:]


def _adamw(w, g, m, v):
    m = ADAM_B1 * m + (1.0 - ADAM_B1) * g
    v = ADAM_B2 * v + (1.0 - ADAM_B2) * (g * g)
    m_hat = m / (1.0 - ADAM_B1 ** ADAM_STEP)
    v_hat = v / (1.0 - ADAM_B2 ** ADAM_STEP)
    delta = -ADAM_LR * (m_hat / (jnp.sqrt(v_hat) + ADAM_EPS) + ADAM_WD * w)
    return delta, m, v


def _adam_packed(name, recv, w, m, v, tm):
    n_rows, width = w.shape
    summed = recv.ndim == 3

    def kern(r_ref, w_ref, m_ref, v_ref, g_out, d_out, m_out, v_out):
        if summed:
            g = r_ref[0].astype(F32)
            for s in range(1, N_DEV):
                g = g + r_ref[s].astype(F32)
        else:
            g = r_ref[...]
        delta, mn, vn = _adamw(w_ref[...], g, m_ref[...], v_ref[...])
        g_out[...] = g
        d_out[...] = delta
        m_out[...] = mn
        v_out[...] = vn

    tile = pl.BlockSpec((tm, width), lambda i: (i, 0))
    rspec = pl.BlockSpec((N_DEV, tm, width), lambda i: (0, i, 0)) if summed else tile
    return pl.pallas_call(
        kern, name=name, grid=(n_rows // tm,), in_specs=[rspec, tile, tile, tile], out_specs=[tile] * 4,
        out_shape=[jax.ShapeDtypeStruct((n_rows, width), F32)] * 4, compiler_params=_cparams(("parallel",)),
    )(recv, w, m, v)


def _sum_slots(recv):
    _, n_rows, width = recv.shape

    def kern(r_ref, o_ref):
        g = r_ref[0]
        for s in range(1, N_DEV):
            g = g + r_ref[s]
        o_ref[...] = g

    return pl.pallas_call(
        kern, name="sum_small", grid=(1,), in_specs=[pl.BlockSpec(recv.shape, lambda i: (0, 0, 0))],
        out_specs=pl.BlockSpec((n_rows, width), lambda i: (0, 0)),
        out_shape=jax.ShapeDtypeStruct((n_rows, width), F32), compiler_params=_cparams(("arbitrary",)),
    )(recv)


def _adam_w_ada(cond_t, dmod_cols, w, m, v):
    n_rows, width = w.shape
    tm = ROW_TILE

    def kern(c_ref, d_ref, w_ref, m_ref, v_ref, g_out, d_out, m_out, v_out):
        g = c_ref[:, 0:1] * d_ref[0:1, :]
        for b in range(1, N_DEV):
            g = g + c_ref[:, b:b + 1] * d_ref[b:b + 1, :]
        delta, mn, vn = _adamw(w_ref[...], g, m_ref[...], v_ref[...])
        g_out[...] = g
        d_out[...] = delta
        m_out[...] = mn
        v_out[...] = vn

    tile = pl.BlockSpec((tm, width), lambda i: (i, 0))
    return pl.pallas_call(
        kern, name="adam_w_ada", grid=(n_rows // tm,),
        in_specs=[pl.BlockSpec((tm, N_DEV), lambda i: (i, 0)), pl.BlockSpec((N_DEV, width), lambda i: (0, 0)),
                  tile, tile, tile],
        out_specs=[tile] * 4, out_shape=[jax.ShapeDtypeStruct((n_rows, width), F32)] * 4,
        compiler_params=_cparams(("parallel",)),
    )(cond_t, dmod_cols, w, m, v)


def _rows(a, n_rows=None):
    flat = a.reshape(-1)
    need = (n_rows if n_rows is not None else -(-flat.shape[0] // D)) * D
    if need != flat.shape[0]:
        flat = jnp.concatenate([flat, jnp.zeros((need - flat.shape[0],), flat.dtype)])
    return flat.reshape(-1, D)


def _slots_to_cols(s, k, cols):
    return s.reshape(N_DEV, k, cols).transpose(1, 0, 2).reshape(k, N_DEV * cols)


def kernel(x, c, w_ada, b_ada, w_in, conv_w, conv_b, lru_wa, lru_ba, lru_wi, lru_bi, lru_lambda, w_proj_rnn, w_proj_attn, b_gate, w_out, ln1_g, ln1_b, w_ffn_gate, w_ffn_up, w_ffn_down, ln2_g, ln2_b, loss_target, m_w_ada, m_b_ada, m_w_in, m_conv_w, m_conv_b, m_lru_wa, m_lru_ba, m_lru_wi, m_lru_bi, m_lru_lambda, m_w_proj_rnn, m_w_proj_attn, m_b_gate, m_w_out, m_ln1_g, m_ln1_b, m_w_ffn_gate, m_w_ffn_up, m_w_ffn_down, m_ln2_g, m_ln2_b, v_w_ada, v_b_ada, v_w_in, v_conv_w, v_conv_b, v_lru_wa, v_lru_ba, v_lru_wi, v_lru_bi, v_lru_lambda, v_w_proj_rnn, v_w_proj_attn, v_b_gate, v_w_out, v_ln1_g, v_ln1_b, v_w_ffn_gate, v_w_ffn_up, v_w_ffn_down, v_ln2_g, v_ln2_b):
    s_len = x.shape[1]
    me = 4 * lax.axis_index("x") + 2 * lax.axis_index("y") + lax.axis_index("c")
    xs = x[0]
    tgt = loss_target[0]
    tm = ROW_TILE

    b_ada_cols = lax.dynamic_slice(b_ada, (0, me * 768), (1, 768))
    pad_cols = lambda a: jnp.concatenate([a, jnp.zeros((a.shape[0], D - a.shape[1]), F32)], axis=1)
    c8 = jnp.concatenate([c, pad_cols(conv_w[0]), pad_cols(b_gate[0]), jnp.zeros((1, D), F32)], axis=0)
    c_all, cond_blocks, mod_parts = _ada_modulation(c8, w_ada[0], b_ada_cols)
    cond_all = cond_blocks[:, 0, :]
    mod = mod_parts[:, 0, :].reshape(6, D)
    mod8 = jnp.concatenate([mod, jnp.zeros((2, D), F32)], axis=0)
    cw = c_all[:, 1:1 + CONV_W, :DR // N_DEV].transpose(1, 0, 2).reshape(CONV_W, DR)
    bg = c_all[:, 1 + CONV_W:3 + CONV_W, :D // N_DEV].transpose(1, 0, 2).reshape(2, D)
    bg8 = jnp.concatenate([bg, jnp.zeros((6, D), F32)], axis=0)

    late = dict(w_ffn_gate=w_ffn_gate, w_ffn_up=w_ffn_up, w_proj_rnn=w_proj_rnn, w_proj_attn=w_proj_attn,
                w_out=w_out, w_ffn_down=w_ffn_down)
    late_shard = jnp.concatenate([late[name][0].reshape(r, D) for name, r in LATE_ROWS], axis=0).astype(BF16)

    cb = conv_b
    wa_b, wi_b = lru_wa[0].astype(BF16), lru_wi[0].astype(BF16)
    wat_b, wit_b = jnp.swapaxes(wa_b, 1, 2), jnp.swapaxes(wi_b, 1, 2)
    ba, bi = lru_ba.reshape(1, DR), lru_bi.reshape(1, DR)
    sp = jax.nn.softplus(-lru_lambda)
    dsp_dlam = -jax.nn.sigmoid(-lru_lambda)
    ln1 = jnp.concatenate([ln1_g, ln1_b, jnp.zeros((6, D), F32)], axis=0)
    ln2 = jnp.concatenate([ln2_g, ln2_b, jnp.zeros((6, D), F32)], axis=0)

    def f1(ri, fi, ro, ao):
        n, _ = _ln(ri[0][...])
        ro[0][...] = (n * (1.0 + fi[0][1:2, :]) + fi[0][0:1, :]).astype(BF16)

    h1, w_in_slots = _rowwise(f1, "ln_mod1", s_len, tm, [(xs, D, 0)], [mod8], [(D, BF16)], [],
                              carry=_ChipGather(w_in[0].astype(BF16)))
    w_in_f = w_in_slots.transpose(1, 0, 2).reshape(D, N_DEV * 960)
    w_rnn, w_qkv, w_gates = w_in_f[:, :2 * DR], w_in_f[:, 2 * DR:2 * DR + 3 * D], w_in_f[:, 2 * DR + 3 * D:]
    proj_rnn = _mm("mm_in_rnn", [(h1, w_rnn)], BF16, 512, DR)
    qkv = _mm("mm_in_qkv", [(h1, w_qkv)], F32, 512, D)
    gates = _mm("mm_in_gates", [(h1, w_gates)], BF16, 512, D)

    hs, u_rnn, kept, (late_all,) = _rnn_fwd(proj_rnn, cw, cb, wa_b, ba, wi_b, bi, sp, _Exchange([(late_shard, False)]))
    offs, o = {}, 0
    for name, r in LATE_ROWS:
        offs[name] = (o, o + r)
        o += r
    part = lambda name: late_all[:, offs[name][0]:offs[name][1], :]
    w_g, w_u = _slots_to_cols(part("w_ffn_gate"), D, 352), _slots_to_cols(part("w_ffn_up"), D, 352)
    w_pr = part("w_proj_rnn").reshape(DR, D)
    w_pa = part("w_proj_attn").reshape(D, D)
    w_o = part("w_out").reshape(D, D)
    w_dn = part("w_ffn_down").reshape(DFF, D)

    u_attn, lse = _attention_fwd(qkv)

    def e_merge(dots, ti, fi, ro, ao):
        g_r = _sigmoid(ti[0][...] + fi[0][0:1, :])
        g_a = _sigmoid(ti[1][...] + fi[0][1:2, :])
        ro[0][...] = dots[0].astype(BF16)
        ro[1][...] = dots[1].astype(BF16)
        ro[2][...] = (g_r * dots[0] + g_a * dots[1]).astype(BF16)

    pr, pa, merged = _mm_fused("mm_proj_merge", [(u_rnn, w_pr), (u_attn, w_pa)], [(gates, 0), (gates, 1)], [bg8],
                               [(BF16, 1), (BF16, 1), (BF16, 1)], [], e_merge, 512, D, sub=256)
    def e_post1(dots, ti, fi, ro, ao):
        md, l1 = fi[0], fi[1]
        fv = dots[0]
        n1, _ = _ln(ALPHA * ti[0][...] + md[2:3, :] * fv)
        x1 = n1 * l1[0:1, :] + l1[1:2, :]
        n0, _ = _ln(x1)
        ro[0][...] = fv
        ro[1][...] = x1
        ro[2][...] = (n0 * (1.0 + md[4:5, :]) + md[3:4, :]).astype(BF16)

    f, x1, h2 = _mm_fused("mm_out_post1", [(merged, w_o)], [(xs, 0)], [mod8, ln1], [(F32, 1), (F32, 1), (BF16, 1)], [],
                          e_post1, 512, D, sub=256)

    def e_swiglu(dots, ti, fi, ro, ao):
        gp, up = dots
        ro[0][...] = gp.astype(BF16)
        ro[1][...] = up.astype(BF16)
        ro[2][...] = (gp * _sigmoid(gp) * up).astype(BF16)

    gpre, upre, act = _mm_fused("mm_ffn_in", [(h2, w_g), (h2, w_u)], [], [], [(BF16, 1)] * 3, [], e_swiglu, 512, DFF // 2)
    def e_loss(dots, ti, fi, ro, ao):
        md, l2 = fi[0], fi[1]
        f2v = dots[0]
        n2, rstd = _ln(ALPHA * ti[0][...] + md[5:6, :] * f2v)
        err = n2 * l2[0:1, :] + l2[1:2, :] - ti[1][...]
        dx2 = err * (1.0 / D)
        dz2 = _ln_bwd(dx2 * l2[0:1, :], n2, rstd)
        ro[0][...] = dz2
        ro[1][...] = (md[5:6, :] * dz2).astype(BF16)
        acc = ao[0]
        acc[0:1, :] += _rowsum(err * err) * (0.5 / D)
        acc[1:2, :] += _rowsum(dx2 * n2)
        acc[2:3, :] += _rowsum(dx2)
        acc[3:4, :] += _rowsum(dz2 * f2v)

    dz2, df2, acc5 = _mm_fused("mm_ffn_out_loss", [(act, w_dn)], [(x1, 0), (tgt, 0)], [mod8, ln2],
                               [(F32, 1), (BF16, 1)], [(8, D)], e_loss, 512, D, sub=256)

    d_w_dn = _mm_tn("mmt_ffn_down", act, df2, 1024, 512)

    def e_dswiglu(dots, ti, fi, ro, ao):
        da = dots[0]
        gp, up = ti[0][...].astype(F32), ti[1][...].astype(F32)
        sg = _sigmoid(gp)
        ro[0][...] = (da * up * sg * (1.0 + gp * (1.0 - sg))).astype(BF16)
        ro[1][...] = (da * gp * sg).astype(BF16)

    dgp, dup = _mm_fused("mm_d_ffn_out", [(df2, w_dn, True)], [(gpre, 0), (upre, 0)], [], [(BF16, 1)] * 2, [],
                         e_dswiglu, 512, DFF // 2, sub=256)
    d_w_g = _mm_tn("mmt_ffn_gate", h2, dgp, 2048, DFF // 2)
    d_w_u = _mm_tn("mmt_ffn_up", h2, dup, 2048, DFF // 2)

    def e_dpost1(dots, ti, fi, ro, ao):
        md, l1 = fi[0], fi[1]
        dh2v = dots[0] + dots[1]
        fv = ti[1][...]
        n1, rstd1 = _ln(ALPHA * ti[0][...] + md[2:3, :] * fv)
        n0, rstd0 = _ln(n1 * l1[0:1, :] + l1[1:2, :])
        dx1 = ALPHA * ti[2][...] + _ln_bwd(dh2v * (1.0 + md[4:5, :]), n0, rstd0)
        dz1 = _ln_bwd(dx1 * l1[0:1, :], n1, rstd1)
        ro[0][...] = ALPHA * dz1
        ro[1][...] = (md[2:3, :] * dz1).astype(BF16)
        acc = ao[0]
        acc[0:1, :] += _rowsum(dh2v * n0)
        acc[1:2, :] += _rowsum(dh2v)
        acc[2:3, :] += _rowsum(dx1 * n1)
        acc[3:4, :] += _rowsum(dx1)
        acc[4:5, :] += _rowsum(dz1 * fv)

    dxp, df, acc2 = _mm_fused("mm_d_ffn_in_post1", [(dgp, w_g, True), (dup, w_u, True)], [(xs, 0), (f, 0), (dz2, 0)],
                              [mod8, ln1], [(F32, 1), (BF16, 1)], [(8, D)], e_dpost1, 256, D)
    d_w_o = _mm_tn("mmt_out", merged, df, 2048, D)

    def e_dmerge(dots, ti, fi, ro, ao):
        dm = dots[0]
        g_r = _sigmoid(ti[0][...] + fi[0][0:1, :])
        g_a = _sigmoid(ti[1][...] + fi[0][1:2, :])
        ro[0][...] = (dm * g_r).astype(BF16)
        ro[1][...] = (dm * g_a).astype(BF16)
        dga = dm * ti[2][...].astype(F32) * g_r * (1.0 - g_r)
        dgb = dm * ti[3][...].astype(F32) * g_a * (1.0 - g_a)
        ro[2][:, 0:D] = dga.astype(BF16)
        ro[2][:, D:2 * D] = dgb.astype(BF16)
        ao[0][0:1, :] += _rowsum(dga)
        ao[0][1:2, :] += _rowsum(dgb)

    dpr, dpa, dgates, acc3 = _mm_fused("mm_d_out", [(df, w_o, True)], [(gates, 0), (gates, 1), (pr, 0), (pa, 0)], [bg8],
                                       [(BF16, 1), (BF16, 1), (BF16, 2)], [(8, D)], e_dmerge, 512, D, sub=256)
    du_rnn = _mm("mm_d_proj_rnn", [(dpr, w_pr, True)], F32, 512, DR)
    du_attn = _mm("mm_d_proj_attn", [(dpa, w_pa, True)], F32, 512, D)
    d_w_pr = _mm_tn("mmt_proj_rnn", u_rnn, dpr, 2048, D)
    d_w_pa = _mm_tn("mmt_proj_attn", u_attn, dpa, 2048, D)

    drnn, d_wa, d_wi, vec = _rnn_bwd(proj_rnn, hs, du_rnn, *kept, cw, wat_b, wit_b, sp, dsp_dlam)

    col_slots = lambda g, cols: g.reshape(g.shape[0], N_DEV, cols).transpose(1, 0, 2).astype(BF16)
    row_slots = lambda g: g.reshape(N_DEV, g.shape[0] // N_DEV, g.shape[1]).astype(BF16)
    early = [("w_ffn_gate", col_slots(d_w_g, 352)), ("w_ffn_up", col_slots(d_w_u, 352)), ("w_ffn_down", row_slots(d_w_dn)),
             ("w_proj_rnn", row_slots(d_w_pr)), ("w_proj_attn", row_slots(d_w_pa)), ("w_out", row_slots(d_w_o))]
    g_small = jnp.concatenate([
        _rows(vec[4], 2), d_wa.reshape(160, D), _rows(vec[5], 2), d_wi.reshape(160, D), _rows(vec[6], 2), _rows(vec[7], 2),
        acc2[2:4], acc5[1:3],
        _rows(vec[0:4], 5), acc3[0:2], acc5[0:1], jnp.zeros((4, D), F32)], axis=0)
    dq, dk, dv, recvd = _attention_bwd(qkv, du_attn, u_attn, lse,
                                       _Exchange([(g, True) for _, g in early] + [(g_small, False)]))
    recv = {name: r for (name, _), r in zip(early, recvd)}
    recv_small = recvd[-1]

    d_w_in = jnp.concatenate([_mm_tn("mmt_in_rnn", h1, drnn, 2048, DR), _mm_tn("mmt_in_q", h1, dq, 2048, D),
                              _mm_tn("mmt_in_k", h1, dk, 2048, D), _mm_tn("mmt_in_v", h1, dv, 2048, D),
                              _mm_tn("mmt_in_gates", h1, dgates, 2048, D)], axis=1)
    g_in = col_slots(d_w_in, 960)
    dh1, (recv_in,) = _mm("mm_d_in", [(drnn, w_rnn, True), (dq, w_qkv[:, :D], True), (dk, w_qkv[:, D:2 * D], True),
                                      (dv, w_qkv[:, 2 * D:], True), (dgates, w_gates, True)], F32, 512, 512,
                          carry=_Exchange([(g_in, True)]))
    recv["w_in"] = recv_in

    def b5(ri, fi, ro, ao):
        md = fi[0]
        n0, rstd0 = _ln(ri[0][...])
        dh = ri[1][...]
        ro[0][...] = ri[2][...] + _ln_bwd(dh * (1.0 + md[1:2, :]), n0, rstd0)
        ao[0][0:1, :] += _rowsum(dh * n0)
        ao[0][1:2, :] += _rowsum(dh)

    grad_x, acc1 = _rowwise(b5, "d_ln_mod1", s_len, tm, [(xs, D, 0), (dh1, D, 0), (dxp, D, 0)], [mod8], [(D, F32)],
                            [((8, D), F32)])

    dmod = jnp.concatenate([acc1[1:2], acc1[0:1], acc2[4:5], acc2[1:2], acc2[0:1], acc5[3:4], jnp.zeros((2, D), F32)],
                           axis=0)
    (recv_dmod,) = _exchange_alone("gather_dmod", _Exchange([(dmod, False)]))
    dmod_all = recv_dmod[:, 0:6, :].reshape(N_DEV, 6 * D)
    red = _sum_slots(recv_small)
    loss = jnp.sum(red[339])

    wmv = (dict(w_in=w_in, w_ffn_gate=w_ffn_gate, w_ffn_up=w_ffn_up, w_proj_rnn=w_proj_rnn, w_proj_attn=w_proj_attn,
                w_out=w_out, w_ffn_down=w_ffn_down),
           dict(w_in=m_w_in, w_ffn_gate=m_w_ffn_gate, w_ffn_up=m_w_ffn_up, w_proj_rnn=m_w_proj_rnn,
                w_proj_attn=m_w_proj_attn, w_out=m_w_out, w_ffn_down=m_w_ffn_down),
           dict(w_in=v_w_in, w_ffn_gate=v_w_ffn_gate, w_ffn_up=v_w_ffn_up, w_proj_rnn=v_w_proj_rnn,
                w_proj_attn=v_w_proj_attn, w_out=v_w_out, w_ffn_down=v_w_ffn_down))
    adam_tile = dict(w_in=128, w_ffn_gate=256, w_ffn_up=256, w_ffn_down=176, w_proj_rnn=160, w_proj_attn=128, w_out=128)
    big_out = {name: _adam_packed("adam_" + name, recv[name], wmv[0][name][0], wmv[1][name][0], wmv[2][name][0], tile)
               for name, tile in adam_tile.items()}
    ada_out = _adam_w_ada(cond_all.T, lax.dynamic_slice(dmod_all, (0, me * 768), (N_DEV, 768)),
                          w_ada[0], m_w_ada[0], v_w_ada[0])

    small_spec = (("b_ada", 6), ("conv_b", 2), ("lru_wa", 160), ("lru_ba", 2), ("lru_wi", 160), ("lru_bi", 2),
                  ("lru_lambda", 2), ("ln1_g", 1), ("ln1_b", 1), ("ln2_g", 1), ("ln2_b", 1), ("conv_w", 1), ("b_gate", 1))
    small_rows = 344

    def pack_small(ps):
        return jnp.concatenate([_rows(ps[name], r) for name, r in small_spec] + [jnp.zeros((4, D), F32)], axis=0)

    names = [n for n, _ in small_spec]
    smalls = [pack_small(dict(zip(names, t))) for t in (
        (b_ada, conv_b, lru_wa, lru_ba, lru_wi, lru_bi, lru_lambda, ln1_g, ln1_b, ln2_g, ln2_b, conv_w, b_gate),
        (m_b_ada, m_conv_b, m_lru_wa, m_lru_ba, m_lru_wi, m_lru_bi, m_lru_lambda, m_ln1_g, m_ln1_b, m_ln2_g, m_ln2_b,
         m_conv_w, m_b_gate),
        (v_b_ada, v_conv_b, v_lru_wa, v_lru_ba, v_lru_wi, v_lru_bi, v_lru_lambda, v_ln1_g, v_ln1_b, v_ln2_g, v_ln2_b,
         v_conv_w, v_b_gate))]
    g_conv_w = lax.dynamic_slice(red[332:337].reshape(-1)[:CONV_W * DR].reshape(CONV_W, DR), (0, me * 160), (CONV_W, 160))
    g_b_gate = lax.dynamic_slice(red[337:339], (0, me * 128), (2, 128))
    g_small_pack = jnp.concatenate([jnp.sum(dmod_all, axis=0).reshape(6, D), red[0:332], _rows(g_conv_w, 1),
                                    _rows(g_b_gate, 1), jnp.zeros((4, D), F32)], axis=0)
    assert g_small_pack.shape == (small_rows, D) and smalls[0].shape == (small_rows, D)
    small_out = _adam_packed("adam_small", g_small_pack, smalls[0], smalls[1], smalls[2], small_rows)

    shapes = dict(w_ada=w_ada.shape, b_ada=b_ada.shape, w_in=w_in.shape, conv_w=conv_w.shape, conv_b=conv_b.shape,
                  lru_wa=lru_wa.shape, lru_ba=lru_ba.shape, lru_wi=lru_wi.shape, lru_bi=lru_bi.shape,
                  lru_lambda=lru_lambda.shape, w_proj_rnn=w_proj_rnn.shape, w_proj_attn=w_proj_attn.shape,
                  b_gate=b_gate.shape, w_out=w_out.shape, ln1_g=ln1_g.shape, ln1_b=ln1_b.shape,
                  w_ffn_gate=w_ffn_gate.shape, w_ffn_up=w_ffn_up.shape, w_ffn_down=w_ffn_down.shape,
                  ln2_g=ln2_g.shape, ln2_b=ln2_b.shape)

    def unpack(kind):
        out = {"w_ada": ada_out[kind].reshape(shapes["w_ada"])}
        for name, _ in PACK_ROWS:
            out[name] = big_out[name][kind].reshape(shapes[name])
        o = 0
        for name, r in small_spec:
            size = 1
            for dim in shapes[name]:
                size *= dim
            out[name] = small_out[kind][o:o + r].reshape(-1)[:size].reshape(shapes[name])
            o += r
        return [out[name] for name in shapes]

    return (loss, grad_x[None], *unpack(0), *unpack(1), *unpack(2), *unpack(3))
```

```python
import functools

import jax
import jax.numpy as jnp
from jax import lax
from jax.experimental import pallas as pl
from jax.experimental.pallas import tpu as pltpu

F32 = jnp.float32
BF16 = jnp.bfloat16
MESH = pl.DeviceIdType.MESH

N_DEV = 8
D = 1024
DR = 1280
N_BLK = 10
BW = 128
HEAD_PAIR = 128
N_PAIR = 8
QB = 128
DFF = 2816
CONV_W = 4
DILATIONS = (1, 4, 16)
ALPHA = 2.0 ** 0.25
LN_EPS = 1e-5
LRU_C = 8.0
SCALE = 0.125
NEG = -1e30
ADAM_LR, ADAM_B1, ADAM_B2, ADAM_EPS, ADAM_WD, ADAM_STEP = 0.001, 0.9, 0.999, 1e-08, 0.01, 10

VMEM_LIMIT = 48 * 1024 * 1024
ROW_TILE = 256
RNN_CHUNK = 256

PACK_ROWS = (("w_in", 960), ("w_ffn_gate", 352), ("w_ffn_up", 352), ("w_proj_rnn", 160),
             ("w_proj_attn", 128), ("w_out", 128), ("w_ffn_down", 352))
LATE_ROWS = PACK_ROWS[1:]


def _cparams(sem):
    return pltpu.CompilerParams(dimension_semantics=sem, vmem_limit_bytes=VMEM_LIMIT)


def _ln(z):
    mu = jnp.mean(z, axis=-1, keepdims=True)
    zc = z - mu
    var = jnp.mean(zc * zc, axis=-1, keepdims=True)
    rstd = lax.rsqrt(var + LN_EPS)
    return zc * rstd, rstd


def _ln_bwd(dn, n, rstd):
    return rstd * (dn - jnp.mean(dn, axis=-1, keepdims=True) - n * jnp.mean(dn * n, axis=-1, keepdims=True))


def _sigmoid(x):
    return 0.5 * jnp.tanh(0.5 * x) + 0.5


_GELU_K = 0.7978845608028654
_GELU_C = 0.044715


def _gelu(y):
    t = jnp.tanh(_GELU_K * (y + _GELU_C * y * y * y))
    return 0.5 * y * (1.0 + t), t


def _gelu_grad(y, t):
    return 0.5 * (1.0 + t) + 0.5 * y * (1.0 - t * t) * _GELU_K * (1.0 + 3.0 * _GELU_C * y * y)


def _rowsum(v):
    return jnp.sum(v, axis=0, keepdims=True)


def _rowwise(body, name, n_rows, tm, row_ins, full_ins, row_outs, acc_outs, carry=None):
    nri, nfi, nro, nao = len(row_ins), len(full_ins), len(row_outs), len(acc_outs)
    n_c = len(carry.arrays) if carry else 0
    n_in = nri + nfi + n_c

    def kern(*refs):
        ri, fi = refs[:nri], refs[nri:nri + nfi]
        ro, ao = refs[n_in:n_in + nro], refs[n_in + nro:n_in + nro + nao]
        if carry:
            i = pl.program_id(0)
            _carry(carry, refs[nri + nfi:n_in], refs[n_in + nro + nao:n_in + nro + nao + n_c], refs[-1], i == 0,
                   i == n_rows // tm - 1)
        if ao:
            @pl.when(pl.program_id(0) == 0)
            def _():
                for a in ao:
                    a[...] = jnp.zeros(a.shape, a.dtype)
        body(ri, fi, ro, ao)

    in_specs = [pl.BlockSpec((tm, w), functools.partial(lambda i, cb: (i, cb), cb=cb)) for _, w, cb in row_ins]
    in_specs += [pl.BlockSpec(a.shape, lambda i: (0, 0)) for a in full_ins]
    out_specs = [pl.BlockSpec((tm, w), lambda i: (i, 0)) for w, _ in row_outs]
    out_specs += [pl.BlockSpec(s, lambda i: (0, 0)) for s, _ in acc_outs]
    out_shape = [jax.ShapeDtypeStruct((n_rows, w), dt) for w, dt in row_outs]
    out_shape += [jax.ShapeDtypeStruct(s, dt) for s, dt in acc_outs]
    operands = [a for a, _, _ in row_ins] + list(full_ins)
    scratch = []
    if carry:
        in_specs, out_specs, out_shape = in_specs + carry.in_specs, out_specs + carry.out_specs, out_shape + carry.out_shape
        operands, scratch = operands + carry.arrays, carry.scratch
    return pl.pallas_call(
        kern, name=name, grid=(n_rows // tm,), in_specs=in_specs, out_specs=out_specs, out_shape=out_shape,
        scratch_shapes=scratch, compiler_params=_cparams(("arbitrary",)),
    )(*operands)


NT_DIMS = (((1,), (1,)), ((), ()))


def _pair_cols(pair):
    return pair[1].shape[0] if len(pair) == 3 else pair[1].shape[1]


def _pair_specs(pairs, tm, tn):
    in_specs, flat = [], []
    for pair in pairs:
        a, w = pair[0], pair[1]
        k = a.shape[1]
        in_specs.append(pl.BlockSpec((tm, k), lambda j, i: (i, 0)))
        if len(pair) == 3:
            in_specs.append(pl.BlockSpec((tn, k), lambda j, i: (j, 0)))
        else:
            in_specs.append(pl.BlockSpec((k, tn), lambda j, i: (0, j)))
        flat += [a, w]
    return in_specs, flat


def _pair_dot(pair, a_ref, w_ref):
    if len(pair) == 3:
        return lax.dot_general(a_ref[...], w_ref[...], NT_DIMS, preferred_element_type=F32)
    return jnp.dot(a_ref[...], w_ref[...], preferred_element_type=F32)


def _mm(name, pairs, out_dtype, tm, tn, carry=None):
    m_rows, n_cols = pairs[0][0].shape[0], _pair_cols(pairs[0])
    n_pairs = len(pairs)
    n_c = len(carry.arrays) if carry else 0
    grid = (n_cols // tn, m_rows // tm)

    def kern(*refs):
        o_ref = refs[2 * n_pairs + n_c]
        if carry:
            step = pl.program_id(0) * grid[1] + pl.program_id(1)
            _carry(carry, refs[2 * n_pairs:2 * n_pairs + n_c], refs[2 * n_pairs + n_c + 1:2 * n_pairs + 2 * n_c + 1],
                   refs[-1], step == 0, step == grid[0] * grid[1] - 1)
        acc = None
        for p in range(n_pairs):
            t = _pair_dot(pairs[p], refs[2 * p], refs[2 * p + 1])
            acc = t if acc is None else acc + t
        o_ref[...] = acc.astype(o_ref.dtype)

    in_specs, flat = _pair_specs(pairs, tm, tn)
    out_specs = [pl.BlockSpec((tm, tn), lambda j, i: (i, j))]
    out_shape = [jax.ShapeDtypeStruct((m_rows, n_cols), out_dtype)]
    if carry:
        res = pl.pallas_call(
            kern, name=name, grid=grid, in_specs=in_specs + carry.in_specs, out_specs=out_specs + carry.out_specs,
            out_shape=out_shape + carry.out_shape, scratch_shapes=carry.scratch,
            compiler_params=_cparams(("arbitrary", "arbitrary")),
        )(*flat, *carry.arrays)
        return res[0], res[1:]
    return pl.pallas_call(
        kern, name=name, grid=grid, in_specs=in_specs, out_specs=out_specs[0], out_shape=out_shape[0],
        compiler_params=_cparams(("parallel", "parallel")),
    )(*flat)


def _mm_tn(name, a, g, tm, tn):
    m_rows, k = a.shape
    n_cols = g.shape[1]

    def kern(a_ref, g_ref, o_ref):
        @pl.when(pl.program_id(1) == 0)
        def _():
            o_ref[...] = jnp.zeros(o_ref.shape, F32)
        o_ref[...] += lax.dot_general(a_ref[...], g_ref[...], (((0,), (0,)), ((), ())), preferred_element_type=F32)

    return pl.pallas_call(
        kern, name=name, grid=(n_cols // tn, m_rows // tm),
        in_specs=[pl.BlockSpec((tm, k), lambda j, m: (m, 0)), pl.BlockSpec((tm, tn), lambda j, m: (m, j))],
        out_specs=pl.BlockSpec((k, tn), lambda j, m: (0, j)),
        out_shape=jax.ShapeDtypeStruct((k, n_cols), F32),
        compiler_params=_cparams(("parallel", "arbitrary")),
    )(a, g)


def _mm_fused(name, pairs, tile_ins, full_ins, outs, accs, epilogue, tm, tn, sub=None):
    m_rows, n_cols = pairs[0][0].shape[0], _pair_cols(pairs[0])
    n_p, n_t, n_f, n_o = len(pairs), len(tile_ins), len(full_ins), len(outs)
    sub = sub or tm

    def kern(*refs):
        base = 2 * n_p
        t_refs, f_refs = refs[base:base + n_t], refs[base + n_t:base + n_t + n_f]
        o_refs, a_refs = refs[base + n_t + n_f:base + n_t + n_f + n_o], refs[base + n_t + n_f + n_o:]
        if a_refs:
            @pl.when((pl.program_id(0) == 0) & (pl.program_id(1) == 0))
            def _():
                for a in a_refs:
                    a[...] = jnp.zeros(a.shape, F32)
        for h in range(tm // sub):
            rows = pl.ds(h * sub, sub)
            dots = [_pair_dot(pairs[p], refs[2 * p].at[rows, :], refs[2 * p + 1]) for p in range(n_p)]
            epilogue(dots, [t.at[rows, :] for t in t_refs], f_refs, [o.at[rows, :] for o in o_refs], a_refs)

    in_specs, flat = _pair_specs(pairs, tm, tn)
    for arr, cb in tile_ins:
        in_specs.append(pl.BlockSpec((tm, tn), functools.partial(lambda j, i, cb: (i, cb + j), cb=cb)))
        flat.append(arr)
    for arr in full_ins:
        in_specs.append(pl.BlockSpec(arr.shape, lambda j, i: (0, 0)))
        flat.append(arr)
    out_specs = [pl.BlockSpec((tm, f * tn), lambda j, i: (i, j)) for _, f in outs]
    out_specs += [pl.BlockSpec(s, lambda j, i: (0, 0)) for s in accs]
    out_shape = [jax.ShapeDtypeStruct((m_rows, f * n_cols), dt) for dt, f in outs]
    out_shape += [jax.ShapeDtypeStruct(s, F32) for s in accs]
    sem = ("arbitrary", "arbitrary") if accs else ("parallel", "parallel")
    return pl.pallas_call(
        kern, name=name, grid=(n_cols // tn, m_rows // tm), in_specs=in_specs, out_specs=out_specs,
        out_shape=out_shape, compiler_params=_cparams(sem),
    )(*flat)


def _peers():
    x, y, c = lax.axis_index("x"), lax.axis_index("y"), lax.axis_index("c")
    me = 4 * x + 2 * y + c
    peers = []
    for k in range(1, N_DEV):
        px = 1 - x if (k >> 2) & 1 else x
        py = 1 - y if (k >> 1) & 1 else y
        pc = 1 - c if k & 1 else c
        peers.append(((px, py, pc), 4 * px + 2 * py + pc))
    return me, peers


def _exchange(me, peers, items, send_sems, recv_sems, t0=0):
    started = []
    for t, (src_of, dst_of) in enumerate(items, start=t0):
        for k, (pid, plin) in enumerate(peers):
            cp = pltpu.make_async_remote_copy(
                src_ref=src_of(plin), dst_ref=dst_of(me), send_sem=send_sems.at[t, k], recv_sem=recv_sems.at[t, k],
                device_id=pid, device_id_type=MESH)
            cp.start()
            started.append(cp)
    for t, (src_of, dst_of) in enumerate(items, start=t0):
        for k, (pid, plin) in enumerate(peers):
            pltpu.make_async_remote_copy(
                src_ref=src_of(plin), dst_ref=dst_of(plin), send_sem=send_sems.at[t, k], recv_sem=recv_sems.at[t, k],
                device_id=pid, device_id_type=MESH).wait_recv()
    for cp in started:
        cp.wait_send()


def _hbm_spec():
    return pl.BlockSpec(memory_space=pltpu.HBM)


class _ChipGather:
    def __init__(self, shard):
        self.arrays = [shard]
        self.out_shape = [jax.ShapeDtypeStruct((N_DEV,) + shard.shape, shard.dtype)]
        self.in_specs, self.out_specs = [_hbm_spec()], [_hbm_spec()]
        self.scratch = [pltpu.SemaphoreType.DMA((SEMS_PER_ITEM,))]

    def _parts(self, ins, outs, sems):
        src, out = ins[0], outs[0]
        x, y, c = lax.axis_index("x"), lax.axis_index("y"), lax.axis_index("c")
        lin = lambda px, py, pc: 4 * px + 2 * py + pc
        sibling = (x, y, 1 - c)
        chips = [(1 - x, y), (x, 1 - y), (1 - x, 1 - y)]

        def copy(k, slot, to, from_src):
            return pltpu.make_async_remote_copy(
                src_ref=src if from_src else out.at[slot], dst_ref=out.at[slot], send_sem=sems.at[k],
                recv_sem=sems.at[N_DEV - 1 + k], device_id=to, device_id_type=MESH)

        own = pltpu.make_async_copy(src, out.at[lin(x, y, c)], sems.at[2 * (N_DEV - 1)])
        first = [copy(0, lin(x, y, c), sibling, True)]
        first += [copy(1 + j, lin(x, y, c), (px, py, c), True) for j, (px, py) in enumerate(chips)]
        passed = [copy(4 + j, lin(px, py, c), sibling, False) for j, (px, py) in enumerate(chips)]
        landed = [copy(1 + j, lin(px, py, c), sibling, True) for j, (px, py) in enumerate(chips)]
        from_sibling = [copy(0, lin(x, y, 1 - c), sibling, True)]
        from_sibling += [copy(4 + j, lin(px, py, 1 - c), sibling, False) for j, (px, py) in enumerate(chips)]
        return own, first, passed, landed, from_sibling

    def start(self, ins, outs, sems):
        own, first, _, _, _ = self._parts(ins, outs, sems)
        own.start()
        for cp in first:
            cp.start()

    def wait(self, ins, outs, sems):
        own, first, passed, landed, from_sibling = self._parts(ins, outs, sems)
        for arrival, forward in zip(landed, passed):
            arrival.wait_recv()
            forward.start()
        for cp in from_sibling:
            cp.wait_recv()
        for cp in first + passed:
            cp.wait_send()
        own.wait()


SEMS_PER_ITEM = 2 * (N_DEV - 1) + 1


class _Exchange:
    def __init__(self, items):
        self.arrays = [a for a, _ in items]
        self.scatter = [s for _, s in items]
        self.out_shape = [jax.ShapeDtypeStruct(a.shape if s else (N_DEV,) + a.shape, a.dtype) for a, s in items]
        self.in_specs = [_hbm_spec() for _ in items]
        self.out_specs = [_hbm_spec() for _ in items]
        self.scratch = [pltpu.SemaphoreType.DMA((SEMS_PER_ITEM * len(items),))]

    def _copies(self, ins, outs, sems, landing):
        me, peers = _peers()
        cps = []
        for t, scatter in enumerate(self.scatter):
            base = t * SEMS_PER_ITEM
            for k, (pid, plin) in enumerate(peers):
                src = ins[t].at[plin] if scatter else ins[t]
                dst = outs[t].at[plin if landing else me]
                cps.append(pltpu.make_async_remote_copy(
                    src_ref=src, dst_ref=dst, send_sem=sems.at[base + k], recv_sem=sems.at[base + N_DEV - 1 + k],
                    device_id=pid, device_id_type=MESH))
        return cps

    def _own(self, ins, outs, sems):
        me, _ = _peers()
        return [pltpu.make_async_copy(ins[t].at[me] if scatter else ins[t], outs[t].at[me],
                                      sems.at[t * SEMS_PER_ITEM + 2 * (N_DEV - 1)])
                for t, scatter in enumerate(self.scatter)]

    def start(self, ins, outs, sems):
        for cp in self._own(ins, outs, sems) + self._copies(ins, outs, sems, False):
            cp.start()

    def wait(self, ins, outs, sems):
        for cp in self._copies(ins, outs, sems, True):
            cp.wait_recv()
        for cp in self._copies(ins, outs, sems, False):
            cp.wait_send()
        for cp in self._own(ins, outs, sems):
            cp.wait()


def _carry(exchange, ins, outs, sems, first, last):
    @pl.when(first)
    def _():
        exchange.start(ins, outs, sems)

    @pl.when(last)
    def _():
        exchange.wait(ins, outs, sems)


def _exchange_alone(name, exchange):
    n = len(exchange.arrays)

    def body(*refs):
        exchange.start(refs[:n], refs[n:2 * n], refs[2 * n])
        exchange.wait(refs[:n], refs[n:2 * n], refs[2 * n])

    return pl.pallas_call(body, name=name, in_specs=exchange.in_specs, out_specs=exchange.out_specs,
                          out_shape=exchange.out_shape, scratch_shapes=exchange.scratch)(*exchange.arrays)


def _ada_modulation(c8, w_ada, b_ada_cols):
    wcols = w_ada.shape[1]

    def body(c_ref, w_ref, b_ref, call_ref, cond_ref, mod_ref, res, send_sems, recv_sems):
        me, peers = _peers()
        call_ref[me] = c_ref[...]
        _exchange(me, peers, [(lambda p: c_ref, lambda s: call_ref.at[s])], send_sems, recv_sems, t0=0)
        for dev in range(N_DEV):
            cv = call_ref[dev]
            cond = cv * _sigmoid(cv)
            cond_ref[dev] = cond
            res[dev] = jnp.dot(cond, w_ref[...], preferred_element_type=F32,
                               precision=lax.Precision.HIGHEST) + b_ref[...]
        mod_ref[me] = res[me]
        _exchange(me, peers, [(lambda p: res.at[p], lambda s: mod_ref.at[s])], send_sems, recv_sems, t0=1)

    vm = pl.BlockSpec(memory_space=pltpu.VMEM)
    return pl.pallas_call(
        body, name="ada_modulation", in_specs=[vm, vm, vm], out_specs=[vm, vm, vm],
        out_shape=[jax.ShapeDtypeStruct((N_DEV, 8, D), F32), jax.ShapeDtypeStruct((N_DEV, 8, D), F32),
                   jax.ShapeDtypeStruct((N_DEV, 8, wcols), F32)],
        scratch_shapes=[pltpu.VMEM((N_DEV, 8, wcols), F32),
                        pltpu.SemaphoreType.DMA((2, N_DEV - 1)), pltpu.SemaphoreType.DMA((2, N_DEV - 1))],
        compiler_params=pltpu.CompilerParams(vmem_limit_bytes=VMEM_LIMIT),
    )(c8, w_ada, b_ada_cols)


def _lru_gates(xc, wa_ref, ba_ref, wi_ref, bi_ref, sp_ref, row0):
    xcb = xc.astype(BF16)
    pre_r, pre_i = [], []
    for n in range(N_BLK):
        xb = xcb[:, n * BW:(n + 1) * BW]
        pre_r.append(jnp.dot(xb, wa_ref[n], preferred_element_type=F32))
        pre_i.append(jnp.dot(xb, wi_ref[n], preferred_element_type=F32))
    r = _sigmoid(jnp.concatenate(pre_r, axis=1) + ba_ref[...])
    ig = _sigmoid(jnp.concatenate(pre_i, axis=1) + bi_ref[...])
    log_a = (-LRU_C) * r * sp_ref[...]
    a = jnp.exp(log_a)
    e2 = jnp.exp(2.0 * log_a)
    mult_raw = jnp.sqrt(1.0 - e2)
    rows = row0 + lax.broadcasted_iota(jnp.int32, xc.shape, 0)
    start = rows == 0
    mult = jnp.where(start, 1.0, mult_raw)
    return xcb, r, ig, a, e2, mult_raw, mult, start


def _conv(xpad, cw_ref, cb_ref, tc):
    out = cb_ref[...]
    for tap in range(CONV_W):
        out = out + cw_ref[tap:tap + 1, :] * xpad[pl.ds(8 - (CONV_W - 1) + tap, tc), :]
    return out


def _rnn_fwd(proj_rnn, cw, cb, wa, ba, wi, bi, sp, carry):
    s_len = proj_rnn.shape[0]
    tc = RNN_CHUNK
    n_c = len(carry.arrays)
    n_out = 5

    def kern(*refs):
        y_ref, xr_ref, cw_ref, cb_ref, wa_ref, ba_ref, wi_ref, bi_ref, sp_ref = refs[:9]
        hs_ref, ur_ref, xc_ref, r_ref, ig_ref = refs[9 + n_c:9 + n_c + n_out]
        xpad, a_scr, u_scr, h_scr = refs[9 + n_out + 2 * n_c:13 + n_out + 2 * n_c]
        i = pl.program_id(0)
        _carry(carry, refs[9:9 + n_c], refs[9 + n_out + n_c:9 + n_out + 2 * n_c], refs[-1], i == 0,
               i == s_len // tc - 1)

        @pl.when(i == 0)
        def _():
            xpad[0:8, :] = jnp.zeros((8, DR), F32)
            h_scr[...] = jnp.zeros((1, DR), F32)

        xpad[8:8 + tc, :] = xr_ref[...].astype(F32)
        xc = _conv(xpad, cw_ref, cb_ref, tc)
        xpad[0:8, :] = xpad[tc:tc + 8, :]
        xcb, r, ig, a, e2, mult_raw, mult, start = _lru_gates(xc, wa_ref, ba_ref, wi_ref, bi_ref, sp_ref, i * tc)
        a_scr[...] = a
        u_scr[...] = mult * (ig * xc)
        xc_ref[...] = xcb
        r_ref[...] = r.astype(BF16)
        ig_ref[...] = ig.astype(BF16)

        def step(t, h):
            h = a_scr[pl.ds(t, 1), :] * h + u_scr[pl.ds(t, 1), :]
            hs_ref[pl.ds(t, 1), :] = h
            return h

        h_scr[...] = lax.fori_loop(0, tc, step, h_scr[...], unroll=8)
        gy, _ = _gelu(y_ref[...].astype(F32))
        ur_ref[...] = (gy * hs_ref[...]).astype(BF16)

    full = lambda a: pl.BlockSpec(a.shape, lambda i: (0,) * a.ndim)
    res = pl.pallas_call(
        kern, name="rnn_fwd", grid=(s_len // tc,),
        in_specs=[pl.BlockSpec((tc, DR), lambda i: (i, 0)), pl.BlockSpec((tc, DR), lambda i: (i, 1)),
                  full(cw), full(cb), full(wa), full(ba), full(wi), full(bi), full(sp)] + carry.in_specs,
        out_specs=[pl.BlockSpec((tc, DR), lambda i: (i, 0))] * n_out + carry.out_specs,
        out_shape=[jax.ShapeDtypeStruct((s_len, DR), F32)] + [jax.ShapeDtypeStruct((s_len, DR), BF16)] * (n_out - 1)
        + carry.out_shape,
        scratch_shapes=[pltpu.VMEM((tc + 8, DR), F32), pltpu.VMEM((tc, DR), F32), pltpu.VMEM((tc, DR), F32),
                        pltpu.VMEM((1, DR), F32)] + carry.scratch,
        compiler_params=_cparams(("arbitrary",)),
    )(proj_rnn, proj_rnn, cw, cb, wa, ba, wi, bi, sp, *carry.arrays)
    return res[0], res[1], res[2:n_out], res[n_out:]


def _rnn_bwd(proj_rnn, hs, du, xc_b, r_b, ig_b, cw, wat, wit, sp, dsp_dlam):
    s_len = proj_rnn.shape[0]
    tc = RNN_CHUNK
    nch = s_len // tc

    def kern(y_ref, xr_ref, xrp_ref, hs_ref, hsp_ref, du_ref, xc_ref, r_ref, ig_ref, cw_ref, wat_ref, wit_ref, sp_ref,
             dspl_ref, drnn_ref, dwa_ref, dwi_ref, vec_ref, xpad, hpad, a_scr, d_scr, g_scr, dxcpad, ag_scr):
        j = pl.program_id(0)
        n = nch - 1 - j

        @pl.when(j == 0)
        def _():
            dwa_ref[...] = jnp.zeros(dwa_ref.shape, F32)
            dwi_ref[...] = jnp.zeros(dwi_ref.shape, F32)
            vec_ref[...] = jnp.zeros(vec_ref.shape, F32)
            ag_scr[...] = jnp.zeros((1, DR), F32)
            dxcpad[tc:tc + 8, :] = jnp.zeros((8, DR), F32)

        has_prev = n > 0
        xpad[0:8, :] = jnp.where(has_prev, xrp_ref[8:16, :].astype(F32), 0.0)
        xpad[8:8 + tc, :] = xr_ref[...].astype(F32)
        hpad[0:8, :] = jnp.where(has_prev, hsp_ref[...], 0.0)
        hpad[8:8 + tc, :] = hs_ref[...]
        xcb = xc_ref[...]
        xc, r, ig = xcb.astype(F32), r_ref[...].astype(F32), ig_ref[...].astype(F32)
        log_a = (-LRU_C) * r * sp_ref[...]
        a = jnp.exp(log_a)
        e2 = jnp.exp(2.0 * log_a)
        mult_raw = jnp.sqrt(1.0 - e2)
        start = (n * tc + lax.broadcasted_iota(jnp.int32, (tc, DR), 0)) == 0
        mult = jnp.where(start, 1.0, mult_raw)

        y = y_ref[...].astype(F32)
        gy, th = _gelu(y)
        duv = du_ref[...].astype(F32)
        drnn_ref[:, 0:DR] = (duv * hs_ref[...] * _gelu_grad(y, th)).astype(BF16)
        a_scr[...] = a
        d_scr[...] = duv * gy

        def step(tt, ag):
            t = tc - 1 - tt
            g = d_scr[pl.ds(t, 1), :] + ag
            g_scr[pl.ds(t, 1), :] = g
            return a_scr[pl.ds(t, 1), :] * g

        ag_scr[...] = lax.fori_loop(0, tc, step, ag_scr[...], unroll=8)
        g = g_scr[...]
        da = g * hpad[pl.ds(7, tc), :]
        gx = g * xc
        dmult = jnp.where(start, 0.0, gx * ig)
        di = gx * mult
        dxc = g * mult * ig
        dlog_a = da * a - jnp.where(start, 0.0, dmult * e2 / mult_raw)
        dr = dlog_a * ((-LRU_C) * sp_ref[...])
        vec_ref[7:8, :] += _rowsum(dlog_a * ((-LRU_C) * r))
        dpr = dr * r * (1.0 - r)
        dpi = di * ig * (1.0 - ig)
        vec_ref[5:6, :] += _rowsum(dpr)
        vec_ref[6:7, :] += _rowsum(dpi)
        dprb, dpib = dpr.astype(BF16), dpi.astype(BF16)
        extra = []
        for b in range(N_BLK):
            sl = slice(b * BW, (b + 1) * BW)
            extra.append(jnp.dot(dprb[:, sl], wat_ref[b], preferred_element_type=F32)
                         + jnp.dot(dpib[:, sl], wit_ref[b], preferred_element_type=F32))
            dn = (((0,), (0,)), ((), ()))
            dwa_ref[b] += lax.dot_general(xcb[:, sl], dprb[:, sl], dn, preferred_element_type=F32)
            dwi_ref[b] += lax.dot_general(xcb[:, sl], dpib[:, sl], dn, preferred_element_type=F32)
        dxc = dxc + jnp.concatenate(extra, axis=1)
        vec_ref[4:5, :] += _rowsum(dxc)
        dxcpad[0:tc, :] = dxc
        dxr = jnp.zeros((tc, DR), F32)
        for tap in range(CONV_W):
            shift = CONV_W - 1 - tap
            dxr = dxr + cw_ref[tap:tap + 1, :] * dxcpad[pl.ds(shift, tc), :]
            vec_ref[tap:tap + 1, :] += _rowsum(dxc * xpad[pl.ds(8 - shift, tc), :])
        dxcpad[tc:tc + 8, :] = dxcpad[0:8, :]
        drnn_ref[:, DR:2 * DR] = dxr.astype(BF16)

        @pl.when(j == nch - 1)
        def _():
            vec_ref[7:8, :] = vec_ref[7:8, :] * dspl_ref[...]

    full = lambda a: pl.BlockSpec(a.shape, lambda j: (0,) * a.ndim)
    rev = lambda j: nch - 1 - j
    prev8 = lambda j: jnp.maximum((nch - 1 - j) * (tc // 8) - 1, 0)
    prev16 = lambda j: jnp.maximum((nch - 1 - j) * (tc // 16) - 1, 0)
    chunk = pl.BlockSpec((tc, DR), lambda j: (rev(j), 0))
    return pl.pallas_call(
        kern, name="rnn_bwd", grid=(nch,),
        in_specs=[chunk, pl.BlockSpec((tc, DR), lambda j: (rev(j), 1)), pl.BlockSpec((16, DR), lambda j: (prev16(j), 1)),
                  chunk, pl.BlockSpec((8, DR), lambda j: (prev8(j), 0)), chunk, chunk, chunk, chunk,
                  full(cw), full(wat), full(wit), full(sp), full(dsp_dlam)],
        out_specs=[pl.BlockSpec((tc, 2 * DR), lambda j: (rev(j), 0)),
                   pl.BlockSpec((N_BLK, BW, BW), lambda j: (0, 0, 0)), pl.BlockSpec((N_BLK, BW, BW), lambda j: (0, 0, 0)),
                   pl.BlockSpec((8, DR), lambda j: (0, 0))],
        out_shape=[jax.ShapeDtypeStruct((s_len, 2 * DR), BF16), jax.ShapeDtypeStruct((N_BLK, BW, BW), F32),
                   jax.ShapeDtypeStruct((N_BLK, BW, BW), F32), jax.ShapeDtypeStruct((8, DR), F32)],
        scratch_shapes=[pltpu.VMEM((tc + 8, DR), F32), pltpu.VMEM((tc + 8, DR), F32), pltpu.VMEM((tc, DR), F32),
                        pltpu.VMEM((tc, DR), F32), pltpu.VMEM((tc, DR), F32), pltpu.VMEM((tc + 8, DR), F32),
                        pltpu.VMEM((1, DR), F32)],
        compiler_params=_cparams(("arbitrary",)),
    )(proj_rnn, proj_rnn, proj_rnn, hs, hs, du, xc_b, r_b, ig_b, cw, wat, wit, sp, dsp_dlam)


ATT_BLK = 2048


def _attn_units():
    return [(g, d, b, b * QB * d + r)
            for g, d in enumerate(DILATIONS) for b in range(ATT_BLK // (QB * d)) for r in range(d)]


def _rows_at(ref, start, n, d):
    return ref[pl.ds(start, n, stride=d), :] if d > 1 else ref[start:start + n, :]


def _add_rows_at(ref, start, n, d, val):
    if d > 1:
        ref[pl.ds(start, n, stride=d), :] = ref[pl.ds(start, n, stride=d), :] + val
    else:
        ref[start:start + n, :] = ref[start:start + n, :] + val


def _stack_heads(x, first):
    zero = jnp.zeros_like(x)
    return jnp.concatenate([jnp.where(first, x, zero), jnp.where(first, zero, x)], axis=0)


def _unstack_heads(x, first):
    return jnp.where(first, x[:QB], x[QB:])


def _head_column(x, lane, half):
    return jnp.sum(jnp.where(lane == 64 * half, x, 0.0), axis=-1, keepdims=True)


def _band_masks():
    row = lax.broadcasted_iota(jnp.int32, (2 * QB, 2 * QB), 0) & (QB - 1)
    col = lax.broadcasted_iota(jnp.int32, (2 * QB, 2 * QB), 1)
    dist = QB + row - col
    lane = lax.broadcasted_iota(jnp.int32, (1, HEAD_PAIR), 1)
    return (dist >= 0) & (dist <= QB), col >= QB, lane


def _attention_specs(nblk, blk_of):
    cur = lambda off: pl.BlockSpec((ATT_BLK, HEAD_PAIR), lambda hp, j: (blk_of(j), off + hp))
    prev = lambda off: pl.BlockSpec((ATT_BLK, HEAD_PAIR), lambda hp, j: (jnp.maximum(blk_of(j) - 1, 0), off + hp))
    return cur, prev


def _attention_fwd(qkv):
    s_len = qkv.shape[0]
    nblk = s_len // ATT_BLK
    units = _attn_units()
    nt = (((1,), (1,)), ((), ()))

    def kern(q_ref, k_ref, v_ref, o_ref, lse_ref, kbuf, vbuf, og, lg):
        blk = pl.program_id(1)

        @pl.when(blk == 0)
        def _():
            kbuf[0:ATT_BLK, :] = jnp.zeros((ATT_BLK, HEAD_PAIR), F32)
            vbuf[0:ATT_BLK, :] = jnp.zeros((ATT_BLK, HEAD_PAIR), F32)

        kbuf[ATT_BLK:2 * ATT_BLK, :] = k_ref[...]
        vbuf[ATT_BLK:2 * ATT_BLK, :] = v_ref[...]
        band, later, lane = _band_masks()
        first = lane < 64
        band0 = band & (later | (blk > 0))
        for g, d, b, start in units:
            kstart = ATT_BLK + start - QB * d
            qs = _stack_heads((_rows_at(q_ref, start, QB, d) * SCALE).astype(BF16), first)
            k2 = _rows_at(kbuf, kstart, 2 * QB, d).astype(BF16)
            v2 = _rows_at(vbuf, kstart, 2 * QB, d).astype(BF16)
            s = lax.dot_general(qs, k2, nt, preferred_element_type=F32)
            s = jnp.where(band if b > 0 else band0, s, NEG)
            mx = jnp.max(s, axis=-1, keepdims=True)
            p = jnp.exp(s - mx)
            den = jnp.sum(p, axis=-1, keepdims=True)
            out = jnp.dot(p.astype(BF16), v2, preferred_element_type=F32) / den
            lse2 = jnp.broadcast_to(mx + jnp.log(den), (2 * QB, HEAD_PAIR))
            if d > 1:
                og[g, pl.ds(start, QB, stride=d), :] = _unstack_heads(out, first)
                lg[g, pl.ds(start, QB, stride=d), :] = _unstack_heads(lse2, first)
            else:
                og[g, start:start + QB, :] = _unstack_heads(out, first)
                lg[g, start:start + QB, :] = _unstack_heads(lse2, first)
        kbuf[0:ATT_BLK, :] = kbuf[ATT_BLK:2 * ATT_BLK, :]
        vbuf[0:ATT_BLK, :] = vbuf[ATT_BLK:2 * ATT_BLK, :]
        mx = jnp.maximum(jnp.maximum(lg[0], lg[1]), lg[2])
        es = [jnp.exp(lg[g] - mx) for g in range(3)]
        tot = es[0] + es[1] + es[2]
        o_ref[...] = ((es[0] * og[0] + es[1] * og[1] + es[2] * og[2]) / tot).astype(BF16)
        lse_ref[...] = mx + jnp.log(tot)

    cur, _ = _attention_specs(nblk, lambda j: j)
    out_spec = pl.BlockSpec((ATT_BLK, HEAD_PAIR), lambda hp, j: (j, hp))
    return pl.pallas_call(
        kern, name="attention_fwd", grid=(N_PAIR, nblk), in_specs=[cur(0), cur(N_PAIR), cur(2 * N_PAIR)],
        out_specs=[out_spec, out_spec],
        out_shape=[jax.ShapeDtypeStruct((s_len, D), BF16), jax.ShapeDtypeStruct((s_len, D), F32)],
        scratch_shapes=[pltpu.VMEM((2 * ATT_BLK, HEAD_PAIR), F32), pltpu.VMEM((2 * ATT_BLK, HEAD_PAIR), F32),
                        pltpu.VMEM((3, ATT_BLK, HEAD_PAIR), F32), pltpu.VMEM((3, ATT_BLK, HEAD_PAIR), F32)],
        compiler_params=_cparams(("arbitrary", "arbitrary")),
    )(qkv, qkv, qkv)


def _attention_bwd(qkv, do, o, lse, carry):
    s_len = qkv.shape[0]
    nblk = s_len // ATT_BLK
    units = _attn_units()
    nt = (((1,), (1,)), ((), ()))
    tn = (((0,), (0,)), ((), ()))
    n_c = len(carry.arrays) if carry else 0
    c_in, c_out = (carry.in_specs, carry.out_specs) if carry else ([], [])
    c_shape, c_scratch, c_arrays = (carry.out_shape, carry.scratch, carry.arrays) if carry else ([], [], [])

    def kern(*refs):
        q_ref, k_ref, v_ref, kp_ref, vp_ref, do_ref, o_ref, lse_ref = refs[:8]
        dq_ref, dk_ref, dv_ref = refs[8 + n_c:11 + n_c]
        kbuf, vbuf, dkbuf, dvbuf, dq_scr, dsum = refs[11 + 2 * n_c:17 + 2 * n_c]
        hp, j = pl.program_id(0), pl.program_id(1)
        blk = nblk - 1 - j
        if carry:
            _carry(carry, refs[8:8 + n_c], refs[11 + n_c:11 + 2 * n_c], refs[-1], (hp == 0) & (j == 0),
                   (hp == N_PAIR - 1) & (j == nblk - 1))
        zeros = jnp.zeros((ATT_BLK, HEAD_PAIR), F32)

        @pl.when(j == 0)
        def _():
            dkbuf[ATT_BLK:2 * ATT_BLK, :] = zeros
            dvbuf[ATT_BLK:2 * ATT_BLK, :] = zeros

        @pl.when(j > 0)
        def _():
            dkbuf[ATT_BLK:2 * ATT_BLK, :] = dkbuf[0:ATT_BLK, :]
            dvbuf[ATT_BLK:2 * ATT_BLK, :] = dvbuf[0:ATT_BLK, :]

        dkbuf[0:ATT_BLK, :] = zeros
        dvbuf[0:ATT_BLK, :] = zeros
        dq_scr[...] = zeros
        kbuf[0:ATT_BLK, :] = kp_ref[...]
        kbuf[ATT_BLK:2 * ATT_BLK, :] = k_ref[...]
        vbuf[0:ATT_BLK, :] = vp_ref[...]
        vbuf[ATT_BLK:2 * ATT_BLK, :] = v_ref[...]
        band, later, lane = _band_masks()
        first = lane < 64
        band0 = band & (later | (blk > 0))
        prod = do_ref[...] * o_ref[...].astype(F32)
        dsum[...] = jnp.where(first, jnp.sum(jnp.where(first, prod, 0.0), axis=-1, keepdims=True),
                              jnp.sum(jnp.where(first, 0.0, prod), axis=-1, keepdims=True))
        for g, d, b, start in units:
            kstart = ATT_BLK + start - QB * d
            qs = _stack_heads((_rows_at(q_ref, start, QB, d) * SCALE).astype(BF16), first)
            dos = _stack_heads(_rows_at(do_ref, start, QB, d).astype(BF16), first)
            k2 = _rows_at(kbuf, kstart, 2 * QB, d).astype(BF16)
            v2 = _rows_at(vbuf, kstart, 2 * QB, d).astype(BF16)
            ds_rows = _rows_at(dsum, start, QB, d)
            lse_rows = _rows_at(lse_ref, start, QB, d)
            dcol = jnp.concatenate([_head_column(ds_rows, lane, 0), _head_column(ds_rows, lane, 1)], axis=0)
            lcol = jnp.concatenate([_head_column(lse_rows, lane, 0), _head_column(lse_rows, lane, 1)], axis=0)
            s = lax.dot_general(qs, k2, nt, preferred_element_type=F32)
            p = jnp.exp(jnp.where(band if b > 0 else band0, s, NEG) - lcol)
            dp = lax.dot_general(dos, v2, nt, preferred_element_type=F32)
            ds = (p * (dp - dcol)).astype(BF16)
            _add_rows_at(dq_scr, start, QB, d,
                         _unstack_heads(jnp.dot(ds, k2, preferred_element_type=F32), first) * SCALE)
            _add_rows_at(dkbuf, kstart, 2 * QB, d, lax.dot_general(ds, qs, tn, preferred_element_type=F32))
            _add_rows_at(dvbuf, kstart, 2 * QB, d, lax.dot_general(p.astype(BF16), dos, tn, preferred_element_type=F32))
        dq_ref[...] = dq_scr[...].astype(BF16)
        dk_ref[...] = dkbuf[ATT_BLK:2 * ATT_BLK, :].astype(BF16)
        dv_ref[...] = dvbuf[ATT_BLK:2 * ATT_BLK, :].astype(BF16)

    rev = lambda j: nblk - 1 - j
    cur, prev = _attention_specs(nblk, rev)
    buf = lambda rows: pltpu.VMEM((rows, HEAD_PAIR), F32)
    res = pl.pallas_call(
        kern, name="attention_bwd", grid=(N_PAIR, nblk),
        in_specs=[cur(0), cur(N_PAIR), cur(2 * N_PAIR), prev(N_PAIR), prev(2 * N_PAIR), cur(0), cur(0), cur(0)] + c_in,
        out_specs=[cur(0)] * 3 + c_out,
        out_shape=[jax.ShapeDtypeStruct((s_len, D), BF16)] * 3 + c_shape,
        scratch_shapes=[buf(2 * ATT_BLK), buf(2 * ATT_BLK), buf(2 * ATT_BLK), buf(2 * ATT_BLK), buf(ATT_BLK),
                        buf(ATT_BLK)] + c_scratch,
        compiler_params=_cparams(("arbitrary", "arbitrary")),
    )(qkv, qkv, qkv, qkv, qkv, do, o, lse, *c_arrays)
    return res[0], res[1], res[2], res[3:]


def _adamw(w, g, m, v):
    m = ADAM_B1 * m + (1.0 - ADAM_B1) * g
    v = ADAM_B2 * v + (1.0 - ADAM_B2) * (g * g)
    m_hat = m / (1.0 - ADAM_B1 ** ADAM_STEP)
    v_hat = v / (1.0 - ADAM_B2 ** ADAM_STEP)
    delta = -ADAM_LR * (m_hat / (jnp.sqrt(v_hat) + ADAM_EPS) + ADAM_WD * w)
    return delta, m, v


def _adam_packed(name, recv, w, m, v, tm):
    n_rows, width = w.shape
    summed = recv.ndim == 3

    def kern(r_ref, w_ref, m_ref, v_ref, g_out, d_out, m_out, v_out):
        if summed:
            g = r_ref[0].astype(F32)
            for s in range(1, N_DEV):
                g = g + r_ref[s].astype(F32)
        else:
            g = r_ref[...]
        delta, mn, vn = _adamw(w_ref[...], g, m_ref[...], v_ref[...])
        g_out[...] = g
        d_out[...] = delta
        m_out[...] = mn
        v_out[...] = vn

    tile = pl.BlockSpec((tm, width), lambda i: (i, 0))
    rspec = pl.BlockSpec((N_DEV, tm, width), lambda i: (0, i, 0)) if summed else tile
    return pl.pallas_call(
        kern, name=name, grid=(n_rows // tm,), in_specs=[rspec, tile, tile, tile], out_specs=[tile] * 4,
        out_shape=[jax.ShapeDtypeStruct((n_rows, width), F32)] * 4, compiler_params=_cparams(("parallel",)),
    )(recv, w, m, v)


def _sum_slots(recv):
    _, n_rows, width = recv.shape

    def kern(r_ref, o_ref):
        g = r_ref[0]
        for s in range(1, N_DEV):
            g = g + r_ref[s]
        o_ref[...] = g

    return pl.pallas_call(
        kern, name="sum_small", grid=(1,), in_specs=[pl.BlockSpec(recv.shape, lambda i: (0, 0, 0))],
        out_specs=pl.BlockSpec((n_rows, width), lambda i: (0, 0)),
        out_shape=jax.ShapeDtypeStruct((n_rows, width), F32), compiler_params=_cparams(("arbitrary",)),
    )(recv)


def _adam_w_ada(cond_t, dmod_cols, w, m, v):
    n_rows, width = w.shape
    tm = ROW_TILE

    def kern(c_ref, d_ref, w_ref, m_ref, v_ref, g_out, d_out, m_out, v_out):
        g = c_ref[:, 0:1] * d_ref[0:1, :]
        for b in range(1, N_DEV):
            g = g + c_ref[:, b:b + 1] * d_ref[b:b + 1, :]
        delta, mn, vn = _adamw(w_ref[...], g, m_ref[...], v_ref[...])
        g_out[...] = g
        d_out[...] = delta
        m_out[...] = mn
        v_out[...] = vn

    tile = pl.BlockSpec((tm, width), lambda i: (i, 0))
    return pl.pallas_call(
        kern, name="adam_w_ada", grid=(n_rows // tm,),
        in_specs=[pl.BlockSpec((tm, N_DEV), lambda i: (i, 0)), pl.BlockSpec((N_DEV, width), lambda i: (0, 0)),
                  tile, tile, tile],
        out_specs=[tile] * 4, out_shape=[jax.ShapeDtypeStruct((n_rows, width), F32)] * 4,
        compiler_params=_cparams(("parallel",)),
    )(cond_t, dmod_cols, w, m, v)


def _rows(a, n_rows=None):
    flat = a.reshape(-1)
    need = (n_rows if n_rows is not None else -(-flat.shape[0] // D)) * D
    if need != flat.shape[0]:
        flat = jnp.concatenate([flat, jnp.zeros((need - flat.shape[0],), flat.dtype)])
    return flat.reshape(-1, D)


def _slots_to_cols(s, k, cols):
    return s.reshape(N_DEV, k, cols).transpose(1, 0, 2).reshape(k, N_DEV * cols)


def kernel(x, c, w_ada, b_ada, w_in, conv_w, conv_b, lru_wa, lru_ba, lru_wi, lru_bi, lru_lambda, w_proj_rnn, w_proj_attn, b_gate, w_out, ln1_g, ln1_b, w_ffn_gate, w_ffn_up, w_ffn_down, ln2_g, ln2_b, loss_target, m_w_ada, m_b_ada, m_w_in, m_conv_w, m_conv_b, m_lru_wa, m_lru_ba, m_lru_wi, m_lru_bi, m_lru_lambda, m_w_proj_rnn, m_w_proj_attn, m_b_gate, m_w_out, m_ln1_g, m_ln1_b, m_w_ffn_gate, m_w_ffn_up, m_w_ffn_down, m_ln2_g, m_ln2_b, v_w_ada, v_b_ada, v_w_in, v_conv_w, v_conv_b, v_lru_wa, v_lru_ba, v_lru_wi, v_lru_bi, v_lru_lambda, v_w_proj_rnn, v_w_proj_attn, v_b_gate, v_w_out, v_ln1_g, v_ln1_b, v_w_ffn_gate, v_w_ffn_up, v_w_ffn_down, v_ln2_g, v_ln2_b):
    s_len = x.shape[1]
    me = 4 * lax.axis_index("x") + 2 * lax.axis_index("y") + lax.axis_index("c")
    xs = x[0]
    tgt = loss_target[0]
    tm = ROW_TILE

    b_ada_cols = lax.dynamic_slice(b_ada, (0, me * 768), (1, 768))
    pad_cols = lambda a: jnp.concatenate([a, jnp.zeros((a.shape[0], D - a.shape[1]), F32)], axis=1)
    c8 = jnp.concatenate([c, pad_cols(conv_w[0]), pad_cols(b_gate[0]), jnp.zeros((1, D), F32)], axis=0)
    c_all, cond_blocks, mod_parts = _ada_modulation(c8, w_ada[0], b_ada_cols)
    cond_all = cond_blocks[:, 0, :]
    mod = mod_parts[:, 0, :].reshape(6, D)
    mod8 = jnp.concatenate([mod, jnp.zeros((2, D), F32)], axis=0)
    cw = c_all[:, 1:1 + CONV_W, :DR // N_DEV].transpose(1, 0, 2).reshape(CONV_W, DR)
    bg = c_all[:, 1 + CONV_W:3 + CONV_W, :D // N_DEV].transpose(1, 0, 2).reshape(2, D)
    bg8 = jnp.concatenate([bg, jnp.zeros((6, D), F32)], axis=0)

    late = dict(w_ffn_gate=w_ffn_gate, w_ffn_up=w_ffn_up, w_proj_rnn=w_proj_rnn, w_proj_attn=w_proj_attn,
                w_out=w_out, w_ffn_down=w_ffn_down)
    late_shard = jnp.concatenate([late[name][0].reshape(r, D) for name, r in LATE_ROWS], axis=0).astype(BF16)

    cb = conv_b
    wa_b, wi_b = lru_wa[0].astype(BF16), lru_wi[0].astype(BF16)
    wat_b, wit_b = jnp.swapaxes(wa_b, 1, 2), jnp.swapaxes(wi_b, 1, 2)
    ba, bi = lru_ba.reshape(1, DR), lru_bi.reshape(1, DR)
    sp = jax.nn.softplus(-lru_lambda)
    dsp_dlam = -jax.nn.sigmoid(-lru_lambda)
    ln1 = jnp.concatenate([ln1_g, ln1_b, jnp.zeros((6, D), F32)], axis=0)
    ln2 = jnp.concatenate([ln2_g, ln2_b, jnp.zeros((6, D), F32)], axis=0)

    def f1(ri, fi, ro, ao):
        n, _ = _ln(ri[0][...])
        ro[0][...] = (n * (1.0 + fi[0][1:2, :]) + fi[0][0:1, :]).astype(BF16)

    h1, w_in_slots = _rowwise(f1, "ln_mod1", s_len, tm, [(xs, D, 0)], [mod8], [(D, BF16)], [],
                              carry=_ChipGather(w_in[0].astype(BF16)))
    w_in_f = w_in_slots.transpose(1, 0, 2).reshape(D, N_DEV * 960)
    w_rnn, w_qkv, w_gates = w_in_f[:, :2 * DR], w_in_f[:, 2 * DR:2 * DR + 3 * D], w_in_f[:, 2 * DR + 3 * D:]
    proj_rnn = _mm("mm_in_rnn", [(h1, w_rnn)], BF16, 512, DR)
    qkv = _mm("mm_in_qkv", [(h1, w_qkv)], F32, 512, D)
    gates = _mm("mm_in_gates", [(h1, w_gates)], BF16, 512, D)

    hs, u_rnn, kept, (late_all,) = _rnn_fwd(proj_rnn, cw, cb, wa_b, ba, wi_b, bi, sp, _Exchange([(late_shard, False)]))
    offs, o = {}, 0
    for name, r in LATE_ROWS:
        offs[name] = (o, o + r)
        o += r
    part = lambda name: late_all[:, offs[name][0]:offs[name][1], :]
    w_g, w_u = _slots_to_cols(part("w_ffn_gate"), D, 352), _slots_to_cols(part("w_ffn_up"), D, 352)
    w_pr = part("w_proj_rnn").reshape(DR, D)
    w_pa = part("w_proj_attn").reshape(D, D)
    w_o = part("w_out").reshape(D, D)
    w_dn = part("w_ffn_down").reshape(DFF, D)

    u_attn, lse = _attention_fwd(qkv)

    def e_merge(dots, ti, fi, ro, ao):
        g_r = _sigmoid(ti[0][...] + fi[0][0:1, :])
        g_a = _sigmoid(ti[1][...] + fi[0][1:2, :])
        ro[0][...] = dots[0].astype(BF16)
        ro[1][...] = dots[1].astype(BF16)
        ro[2][...] = (g_r * dots[0] + g_a * dots[1]).astype(BF16)

    pr, pa, merged = _mm_fused("mm_proj_merge", [(u_rnn, w_pr), (u_attn, w_pa)], [(gates, 0), (gates, 1)], [bg8],
                               [(BF16, 1), (BF16, 1), (BF16, 1)], [], e_merge, 512, D, sub=256)
    def e_post1(dots, ti, fi, ro, ao):
        md, l1 = fi[0], fi[1]
        fv = dots[0]
        n1, _ = _ln(ALPHA * ti[0][...] + md[2:3, :] * fv)
        x1 = n1 * l1[0:1, :] + l1[1:2, :]
        n0, _ = _ln(x1)
        ro[0][...] = fv
        ro[1][...] = x1
        ro[2][...] = (n0 * (1.0 + md[4:5, :]) + md[3:4, :]).astype(BF16)

    f, x1, h2 = _mm_fused("mm_out_post1", [(merged, w_o)], [(xs, 0)], [mod8, ln1], [(F32, 1), (F32, 1), (BF16, 1)], [],
                          e_post1, 512, D, sub=256)

    def e_swiglu(dots, ti, fi, ro, ao):
        gp, up = dots
        ro[0][...] = gp.astype(BF16)
        ro[1][...] = up.astype(BF16)
        ro[2][...] = (gp * _sigmoid(gp) * up).astype(BF16)

    gpre, upre, act = _mm_fused("mm_ffn_in", [(h2, w_g), (h2, w_u)], [], [], [(BF16, 1)] * 3, [], e_swiglu, 512, DFF // 2)
    def e_loss(dots, ti, fi, ro, ao):
        md, l2 = fi[0], fi[1]
        f2v = dots[0]
        n2, rstd = _ln(ALPHA * ti[0][...] + md[5:6, :] * f2v)
        err = n2 * l2[0:1, :] + l2[1:2, :] - ti[1][...]
        dx2 = err * (1.0 / D)
        dz2 = _ln_bwd(dx2 * l2[0:1, :], n2, rstd)
        ro[0][...] = dz2
        ro[1][...] = (md[5:6, :] * dz2).astype(BF16)
        acc = ao[0]
        acc[0:1, :] += _rowsum(err * err) * (0.5 / D)
        acc[1:2, :] += _rowsum(dx2 * n2)
        acc[2:3, :] += _rowsum(dx2)
        acc[3:4, :] += _rowsum(dz2 * f2v)

    dz2, df2, acc5 = _mm_fused("mm_ffn_out_loss", [(act, w_dn)], [(x1, 0), (tgt, 0)], [mod8, ln2],
                               [(F32, 1), (BF16, 1)], [(8, D)], e_loss, 512, D, sub=256)

    d_w_dn = _mm_tn("mmt_ffn_down", act, df2, 1024, 512)

    def e_dswiglu(dots, ti, fi, ro, ao):
        da = dots[0]
        gp, up = ti[0][...].astype(F32), ti[1][...].astype(F32)
        sg = _sigmoid(gp)
        ro[0][...] = (da * up * sg * (1.0 + gp * (1.0 - sg))).astype(BF16)
        ro[1][...] = (da * gp * sg).astype(BF16)

    dgp, dup = _mm_fused("mm_d_ffn_out", [(df2, w_dn, True)], [(gpre, 0), (upre, 0)], [], [(BF16, 1)] * 2, [],
                         e_dswiglu, 512, DFF // 2, sub=256)
    d_w_g = _mm_tn("mmt_ffn_gate", h2, dgp, 2048, DFF // 2)
    d_w_u = _mm_tn("mmt_ffn_up", h2, dup, 2048, DFF // 2)

    def e_dpost1(dots, ti, fi, ro, ao):
        md, l1 = fi[0], fi[1]
        dh2v = dots[0] + dots[1]
        fv = ti[1][...]
        n1, rstd1 = _ln(ALPHA * ti[0][...] + md[2:3, :] * fv)
        n0, rstd0 = _ln(n1 * l1[0:1, :] + l1[1:2, :])
        dx1 = ALPHA * ti[2][...] + _ln_bwd(dh2v * (1.0 + md[4:5, :]), n0, rstd0)
        dz1 = _ln_bwd(dx1 * l1[0:1, :], n1, rstd1)
        ro[0][...] = ALPHA * dz1
        ro[1][...] = (md[2:3, :] * dz1).astype(BF16)
        acc = ao[0]
        acc[0:1, :] += _rowsum(dh2v * n0)
        acc[1:2, :] += _rowsum(dh2v)
        acc[2:3, :] += _rowsum(dx1 * n1)
        acc[3:4, :] += _rowsum(dx1)
        acc[4:5, :] += _rowsum(dz1 * fv)

    dxp, df, acc2 = _mm_fused("mm_d_ffn_in_post1", [(dgp, w_g, True), (dup, w_u, True)], [(xs, 0), (f, 0), (dz2, 0)],
                              [mod8, ln1], [(F32, 1), (BF16, 1)], [(8, D)], e_dpost1, 256, D)
    d_w_o = _mm_tn("mmt_out", merged, df, 2048, D)

    def e_dmerge(dots, ti, fi, ro, ao):
        dm = dots[0]
        g_r = _sigmoid(ti[0][...] + fi[0][0:1, :])
        g_a = _sigmoid(ti[1][...] + fi[0][1:2, :])
        ro[0][...] = (dm * g_r).astype(BF16)
        ro[1][...] = (dm * g_a).astype(BF16)
        dga = dm * ti[2][...].astype(F32) * g_r * (1.0 - g_r)
        dgb = dm * ti[3][...].astype(F32) * g_a * (1.0 - g_a)
        ro[2][:, 0:D] = dga.astype(BF16)
        ro[2][:, D:2 * D] = dgb.astype(BF16)
        ao[0][0:1, :] += _rowsum(dga)
        ao[0][1:2, :] += _rowsum(dgb)

    dpr, dpa, dgates, acc3 = _mm_fused("mm_d_out", [(df, w_o, True)], [(gates, 0), (gates, 1), (pr, 0), (pa, 0)], [bg8],
                                       [(BF16, 1), (BF16, 1), (BF16, 2)], [(8, D)], e_dmerge, 512, D, sub=256)
    du_rnn = _mm("mm_d_proj_rnn", [(dpr, w_pr, True)], F32, 512, DR)
    du_attn = _mm("mm_d_proj_attn", [(dpa, w_pa, True)], F32, 512, D)
    d_w_pr = _mm_tn("mmt_proj_rnn", u_rnn, dpr, 2048, D)
    d_w_pa = _mm_tn("mmt_proj_attn", u_attn, dpa, 2048, D)

    drnn, d_wa, d_wi, vec = _rnn_bwd(proj_rnn, hs, du_rnn, *kept, cw, wat_b, wit_b, sp, dsp_dlam)

    col_slots = lambda g, cols: g.reshape(g.shape[0], N_DEV, cols).transpose(1, 0, 2).astype(BF16)
    row_slots = lambda g: g.reshape(N_DEV, g.shape[0] // N_DEV, g.shape[1]).astype(BF16)
    early = [("w_ffn_gate", col_slots(d_w_g, 352)), ("w_ffn_up", col_slots(d_w_u, 352)), ("w_ffn_down", row_slots(d_w_dn)),
             ("w_proj_rnn", row_slots(d_w_pr)), ("w_proj_attn", row_slots(d_w_pa)), ("w_out", row_slots(d_w_o))]
    g_small = jnp.concatenate([
        _rows(vec[4], 2), d_wa.reshape(160, D), _rows(vec[5], 2), d_wi.reshape(160, D), _rows(vec[6], 2), _rows(vec[7], 2),
        acc2[2:4], acc5[1:3],
        _rows(vec[0:4], 5), acc3[0:2], acc5[0:1], jnp.zeros((4, D), F32)], axis=0)
    dq, dk, dv, recvd = _attention_bwd(qkv, du_attn, u_attn, lse,
                                       _Exchange([(g, True) for _, g in early] + [(g_small, False)]))
    recv = {name: r for (name, _), r in zip(early, recvd)}
    recv_small = recvd[-1]

    d_w_in = jnp.concatenate([_mm_tn("mmt_in_rnn", h1, drnn, 2048, DR), _mm_tn("mmt_in_q", h1, dq, 2048, D),
                              _mm_tn("mmt_in_k", h1, dk, 2048, D), _mm_tn("mmt_in_v", h1, dv, 2048, D),
                              _mm_tn("mmt_in_gates", h1, dgates, 2048, D)], axis=1)
    g_in = col_slots(d_w_in, 960)
    dh1, (recv_in,) = _mm("mm_d_in", [(drnn, w_rnn, True), (dq, w_qkv[:, :D], True), (dk, w_qkv[:, D:2 * D], True),
                                      (dv, w_qkv[:, 2 * D:], True), (dgates, w_gates, True)], F32, 512, 512,
                          carry=_Exchange([(g_in, True)]))
    recv["w_in"] = recv_in

    def b5(ri, fi, ro, ao):
        md = fi[0]
        n0, rstd0 = _ln(ri[0][...])
        dh = ri[1][...]
        ro[0][...] = ri[2][...] + _ln_bwd(dh * (1.0 + md[1:2, :]), n0, rstd0)
        ao[0][0:1, :] += _rowsum(dh * n0)
        ao[0][1:2, :] += _rowsum(dh)

    grad_x, acc1 = _rowwise(b5, "d_ln_mod1", s_len, tm, [(xs, D, 0), (dh1, D, 0), (dxp, D, 0)], [mod8], [(D, F32)],
                            [((8, D), F32)])

    dmod = jnp.concatenate([acc1[1:2], acc1[0:1], acc2[4:5], acc2[1:2], acc2[0:1], acc5[3:4], jnp.zeros((2, D), F32)],
                           axis=0)
    (recv_dmod,) = _exchange_alone("gather_dmod", _Exchange([(dmod, False)]))
    dmod_all = recv_dmod[:, 0:6, :].reshape(N_DEV, 6 * D)
    red = _sum_slots(recv_small)
    loss = jnp.sum(red[339])

    wmv = (dict(w_in=w_in, w_ffn_gate=w_ffn_gate, w_ffn_up=w_ffn_up, w_proj_rnn=w_proj_rnn, w_proj_attn=w_proj_attn,
                w_out=w_out, w_ffn_down=w_ffn_down),
           dict(w_in=m_w_in, w_ffn_gate=m_w_ffn_gate, w_ffn_up=m_w_ffn_up, w_proj_rnn=m_w_proj_rnn,
                w_proj_attn=m_w_proj_attn, w_out=m_w_out, w_ffn_down=m_w_ffn_down),
           dict(w_in=v_w_in, w_ffn_gate=v_w_ffn_gate, w_ffn_up=v_w_ffn_up, w_proj_rnn=v_w_proj_rnn,
                w_proj_attn=v_w_proj_attn, w_out=v_w_out, w_ffn_down=v_w_ffn_down))
    adam_tile = dict(w_in=128, w_ffn_gate=256, w_ffn_up=256, w_ffn_down=176, w_proj_rnn=160, w_proj_attn=128, w_out=128)
    big_out = {name: _adam_packed("adam_" + name, recv[name], wmv[0][name][0], wmv[1][name][0], wmv[2][name][0], tile)
               for name, tile in adam_tile.items()}
    ada_out = _adam_w_ada(cond_all.T, lax.dynamic_slice(dmod_all, (0, me * 768), (N_DEV, 768)),
                          w_ada[0], m_w_ada[0], v_w_ada[0])

    small_spec = (("b_ada", 6), ("conv_b", 2), ("lru_wa", 160), ("lru_ba", 2), ("lru_wi", 160), ("lru_bi", 2),
                  ("lru_lambda", 2), ("ln1_g", 1), ("ln1_b", 1), ("ln2_g", 1), ("ln2_b", 1), ("conv_w", 1), ("b_gate", 1))
    small_rows = 344

    def pack_small(ps):
        return jnp.concatenate([_rows(ps[name], r) for name, r in small_spec] + [jnp.zeros((4, D), F32)], axis=0)

    names = [n for n, _ in small_spec]
    smalls = [pack_small(dict(zip(names, t))) for t in (
        (b_ada, conv_b, lru_wa, lru_ba, lru_wi, lru_bi, lru_lambda, ln1_g, ln1_b, ln2_g, ln2_b, conv_w, b_gate),
        (m_b_ada, m_conv_b, m_lru_wa, m_lru_ba, m_lru_wi, m_lru_bi, m_lru_lambda, m_ln1_g, m_ln1_b, m_ln2_g, m_ln2_b,
         m_conv_w, m_b_gate),
        (v_b_ada, v_conv_b, v_lru_wa, v_lru_ba, v_lru_wi, v_lru_bi, v_lru_lambda, v_ln1_g, v_ln1_b, v_ln2_g, v_ln2_b,
         v_conv_w, v_b_gate))]
    g_conv_w = lax.dynamic_slice(red[332:337].reshape(-1)[:CONV_W * DR].reshape(CONV_W, DR), (0, me * 160), (CONV_W, 160))
    g_b_gate = lax.dynamic_slice(red[337:339], (0, me * 128), (2, 128))
    g_small_pack = jnp.concatenate([jnp.sum(dmod_all, axis=0).reshape(6, D), red[0:332], _rows(g_conv_w, 1),
                                    _rows(g_b_gate, 1), jnp.zeros((4, D), F32)], axis=0)
    assert g_small_pack.shape == (small_rows, D) and smalls[0].shape == (small_rows, D)
    small_out = _adam_packed("adam_small", g_small_pack, smalls[0], smalls[1], smalls[2], small_rows)

    shapes = dict(w_ada=w_ada.shape, b_ada=b_ada.shape, w_in=w_in.shape, conv_w=conv_w.shape, conv_b=conv_b.shape,
                  lru_wa=lru_wa.shape, lru_ba=lru_ba.shape, lru_wi=lru_wi.shape, lru_bi=lru_bi.shape,
                  lru_lambda=lru_lambda.shape, w_proj_rnn=w_proj_rnn.shape, w_proj_attn=w_proj_attn.shape,
                  b_gate=b_gate.shape, w_out=w_out.shape, ln1_g=ln1_g.shape, ln1_b=ln1_b.shape,
                  w_ffn_gate=w_ffn_gate.shape, w_ffn_up=w_ffn_up.shape, w_ffn_down=w_ffn_down.shape,
                  ln2_g=ln2_g.shape, ln2_b=ln2_b.shape)

    def unpack(kind):
        out = {"w_ada": ada_out[kind].reshape(shapes["w_ada"])}
        for name, _ in PACK_ROWS:
            out[name] = big_out[name][kind].reshape(shapes[name])
        o = 0
        for name, r in small_spec:
            size = 1
            for dim in shapes[name]:
                size *= dim
            out[name] = small_out[kind][o:o + r].reshape(-1)[:size].reshape(shapes[name])
            o += r
        return [out[name] for name in shapes]

    return (loss, grad_x[None], *unpack(0), *unpack(1), *unpack(2), *unpack(3))
```

```python
import functools

import jax
import jax.numpy as jnp
from jax import lax
from jax.experimental import pallas as pl
from jax.experimental.pallas import tpu as pltpu

F32 = jnp.float32
BF16 = jnp.bfloat16
MESH = pl.DeviceIdType.MESH

N_DEV = 8
D = 1024
DR = 1280
N_BLK = 10
BW = 128
HEAD_PAIR = 128
N_PAIR = 8
QB = 128
DFF = 2816
CONV_W = 4
DILATIONS = (1, 4, 16)
ALPHA = 2.0 ** 0.25
LN_EPS = 1e-5
LRU_C = 8.0
SCALE = 0.125
NEG = -1e30
ADAM_LR, ADAM_B1, ADAM_B2, ADAM_EPS, ADAM_WD, ADAM_STEP = 0.001, 0.9, 0.999, 1e-08, 0.01, 10

VMEM_LIMIT = 48 * 1024 * 1024
ROW_TILE = 256
RNN_CHUNK = 256

PACK_ROWS = (("w_in", 960), ("w_ffn_gate", 352), ("w_ffn_up", 352), ("w_proj_rnn", 160),
             ("w_proj_attn", 128), ("w_out", 128), ("w_ffn_down", 352))
LATE_ROWS = PACK_ROWS[1:]


def _cparams(sem):
    return pltpu.CompilerParams(dimension_semantics=sem, vmem_limit_bytes=VMEM_LIMIT)


def _ln(z):
    mu = jnp.mean(z, axis=-1, keepdims=True)
    zc = z - mu
    var = jnp.mean(zc * zc, axis=-1, keepdims=True)
    rstd = lax.rsqrt(var + LN_EPS)
    return zc * rstd, rstd


def _ln_bwd(dn, n, rstd):
    return rstd * (dn - jnp.mean(dn, axis=-1, keepdims=True) - n * jnp.mean(dn * n, axis=-1, keepdims=True))


def _sigmoid(x):
    return 0.5 * jnp.tanh(0.5 * x) + 0.5


_GELU_K = 0.7978845608028654
_GELU_C = 0.044715


def _gelu(y):
    t = jnp.tanh(_GELU_K * (y + _GELU_C * y * y * y))
    return 0.5 * y * (1.0 + t), t


def _gelu_grad(y, t):
    return 0.5 * (1.0 + t) + 0.5 * y * (1.0 - t * t) * _GELU_K * (1.0 + 3.0 * _GELU_C * y * y)


def _rowsum(v):
    return jnp.sum(v, axis=0, keepdims=True)


def _rowwise(body, name, n_rows, tm, row_ins, full_ins, row_outs, acc_outs, carry=None):
    nri, nfi, nro, nao = len(row_ins), len(full_ins), len(row_outs), len(acc_outs)
    n_c = len(carry.arrays) if carry else 0
    n_in = nri + nfi + n_c

    def kern(*refs):
        ri, fi = refs[:nri], refs[nri:nri + nfi]
        ro, ao = refs[n_in:n_in + nro], refs[n_in + nro:n_in + nro + nao]
        if carry:
            i = pl.program_id(0)
            _carry(carry, refs[nri + nfi:n_in], refs[n_in + nro + nao:n_in + nro + nao + n_c], refs[-1], i == 0,
                   i == n_rows // tm - 1)
        if ao:
            @pl.when(pl.program_id(0) == 0)
            def _():
                for a in ao:
                    a[...] = jnp.zeros(a.shape, a.dtype)
        body(ri, fi, ro, ao)

    in_specs = [pl.BlockSpec((tm, w), functools.partial(lambda i, cb: (i, cb), cb=cb)) for _, w, cb in row_ins]
    in_specs += [pl.BlockSpec(a.shape, lambda i: (0, 0)) for a in full_ins]
    out_specs = [pl.BlockSpec((tm, w), lambda i: (i, 0)) for w, _ in row_outs]
    out_specs += [pl.BlockSpec(s, lambda i: (0, 0)) for s, _ in acc_outs]
    out_shape = [jax.ShapeDtypeStruct((n_rows, w), dt) for w, dt in row_outs]
    out_shape += [jax.ShapeDtypeStruct(s, dt) for s, dt in acc_outs]
    operands = [a for a, _, _ in row_ins] + list(full_ins)
    scratch = []
    if carry:
        in_specs, out_specs, out_shape = in_specs + carry.in_specs, out_specs + carry.out_specs, out_shape + carry.out_shape
        operands, scratch = operands + carry.arrays, carry.scratch
    return pl.pallas_call(
        kern, name=name, grid=(n_rows // tm,), in_specs=in_specs, out_specs=out_specs, out_shape=out_shape,
        scratch_shapes=scratch, compiler_params=_cparams(("arbitrary",)),
    )(*operands)


NT_DIMS = (((1,), (1,)), ((), ()))


def _pair_cols(pair):
    return pair[1].shape[0] if len(pair) == 3 else pair[1].shape[1]


def _pair_specs(pairs, tm, tn):
    in_specs, flat = [], []
    for pair in pairs:
        a, w = pair[0], pair[1]
        k = a.shape[1]
        in_specs.append(pl.BlockSpec((tm, k), lambda j, i: (i, 0)))
        if len(pair) == 3:
            in_specs.append(pl.BlockSpec((tn, k), lambda j, i: (j, 0)))
        else:
            in_specs.append(pl.BlockSpec((k, tn), lambda j, i: (0, j)))
        flat += [a, w]
    return in_specs, flat


def _pair_dot(pair, a_ref, w_ref):
    if len(pair) == 3:
        return lax.dot_general(a_ref[...], w_ref[...], NT_DIMS, preferred_element_type=F32)
    return jnp.dot(a_ref[...], w_ref[...], preferred_element_type=F32)


def _mm(name, pairs, out_dtype, tm, tn, carry=None):
    m_rows, n_cols = pairs[0][0].shape[0], _pair_cols(pairs[0])
    n_pairs = len(pairs)
    n_c = len(carry.arrays) if carry else 0
    grid = (n_cols // tn, m_rows // tm)

    def kern(*refs):
        o_ref = refs[2 * n_pairs + n_c]
        if carry:
            step = pl.program_id(0) * grid[1] + pl.program_id(1)
            _carry(carry, refs[2 * n_pairs:2 * n_pairs + n_c], refs[2 * n_pairs + n_c + 1:2 * n_pairs + 2 * n_c + 1],
                   refs[-1], step == 0, step == grid[0] * grid[1] - 1)
        acc = None
        for p in range(n_pairs):
            t = _pair_dot(pairs[p], refs[2 * p], refs[2 * p + 1])
            acc = t if acc is None else acc + t
        o_ref[...] = acc.astype(o_ref.dtype)

    in_specs, flat = _pair_specs(pairs, tm, tn)
    out_specs = [pl.BlockSpec((tm, tn), lambda j, i: (i, j))]
    out_shape = [jax.ShapeDtypeStruct((m_rows, n_cols), out_dtype)]
    if carry:
        res = pl.pallas_call(
            kern, name=name, grid=grid, in_specs=in_specs + carry.in_specs, out_specs=out_specs + carry.out_specs,
            out_shape=out_shape + carry.out_shape, scratch_shapes=carry.scratch,
            compiler_params=_cparams(("arbitrary", "arbitrary")),
        )(*flat, *carry.arrays)
        return res[0], res[1:]
    return pl.pallas_call(
        kern, name=name, grid=grid, in_specs=in_specs, out_specs=out_specs[0], out_shape=out_shape[0],
        compiler_params=_cparams(("parallel", "parallel")),
    )(*flat)


def _mm_tn(name, a, g, tm, tn):
    m_rows, k = a.shape
    n_cols = g.shape[1]

    def kern(a_ref, g_ref, o_ref):
        @pl.when(pl.program_id(1) == 0)
        def _():
            o_ref[...] = jnp.zeros(o_ref.shape, F32)
        o_ref[...] += lax.dot_general(a_ref[...], g_ref[...], (((0,), (0,)), ((), ())), preferred_element_type=F32)

    return pl.pallas_call(
        kern, name=name, grid=(n_cols // tn, m_rows // tm),
        in_specs=[pl.BlockSpec((tm, k), lambda j, m: (m, 0)), pl.BlockSpec((tm, tn), lambda j, m: (m, j))],
        out_specs=pl.BlockSpec((k, tn), lambda j, m: (0, j)),
        out_shape=jax.ShapeDtypeStruct((k, n_cols), F32),
        compiler_params=_cparams(("parallel", "arbitrary")),
    )(a, g)


def _mm_fused(name, pairs, tile_ins, full_ins, outs, accs, epilogue, tm, tn, sub=None):
    m_rows, n_cols = pairs[0][0].shape[0], _pair_cols(pairs[0])
    n_p, n_t, n_f, n_o = len(pairs), len(tile_ins), len(full_ins), len(outs)
    sub = sub or tm

    def kern(*refs):
        base = 2 * n_p
        t_refs, f_refs = refs[base:base + n_t], refs[base + n_t:base + n_t + n_f]
        o_refs, a_refs = refs[base + n_t + n_f:base + n_t + n_f + n_o], refs[base + n_t + n_f + n_o:]
        if a_refs:
            @pl.when((pl.program_id(0) == 0) & (pl.program_id(1) == 0))
            def _():
                for a in a_refs:
                    a[...] = jnp.zeros(a.shape, F32)
        for h in range(tm // sub):
            rows = pl.ds(h * sub, sub)
            dots = [_pair_dot(pairs[p], refs[2 * p].at[rows, :], refs[2 * p + 1]) for p in range(n_p)]
            epilogue(dots, [t.at[rows, :] for t in t_refs], f_refs, [o.at[rows, :] for o in o_refs], a_refs)

    in_specs, flat = _pair_specs(pairs, tm, tn)
    for arr, cb in tile_ins:
        in_specs.append(pl.BlockSpec((tm, tn), functools.partial(lambda j, i, cb: (i, cb + j), cb=cb)))
        flat.append(arr)
    for arr in full_ins:
        in_specs.append(pl.BlockSpec(arr.shape, lambda j, i: (0, 0)))
        flat.append(arr)
    out_specs = [pl.BlockSpec((tm, f * tn), lambda j, i: (i, j)) for _, f in outs]
    out_specs += [pl.BlockSpec(s, lambda j, i: (0, 0)) for s in accs]
    out_shape = [jax.ShapeDtypeStruct((m_rows, f * n_cols), dt) for dt, f in outs]
    out_shape += [jax.ShapeDtypeStruct(s, F32) for s in accs]
    sem = ("arbitrary", "arbitrary") if accs else ("parallel", "parallel")
    return pl.pallas_call(
        kern, name=name, grid=(n_cols // tn, m_rows // tm), in_specs=in_specs, out_specs=out_specs,
        out_shape=out_shape, compiler_params=_cparams(sem),
    )(*flat)


def _peers():
    x, y, c = lax.axis_index("x"), lax.axis_index("y"), lax.axis_index("c")
    me = 4 * x + 2 * y + c
    peers = []
    for k in range(1, N_DEV):
        px = 1 - x if (k >> 2) & 1 else x
        py = 1 - y if (k >> 1) & 1 else y
        pc = 1 - c if k & 1 else c
        peers.append(((px, py, pc), 4 * px + 2 * py + pc))
    return me, peers


def _exchange(me, peers, items, send_sems, recv_sems, t0=0):
    started = []
    for t, (src_of, dst_of) in enumerate(items, start=t0):
        for k, (pid, plin) in enumerate(peers):
            cp = pltpu.make_async_remote_copy(
                src_ref=src_of(plin), dst_ref=dst_of(me), send_sem=send_sems.at[t, k], recv_sem=recv_sems.at[t, k],
                device_id=pid, device_id_type=MESH)
            cp.start()
            started.append(cp)
    for t, (src_of, dst_of) in enumerate(items, start=t0):
        for k, (pid, plin) in enumerate(peers):
            pltpu.make_async_remote_copy(
                src_ref=src_of(plin), dst_ref=dst_of(plin), send_sem=send_sems.at[t, k], recv_sem=recv_sems.at[t, k],
                device_id=pid, device_id_type=MESH).wait_recv()
    for cp in started:
        cp.wait_send()


def _hbm_spec():
    return pl.BlockSpec(memory_space=pltpu.HBM)


class _ChipGather:
    def __init__(self, shard):
        self.arrays = [shard]
        self.out_shape = [jax.ShapeDtypeStruct((N_DEV,) + shard.shape, shard.dtype)]
        self.in_specs, self.out_specs = [_hbm_spec()], [_hbm_spec()]
        self.scratch = [pltpu.SemaphoreType.DMA((SEMS_PER_ITEM,))]

    def _parts(self, ins, outs, sems):
        src, out = ins[0], outs[0]
        x, y, c = lax.axis_index("x"), lax.axis_index("y"), lax.axis_index("c")
        lin = lambda px, py, pc: 4 * px + 2 * py + pc
        sibling = (x, y, 1 - c)
        chips = [(1 - x, y), (x, 1 - y), (1 - x, 1 - y)]

        def copy(k, slot, to, from_src):
            return pltpu.make_async_remote_copy(
                src_ref=src if from_src else out.at[slot], dst_ref=out.at[slot], send_sem=sems.at[k],
                recv_sem=sems.at[N_DEV - 1 + k], device_id=to, device_id_type=MESH)

        own = pltpu.make_async_copy(src, out.at[lin(x, y, c)], sems.at[2 * (N_DEV - 1)])
        first = [copy(0, lin(x, y, c), sibling, True)]
        first += [copy(1 + j, lin(x, y, c), (px, py, c), True) for j, (px, py) in enumerate(chips)]
        passed = [copy(4 + j, lin(px, py, c), sibling, False) for j, (px, py) in enumerate(chips)]
        landed = [copy(1 + j, lin(px, py, c), sibling, True) for j, (px, py) in enumerate(chips)]
        from_sibling = [copy(0, lin(x, y, 1 - c), sibling, True)]
        from_sibling += [copy(4 + j, lin(px, py, 1 - c), sibling, False) for j, (px, py) in enumerate(chips)]
        return own, first, passed, landed, from_sibling

    def start(self, ins, outs, sems):
        own, first, _, _, _ = self._parts(ins, outs, sems)
        own.start()
        for cp in first:
            cp.start()

    def wait(self, ins, outs, sems):
        own, first, passed, landed, from_sibling = self._parts(ins, outs, sems)
        for arrival, forward in zip(landed, passed):
            arrival.wait_recv()
            forward.start()
        for cp in from_sibling:
            cp.wait_recv()
        for cp in first + passed:
            cp.wait_send()
        own.wait()


SEMS_PER_ITEM = 2 * (N_DEV - 1) + 1


class _Exchange:
    def __init__(self, items):
        self.arrays = [a for a, _ in items]
        self.scatter = [s for _, s in items]
        self.out_shape = [jax.ShapeDtypeStruct(a.shape if s else (N_DEV,) + a.shape, a.dtype) for a, s in items]
        self.in_specs = [_hbm_spec() for _ in items]
        self.out_specs = [_hbm_spec() for _ in items]
        self.scratch = [pltpu.SemaphoreType.DMA((SEMS_PER_ITEM * len(items),))]

    def _copies(self, ins, outs, sems, landing):
        me, peers = _peers()
        cps = []
        for t, scatter in enumerate(self.scatter):
            base = t * SEMS_PER_ITEM
            for k, (pid, plin) in enumerate(peers):
                src = ins[t].at[plin] if scatter else ins[t]
                dst = outs[t].at[plin if landing else me]
                cps.append(pltpu.make_async_remote_copy(
                    src_ref=src, dst_ref=dst, send_sem=sems.at[base + k], recv_sem=sems.at[base + N_DEV - 1 + k],
                    device_id=pid, device_id_type=MESH))
        return cps

    def _own(self, ins, outs, sems):
        me, _ = _peers()
        return [pltpu.make_async_copy(ins[t].at[me] if scatter else ins[t], outs[t].at[me],
                                      sems.at[t * SEMS_PER_ITEM + 2 * (N_DEV - 1)])
                for t, scatter in enumerate(self.scatter)]

    def start(self, ins, outs, sems):
        for cp in self._own(ins, outs, sems) + self._copies(ins, outs, sems, False):
            cp.start()

    def wait(self, ins, outs, sems):
        for cp in self._copies(ins, outs, sems, True):
            cp.wait_recv()
        for cp in self._copies(ins, outs, sems, False):
            cp.wait_send()
        for cp in self._own(ins, outs, sems):
            cp.wait()


def _carry(exchange, ins, outs, sems, first, last):
    @pl.when(first)
    def _():
        exchange.start(ins, outs, sems)

    @pl.when(last)
    def _():
        exchange.wait(ins, outs, sems)


def _exchange_alone(name, exchange):
    n = len(exchange.arrays)

    def body(*refs):
        exchange.start(refs[:n], refs[n:2 * n], refs[2 * n])
        exchange.wait(refs[:n], refs[n:2 * n], refs[2 * n])

    return pl.pallas_call(body, name=name, in_specs=exchange.in_specs, out_specs=exchange.out_specs,
                          out_shape=exchange.out_shape, scratch_shapes=exchange.scratch)(*exchange.arrays)


def _ada_modulation(c8, w_ada, b_ada_cols):
    wcols = w_ada.shape[1]

    def body(c_ref, w_ref, b_ref, call_ref, cond_ref, mod_ref, res, send_sems, recv_sems):
        me, peers = _peers()
        call_ref[me] = c_ref[...]
        _exchange(me, peers, [(lambda p: c_ref, lambda s: call_ref.at[s])], send_sems, recv_sems, t0=0)
        for dev in range(N_DEV):
            cv = call_ref[dev]
            cond = cv * _sigmoid(cv)
            cond_ref[dev] = cond
            res[dev] = jnp.dot(cond, w_ref[...], preferred_element_type=F32,
                               precision=lax.Precision.HIGHEST) + b_ref[...]
        mod_ref[me] = res[me]
        _exchange(me, peers, [(lambda p: res.at[p], lambda s: mod_ref.at[s])], send_sems, recv_sems, t0=1)

    vm = pl.BlockSpec(memory_space=pltpu.VMEM)
    return pl.pallas_call(
        body, name="ada_modulation", in_specs=[vm, vm, vm], out_specs=[vm, vm, vm],
        out_shape=[jax.ShapeDtypeStruct((N_DEV, 8, D), F32), jax.ShapeDtypeStruct((N_DEV, 8, D), F32),
                   jax.ShapeDtypeStruct((N_DEV, 8, wcols), F32)],
        scratch_shapes=[pltpu.VMEM((N_DEV, 8, wcols), F32),
                        pltpu.SemaphoreType.DMA((2, N_DEV - 1)), pltpu.SemaphoreType.DMA((2, N_DEV - 1))],
        compiler_params=pltpu.CompilerParams(vmem_limit_bytes=VMEM_LIMIT),
    )(c8, w_ada, b_ada_cols)


def _lru_gates(xc, wa_ref, ba_ref, wi_ref, bi_ref, sp_ref, row0):
    xcb = xc.astype(BF16)
    pre_r, pre_i = [], []
    for n in range(N_BLK):
        xb = xcb[:, n * BW:(n + 1) * BW]
        pre_r.append(jnp.dot(xb, wa_ref[n], preferred_element_type=F32))
        pre_i.append(jnp.dot(xb, wi_ref[n], preferred_element_type=F32))
    r = _sigmoid(jnp.concatenate(pre_r, axis=1) + ba_ref[...])
    ig = _sigmoid(jnp.concatenate(pre_i, axis=1) + bi_ref[...])
    log_a = (-LRU_C) * r * sp_ref[...]
    a = jnp.exp(log_a)
    e2 = jnp.exp(2.0 * log_a)
    mult_raw = jnp.sqrt(1.0 - e2)
    rows = row0 + lax.broadcasted_iota(jnp.int32, xc.shape, 0)
    start = rows == 0
    mult = jnp.where(start, 1.0, mult_raw)
    return xcb, r, ig, a, e2, mult_raw, mult, start


def _conv(xpad, cw_ref, cb_ref, tc):
    out = cb_ref[...]
    for tap in range(CONV_W):
        out = out + cw_ref[tap:tap + 1, :] * xpad[pl.ds(8 - (CONV_W - 1) + tap, tc), :]
    return out


def _rnn_fwd(proj_rnn, cw, cb, wa, ba, wi, bi, sp, carry):
    s_len = proj_rnn.shape[0]
    tc = RNN_CHUNK
    n_c = len(carry.arrays)
    n_out = 5

    def kern(*refs):
        y_ref, xr_ref, cw_ref, cb_ref, wa_ref, ba_ref, wi_ref, bi_ref, sp_ref = refs[:9]
        hs_ref, ur_ref, xc_ref, r_ref, ig_ref = refs[9 + n_c:9 + n_c + n_out]
        xpad, a_scr, u_scr, h_scr = refs[9 + n_out + 2 * n_c:13 + n_out + 2 * n_c]
        i = pl.program_id(0)
        _carry(carry, refs[9:9 + n_c], refs[9 + n_out + n_c:9 + n_out + 2 * n_c], refs[-1], i == 0,
               i == s_len // tc - 1)

        @pl.when(i == 0)
        def _():
            xpad[0:8, :] = jnp.zeros((8, DR), F32)
            h_scr[...] = jnp.zeros((1, DR), F32)

        xpad[8:8 + tc, :] = xr_ref[...].astype(F32)
        xc = _conv(xpad, cw_ref, cb_ref, tc)
        xpad[0:8, :] = xpad[tc:tc + 8, :]
        xcb, r, ig, a, e2, mult_raw, mult, start = _lru_gates(xc, wa_ref, ba_ref, wi_ref, bi_ref, sp_ref, i * tc)
        a_scr[...] = a
        u_scr[...] = mult * (ig * xc)
        xc_ref[...] = xcb
        r_ref[...] = r.astype(BF16)
        ig_ref[...] = ig.astype(BF16)

        def step(t, h):
            h = a_scr[pl.ds(t, 1), :] * h + u_scr[pl.ds(t, 1), :]
            hs_ref[pl.ds(t, 1), :] = h
            return h

        h_scr[...] = lax.fori_loop(0, tc, step, h_scr[...], unroll=8)
        gy, _ = _gelu(y_ref[...].astype(F32))
        ur_ref[...] = (gy * hs_ref[...]).astype(BF16)

    full = lambda a: pl.BlockSpec(a.shape, lambda i: (0,) * a.ndim)
    res = pl.pallas_call(
        kern, name="rnn_fwd", grid=(s_len // tc,),
        in_specs=[pl.BlockSpec((tc, DR), lambda i: (i, 0)), pl.BlockSpec((tc, DR), lambda i: (i, 1)),
                  full(cw), full(cb), full(wa), full(ba), full(wi), full(bi), full(sp)] + carry.in_specs,
        out_specs=[pl.BlockSpec((tc, DR), lambda i: (i, 0))] * n_out + carry.out_specs,
        out_shape=[jax.ShapeDtypeStruct((s_len, DR), F32)] + [jax.ShapeDtypeStruct((s_len, DR), BF16)] * (n_out - 1)
        + carry.out_shape,
        scratch_shapes=[pltpu.VMEM((tc + 8, DR), F32), pltpu.VMEM((tc, DR), F32), pltpu.VMEM((tc, DR), F32),
                        pltpu.VMEM((1, DR), F32)] + carry.scratch,
        compiler_params=_cparams(("arbitrary",)),
    )(proj_rnn, proj_rnn, cw, cb, wa, ba, wi, bi, sp, *carry.arrays)
    return res[0], res[1], res[2:n_out], res[n_out:]


def _rnn_bwd(proj_rnn, hs, du, xc_b, r_b, ig_b, cw, wat, wit, sp, dsp_dlam):
    s_len = proj_rnn.shape[0]
    tc = RNN_CHUNK
    nch = s_len // tc

    def kern(y_ref, xr_ref, xrp_ref, hs_ref, hsp_ref, du_ref, xc_ref, r_ref, ig_ref, cw_ref, wat_ref, wit_ref, sp_ref,
             dspl_ref, drnn_ref, dwa_ref, dwi_ref, vec_ref, xpad, hpad, a_scr, d_scr, g_scr, dxcpad, ag_scr):
        j = pl.program_id(0)
        n = nch - 1 - j

        @pl.when(j == 0)
        def _():
            dwa_ref[...] = jnp.zeros(dwa_ref.shape, F32)
            dwi_ref[...] = jnp.zeros(dwi_ref.shape, F32)
            vec_ref[...] = jnp.zeros(vec_ref.shape, F32)
            ag_scr[...] = jnp.zeros((1, DR), F32)
            dxcpad[tc:tc + 8, :] = jnp.zeros((8, DR), F32)

        has_prev = n > 0
        xpad[0:8, :] = jnp.where(has_prev, xrp_ref[8:16, :].astype(F32), 0.0)
        xpad[8:8 + tc, :] = xr_ref[...].astype(F32)
        hpad[0:8, :] = jnp.where(has_prev, hsp_ref[...], 0.0)
        hpad[8:8 + tc, :] = hs_ref[...]
        xcb = xc_ref[...]
        xc, r, ig = xcb.astype(F32), r_ref[...].astype(F32), ig_ref[...].astype(F32)
        log_a = (-LRU_C) * r * sp_ref[...]
        a = jnp.exp(log_a)
        e2 = jnp.exp(2.0 * log_a)
        mult_raw = jnp.sqrt(1.0 - e2)
        start = (n * tc + lax.broadcasted_iota(jnp.int32, (tc, DR), 0)) == 0
        mult = jnp.where(start, 1.0, mult_raw)

        y = y_ref[...].astype(F32)
        gy, th = _gelu(y)
        duv = du_ref[...].astype(F32)
        drnn_ref[:, 0:DR] = (duv * hs_ref[...] * _gelu_grad(y, th)).astype(BF16)
        a_scr[...] = a
        d_scr[...] = duv * gy

        def step(tt, ag):
            t = tc - 1 - tt
            g = d_scr[pl.ds(t, 1), :] + ag
            g_scr[pl.ds(t, 1), :] = g
            return a_scr[pl.ds(t, 1), :] * g

        ag_scr[...] = lax.fori_loop(0, tc, step, ag_scr[...], unroll=8)
        g = g_scr[...]
        da = g * hpad[pl.ds(7, tc), :]
        gx = g * xc
        dmult = jnp.where(start, 0.0, gx * ig)
        di = gx * mult
        dxc = g * mult * ig
        dlog_a = da * a - jnp.where(start, 0.0, dmult * e2 / mult_raw)
        dr = dlog_a * ((-LRU_C) * sp_ref[...])
        vec_ref[7:8, :] += _rowsum(dlog_a * ((-LRU_C) * r))
        dpr = dr * r * (1.0 - r)
        dpi = di * ig * (1.0 - ig)
        vec_ref[5:6, :] += _rowsum(dpr)
        vec_ref[6:7, :] += _rowsum(dpi)
        dprb, dpib = dpr.astype(BF16), dpi.astype(BF16)
        extra = []
        for b in range(N_BLK):
            sl = slice(b * BW, (b + 1) * BW)
            extra.append(jnp.dot(dprb[:, sl], wat_ref[b], preferred_element_type=F32)
                         + jnp.dot(dpib[:, sl], wit_ref[b], preferred_element_type=F32))
            dn = (((0,), (0,)), ((), ()))
            dwa_ref[b] += lax.dot_general(xcb[:, sl], dprb[:, sl], dn, preferred_element_type=F32)
            dwi_ref[b] += lax.dot_general(xcb[:, sl], dpib[:, sl], dn, preferred_element_type=F32)
        dxc = dxc + jnp.concatenate(extra, axis=1)
        vec_ref[4:5, :] += _rowsum(dxc)
        dxcpad[0:tc, :] = dxc
        dxr = jnp.zeros((tc, DR), F32)
        for tap in range(CONV_W):
            shift = CONV_W - 1 - tap
            dxr = dxr + cw_ref[tap:tap + 1, :] * dxcpad[pl.ds(shift, tc), :]
            vec_ref[tap:tap + 1, :] += _rowsum(dxc * xpad[pl.ds(8 - shift, tc), :])
        dxcpad[tc:tc + 8, :] = dxcpad[0:8, :]
        drnn_ref[:, DR:2 * DR] = dxr.astype(BF16)

        @pl.when(j == nch - 1)
        def _():
            vec_ref[7:8, :] = vec_ref[7:8, :] * dspl_ref[...]

    full = lambda a: pl.BlockSpec(a.shape, lambda j: (0,) * a.ndim)
    rev = lambda j: nch - 1 - j
    prev8 = lambda j: jnp.maximum((nch - 1 - j) * (tc // 8) - 1, 0)
    prev16 = lambda j: jnp.maximum((nch - 1 - j) * (tc // 16) - 1, 0)
    chunk = pl.BlockSpec((tc, DR), lambda j: (rev(j), 0))
    return pl.pallas_call(
        kern, name="rnn_bwd", grid=(nch,),
        in_specs=[chunk, pl.BlockSpec((tc, DR), lambda j: (rev(j), 1)), pl.BlockSpec((16, DR), lambda j: (prev16(j), 1)),
                  chunk, pl.BlockSpec((8, DR), lambda j: (prev8(j), 0)), chunk, chunk, chunk, chunk,
                  full(cw), full(wat), full(wit), full(sp), full(dsp_dlam)],
        out_specs=[pl.BlockSpec((tc, 2 * DR), lambda j: (rev(j), 0)),
                   pl.BlockSpec((N_BLK, BW, BW), lambda j: (0, 0, 0)), pl.BlockSpec((N_BLK, BW, BW), lambda j: (0, 0, 0)),
                   pl.BlockSpec((8, DR), lambda j: (0, 0))],
        out_shape=[jax.ShapeDtypeStruct((s_len, 2 * DR), BF16), jax.ShapeDtypeStruct((N_BLK, BW, BW), F32),
                   jax.ShapeDtypeStruct((N_BLK, BW, BW), F32), jax.ShapeDtypeStruct((8, DR), F32)],
        scratch_shapes=[pltpu.VMEM((tc + 8, DR), F32), pltpu.VMEM((tc + 8, DR), F32), pltpu.VMEM((tc, DR), F32),
                        pltpu.VMEM((tc, DR), F32), pltpu.VMEM((tc, DR), F32), pltpu.VMEM((tc + 8, DR), F32),
                        pltpu.VMEM((1, DR), F32)],
        compiler_params=_cparams(("arbitrary",)),
    )(proj_rnn, proj_rnn, proj_rnn, hs, hs, du, xc_b, r_b, ig_b, cw, wat, wit, sp, dsp_dlam)


ATT_BLK = 2048


def _attn_units():
    return [(g, d, b, b * QB * d + r)
            for g, d in enumerate(DILATIONS) for b in range(ATT_BLK // (QB * d)) for r in range(d)]


def _rows_at(ref, start, n, d):
    return ref[pl.ds(start, n, stride=d), :] if d > 1 else ref[start:start + n, :]


def _add_rows_at(ref, start, n, d, val):
    if d > 1:
        ref[pl.ds(start, n, stride=d), :] = ref[pl.ds(start, n, stride=d), :] + val
    else:
        ref[start:start + n, :] = ref[start:start + n, :] + val


def _stack_heads(x, first):
    zero = jnp.zeros_like(x)
    return jnp.concatenate([jnp.where(first, x, zero), jnp.where(first, zero, x)], axis=0)


def _unstack_heads(x, first):
    return jnp.where(first, x[:QB], x[QB:])


def _head_column(x, lane, half):
    return jnp.sum(jnp.where(lane == 64 * half, x, 0.0), axis=-1, keepdims=True)


def _band_masks():
    row = lax.broadcasted_iota(jnp.int32, (2 * QB, 2 * QB), 0) & (QB - 1)
    col = lax.broadcasted_iota(jnp.int32, (2 * QB, 2 * QB), 1)
    dist = QB + row - col
    lane = lax.broadcasted_iota(jnp.int32, (1, HEAD_PAIR), 1)
    return (dist >= 0) & (dist <= QB), col >= QB, lane


def _attention_specs(nblk, blk_of):
    cur = lambda off: pl.BlockSpec((ATT_BLK, HEAD_PAIR), lambda hp, j: (blk_of(j), off + hp))
    prev = lambda off: pl.BlockSpec((ATT_BLK, HEAD_PAIR), lambda hp, j: (jnp.maximum(blk_of(j) - 1, 0), off + hp))
    return cur, prev


def _attention_fwd(qkv):
    s_len = qkv.shape[0]
    nblk = s_len // ATT_BLK
    units = _attn_units()
    nt = (((1,), (1,)), ((), ()))

    def kern(q_ref, k_ref, v_ref, o_ref, lse_ref, kbuf, vbuf, og, lg):
        blk = pl.program_id(1)

        @pl.when(blk == 0)
        def _():
            kbuf[0:ATT_BLK, :] = jnp.zeros((ATT_BLK, HEAD_PAIR), F32)
            vbuf[0:ATT_BLK, :] = jnp.zeros((ATT_BLK, HEAD_PAIR), F32)

        kbuf[ATT_BLK:2 * ATT_BLK, :] = k_ref[...]
        vbuf[ATT_BLK:2 * ATT_BLK, :] = v_ref[...]
        band, later, lane = _band_masks()
        first = lane < 64
        band0 = band & (later | (blk > 0))
        for g, d, b, start in units:
            kstart = ATT_BLK + start - QB * d
            qs = _stack_heads((_rows_at(q_ref, start, QB, d) * SCALE).astype(BF16), first)
            k2 = _rows_at(kbuf, kstart, 2 * QB, d).astype(BF16)
            v2 = _rows_at(vbuf, kstart, 2 * QB, d).astype(BF16)
            s = lax.dot_general(qs, k2, nt, preferred_element_type=F32)
            s = jnp.where(band if b > 0 else band0, s, NEG)
            mx = jnp.max(s, axis=-1, keepdims=True)
            p = jnp.exp(s - mx)
            den = jnp.sum(p, axis=-1, keepdims=True)
            out = jnp.dot(p.astype(BF16), v2, preferred_element_type=F32) / den
            lse2 = jnp.broadcast_to(mx + jnp.log(den), (2 * QB, HEAD_PAIR))
            if d > 1:
                og[g, pl.ds(start, QB, stride=d), :] = _unstack_heads(out, first)
                lg[g, pl.ds(start, QB, stride=d), :] = _unstack_heads(lse2, first)
            else:
                og[g, start:start + QB, :] = _unstack_heads(out, first)
                lg[g, start:start + QB, :] = _unstack_heads(lse2, first)
        kbuf[0:ATT_BLK, :] = kbuf[ATT_BLK:2 * ATT_BLK, :]
        vbuf[0:ATT_BLK, :] = vbuf[ATT_BLK:2 * ATT_BLK, :]
        mx = jnp.maximum(jnp.maximum(lg[0], lg[1]), lg[2])
        es = [jnp.exp(lg[g] - mx) for g in range(3)]
        tot = es[0] + es[1] + es[2]
        o_ref[...] = ((es[0] * og[0] + es[1] * og[1] + es[2] * og[2]) / tot).astype(BF16)
        lse_ref[...] = mx + jnp.log(tot)

    cur, _ = _attention_specs(nblk, lambda j: j)
    out_spec = pl.BlockSpec((ATT_BLK, HEAD_PAIR), lambda hp, j: (j, hp))
    return pl.pallas_call(
        kern, name="attention_fwd", grid=(N_PAIR, nblk), in_specs=[cur(0), cur(N_PAIR), cur(2 * N_PAIR)],
        out_specs=[out_spec, out_spec],
        out_shape=[jax.ShapeDtypeStruct((s_len, D), BF16), jax.ShapeDtypeStruct((s_len, D), F32)],
        scratch_shapes=[pltpu.VMEM((2 * ATT_BLK, HEAD_PAIR), F32), pltpu.VMEM((2 * ATT_BLK, HEAD_PAIR), F32),
                        pltpu.VMEM((3, ATT_BLK, HEAD_PAIR), F32), pltpu.VMEM((3, ATT_BLK, HEAD_PAIR), F32)],
        compiler_params=_cparams(("arbitrary", "arbitrary")),
    )(qkv, qkv, qkv)


def _attention_bwd(qkv, do, o, lse, carry):
    s_len = qkv.shape[0]
    nblk = s_len // ATT_BLK
    units = _attn_units()
    nt = (((1,), (1,)), ((), ()))
    tn = (((0,), (0,)), ((), ()))
    n_c = len(carry.arrays) if carry else 0
    c_in, c_out = (carry.in_specs, carry.out_specs) if carry else ([], [])
    c_shape, c_scratch, c_arrays = (carry.out_shape, carry.scratch, carry.arrays) if carry else ([], [], [])

    def kern(*refs):
        q_ref, k_ref, v_ref, kp_ref, vp_ref, do_ref, o_ref, lse_ref = refs[:8]
        dq_ref, dk_ref, dv_ref = refs[8 + n_c:11 + n_c]
        kbuf, vbuf, dkbuf, dvbuf, dq_scr, dsum = refs[11 + 2 * n_c:17 + 2 * n_c]
        hp, j = pl.program_id(0), pl.program_id(1)
        blk = nblk - 1 - j
        if carry:
            _carry(carry, refs[8:8 + n_c], refs[11 + n_c:11 + 2 * n_c], refs[-1], (hp == 0) & (j == 0),
                   (hp == N_PAIR - 1) & (j == nblk - 1))
        zeros = jnp.zeros((ATT_BLK, HEAD_PAIR), F32)

        @pl.when(j == 0)
        def _():
            dkbuf[ATT_BLK:2 * ATT_BLK, :] = zeros
            dvbuf[ATT_BLK:2 * ATT_BLK, :] = zeros

        @pl.when(j > 0)
        def _():
            dkbuf[ATT_BLK:2 * ATT_BLK, :] = dkbuf[0:ATT_BLK, :]
            dvbuf[ATT_BLK:2 * ATT_BLK, :] = dvbuf[0:ATT_BLK, :]

        dkbuf[0:ATT_BLK, :] = zeros
        dvbuf[0:ATT_BLK, :] = zeros
        dq_scr[...] = zeros
        kbuf[0:ATT_BLK, :] = kp_ref[...]
        kbuf[ATT_BLK:2 * ATT_BLK, :] = k_ref[...]
        vbuf[0:ATT_BLK, :] = vp_ref[...]
        vbuf[ATT_BLK:2 * ATT_BLK, :] = v_ref[...]
        band, later, lane = _band_masks()
        first = lane < 64
        band0 = band & (later | (blk > 0))
        prod = do_ref[...] * o_ref[...].astype(F32)
        dsum[...] = jnp.where(first, jnp.sum(jnp.where(first, prod, 0.0), axis=-1, keepdims=True),
                              jnp.sum(jnp.where(first, 0.0, prod), axis=-1, keepdims=True))
        for g, d, b, start in units:
            kstart = ATT_BLK + start - QB * d
            qs = _stack_heads((_rows_at(q_ref, start, QB, d) * SCALE).astype(BF16), first)
            dos = _stack_heads(_rows_at(do_ref, start, QB, d).astype(BF16), first)
            k2 = _rows_at(kbuf, kstart, 2 * QB, d).astype(BF16)
            v2 = _rows_at(vbuf, kstart, 2 * QB, d).astype(BF16)
            ds_rows = _rows_at(dsum, start, QB, d)
            lse_rows = _rows_at(lse_ref, start, QB, d)
            dcol = jnp.concatenate([_head_column(ds_rows, lane, 0), _head_column(ds_rows, lane, 1)], axis=0)
            lcol = jnp.concatenate([_head_column(lse_rows, lane, 0), _head_column(lse_rows, lane, 1)], axis=0)
            s = lax.dot_general(qs, k2, nt, preferred_element_type=F32)
            p = jnp.exp(jnp.where(band if b > 0 else band0, s, NEG) - lcol)
            dp = lax.dot_general(dos, v2, nt, preferred_element_type=F32)
            ds = (p * (dp - dcol)).astype(BF16)
            _add_rows_at(dq_scr, start, QB, d,
                         _unstack_heads(jnp.dot(ds, k2, preferred_element_type=F32), first) * SCALE)
            _add_rows_at(dkbuf, kstart, 2 * QB, d, lax.dot_general(ds, qs, tn, preferred_element_type=F32))
            _add_rows_at(dvbuf, kstart, 2 * QB, d, lax.dot_general(p.astype(BF16), dos, tn, preferred_element_type=F32))
        dq_ref[...] = dq_scr[...].astype(BF16)
        dk_ref[...] = dkbuf[ATT_BLK:2 * ATT_BLK, :].astype(BF16)
        dv_ref[...] = dvbuf[ATT_BLK:2 * ATT_BLK, :].astype(BF16)

    rev = lambda j: nblk - 1 - j
    cur, prev = _attention_specs(nblk, rev)
    buf = lambda rows: pltpu.VMEM((rows, HEAD_PAIR), F32)
    res = pl.pallas_call(
        kern, name="attention_bwd", grid=(N_PAIR, nblk),
        in_specs=[cur(0), cur(N_PAIR), cur(2 * N_PAIR), prev(N_PAIR), prev(2 * N_PAIR), cur(0), cur(0), cur(0)] + c_in,
        out_specs=[cur(0)] * 3 + c_out,
        out_shape=[jax.ShapeDtypeStruct((s_len, D), BF16)] * 3 + c_shape,
        scratch_shapes=[buf(2 * ATT_BLK), buf(2 * ATT_BLK), buf(2 * ATT_BLK), buf(2 * ATT_BLK), buf(ATT_BLK),
                        buf(ATT_BLK)] + c_scratch,
        compiler_params=_cparams(("arbitrary", "arbitrary")),
    )(qkv, qkv, qkv, qkv, qkv, do, o, lse, *c_arrays)
    return res[0], res[1], res[2], res[3:]


def _adamw(w, g, m, v):
    m = ADAM_B1 * m + (1.0 - ADAM_B1) * g
    v = ADAM_B2 * v + (1.0 - ADAM_B2) * (g * g)
    m_hat = m / (1.0 - ADAM_B1 ** ADAM_STEP)
    v_hat = v / (1.0 - ADAM_B2 ** ADAM_STEP)
    delta = -ADAM_LR * (m_hat / (jnp.sqrt(v_hat) + ADAM_EPS) + ADAM_WD * w)
    return delta, m, v


def _adam_packed(name, recv, w, m, v, tm):
    n_rows, width = w.shape
    summed = recv.ndim == 3

    def kern(r_ref, w_ref, m_ref, v_ref, g_out, d_out, m_out, v_out):
        if summed:
            g = r_ref[0].astype(F32)
            for s in range(1, N_DEV):
                g = g + r_ref[s].astype(F32)
        else:
            g = r_ref[...]
        delta, mn, vn = _adamw(w_ref[...], g, m_ref[...], v_ref[...])
        g_out[...] = g
        d_out[...] = delta
        m_out[...] = mn
        v_out[...] = vn

    tile = pl.BlockSpec((tm, width), lambda i: (i, 0))
    rspec = pl.BlockSpec((N_DEV, tm, width), lambda i: (0, i, 0)) if summed else tile
    return pl.pallas_call(
        kern, name=name, grid=(n_rows // tm,), in_specs=[rspec, tile, tile, tile], out_specs=[tile] * 4,
        out_shape=[jax.ShapeDtypeStruct((n_rows, width), F32)] * 4, compiler_params=_cparams(("parallel",)),
    )(recv, w, m, v)


def _sum_slots(recv):
    _, n_rows, width = recv.shape

    def kern(r_ref, o_ref):
        g = r_ref[0]
        for s in range(1, N_DEV):
            g = g + r_ref[s]
        o_ref[...] = g

    return pl.pallas_call(
        kern, name="sum_small", grid=(1,), in_specs=[pl.BlockSpec(recv.shape, lambda i: (0, 0, 0))],
        out_specs=pl.BlockSpec((n_rows, width), lambda i: (0, 0)),
        out_shape=jax.ShapeDtypeStruct((n_rows, width), F32), compiler_params=_cparams(("arbitrary",)),
    )(recv)


def _adam_w_ada(cond_t, dmod_cols, w, m, v):
    n_rows, width = w.shape
    tm = ROW_TILE

    def kern(c_ref, d_ref, w_ref, m_ref, v_ref, g_out, d_out, m_out, v_out):
        g = c_ref[:, 0:1] * d_ref[0:1, :]
        for b in range(1, N_DEV):
            g = g + c_ref[:, b:b + 1] * d_ref[b:b + 1, :]
        delta, mn, vn = _adamw(w_ref[...], g, m_ref[...], v_ref[...])
        g_out[...] = g
        d_out[...] = delta
        m_out[...] = mn
        v_out[...] = vn

    tile = pl.BlockSpec((tm, width), lambda i: (i, 0))
    return pl.pallas_call(
        kern, name="adam_w_ada", grid=(n_rows // tm,),
        in_specs=[pl.BlockSpec((tm, N_DEV), lambda i: (i, 0)), pl.BlockSpec((N_DEV, width), lambda i: (0, 0)),
                  tile, tile, tile],
        out_specs=[tile] * 4, out_shape=[jax.ShapeDtypeStruct((n_rows, width), F32)] * 4,
        compiler_params=_cparams(("parallel",)),
    )(cond_t, dmod_cols, w, m, v)


def _rows(a, n_rows=None):
    flat = a.reshape(-1)
    need = (n_rows if n_rows is not None else -(-flat.shape[0] // D)) * D
    if need != flat.shape[0]:
        flat = jnp.concatenate([flat, jnp.zeros((need - flat.shape[0],), flat.dtype)])
    return flat.reshape(-1, D)


def _slots_to_cols(s, k, cols):
    return s.reshape(N_DEV, k, cols).transpose(1, 0, 2).reshape(k, N_DEV * cols)


def kernel(x, c, w_ada, b_ada, w_in, conv_w, conv_b, lru_wa, lru_ba, lru_wi, lru_bi, lru_lambda, w_proj_rnn, w_proj_attn, b_gate, w_out, ln1_g, ln1_b, w_ffn_gate, w_ffn_up, w_ffn_down, ln2_g, ln2_b, loss_target, m_w_ada, m_b_ada, m_w_in, m_conv_w, m_conv_b, m_lru_wa, m_lru_ba, m_lru_wi, m_lru_bi, m_lru_lambda, m_w_proj_rnn, m_w_proj_attn, m_b_gate, m_w_out, m_ln1_g, m_ln1_b, m_w_ffn_gate, m_w_ffn_up, m_w_ffn_down, m_ln2_g, m_ln2_b, v_w_ada, v_b_ada, v_w_in, v_conv_w, v_conv_b, v_lru_wa, v_lru_ba, v_lru_wi, v_lru_bi, v_lru_lambda, v_w_proj_rnn, v_w_proj_attn, v_b_gate, v_w_out, v_ln1_g, v_ln1_b, v_w_ffn_gate, v_w_ffn_up, v_w_ffn_down, v_ln2_g, v_ln2_b):
    s_len = x.shape[1]
    me = 4 * lax.axis_index("x") + 2 * lax.axis_index("y") + lax.axis_index("c")
    xs = x[0]
    tgt = loss_target[0]
    tm = ROW_TILE

    b_ada_cols = lax.dynamic_slice(b_ada, (0, me * 768), (1, 768))
    pad_cols = lambda a: jnp.concatenate([a, jnp.zeros((a.shape[0], D - a.shape[1]), F32)], axis=1)
    c8 = jnp.concatenate([c, pad_cols(conv_w[0]), pad_cols(b_gate[0]), jnp.zeros((1, D), F32)], axis=0)
    c_all, cond_blocks, mod_parts = _ada_modulation(c8, w_ada[0], b_ada_cols)
    cond_all = cond_blocks[:, 0, :]
    mod = mod_parts[:, 0, :].reshape(6, D)
    mod8 = jnp.concatenate([mod, jnp.zeros((2, D), F32)], axis=0)
    cw = c_all[:, 1:1 + CONV_W, :DR // N_DEV].transpose(1, 0, 2).reshape(CONV_W, DR)
    bg = c_all[:, 1 + CONV_W:3 + CONV_W, :D // N_DEV].transpose(1, 0, 2).reshape(2, D)
    bg8 = jnp.concatenate([bg, jnp.zeros((6, D), F32)], axis=0)

    late = dict(w_ffn_gate=w_ffn_gate, w_ffn_up=w_ffn_up, w_proj_rnn=w_proj_rnn, w_proj_attn=w_proj_attn,
                w_out=w_out, w_ffn_down=w_ffn_down)
    late_shard = jnp.concatenate([late[name][0].reshape(r, D) for name, r in LATE_ROWS], axis=0).astype(BF16)

    cb = conv_b
    wa_b, wi_b = lru_wa[0].astype(BF16), lru_wi[0].astype(BF16)
    wat_b, wit_b = jnp.swapaxes(wa_b, 1, 2), jnp.swapaxes(wi_b, 1, 2)
    ba, bi = lru_ba.reshape(1, DR), lru_bi.reshape(1, DR)
    sp = jax.nn.softplus(-lru_lambda)
    dsp_dlam = -jax.nn.sigmoid(-lru_lambda)
    ln1 = jnp.concatenate([ln1_g, ln1_b, jnp.zeros((6, D), F32)], axis=0)
    ln2 = jnp.concatenate([ln2_g, ln2_b, jnp.zeros((6, D), F32)], axis=0)

    def f1(ri, fi, ro, ao):
        n, _ = _ln(ri[0][...])
        ro[0][...] = (n * (1.0 + fi[0][1:2, :]) + fi[0][0:1, :]).astype(BF16)

    h1, w_in_slots = _rowwise(f1, "ln_mod1", s_len, tm, [(xs, D, 0)], [mod8], [(D, BF16)], [],
                              carry=_ChipGather(w_in[0].astype(BF16)))
    w_in_f = w_in_slots.transpose(1, 0, 2).reshape(D, N_DEV * 960)
    w_rnn, w_qkv, w_gates = w_in_f[:, :2 * DR], w_in_f[:, 2 * DR:2 * DR + 3 * D], w_in_f[:, 2 * DR + 3 * D:]
    proj_rnn = _mm("mm_in_rnn", [(h1, w_rnn)], BF16, 512, DR)
    qkv = _mm("mm_in_qkv", [(h1, w_qkv)], F32, 512, D)
    gates = _mm("mm_in_gates", [(h1, w_gates)], BF16, 512, D)

    hs, u_rnn, kept, (late_all,) = _rnn_fwd(proj_rnn, cw, cb, wa_b, ba, wi_b, bi, sp, _Exchange([(late_shard, False)]))
    offs, o = {}, 0
    for name, r in LATE_ROWS:
        offs[name] = (o, o + r)
        o += r
    part = lambda name: late_all[:, offs[name][0]:offs[name][1], :]
    w_g, w_u = _slots_to_cols(part("w_ffn_gate"), D, 352), _slots_to_cols(part("w_ffn_up"), D, 352)
    w_pr = part("w_proj_rnn").reshape(DR, D)
    w_pa = part("w_proj_attn").reshape(D, D)
    w_o = part("w_out").reshape(D, D)
    w_dn = part("w_ffn_down").reshape(DFF, D)

    u_attn, lse = _attention_fwd(qkv)

    def e_merge(dots, ti, fi, ro, ao):
        g_r = _sigmoid(ti[0][...] + fi[0][0:1, :])
        g_a = _sigmoid(ti[1][...] + fi[0][1:2, :])
        ro[0][...] = dots[0].astype(BF16)
        ro[1][...] = dots[1].astype(BF16)
        ro[2][...] = (g_r * dots[0] + g_a * dots[1]).astype(BF16)

    pr, pa, merged = _mm_fused("mm_proj_merge", [(u_rnn, w_pr), (u_attn, w_pa)], [(gates, 0), (gates, 1)], [bg8],
                               [(BF16, 1), (BF16, 1), (BF16, 1)], [], e_merge, 512, D, sub=256)
    def e_post1(dots, ti, fi, ro, ao):
        md, l1 = fi[0], fi[1]
        fv = dots[0]
        n1, _ = _ln(ALPHA * ti[0][...] + md[2:3, :] * fv)
        x1 = n1 * l1[0:1, :] + l1[1:2, :]
        n0, _ = _ln(x1)
        ro[0][...] = fv
        ro[1][...] = x1
        ro[2][...] = (n0 * (1.0 + md[4:5, :]) + md[3:4, :]).astype(BF16)

    f, x1, h2 = _mm_fused("mm_out_post1", [(merged, w_o)], [(xs, 0)], [mod8, ln1], [(F32, 1), (F32, 1), (BF16, 1)], [],
                          e_post1, 512, D, sub=256)

    def e_swiglu(dots, ti, fi, ro, ao):
        gp, up = dots
        ro[0][...] = gp.astype(BF16)
        ro[1][...] = up.astype(BF16)
        ro[2][...] = (gp * _sigmoid(gp) * up).astype(BF16)

    gpre, upre, act = _mm_fused("mm_ffn_in", [(h2, w_g), (h2, w_u)], [], [], [(BF16, 1)] * 3, [], e_swiglu, 512, DFF // 2)
    def e_loss(dots, ti, fi, ro, ao):
        md, l2 = fi[0], fi[1]
        f2v = dots[0]
        n2, rstd = _ln(ALPHA * ti[0][...] + md[5:6, :] * f2v)
        err = n2 * l2[0:1, :] + l2[1:2, :] - ti[1][...]
        dx2 = err * (1.0 / D)
        dz2 = _ln_bwd(dx2 * l2[0:1, :], n2, rstd)
        ro[0][...] = dz2.astype(BF16)
        ro[1][...] = (md[5:6, :] * dz2).astype(BF16)
        acc = ao[0]
        acc[0:1, :] += _rowsum(err * err) * (0.5 / D)
        acc[1:2, :] += _rowsum(dx2 * n2)
        acc[2:3, :] += _rowsum(dx2)
        acc[3:4, :] += _rowsum(dz2 * f2v)

    dz2, df2, acc5 = _mm_fused("mm_ffn_out_loss", [(act, w_dn)], [(x1, 0), (tgt, 0)], [mod8, ln2],
                               [(BF16, 1), (BF16, 1)], [(8, D)], e_loss, 512, D, sub=256)

    d_w_dn = _mm_tn("mmt_ffn_down", act, df2, 1024, 512)

    def e_dswiglu(dots, ti, fi, ro, ao):
        da = dots[0]
        gp, up = ti[0][...].astype(F32), ti[1][...].astype(F32)
        sg = _sigmoid(gp)
        ro[0][...] = (da * up * sg * (1.0 + gp * (1.0 - sg))).astype(BF16)
        ro[1][...] = (da * gp * sg).astype(BF16)

    dgp, dup = _mm_fused("mm_d_ffn_out", [(df2, w_dn, True)], [(gpre, 0), (upre, 0)], [], [(BF16, 1)] * 2, [],
                         e_dswiglu, 512, DFF // 2, sub=256)
    d_w_g = _mm_tn("mmt_ffn_gate", h2, dgp, 2048, DFF // 2)
    d_w_u = _mm_tn("mmt_ffn_up", h2, dup, 2048, DFF // 2)

    def e_dpost1(dots, ti, fi, ro, ao):
        md, l1 = fi[0], fi[1]
        dh2v = dots[0] + dots[1]
        fv = ti[1][...]
        n1, rstd1 = _ln(ALPHA * ti[0][...] + md[2:3, :] * fv)
        n0, rstd0 = _ln(n1 * l1[0:1, :] + l1[1:2, :])
        dx1 = ALPHA * ti[2][...].astype(F32) + _ln_bwd(dh2v * (1.0 + md[4:5, :]), n0, rstd0)
        dz1 = _ln_bwd(dx1 * l1[0:1, :], n1, rstd1)
        ro[0][...] = (ALPHA * dz1).astype(BF16)
        ro[1][...] = (md[2:3, :] * dz1).astype(BF16)
        acc = ao[0]
        acc[0:1, :] += _rowsum(dh2v * n0)
        acc[1:2, :] += _rowsum(dh2v)
        acc[2:3, :] += _rowsum(dx1 * n1)
        acc[3:4, :] += _rowsum(dx1)
        acc[4:5, :] += _rowsum(dz1 * fv)

    dxp, df, acc2 = _mm_fused("mm_d_ffn_in_post1", [(dgp, w_g, True), (dup, w_u, True)], [(xs, 0), (f, 0), (dz2, 0)],
                              [mod8, ln1], [(BF16, 1), (BF16, 1)], [(8, D)], e_dpost1, 256, D)
    d_w_o = _mm_tn("mmt_out", merged, df, 2048, D)

    def e_dmerge(dots, ti, fi, ro, ao):
        dm = dots[0]
        g_r = _sigmoid(ti[0][...] + fi[0][0:1, :])
        g_a = _sigmoid(ti[1][...] + fi[0][1:2, :])
        ro[0][...] = (dm * g_r).astype(BF16)
        ro[1][...] = (dm * g_a).astype(BF16)
        dga = dm * ti[2][...].astype(F32) * g_r * (1.0 - g_r)
        dgb = dm * ti[3][...].astype(F32) * g_a * (1.0 - g_a)
        ro[2][:, 0:D] = dga.astype(BF16)
        ro[2][:, D:2 * D] = dgb.astype(BF16)
        ao[0][0:1, :] += _rowsum(dga)
        ao[0][1:2, :] += _rowsum(dgb)

    dpr, dpa, dgates, acc3 = _mm_fused("mm_d_out", [(df, w_o, True)], [(gates, 0), (gates, 1), (pr, 0), (pa, 0)], [bg8],
                                       [(BF16, 1), (BF16, 1), (BF16, 2)], [(8, D)], e_dmerge, 512, D, sub=256)
    du_rnn = _mm("mm_d_proj_rnn", [(dpr, w_pr, True)], F32, 512, DR)
    du_attn = _mm("mm_d_proj_attn", [(dpa, w_pa, True)], F32, 512, D)
    d_w_pr = _mm_tn("mmt_proj_rnn", u_rnn, dpr, 2048, D)
    d_w_pa = _mm_tn("mmt_proj_attn", u_attn, dpa, 2048, D)

    drnn, d_wa, d_wi, vec = _rnn_bwd(proj_rnn, hs, du_rnn, *kept, cw, wat_b, wit_b, sp, dsp_dlam)

    col_slots = lambda g, cols: g.reshape(g.shape[0], N_DEV, cols).transpose(1, 0, 2).astype(BF16)
    row_slots = lambda g: g.reshape(N_DEV, g.shape[0] // N_DEV, g.shape[1]).astype(BF16)
    early = [("w_ffn_gate", col_slots(d_w_g, 352)), ("w_ffn_up", col_slots(d_w_u, 352)), ("w_ffn_down", row_slots(d_w_dn)),
             ("w_proj_rnn", row_slots(d_w_pr)), ("w_proj_attn", row_slots(d_w_pa)), ("w_out", row_slots(d_w_o))]
    g_small = jnp.concatenate([
        _rows(vec[4], 2), d_wa.reshape(160, D), _rows(vec[5], 2), d_wi.reshape(160, D), _rows(vec[6], 2), _rows(vec[7], 2),
        acc2[2:4], acc5[1:3],
        _rows(vec[0:4], 5), acc3[0:2], acc5[0:1], jnp.zeros((4, D), F32)], axis=0)
    dq, dk, dv, recvd = _attention_bwd(qkv, du_attn, u_attn, lse,
                                       _Exchange([(g, True) for _, g in early] + [(g_small, False)]))
    recv = {name: r for (name, _), r in zip(early, recvd)}
    recv_small = recvd[-1]

    d_w_in = jnp.concatenate([_mm_tn("mmt_in_rnn", h1, drnn, 2048, DR), _mm_tn("mmt_in_q", h1, dq, 2048, D),
                              _mm_tn("mmt_in_k", h1, dk, 2048, D), _mm_tn("mmt_in_v", h1, dv, 2048, D),
                              _mm_tn("mmt_in_gates", h1, dgates, 2048, D)], axis=1)
    g_in = col_slots(d_w_in, 960)
    dh1, (recv_in,) = _mm("mm_d_in", [(drnn, w_rnn, True), (dq, w_qkv[:, :D], True), (dk, w_qkv[:, D:2 * D], True),
                                      (dv, w_qkv[:, 2 * D:], True), (dgates, w_gates, True)], BF16, 512, 512,
                          carry=_Exchange([(g_in, True)]))
    recv["w_in"] = recv_in

    def b5(ri, fi, ro, ao):
        md = fi[0]
        n0, rstd0 = _ln(ri[0][...])
        dh = ri[1][...].astype(F32)
        ro[0][...] = ri[2][...].astype(F32) + _ln_bwd(dh * (1.0 + md[1:2, :]), n0, rstd0)
        ao[0][0:1, :] += _rowsum(dh * n0)
        ao[0][1:2, :] += _rowsum(dh)

    grad_x, acc1 = _rowwise(b5, "d_ln_mod1", s_len, tm, [(xs, D, 0), (dh1, D, 0), (dxp, D, 0)], [mod8], [(D, F32)],
                            [((8, D), F32)])

    dmod = jnp.concatenate([acc1[1:2], acc1[0:1], acc2[4:5], acc2[1:2], acc2[0:1], acc5[3:4], jnp.zeros((2, D), F32)],
                           axis=0)
    (recv_dmod,) = _exchange_alone("gather_dmod", _Exchange([(dmod, False)]))
    dmod_all = recv_dmod[:, 0:6, :].reshape(N_DEV, 6 * D)
    red = _sum_slots(recv_small)
    loss = jnp.sum(red[339])

    wmv = (dict(w_in=w_in, w_ffn_gate=w_ffn_gate, w_ffn_up=w_ffn_up, w_proj_rnn=w_proj_rnn, w_proj_attn=w_proj_attn,
                w_out=w_out, w_ffn_down=w_ffn_down),
           dict(w_in=m_w_in, w_ffn_gate=m_w_ffn_gate, w_ffn_up=m_w_ffn_up, w_proj_rnn=m_w_proj_rnn,
                w_proj_attn=m_w_proj_attn, w_out=m_w_out, w_ffn_down=m_w_ffn_down),
           dict(w_in=v_w_in, w_ffn_gate=v_w_ffn_gate, w_ffn_up=v_w_ffn_up, w_proj_rnn=v_w_proj_rnn,
                w_proj_attn=v_w_proj_attn, w_out=v_w_out, w_ffn_down=v_w_ffn_down))
    adam_tile = dict(w_in=128, w_ffn_gate=256, w_ffn_up=256, w_ffn_down=176, w_proj_rnn=160, w_proj_attn=128, w_out=128)
    big_out = {name: _adam_packed("adam_" + name, recv[name], wmv[0][name][0], wmv[1][name][0], wmv[2][name][0], tile)
               for name, tile in adam_tile.items()}
    ada_out = _adam_w_ada(cond_all.T, lax.dynamic_slice(dmod_all, (0, me * 768), (N_DEV, 768)),
                          w_ada[0], m_w_ada[0], v_w_ada[0])

    small_spec = (("b_ada", 6), ("conv_b", 2), ("lru_wa", 160), ("lru_ba", 2), ("lru_wi", 160), ("lru_bi", 2),
                  ("lru_lambda", 2), ("ln1_g", 1), ("ln1_b", 1), ("ln2_g", 1), ("ln2_b", 1), ("conv_w", 1), ("b_gate", 1))
    small_rows = 344

    def pack_small(ps):
        return jnp.concatenate([_rows(ps[name], r) for name, r in small_spec] + [jnp.zeros((4, D), F32)], axis=0)

    names = [n for n, _ in small_spec]
    smalls = [pack_small(dict(zip(names, t))) for t in (
        (b_ada, conv_b, lru_wa, lru_ba, lru_wi, lru_bi, lru_lambda, ln1_g, ln1_b, ln2_g, ln2_b, conv_w, b_gate),
        (m_b_ada, m_conv_b, m_lru_wa, m_lru_ba, m_lru_wi, m_lru_bi, m_lru_lambda, m_ln1_g, m_ln1_b, m_ln2_g, m_ln2_b,
         m_conv_w, m_b_gate),
        (v_b_ada, v_conv_b, v_lru_wa, v_lru_ba, v_lru_wi, v_lru_bi, v_lru_lambda, v_ln1_g, v_ln1_b, v_ln2_g, v_ln2_b,
         v_conv_w, v_b_gate))]
    g_conv_w = lax.dynamic_slice(red[332:337].reshape(-1)[:CONV_W * DR].reshape(CONV_W, DR), (0, me * 160), (CONV_W, 160))
    g_b_gate = lax.dynamic_slice(red[337:339], (0, me * 128), (2, 128))
    g_small_pack = jnp.concatenate([jnp.sum(dmod_all, axis=0).reshape(6, D), red[0:332], _rows(g_conv_w, 1),
                                    _rows(g_b_gate, 1), jnp.zeros((4, D), F32)], axis=0)
    assert g_small_pack.shape == (small_rows, D) and smalls[0].shape == (small_rows, D)
    small_out = _adam_packed("adam_small", g_small_pack, smalls[0], smalls[1], smalls[2], small_rows)

    shapes = dict(w_ada=w_ada.shape, b_ada=b_ada.shape, w_in=w_in.shape, conv_w=conv_w.shape, conv_b=conv_b.shape,
                  lru_wa=lru_wa.shape, lru_ba=lru_ba.shape, lru_wi=lru_wi.shape, lru_bi=lru_bi.shape,
                  lru_lambda=lru_lambda.shape, w_proj_rnn=w_proj_rnn.shape, w_proj_attn=w_proj_attn.shape,
                  b_gate=b_gate.shape, w_out=w_out.shape, ln1_g=ln1_g.shape, ln1_b=ln1_b.shape,
                  w_ffn_gate=w_ffn_gate.shape, w_ffn_up=w_ffn_up.shape, w_ffn_down=w_ffn_down.shape,
                  ln2_g=ln2_g.shape, ln2_b=ln2_b.shape)

    def unpack(kind):
        out = {"w_ada": ada_out[kind].reshape(shapes["w_ada"])}
        for name, _ in PACK_ROWS:
            out[name] = big_out[name][kind].reshape(shapes[name])
        o = 0
        for name, r in small_spec:
            size = 1
            for dim in shapes[name]:
                size *= dim
            out[name] = small_out[kind][o:o + r].reshape(-1)[:size].reshape(shapes[name])
            o += r
        return [out[name] for name in shapes]

    return (loss, grad_x[None], *unpack(0), *unpack(1), *unpack(2), *unpack(3))
```

```python
import functools

import jax
import jax.numpy as jnp
from jax import lax
from jax.experimental import pallas as pl
from jax.experimental.pallas import tpu as pltpu

F32 = jnp.float32
BF16 = jnp.bfloat16
MESH = pl.DeviceIdType.MESH

N_DEV = 8
D = 1024
DR = 1280
N_BLK = 10
BW = 128
HEAD_PAIR = 128
N_PAIR = 8
QB = 128
DFF = 2816
CONV_W = 4
DILATIONS = (1, 4, 16)
ALPHA = 2.0 ** 0.25
LN_EPS = 1e-5
LRU_C = 8.0
SCALE = 0.125
NEG = -1e30
ADAM_LR, ADAM_B1, ADAM_B2, ADAM_EPS, ADAM_WD, ADAM_STEP = 0.001, 0.9, 0.999, 1e-08, 0.01, 10

VMEM_LIMIT = 48 * 1024 * 1024
ROW_TILE = 256
RNN_CHUNK = 256

PACK_ROWS = (("w_in", 960), ("w_ffn_gate", 352), ("w_ffn_up", 352), ("w_proj_rnn", 160),
             ("w_proj_attn", 128), ("w_out", 128), ("w_ffn_down", 352))
LATE_ROWS = PACK_ROWS[1:]


def _cparams(sem):
    return pltpu.CompilerParams(dimension_semantics=sem, vmem_limit_bytes=VMEM_LIMIT)


def _ln(z):
    mu = jnp.mean(z, axis=-1, keepdims=True)
    zc = z - mu
    var = jnp.mean(zc * zc, axis=-1, keepdims=True)
    rstd = lax.rsqrt(var + LN_EPS)
    return zc * rstd, rstd


def _ln_bwd(dn, n, rstd):
    return rstd * (dn - jnp.mean(dn, axis=-1, keepdims=True) - n * jnp.mean(dn * n, axis=-1, keepdims=True))


def _sigmoid(x):
    return 0.5 * jnp.tanh(0.5 * x) + 0.5


_GELU_K = 0.7978845608028654
_GELU_C = 0.044715


def _gelu(y):
    t = jnp.tanh(_GELU_K * (y + _GELU_C * y * y * y))
    return 0.5 * y * (1.0 + t), t


def _gelu_grad(y, t):
    return 0.5 * (1.0 + t) + 0.5 * y * (1.0 - t * t) * _GELU_K * (1.0 + 3.0 * _GELU_C * y * y)


def _rowsum(v):
    return jnp.sum(v, axis=0, keepdims=True)


def _rowwise(body, name, n_rows, tm, row_ins, full_ins, row_outs, acc_outs, carry=None):
    nri, nfi, nro, nao = len(row_ins), len(full_ins), len(row_outs), len(acc_outs)
    n_c = len(carry.arrays) if carry else 0
    n_in = nri + nfi + n_c

    def kern(*refs):
        ri, fi = refs[:nri], refs[nri:nri + nfi]
        ro, ao = refs[n_in:n_in + nro], refs[n_in + nro:n_in + nro + nao]
        if carry:
            i = pl.program_id(0)
            _carry(carry, refs[nri + nfi:n_in], refs[n_in + nro + nao:n_in + nro + nao + n_c], refs[-1], i == 0,
                   i == n_rows // tm - 1)
        if ao:
            @pl.when(pl.program_id(0) == 0)
            def _():
                for a in ao:
                    a[...] = jnp.zeros(a.shape, a.dtype)
        body(ri, fi, ro, ao)

    in_specs = [pl.BlockSpec((tm, w), functools.partial(lambda i, cb: (i, cb), cb=cb)) for _, w, cb in row_ins]
    in_specs += [pl.BlockSpec(a.shape, lambda i: (0, 0)) for a in full_ins]
    out_specs = [pl.BlockSpec((tm, w), lambda i: (i, 0)) for w, _ in row_outs]
    out_specs += [pl.BlockSpec(s, lambda i: (0, 0)) for s, _ in acc_outs]
    out_shape = [jax.ShapeDtypeStruct((n_rows, w), dt) for w, dt in row_outs]
    out_shape += [jax.ShapeDtypeStruct(s, dt) for s, dt in acc_outs]
    operands = [a for a, _, _ in row_ins] + list(full_ins)
    scratch = []
    if carry:
        in_specs, out_specs, out_shape = in_specs + carry.in_specs, out_specs + carry.out_specs, out_shape + carry.out_shape
        operands, scratch = operands + carry.arrays, carry.scratch
    return pl.pallas_call(
        kern, name=name, grid=(n_rows // tm,), in_specs=in_specs, out_specs=out_specs, out_shape=out_shape,
        scratch_shapes=scratch, compiler_params=_cparams(("arbitrary",)),
    )(*operands)


NT_DIMS = (((1,), (1,)), ((), ()))


def _pair_cols(pair):
    return pair[1].shape[0] if len(pair) == 3 else pair[1].shape[1]


def _pair_specs(pairs, tm, tn):
    in_specs, flat = [], []
    for pair in pairs:
        a, w = pair[0], pair[1]
        k = a.shape[1]
        in_specs.append(pl.BlockSpec((tm, k), lambda j, i: (i, 0)))
        if len(pair) == 3:
            in_specs.append(pl.BlockSpec((tn, k), lambda j, i: (j, 0)))
        else:
            in_specs.append(pl.BlockSpec((k, tn), lambda j, i: (0, j)))
        flat += [a, w]
    return in_specs, flat


def _pair_dot(pair, a_ref, w_ref):
    if len(pair) == 3:
        return lax.dot_general(a_ref[...], w_ref[...], NT_DIMS, preferred_element_type=F32)
    return jnp.dot(a_ref[...], w_ref[...], preferred_element_type=F32)


def _mm(name, pairs, out_dtype, tm, tn, carry=None):
    m_rows, n_cols = pairs[0][0].shape[0], _pair_cols(pairs[0])
    n_pairs = len(pairs)
    n_c = len(carry.arrays) if carry else 0
    grid = (n_cols // tn, m_rows // tm)

    def kern(*refs):
        o_ref = refs[2 * n_pairs + n_c]
        if carry:
            step = pl.program_id(0) * grid[1] + pl.program_id(1)
            _carry(carry, refs[2 * n_pairs:2 * n_pairs + n_c], refs[2 * n_pairs + n_c + 1:2 * n_pairs + 2 * n_c + 1],
                   refs[-1], step == 0, step == grid[0] * grid[1] - 1)
        acc = None
        for p in range(n_pairs):
            t = _pair_dot(pairs[p], refs[2 * p], refs[2 * p + 1])
            acc = t if acc is None else acc + t
        o_ref[...] = acc.astype(o_ref.dtype)

    in_specs, flat = _pair_specs(pairs, tm, tn)
    out_specs = [pl.BlockSpec((tm, tn), lambda j, i: (i, j))]
    out_shape = [jax.ShapeDtypeStruct((m_rows, n_cols), out_dtype)]
    if carry:
        res = pl.pallas_call(
            kern, name=name, grid=grid, in_specs=in_specs + carry.in_specs, out_specs=out_specs + carry.out_specs,
            out_shape=out_shape + carry.out_shape, scratch_shapes=carry.scratch,
            compiler_params=_cparams(("arbitrary", "arbitrary")),
        )(*flat, *carry.arrays)
        return res[0], res[1:]
    return pl.pallas_call(
        kern, name=name, grid=grid, in_specs=in_specs, out_specs=out_specs[0], out_shape=out_shape[0],
        compiler_params=_cparams(("parallel", "parallel")),
    )(*flat)


def _mm_tn(name, a, g, tm, tn):
    m_rows, k = a.shape
    n_cols = g.shape[1]

    def kern(a_ref, g_ref, o_ref):
        @pl.when(pl.program_id(1) == 0)
        def _():
            o_ref[...] = jnp.zeros(o_ref.shape, F32)
        o_ref[...] += lax.dot_general(a_ref[...], g_ref[...], (((0,), (0,)), ((), ())), preferred_element_type=F32)

    return pl.pallas_call(
        kern, name=name, grid=(n_cols // tn, m_rows // tm),
        in_specs=[pl.BlockSpec((tm, k), lambda j, m: (m, 0)), pl.BlockSpec((tm, tn), lambda j, m: (m, j))],
        out_specs=pl.BlockSpec((k, tn), lambda j, m: (0, j)),
        out_shape=jax.ShapeDtypeStruct((k, n_cols), F32),
        compiler_params=_cparams(("parallel", "arbitrary")),
    )(a, g)


def _mm_fused(name, pairs, tile_ins, full_ins, outs, accs, epilogue, tm, tn, sub=None):
    m_rows, n_cols = pairs[0][0].shape[0], _pair_cols(pairs[0])
    n_p, n_t, n_f, n_o = len(pairs), len(tile_ins), len(full_ins), len(outs)
    sub = sub or tm

    def kern(*refs):
        base = 2 * n_p
        t_refs, f_refs = refs[base:base + n_t], refs[base + n_t:base + n_t + n_f]
        o_refs, a_refs = refs[base + n_t + n_f:base + n_t + n_f + n_o], refs[base + n_t + n_f + n_o:]
        if a_refs:
            @pl.when((pl.program_id(0) == 0) & (pl.program_id(1) == 0))
            def _():
                for a in a_refs:
                    a[...] = jnp.zeros(a.shape, F32)
        for h in range(tm // sub):
            rows = pl.ds(h * sub, sub)
            dots = [_pair_dot(pairs[p], refs[2 * p].at[rows, :], refs[2 * p + 1]) for p in range(n_p)]
            epilogue(dots, [t.at[rows, :] for t in t_refs], f_refs, [o.at[rows, :] for o in o_refs], a_refs)

    in_specs, flat = _pair_specs(pairs, tm, tn)
    for arr, cb in tile_ins:
        in_specs.append(pl.BlockSpec((tm, tn), functools.partial(lambda j, i, cb: (i, cb + j), cb=cb)))
        flat.append(arr)
    for arr in full_ins:
        in_specs.append(pl.BlockSpec(arr.shape, lambda j, i: (0, 0)))
        flat.append(arr)
    out_specs = [pl.BlockSpec((tm, f * tn), lambda j, i: (i, j)) for _, f in outs]
    out_specs += [pl.BlockSpec(s, lambda j, i: (0, 0)) for s in accs]
    out_shape = [jax.ShapeDtypeStruct((m_rows, f * n_cols), dt) for dt, f in outs]
    out_shape += [jax.ShapeDtypeStruct(s, F32) for s in accs]
    sem = ("arbitrary", "arbitrary") if accs else ("parallel", "parallel")
    return pl.pallas_call(
        kern, name=name, grid=(n_cols // tn, m_rows // tm), in_specs=in_specs, out_specs=out_specs,
        out_shape=out_shape, compiler_params=_cparams(sem),
    )(*flat)


def _peers():
    x, y, c = lax.axis_index("x"), lax.axis_index("y"), lax.axis_index("c")
    me = 4 * x + 2 * y + c
    peers = []
    for k in range(1, N_DEV):
        px = 1 - x if (k >> 2) & 1 else x
        py = 1 - y if (k >> 1) & 1 else y
        pc = 1 - c if k & 1 else c
        peers.append(((px, py, pc), 4 * px + 2 * py + pc))
    return me, peers


def _exchange(me, peers, items, send_sems, recv_sems, t0=0):
    started = []
    for t, (src_of, dst_of) in enumerate(items, start=t0):
        for k, (pid, plin) in enumerate(peers):
            cp = pltpu.make_async_remote_copy(
                src_ref=src_of(plin), dst_ref=dst_of(me), send_sem=send_sems.at[t, k], recv_sem=recv_sems.at[t, k],
                device_id=pid, device_id_type=MESH)
            cp.start()
            started.append(cp)
    for t, (src_of, dst_of) in enumerate(items, start=t0):
        for k, (pid, plin) in enumerate(peers):
            pltpu.make_async_remote_copy(
                src_ref=src_of(plin), dst_ref=dst_of(plin), send_sem=send_sems.at[t, k], recv_sem=recv_sems.at[t, k],
                device_id=pid, device_id_type=MESH).wait_recv()
    for cp in started:
        cp.wait_send()


def _hbm_spec():
    return pl.BlockSpec(memory_space=pltpu.HBM)


class _ChipGather:
    def __init__(self, shard):
        self.arrays = [shard]
        self.out_shape = [jax.ShapeDtypeStruct((N_DEV,) + shard.shape, shard.dtype)]
        self.in_specs, self.out_specs = [_hbm_spec()], [_hbm_spec()]
        self.scratch = [pltpu.SemaphoreType.DMA((SEMS_PER_ITEM,))]

    def _parts(self, ins, outs, sems):
        src, out = ins[0], outs[0]
        x, y, c = lax.axis_index("x"), lax.axis_index("y"), lax.axis_index("c")
        lin = lambda px, py, pc: 4 * px + 2 * py + pc
        sibling = (x, y, 1 - c)
        chips = [(1 - x, y), (x, 1 - y), (1 - x, 1 - y)]

        def copy(k, slot, to, from_src):
            return pltpu.make_async_remote_copy(
                src_ref=src if from_src else out.at[slot], dst_ref=out.at[slot], send_sem=sems.at[k],
                recv_sem=sems.at[N_DEV - 1 + k], device_id=to, device_id_type=MESH)

        own = pltpu.make_async_copy(src, out.at[lin(x, y, c)], sems.at[2 * (N_DEV - 1)])
        first = [copy(0, lin(x, y, c), sibling, True)]
        first += [copy(1 + j, lin(x, y, c), (px, py, c), True) for j, (px, py) in enumerate(chips)]
        passed = [copy(4 + j, lin(px, py, c), sibling, False) for j, (px, py) in enumerate(chips)]
        landed = [copy(1 + j, lin(px, py, c), sibling, True) for j, (px, py) in enumerate(chips)]
        from_sibling = [copy(0, lin(x, y, 1 - c), sibling, True)]
        from_sibling += [copy(4 + j, lin(px, py, 1 - c), sibling, False) for j, (px, py) in enumerate(chips)]
        return own, first, passed, landed, from_sibling

    def start(self, ins, outs, sems):
        own, first, _, _, _ = self._parts(ins, outs, sems)
        own.start()
        for cp in first:
            cp.start()

    def wait(self, ins, outs, sems):
        own, first, passed, landed, from_sibling = self._parts(ins, outs, sems)
        for arrival, forward in zip(landed, passed):
            arrival.wait_recv()
            forward.start()
        for cp in from_sibling:
            cp.wait_recv()
        for cp in first + passed:
            cp.wait_send()
        own.wait()


SEMS_PER_ITEM = 2 * (N_DEV - 1) + 1


class _Exchange:
    def __init__(self, items):
        self.arrays = [a for a, _ in items]
        self.scatter = [s for _, s in items]
        self.out_shape = [jax.ShapeDtypeStruct(a.shape if s else (N_DEV,) + a.shape, a.dtype) for a, s in items]
        self.in_specs = [_hbm_spec() for _ in items]
        self.out_specs = [_hbm_spec() for _ in items]
        self.scratch = [pltpu.SemaphoreType.DMA((SEMS_PER_ITEM * len(items),))]

    def _copies(self, ins, outs, sems, landing):
        me, peers = _peers()
        cps = []
        for t, scatter in enumerate(self.scatter):
            base = t * SEMS_PER_ITEM
            for k, (pid, plin) in enumerate(peers):
                src = ins[t].at[plin] if scatter else ins[t]
                dst = outs[t].at[plin if landing else me]
                cps.append(pltpu.make_async_remote_copy(
                    src_ref=src, dst_ref=dst, send_sem=sems.at[base + k], recv_sem=sems.at[base + N_DEV - 1 + k],
                    device_id=pid, device_id_type=MESH))
        return cps

    def _own(self, ins, outs, sems):
        me, _ = _peers()
        return [pltpu.make_async_copy(ins[t].at[me] if scatter else ins[t], outs[t].at[me],
                                      sems.at[t * SEMS_PER_ITEM + 2 * (N_DEV - 1)])
                for t, scatter in enumerate(self.scatter)]

    def start(self, ins, outs, sems):
        for cp in self._own(ins, outs, sems) + self._copies(ins, outs, sems, False):
            cp.start()

    def wait(self, ins, outs, sems):
        for cp in self._copies(ins, outs, sems, True):
            cp.wait_recv()
        for cp in self._copies(ins, outs, sems, False):
            cp.wait_send()
        for cp in self._own(ins, outs, sems):
            cp.wait()


def _carry(exchange, ins, outs, sems, first, last):
    @pl.when(first)
    def _():
        exchange.start(ins, outs, sems)

    @pl.when(last)
    def _():
        exchange.wait(ins, outs, sems)


def _exchange_alone(name, exchange):
    n = len(exchange.arrays)

    def body(*refs):
        exchange.start(refs[:n], refs[n:2 * n], refs[2 * n])
        exchange.wait(refs[:n], refs[n:2 * n], refs[2 * n])

    return pl.pallas_call(body, name=name, in_specs=exchange.in_specs, out_specs=exchange.out_specs,
                          out_shape=exchange.out_shape, scratch_shapes=exchange.scratch)(*exchange.arrays)


def _ada_modulation(c8, w_ada, b_ada_cols):
    wcols = w_ada.shape[1]

    def body(c_ref, w_ref, b_ref, call_ref, cond_ref, mod_ref, res, send_sems, recv_sems):
        me, peers = _peers()
        call_ref[me] = c_ref[...]
        _exchange(me, peers, [(lambda p: c_ref, lambda s: call_ref.at[s])], send_sems, recv_sems, t0=0)
        for dev in range(N_DEV):
            cv = call_ref[dev]
            cond = cv * _sigmoid(cv)
            cond_ref[dev] = cond
            res[dev] = jnp.dot(cond, w_ref[...], preferred_element_type=F32,
                               precision=lax.Precision.HIGHEST) + b_ref[...]
        mod_ref[me] = res[me]
        _exchange(me, peers, [(lambda p: res.at[p], lambda s: mod_ref.at[s])], send_sems, recv_sems, t0=1)

    vm = pl.BlockSpec(memory_space=pltpu.VMEM)
    return pl.pallas_call(
        body, name="ada_modulation", in_specs=[vm, vm, vm], out_specs=[vm, vm, vm],
        out_shape=[jax.ShapeDtypeStruct((N_DEV, 8, D), F32), jax.ShapeDtypeStruct((N_DEV, 8, D), F32),
                   jax.ShapeDtypeStruct((N_DEV, 8, wcols), F32)],
        scratch_shapes=[pltpu.VMEM((N_DEV, 8, wcols), F32),
                        pltpu.SemaphoreType.DMA((2, N_DEV - 1)), pltpu.SemaphoreType.DMA((2, N_DEV - 1))],
        compiler_params=pltpu.CompilerParams(vmem_limit_bytes=VMEM_LIMIT),
    )(c8, w_ada, b_ada_cols)


def _lru_gates(xc, wa_ref, ba_ref, wi_ref, bi_ref, sp_ref, row0):
    xcb = xc.astype(BF16)
    pre_r, pre_i = [], []
    for n in range(N_BLK):
        xb = xcb[:, n * BW:(n + 1) * BW]
        pre_r.append(jnp.dot(xb, wa_ref[n], preferred_element_type=F32))
        pre_i.append(jnp.dot(xb, wi_ref[n], preferred_element_type=F32))
    r = _sigmoid(jnp.concatenate(pre_r, axis=1) + ba_ref[...])
    ig = _sigmoid(jnp.concatenate(pre_i, axis=1) + bi_ref[...])
    log_a = (-LRU_C) * r * sp_ref[...]
    a = jnp.exp(log_a)
    e2 = jnp.exp(2.0 * log_a)
    mult_raw = jnp.sqrt(1.0 - e2)
    rows = row0 + lax.broadcasted_iota(jnp.int32, xc.shape, 0)
    start = rows == 0
    mult = jnp.where(start, 1.0, mult_raw)
    return xcb, r, ig, a, e2, mult_raw, mult, start


def _conv(xpad, cw_ref, cb_ref, tc):
    out = cb_ref[...]
    for tap in range(CONV_W):
        out = out + cw_ref[tap:tap + 1, :] * xpad[pl.ds(8 - (CONV_W - 1) + tap, tc), :]
    return out


def _rnn_fwd(proj_rnn, cw, cb, wa, ba, wi, bi, sp, carry):
    s_len = proj_rnn.shape[0]
    tc = RNN_CHUNK
    n_c = len(carry.arrays)
    n_out = 5

    def kern(*refs):
        y_ref, xr_ref, cw_ref, cb_ref, wa_ref, ba_ref, wi_ref, bi_ref, sp_ref = refs[:9]
        hs_ref, ur_ref, xc_ref, r_ref, ig_ref = refs[9 + n_c:9 + n_c + n_out]
        xpad, a_scr, u_scr, h_scr = refs[9 + n_out + 2 * n_c:13 + n_out + 2 * n_c]
        i = pl.program_id(0)
        _carry(carry, refs[9:9 + n_c], refs[9 + n_out + n_c:9 + n_out + 2 * n_c], refs[-1], i == 0,
               i == s_len // tc - 1)

        @pl.when(i == 0)
        def _():
            xpad[0:8, :] = jnp.zeros((8, DR), F32)
            h_scr[...] = jnp.zeros((1, DR), F32)

        xpad[8:8 + tc, :] = xr_ref[...].astype(F32)
        xc = _conv(xpad, cw_ref, cb_ref, tc)
        xpad[0:8, :] = xpad[tc:tc + 8, :]
        xcb, r, ig, a, e2, mult_raw, mult, start = _lru_gates(xc, wa_ref, ba_ref, wi_ref, bi_ref, sp_ref, i * tc)
        a_scr[...] = a
        u_scr[...] = mult * (ig * xc)
        xc_ref[...] = xcb
        r_ref[...] = r.astype(BF16)
        ig_ref[...] = ig.astype(BF16)

        def step(t, h):
            h = a_scr[pl.ds(t, 1), :] * h + u_scr[pl.ds(t, 1), :]
            hs_ref[pl.ds(t, 1), :] = h
            return h

        h_scr[...] = lax.fori_loop(0, tc, step, h_scr[...], unroll=8)
        gy, _ = _gelu(y_ref[...].astype(F32))
        ur_ref[...] = (gy * hs_ref[...]).astype(BF16)

    full = lambda a: pl.BlockSpec(a.shape, lambda i: (0,) * a.ndim)
    res = pl.pallas_call(
        kern, name="rnn_fwd", grid=(s_len // tc,),
        in_specs=[pl.BlockSpec((tc, DR), lambda i: (i, 0)), pl.BlockSpec((tc, DR), lambda i: (i, 1)),
                  full(cw), full(cb), full(wa), full(ba), full(wi), full(bi), full(sp)] + carry.in_specs,
        out_specs=[pl.BlockSpec((tc, DR), lambda i: (i, 0))] * n_out + carry.out_specs,
        out_shape=[jax.ShapeDtypeStruct((s_len, DR), F32)] + [jax.ShapeDtypeStruct((s_len, DR), BF16)] * (n_out - 1)
        + carry.out_shape,
        scratch_shapes=[pltpu.VMEM((tc + 8, DR), F32), pltpu.VMEM((tc, DR), F32), pltpu.VMEM((tc, DR), F32),
                        pltpu.VMEM((1, DR), F32)] + carry.scratch,
        compiler_params=_cparams(("arbitrary",)),
    )(proj_rnn, proj_rnn, cw, cb, wa, ba, wi, bi, sp, *carry.arrays)
    return res[0], res[1], res[2:n_out], res[n_out:]


def _rnn_bwd(proj_rnn, hs, du, xc_b, r_b, ig_b, cw, wat, wit, sp, dsp_dlam):
    s_len = proj_rnn.shape[0]
    tc = RNN_CHUNK
    nch = s_len // tc

    def kern(y_ref, xr_ref, xrp_ref, hs_ref, hsp_ref, du_ref, xc_ref, r_ref, ig_ref, cw_ref, wat_ref, wit_ref, sp_ref,
             dspl_ref, drnn_ref, dwa_ref, dwi_ref, vec_ref, xpad, hpad, a_scr, d_scr, g_scr, dxcpad, ag_scr):
        j = pl.program_id(0)
        n = nch - 1 - j

        @pl.when(j == 0)
        def _():
            dwa_ref[...] = jnp.zeros(dwa_ref.shape, F32)
            dwi_ref[...] = jnp.zeros(dwi_ref.shape, F32)
            vec_ref[...] = jnp.zeros(vec_ref.shape, F32)
            ag_scr[...] = jnp.zeros((1, DR), F32)
            dxcpad[tc:tc + 8, :] = jnp.zeros((8, DR), F32)

        has_prev = n > 0
        xpad[0:8, :] = jnp.where(has_prev, xrp_ref[8:16, :].astype(F32), 0.0)
        xpad[8:8 + tc, :] = xr_ref[...].astype(F32)
        hpad[0:8, :] = jnp.where(has_prev, hsp_ref[...], 0.0)
        hpad[8:8 + tc, :] = hs_ref[...]
        xcb = xc_ref[...]
        xc, r, ig = xcb.astype(F32), r_ref[...].astype(F32), ig_ref[...].astype(F32)
        log_a = (-LRU_C) * r * sp_ref[...]
        a = jnp.exp(log_a)
        e2 = jnp.exp(2.0 * log_a)
        mult_raw = jnp.sqrt(1.0 - e2)
        start = (n * tc + lax.broadcasted_iota(jnp.int32, (tc, DR), 0)) == 0
        mult = jnp.where(start, 1.0, mult_raw)

        y = y_ref[...].astype(F32)
        gy, th = _gelu(y)
        duv = du_ref[...].astype(F32)
        drnn_ref[:, 0:DR] = (duv * hs_ref[...] * _gelu_grad(y, th)).astype(BF16)
        a_scr[...] = a
        d_scr[...] = duv * gy

        def step(tt, ag):
            t = tc - 1 - tt
            g = d_scr[pl.ds(t, 1), :] + ag
            g_scr[pl.ds(t, 1), :] = g
            return a_scr[pl.ds(t, 1), :] * g

        ag_scr[...] = lax.fori_loop(0, tc, step, ag_scr[...], unroll=8)
        g = g_scr[...]
        da = g * hpad[pl.ds(7, tc), :]
        gx = g * xc
        dmult = jnp.where(start, 0.0, gx * ig)
        di = gx * mult
        dxc = g * mult * ig
        dlog_a = da * a - jnp.where(start, 0.0, dmult * e2 / mult_raw)
        dr = dlog_a * ((-LRU_C) * sp_ref[...])
        vec_ref[7:8, :] += _rowsum(dlog_a * ((-LRU_C) * r))
        dpr = dr * r * (1.0 - r)
        dpi = di * ig * (1.0 - ig)
        vec_ref[5:6, :] += _rowsum(dpr)
        vec_ref[6:7, :] += _rowsum(dpi)
        dprb, dpib = dpr.astype(BF16), dpi.astype(BF16)
        extra = []
        for b in range(N_BLK):
            sl = slice(b * BW, (b + 1) * BW)
            extra.append(jnp.dot(dprb[:, sl], wat_ref[b], preferred_element_type=F32)
                         + jnp.dot(dpib[:, sl], wit_ref[b], preferred_element_type=F32))
            dn = (((0,), (0,)), ((), ()))
            dwa_ref[b] += lax.dot_general(xcb[:, sl], dprb[:, sl], dn, preferred_element_type=F32)
            dwi_ref[b] += lax.dot_general(xcb[:, sl], dpib[:, sl], dn, preferred_element_type=F32)
        dxc = dxc + jnp.concatenate(extra, axis=1)
        vec_ref[4:5, :] += _rowsum(dxc)
        dxcpad[0:tc, :] = dxc
        dxr = jnp.zeros((tc, DR), F32)
        for tap in range(CONV_W):
            shift = CONV_W - 1 - tap
            dxr = dxr + cw_ref[tap:tap + 1, :] * dxcpad[pl.ds(shift, tc), :]
            vec_ref[tap:tap + 1, :] += _rowsum(dxc * xpad[pl.ds(8 - shift, tc), :])
        dxcpad[tc:tc + 8, :] = dxcpad[0:8, :]
        drnn_ref[:, DR:2 * DR] = dxr.astype(BF16)

        @pl.when(j == nch - 1)
        def _():
            vec_ref[7:8, :] = vec_ref[7:8, :] * dspl_ref[...]

    full = lambda a: pl.BlockSpec(a.shape, lambda j: (0,) * a.ndim)
    rev = lambda j: nch - 1 - j
    prev8 = lambda j: jnp.maximum((nch - 1 - j) * (tc // 8) - 1, 0)
    prev16 = lambda j: jnp.maximum((nch - 1 - j) * (tc // 16) - 1, 0)
    chunk = pl.BlockSpec((tc, DR), lambda j: (rev(j), 0))
    return pl.pallas_call(
        kern, name="rnn_bwd", grid=(nch,),
        in_specs=[chunk, pl.BlockSpec((tc, DR), lambda j: (rev(j), 1)), pl.BlockSpec((16, DR), lambda j: (prev16(j), 1)),
                  chunk, pl.BlockSpec((8, DR), lambda j: (prev8(j), 0)), chunk, chunk, chunk, chunk,
                  full(cw), full(wat), full(wit), full(sp), full(dsp_dlam)],
        out_specs=[pl.BlockSpec((tc, 2 * DR), lambda j: (rev(j), 0)),
                   pl.BlockSpec((N_BLK, BW, BW), lambda j: (0, 0, 0)), pl.BlockSpec((N_BLK, BW, BW), lambda j: (0, 0, 0)),
                   pl.BlockSpec((8, DR), lambda j: (0, 0))],
        out_shape=[jax.ShapeDtypeStruct((s_len, 2 * DR), BF16), jax.ShapeDtypeStruct((N_BLK, BW, BW), F32),
                   jax.ShapeDtypeStruct((N_BLK, BW, BW), F32), jax.ShapeDtypeStruct((8, DR), F32)],
        scratch_shapes=[pltpu.VMEM((tc + 8, DR), F32), pltpu.VMEM((tc + 8, DR), F32), pltpu.VMEM((tc, DR), F32),
                        pltpu.VMEM((tc, DR), F32), pltpu.VMEM((tc, DR), F32), pltpu.VMEM((tc + 8, DR), F32),
                        pltpu.VMEM((1, DR), F32)],
        compiler_params=_cparams(("arbitrary",)),
    )(proj_rnn, proj_rnn, proj_rnn, hs, hs, du, xc_b, r_b, ig_b, cw, wat, wit, sp, dsp_dlam)


ATT_BLK = 2048


def _attn_units():
    return [(g, d, b, b * QB * d + r)
            for g, d in enumerate(DILATIONS) for b in range(ATT_BLK // (QB * d)) for r in range(d)]


def _rows_at(ref, start, n, d):
    return ref[pl.ds(start, n, stride=d), :] if d > 1 else ref[start:start + n, :]


def _add_rows_at(ref, start, n, d, val):
    if d > 1:
        ref[pl.ds(start, n, stride=d), :] = ref[pl.ds(start, n, stride=d), :] + val
    else:
        ref[start:start + n, :] = ref[start:start + n, :] + val


def _stack_heads(x, first):
    zero = jnp.zeros_like(x)
    return jnp.concatenate([jnp.where(first, x, zero), jnp.where(first, zero, x)], axis=0)


def _unstack_heads(x, first):
    return jnp.where(first, x[:QB], x[QB:])


def _head_column(x, lane, half):
    return jnp.sum(jnp.where(lane == 64 * half, x, 0.0), axis=-1, keepdims=True)


def _band_masks():
    row = lax.broadcasted_iota(jnp.int32, (2 * QB, 2 * QB), 0) & (QB - 1)
    col = lax.broadcasted_iota(jnp.int32, (2 * QB, 2 * QB), 1)
    dist = QB + row - col
    lane = lax.broadcasted_iota(jnp.int32, (1, HEAD_PAIR), 1)
    return (dist >= 0) & (dist <= QB), col >= QB, lane


def _attention_specs(nblk, blk_of):
    cur = lambda off: pl.BlockSpec((ATT_BLK, HEAD_PAIR), lambda hp, j: (blk_of(j), off + hp))
    prev = lambda off: pl.BlockSpec((ATT_BLK, HEAD_PAIR), lambda hp, j: (jnp.maximum(blk_of(j) - 1, 0), off + hp))
    return cur, prev


def _attention_fwd(qkv):
    s_len = qkv.shape[0]
    nblk = s_len // ATT_BLK
    units = _attn_units()
    nt = (((1,), (1,)), ((), ()))

    def kern(q_ref, k_ref, v_ref, o_ref, lse_ref, kbuf, vbuf, og, lg):
        blk = pl.program_id(1)

        @pl.when(blk == 0)
        def _():
            kbuf[0:ATT_BLK, :] = jnp.zeros((ATT_BLK, HEAD_PAIR), F32)
            vbuf[0:ATT_BLK, :] = jnp.zeros((ATT_BLK, HEAD_PAIR), F32)

        kbuf[ATT_BLK:2 * ATT_BLK, :] = k_ref[...]
        vbuf[ATT_BLK:2 * ATT_BLK, :] = v_ref[...]
        band, later, lane = _band_masks()
        first = lane < 64
        band0 = band & (later | (blk > 0))
        for g, d, b, start in units:
            kstart = ATT_BLK + start - QB * d
            qs = _stack_heads((_rows_at(q_ref, start, QB, d) * SCALE).astype(BF16), first)
            k2 = _rows_at(kbuf, kstart, 2 * QB, d).astype(BF16)
            v2 = _rows_at(vbuf, kstart, 2 * QB, d).astype(BF16)
            s = lax.dot_general(qs, k2, nt, preferred_element_type=F32)
            valid = band if b > 0 else band0
            ps, dens, lses = [], [], []
            for r0 in (0, QB):
                sh = jnp.where(valid[r0:r0 + QB], s[r0:r0 + QB], NEG)
                mx = jnp.max(sh, axis=-1, keepdims=True)
                ph = jnp.exp(sh - mx)
                den = jnp.sum(ph, axis=-1, keepdims=True)
                ps.append(ph.astype(BF16))
                dens.append(jnp.broadcast_to(den, (QB, HEAD_PAIR)))
                lses.append(jnp.broadcast_to(mx + jnp.log(den), (QB, HEAD_PAIR)))
            out = (jnp.dot(jnp.concatenate(ps, axis=0), v2, preferred_element_type=F32)
                   / jnp.concatenate(dens, axis=0))
            lse2 = jnp.concatenate(lses, axis=0)
            if d > 1:
                og[g, pl.ds(start, QB, stride=d), :] = _unstack_heads(out, first)
                lg[g, pl.ds(start, QB, stride=d), :] = _unstack_heads(lse2, first)
            else:
                og[g, start:start + QB, :] = _unstack_heads(out, first)
                lg[g, start:start + QB, :] = _unstack_heads(lse2, first)
        kbuf[0:ATT_BLK, :] = kbuf[ATT_BLK:2 * ATT_BLK, :]
        vbuf[0:ATT_BLK, :] = vbuf[ATT_BLK:2 * ATT_BLK, :]
        mx = jnp.maximum(jnp.maximum(lg[0], lg[1]), lg[2])
        es = [jnp.exp(lg[g] - mx) for g in range(3)]
        tot = es[0] + es[1] + es[2]
        o_ref[...] = ((es[0] * og[0] + es[1] * og[1] + es[2] * og[2]) / tot).astype(BF16)
        lse_ref[...] = mx + jnp.log(tot)

    cur, _ = _attention_specs(nblk, lambda j: j)
    out_spec = pl.BlockSpec((ATT_BLK, HEAD_PAIR), lambda hp, j: (j, hp))
    return pl.pallas_call(
        kern, name="attention_fwd", grid=(N_PAIR, nblk), in_specs=[cur(0), cur(N_PAIR), cur(2 * N_PAIR)],
        out_specs=[out_spec, out_spec],
        out_shape=[jax.ShapeDtypeStruct((s_len, D), BF16), jax.ShapeDtypeStruct((s_len, D), F32)],
        scratch_shapes=[pltpu.VMEM((2 * ATT_BLK, HEAD_PAIR), F32), pltpu.VMEM((2 * ATT_BLK, HEAD_PAIR), F32),
                        pltpu.VMEM((3, ATT_BLK, HEAD_PAIR), F32), pltpu.VMEM((3, ATT_BLK, HEAD_PAIR), F32)],
        compiler_params=_cparams(("arbitrary", "arbitrary")),
    )(qkv, qkv, qkv)


def _attention_bwd(qkv, do, o, lse, carry):
    s_len = qkv.shape[0]
    nblk = s_len // ATT_BLK
    units = _attn_units()
    nt = (((1,), (1,)), ((), ()))
    tn = (((0,), (0,)), ((), ()))
    n_c = len(carry.arrays) if carry else 0
    c_in, c_out = (carry.in_specs, carry.out_specs) if carry else ([], [])
    c_shape, c_scratch, c_arrays = (carry.out_shape, carry.scratch, carry.arrays) if carry else ([], [], [])

    def kern(*refs):
        q_ref, k_ref, v_ref, kp_ref, vp_ref, do_ref, o_ref, lse_ref = refs[:8]
        dq_ref, dk_ref, dv_ref = refs[8 + n_c:11 + n_c]
        kbuf, vbuf, dkbuf, dvbuf, dq_scr, dsum = refs[11 + 2 * n_c:17 + 2 * n_c]
        hp, j = pl.program_id(0), pl.program_id(1)
        blk = nblk - 1 - j
        if carry:
            _carry(carry, refs[8:8 + n_c], refs[11 + n_c:11 + 2 * n_c], refs[-1], (hp == 0) & (j == 0),
                   (hp == N_PAIR - 1) & (j == nblk - 1))
        zeros = jnp.zeros((ATT_BLK, HEAD_PAIR), F32)

        @pl.when(j == 0)
        def _():
            dkbuf[ATT_BLK:2 * ATT_BLK, :] = zeros
            dvbuf[ATT_BLK:2 * ATT_BLK, :] = zeros

        @pl.when(j > 0)
        def _():
            dkbuf[ATT_BLK:2 * ATT_BLK, :] = dkbuf[0:ATT_BLK, :]
            dvbuf[ATT_BLK:2 * ATT_BLK, :] = dvbuf[0:ATT_BLK, :]

        dkbuf[0:ATT_BLK, :] = zeros
        dvbuf[0:ATT_BLK, :] = zeros
        dq_scr[...] = zeros
        kbuf[0:ATT_BLK, :] = kp_ref[...]
        kbuf[ATT_BLK:2 * ATT_BLK, :] = k_ref[...]
        vbuf[0:ATT_BLK, :] = vp_ref[...]
        vbuf[ATT_BLK:2 * ATT_BLK, :] = v_ref[...]
        band, later, lane = _band_masks()
        first = lane < 64
        band0 = band & (later | (blk > 0))
        prod = do_ref[...] * o_ref[...].astype(F32)
        dsum[...] = jnp.where(first, jnp.sum(jnp.where(first, prod, 0.0), axis=-1, keepdims=True),
                              jnp.sum(jnp.where(first, 0.0, prod), axis=-1, keepdims=True))
        for g, d, b, start in units:
            kstart = ATT_BLK + start - QB * d
            qs = _stack_heads((_rows_at(q_ref, start, QB, d) * SCALE).astype(BF16), first)
            dos = _stack_heads(_rows_at(do_ref, start, QB, d).astype(BF16), first)
            k2 = _rows_at(kbuf, kstart, 2 * QB, d).astype(BF16)
            v2 = _rows_at(vbuf, kstart, 2 * QB, d).astype(BF16)
            ds_rows = _rows_at(dsum, start, QB, d)
            lse_rows = _rows_at(lse_ref, start, QB, d)
            dcol = jnp.concatenate([_head_column(ds_rows, lane, 0), _head_column(ds_rows, lane, 1)], axis=0)
            lcol = jnp.concatenate([_head_column(lse_rows, lane, 0), _head_column(lse_rows, lane, 1)], axis=0)
            s = lax.dot_general(qs, k2, nt, preferred_element_type=F32)
            p = jnp.exp(jnp.where(band if b > 0 else band0, s, NEG) - lcol)
            dp = lax.dot_general(dos, v2, nt, preferred_element_type=F32)
            ds = (p * (dp - dcol)).astype(BF16)
            _add_rows_at(dq_scr, start, QB, d,
                         _unstack_heads(jnp.dot(ds, k2, preferred_element_type=F32), first) * SCALE)
            _add_rows_at(dkbuf, kstart, 2 * QB, d, lax.dot_general(ds, qs, tn, preferred_element_type=F32))
            _add_rows_at(dvbuf, kstart, 2 * QB, d, lax.dot_general(p.astype(BF16), dos, tn, preferred_element_type=F32))
        dq_ref[...] = dq_scr[...].astype(BF16)
        dk_ref[...] = dkbuf[ATT_BLK:2 * ATT_BLK, :].astype(BF16)
        dv_ref[...] = dvbuf[ATT_BLK:2 * ATT_BLK, :].astype(BF16)

    rev = lambda j: nblk - 1 - j
    cur, prev = _attention_specs(nblk, rev)
    buf = lambda rows: pltpu.VMEM((rows, HEAD_PAIR), F32)
    res = pl.pallas_call(
        kern, name="attention_bwd", grid=(N_PAIR, nblk),
        in_specs=[cur(0), cur(N_PAIR), cur(2 * N_PAIR), prev(N_PAIR), prev(2 * N_PAIR), cur(0), cur(0), cur(0)] + c_in,
        out_specs=[cur(0)] * 3 + c_out,
        out_shape=[jax.ShapeDtypeStruct((s_len, D), BF16)] * 3 + c_shape,
        scratch_shapes=[buf(2 * ATT_BLK), buf(2 * ATT_BLK), buf(2 * ATT_BLK), buf(2 * ATT_BLK), buf(ATT_BLK),
                        buf(ATT_BLK)] + c_scratch,
        compiler_params=_cparams(("arbitrary", "arbitrary")),
    )(qkv, qkv, qkv, qkv, qkv, do, o, lse, *c_arrays)
    return res[0], res[1], res[2], res[3:]


def _adamw(w, g, m, v):
    m = ADAM_B1 * m + (1.0 - ADAM_B1) * g
    v = ADAM_B2 * v + (1.0 - ADAM_B2) * (g * g)
    m_hat = m / (1.0 - ADAM_B1 ** ADAM_STEP)
    v_hat = v / (1.0 - ADAM_B2 ** ADAM_STEP)
    delta = -ADAM_LR * (m_hat / (jnp.sqrt(v_hat) + ADAM_EPS) + ADAM_WD * w)
    return delta, m, v


def _adam_packed(name, recv, w, m, v, tm):
    n_rows, width = w.shape
    summed = recv.ndim == 3

    def kern(r_ref, w_ref, m_ref, v_ref, g_out, d_out, m_out, v_out):
        if summed:
            g = r_ref[0].astype(F32)
            for s in range(1, N_DEV):
                g = g + r_ref[s].astype(F32)
        else:
            g = r_ref[...]
        delta, mn, vn = _adamw(w_ref[...], g, m_ref[...], v_ref[...])
        g_out[...] = g
        d_out[...] = delta
        m_out[...] = mn
        v_out[...] = vn

    tile = pl.BlockSpec((tm, width), lambda i: (i, 0))
    rspec = pl.BlockSpec((N_DEV, tm, width), lambda i: (0, i, 0)) if summed else tile
    return pl.pallas_call(
        kern, name=name, grid=(n_rows // tm,), in_specs=[rspec, tile, tile, tile], out_specs=[tile] * 4,
        out_shape=[jax.ShapeDtypeStruct((n_rows, width), F32)] * 4, compiler_params=_cparams(("parallel",)),
    )(recv, w, m, v)


def _sum_slots(recv):
    _, n_rows, width = recv.shape

    def kern(r_ref, o_ref):
        g = r_ref[0]
        for s in range(1, N_DEV):
            g = g + r_ref[s]
        o_ref[...] = g

    return pl.pallas_call(
        kern, name="sum_small", grid=(1,), in_specs=[pl.BlockSpec(recv.shape, lambda i: (0, 0, 0))],
        out_specs=pl.BlockSpec((n_rows, width), lambda i: (0, 0)),
        out_shape=jax.ShapeDtypeStruct((n_rows, width), F32), compiler_params=_cparams(("arbitrary",)),
    )(recv)


def _adam_w_ada(cond_t, dmod_cols, w, m, v):
    n_rows, width = w.shape
    tm = ROW_TILE

    def kern(c_ref, d_ref, w_ref, m_ref, v_ref, g_out, d_out, m_out, v_out):
        g = c_ref[:, 0:1] * d_ref[0:1, :]
        for b in range(1, N_DEV):
            g = g + c_ref[:, b:b + 1] * d_ref[b:b + 1, :]
        delta, mn, vn = _adamw(w_ref[...], g, m_ref[...], v_ref[...])
        g_out[...] = g
        d_out[...] = delta
        m_out[...] = mn
        v_out[...] = vn

    tile = pl.BlockSpec((tm, width), lambda i: (i, 0))
    return pl.pallas_call(
        kern, name="adam_w_ada", grid=(n_rows // tm,),
        in_specs=[pl.BlockSpec((tm, N_DEV), lambda i: (i, 0)), pl.BlockSpec((N_DEV, width), lambda i: (0, 0)),
                  tile, tile, tile],
        out_specs=[tile] * 4, out_shape=[jax.ShapeDtypeStruct((n_rows, width), F32)] * 4,
        compiler_params=_cparams(("parallel",)),
    )(cond_t, dmod_cols, w, m, v)


def _rows(a, n_rows=None):
    flat = a.reshape(-1)
    need = (n_rows if n_rows is not None else -(-flat.shape[0] // D)) * D
    if need != flat.shape[0]:
        flat = jnp.concatenate([flat, jnp.zeros((need - flat.shape[0],), flat.dtype)])
    return flat.reshape(-1, D)


def _slots_to_cols(s, k, cols):
    return s.reshape(N_DEV, k, cols).transpose(1, 0, 2).reshape(k, N_DEV * cols)


def kernel(x, c, w_ada, b_ada, w_in, conv_w, conv_b, lru_wa, lru_ba, lru_wi, lru_bi, lru_lambda, w_proj_rnn, w_proj_attn, b_gate, w_out, ln1_g, ln1_b, w_ffn_gate, w_ffn_up, w_ffn_down, ln2_g, ln2_b, loss_target, m_w_ada, m_b_ada, m_w_in, m_conv_w, m_conv_b, m_lru_wa, m_lru_ba, m_lru_wi, m_lru_bi, m_lru_lambda, m_w_proj_rnn, m_w_proj_attn, m_b_gate, m_w_out, m_ln1_g, m_ln1_b, m_w_ffn_gate, m_w_ffn_up, m_w_ffn_down, m_ln2_g, m_ln2_b, v_w_ada, v_b_ada, v_w_in, v_conv_w, v_conv_b, v_lru_wa, v_lru_ba, v_lru_wi, v_lru_bi, v_lru_lambda, v_w_proj_rnn, v_w_proj_attn, v_b_gate, v_w_out, v_ln1_g, v_ln1_b, v_w_ffn_gate, v_w_ffn_up, v_w_ffn_down, v_ln2_g, v_ln2_b):
    s_len = x.shape[1]
    me = 4 * lax.axis_index("x") + 2 * lax.axis_index("y") + lax.axis_index("c")
    xs = x[0]
    tgt = loss_target[0]
    tm = ROW_TILE

    b_ada_cols = lax.dynamic_slice(b_ada, (0, me * 768), (1, 768))
    pad_cols = lambda a: jnp.concatenate([a, jnp.zeros((a.shape[0], D - a.shape[1]), F32)], axis=1)
    c8 = jnp.concatenate([c, pad_cols(conv_w[0]), pad_cols(b_gate[0]), jnp.zeros((1, D), F32)], axis=0)
    c_all, cond_blocks, mod_parts = _ada_modulation(c8, w_ada[0], b_ada_cols)
    cond_all = cond_blocks[:, 0, :]
    mod = mod_parts[:, 0, :].reshape(6, D)
    mod8 = jnp.concatenate([mod, jnp.zeros((2, D), F32)], axis=0)
    cw = c_all[:, 1:1 + CONV_W, :DR // N_DEV].transpose(1, 0, 2).reshape(CONV_W, DR)
    bg = c_all[:, 1 + CONV_W:3 + CONV_W, :D // N_DEV].transpose(1, 0, 2).reshape(2, D)
    bg8 = jnp.concatenate([bg, jnp.zeros((6, D), F32)], axis=0)

    late = dict(w_ffn_gate=w_ffn_gate, w_ffn_up=w_ffn_up, w_proj_rnn=w_proj_rnn, w_proj_attn=w_proj_attn,
                w_out=w_out, w_ffn_down=w_ffn_down)
    late_shard = jnp.concatenate([late[name][0].reshape(r, D) for name, r in LATE_ROWS], axis=0).astype(BF16)

    cb = conv_b
    wa_b, wi_b = lru_wa[0].astype(BF16), lru_wi[0].astype(BF16)
    wat_b, wit_b = jnp.swapaxes(wa_b, 1, 2), jnp.swapaxes(wi_b, 1, 2)
    ba, bi = lru_ba.reshape(1, DR), lru_bi.reshape(1, DR)
    sp = jax.nn.softplus(-lru_lambda)
    dsp_dlam = -jax.nn.sigmoid(-lru_lambda)
    ln1 = jnp.concatenate([ln1_g, ln1_b, jnp.zeros((6, D), F32)], axis=0)
    ln2 = jnp.concatenate([ln2_g, ln2_b, jnp.zeros((6, D), F32)], axis=0)

    def f1(ri, fi, ro, ao):
        n, _ = _ln(ri[0][...])
        ro[0][...] = (n * (1.0 + fi[0][1:2, :]) + fi[0][0:1, :]).astype(BF16)

    h1, w_in_slots = _rowwise(f1, "ln_mod1", s_len, tm, [(xs, D, 0)], [mod8], [(D, BF16)], [],
                              carry=_ChipGather(w_in[0].astype(BF16)))
    w_in_f = w_in_slots.transpose(1, 0, 2).reshape(D, N_DEV * 960)
    w_rnn, w_qkv, w_gates = w_in_f[:, :2 * DR], w_in_f[:, 2 * DR:2 * DR + 3 * D], w_in_f[:, 2 * DR + 3 * D:]
    proj_rnn = _mm("mm_in_rnn", [(h1, w_rnn)], BF16, 512, DR)
    qkv = _mm("mm_in_qkv", [(h1, w_qkv)], F32, 512, D)
    gates = _mm("mm_in_gates", [(h1, w_gates)], BF16, 512, D)

    hs, u_rnn, kept, (late_all,) = _rnn_fwd(proj_rnn, cw, cb, wa_b, ba, wi_b, bi, sp, _Exchange([(late_shard, False)]))
    offs, o = {}, 0
    for name, r in LATE_ROWS:
        offs[name] = (o, o + r)
        o += r
    part = lambda name: late_all[:, offs[name][0]:offs[name][1], :]
    w_g, w_u = _slots_to_cols(part("w_ffn_gate"), D, 352), _slots_to_cols(part("w_ffn_up"), D, 352)
    w_pr = part("w_proj_rnn").reshape(DR, D)
    w_pa = part("w_proj_attn").reshape(D, D)
    w_o = part("w_out").reshape(D, D)
    w_dn = part("w_ffn_down").reshape(DFF, D)

    u_attn, lse = _attention_fwd(qkv)

    def e_merge(dots, ti, fi, ro, ao):
        g_r = _sigmoid(ti[0][...] + fi[0][0:1, :])
        g_a = _sigmoid(ti[1][...] + fi[0][1:2, :])
        ro[0][...] = dots[0].astype(BF16)
        ro[1][...] = dots[1].astype(BF16)
        ro[2][...] = (g_r * dots[0] + g_a * dots[1]).astype(BF16)

    pr, pa, merged = _mm_fused("mm_proj_merge", [(u_rnn, w_pr), (u_attn, w_pa)], [(gates, 0), (gates, 1)], [bg8],
                               [(BF16, 1), (BF16, 1), (BF16, 1)], [], e_merge, 512, D, sub=256)
    def e_post1(dots, ti, fi, ro, ao):
        md, l1 = fi[0], fi[1]
        fv = dots[0]
        n1, _ = _ln(ALPHA * ti[0][...] + md[2:3, :] * fv)
        x1 = n1 * l1[0:1, :] + l1[1:2, :]
        n0, _ = _ln(x1)
        ro[0][...] = fv
        ro[1][...] = x1
        ro[2][...] = (n0 * (1.0 + md[4:5, :]) + md[3:4, :]).astype(BF16)

    f, x1, h2 = _mm_fused("mm_out_post1", [(merged, w_o)], [(xs, 0)], [mod8, ln1], [(F32, 1), (F32, 1), (BF16, 1)], [],
                          e_post1, 512, D, sub=256)

    def e_swiglu(dots, ti, fi, ro, ao):
        gp, up = dots
        ro[0][...] = gp.astype(BF16)
        ro[1][...] = up.astype(BF16)
        ro[2][...] = (gp * _sigmoid(gp) * up).astype(BF16)

    gpre, upre, act = _mm_fused("mm_ffn_in", [(h2, w_g), (h2, w_u)], [], [], [(BF16, 1)] * 3, [], e_swiglu, 512, DFF // 2)
    def e_loss(dots, ti, fi, ro, ao):
        md, l2 = fi[0], fi[1]
        f2v = dots[0]
        n2, rstd = _ln(ALPHA * ti[0][...] + md[5:6, :] * f2v)
        err = n2 * l2[0:1, :] + l2[1:2, :] - ti[1][...]
        dx2 = err * (1.0 / D)
        dz2 = _ln_bwd(dx2 * l2[0:1, :], n2, rstd)
        ro[0][...] = dz2
        ro[1][...] = (md[5:6, :] * dz2).astype(BF16)
        acc = ao[0]
        acc[0:1, :] += _rowsum(err * err) * (0.5 / D)
        acc[1:2, :] += _rowsum(dx2 * n2)
        acc[2:3, :] += _rowsum(dx2)
        acc[3:4, :] += _rowsum(dz2 * f2v)

    dz2, df2, acc5 = _mm_fused("mm_ffn_out_loss", [(act, w_dn)], [(x1, 0), (tgt, 0)], [mod8, ln2],
                               [(F32, 1), (BF16, 1)], [(8, D)], e_loss, 512, D, sub=256)

    d_w_dn = _mm_tn("mmt_ffn_down", act, df2, 1024, 512)

    def e_dswiglu(dots, ti, fi, ro, ao):
        da = dots[0]
        gp, up = ti[0][...].astype(F32), ti[1][...].astype(F32)
        sg = _sigmoid(gp)
        ro[0][...] = (da * up * sg * (1.0 + gp * (1.0 - sg))).astype(BF16)
        ro[1][...] = (da * gp * sg).astype(BF16)

    dgp, dup = _mm_fused("mm_d_ffn_out", [(df2, w_dn, True)], [(gpre, 0), (upre, 0)], [], [(BF16, 1)] * 2, [],
                         e_dswiglu, 512, DFF // 2, sub=256)
    d_w_g = _mm_tn("mmt_ffn_gate", h2, dgp, 2048, DFF // 2)
    d_w_u = _mm_tn("mmt_ffn_up", h2, dup, 2048, DFF // 2)

    def e_dpost1(dots, ti, fi, ro, ao):
        md, l1 = fi[0], fi[1]
        dh2v = dots[0] + dots[1]
        fv = ti[1][...]
        n1, rstd1 = _ln(ALPHA * ti[0][...] + md[2:3, :] * fv)
        n0, rstd0 = _ln(n1 * l1[0:1, :] + l1[1:2, :])
        dx1 = ALPHA * ti[2][...] + _ln_bwd(dh2v * (1.0 + md[4:5, :]), n0, rstd0)
        dz1 = _ln_bwd(dx1 * l1[0:1, :], n1, rstd1)
        ro[0][...] = ALPHA * dz1
        ro[1][...] = (md[2:3, :] * dz1).astype(BF16)
        acc = ao[0]
        acc[0:1, :] += _rowsum(dh2v * n0)
        acc[1:2, :] += _rowsum(dh2v)
        acc[2:3, :] += _rowsum(dx1 * n1)
        acc[3:4, :] += _rowsum(dx1)
        acc[4:5, :] += _rowsum(dz1 * fv)

    dxp, df, acc2 = _mm_fused("mm_d_ffn_in_post1", [(dgp, w_g, True), (dup, w_u, True)], [(xs, 0), (f, 0), (dz2, 0)],
                              [mod8, ln1], [(F32, 1), (BF16, 1)], [(8, D)], e_dpost1, 256, D)
    d_w_o = _mm_tn("mmt_out", merged, df, 2048, D)

    def e_dmerge(dots, ti, fi, ro, ao):
        dm = dots[0]
        g_r = _sigmoid(ti[0][...] + fi[0][0:1, :])
        g_a = _sigmoid(ti[1][...] + fi[0][1:2, :])
        ro[0][...] = (dm * g_r).astype(BF16)
        ro[1][...] = (dm * g_a).astype(BF16)
        dga = dm * ti[2][...].astype(F32) * g_r * (1.0 - g_r)
        dgb = dm * ti[3][...].astype(F32) * g_a * (1.0 - g_a)
        ro[2][:, 0:D] = dga.astype(BF16)
        ro[2][:, D:2 * D] = dgb.astype(BF16)
        ao[0][0:1, :] += _rowsum(dga)
        ao[0][1:2, :] += _rowsum(dgb)

    dpr, dpa, dgates, acc3 = _mm_fused("mm_d_out", [(df, w_o, True)], [(gates, 0), (gates, 1), (pr, 0), (pa, 0)], [bg8],
                                       [(BF16, 1), (BF16, 1), (BF16, 2)], [(8, D)], e_dmerge, 512, D, sub=256)
    du_rnn = _mm("mm_d_proj_rnn", [(dpr, w_pr, True)], F32, 512, DR)
    du_attn = _mm("mm_d_proj_attn", [(dpa, w_pa, True)], F32, 512, D)
    d_w_pr = _mm_tn("mmt_proj_rnn", u_rnn, dpr, 2048, D)
    d_w_pa = _mm_tn("mmt_proj_attn", u_attn, dpa, 2048, D)

    drnn, d_wa, d_wi, vec = _rnn_bwd(proj_rnn, hs, du_rnn, *kept, cw, wat_b, wit_b, sp, dsp_dlam)

    col_slots = lambda g, cols: g.reshape(g.shape[0], N_DEV, cols).transpose(1, 0, 2).astype(BF16)
    row_slots = lambda g: g.reshape(N_DEV, g.shape[0] // N_DEV, g.shape[1]).astype(BF16)
    early = [("w_ffn_gate", col_slots(d_w_g, 352)), ("w_ffn_up", col_slots(d_w_u, 352)), ("w_ffn_down", row_slots(d_w_dn)),
             ("w_proj_rnn", row_slots(d_w_pr)), ("w_proj_attn", row_slots(d_w_pa)), ("w_out", row_slots(d_w_o))]
    g_small = jnp.concatenate([
        _rows(vec[4], 2), d_wa.reshape(160, D), _rows(vec[5], 2), d_wi.reshape(160, D), _rows(vec[6], 2), _rows(vec[7], 2),
        acc2[2:4], acc5[1:3],
        _rows(vec[0:4], 5), acc3[0:2], acc5[0:1], jnp.zeros((4, D), F32)], axis=0)
    dq, dk, dv, recvd = _attention_bwd(qkv, du_attn, u_attn, lse,
                                       _Exchange([(g, True) for _, g in early] + [(g_small, False)]))
    recv = {name: r for (name, _), r in zip(early, recvd)}
    recv_small = recvd[-1]

    d_w_in = jnp.concatenate([_mm_tn("mmt_in_rnn", h1, drnn, 2048, DR), _mm_tn("mmt_in_q", h1, dq, 2048, D),
                              _mm_tn("mmt_in_k", h1, dk, 2048, D), _mm_tn("mmt_in_v", h1, dv, 2048, D),
                              _mm_tn("mmt_in_gates", h1, dgates, 2048, D)], axis=1)
    g_in = col_slots(d_w_in, 960)
    dh1, (recv_in,) = _mm("mm_d_in", [(drnn, w_rnn, True), (dq, w_qkv[:, :D], True), (dk, w_qkv[:, D:2 * D], True),
                                      (dv, w_qkv[:, 2 * D:], True), (dgates, w_gates, True)], F32, 512, 512,
                          carry=_Exchange([(g_in, True)]))
    recv["w_in"] = recv_in

    def b5(ri, fi, ro, ao):
        md = fi[0]
        n0, rstd0 = _ln(ri[0][...])
        dh = ri[1][...]
        ro[0][...] = ri[2][...] + _ln_bwd(dh * (1.0 + md[1:2, :]), n0, rstd0)
        ao[0][0:1, :] += _rowsum(dh * n0)
        ao[0][1:2, :] += _rowsum(dh)

    grad_x, acc1 = _rowwise(b5, "d_ln_mod1", s_len, tm, [(xs, D, 0), (dh1, D, 0), (dxp, D, 0)], [mod8], [(D, F32)],
                            [((8, D), F32)])

    dmod = jnp.concatenate([acc1[1:2], acc1[0:1], acc2[4:5], acc2[1:2], acc2[0:1], acc5[3:4], jnp.zeros((2, D), F32)],
                           axis=0)
    (recv_dmod,) = _exchange_alone("gather_dmod", _Exchange([(dmod, False)]))
    dmod_all = recv_dmod[:, 0:6, :].reshape(N_DEV, 6 * D)
    red = _sum_slots(recv_small)
    loss = jnp.sum(red[339])

    wmv = (dict(w_in=w_in, w_ffn_gate=w_ffn_gate, w_ffn_up=w_ffn_up, w_proj_rnn=w_proj_rnn, w_proj_attn=w_proj_attn,
                w_out=w_out, w_ffn_down=w_ffn_down),
           dict(w_in=m_w_in, w_ffn_gate=m_w_ffn_gate, w_ffn_up=m_w_ffn_up, w_proj_rnn=m_w_proj_rnn,
                w_proj_attn=m_w_proj_attn, w_out=m_w_out, w_ffn_down=m_w_ffn_down),
           dict(w_in=v_w_in, w_ffn_gate=v_w_ffn_gate, w_ffn_up=v_w_ffn_up, w_proj_rnn=v_w_proj_rnn,
                w_proj_attn=v_w_proj_attn, w_out=v_w_out, w_ffn_down=v_w_ffn_down))
    adam_tile = dict(w_in=128, w_ffn_gate=256, w_ffn_up=256, w_ffn_down=176, w_proj_rnn=160, w_proj_attn=128, w_out=128)
    big_out = {name: _adam_packed("adam_" + name, recv[name], wmv[0][name][0], wmv[1][name][0], wmv[2][name][0], tile)
               for name, tile in adam_tile.items()}
    ada_out = _adam_w_ada(cond_all.T, lax.dynamic_slice(dmod_all, (0, me * 768), (N_DEV, 768)),
                          w_ada[0], m_w_ada[0], v_w_ada[0])

    small_spec = (("b_ada", 6), ("conv_b", 2), ("lru_wa", 160), ("lru_ba", 2), ("lru_wi", 160), ("lru_bi", 2),
                  ("lru_lambda", 2), ("ln1_g", 1), ("ln1_b", 1), ("ln2_g", 1), ("ln2_b", 1), ("conv_w", 1), ("b_gate", 1))
    small_rows = 344

    def pack_small(ps):
        return jnp.concatenate([_rows(ps[name], r) for name, r in small_spec] + [jnp.zeros((4, D), F32)], axis=0)

    names = [n for n, _ in small_spec]
    smalls = [pack_small(dict(zip(names, t))) for t in (
        (b_ada, conv_b, lru_wa, lru_ba, lru_wi, lru_bi, lru_lambda, ln1_g, ln1_b, ln2_g, ln2_b, conv_w, b_gate),
        (m_b_ada, m_conv_b, m_lru_wa, m_lru_ba, m_lru_wi, m_lru_bi, m_lru_lambda, m_ln1_g, m_ln1_b, m_ln2_g, m_ln2_b,
         m_conv_w, m_b_gate),
        (v_b_ada, v_conv_b, v_lru_wa, v_lru_ba, v_lru_wi, v_lru_bi, v_lru_lambda, v_ln1_g, v_ln1_b, v_ln2_g, v_ln2_b,
         v_conv_w, v_b_gate))]
    g_conv_w = lax.dynamic_slice(red[332:337].reshape(-1)[:CONV_W * DR].reshape(CONV_W, DR), (0, me * 160), (CONV_W, 160))
    g_b_gate = lax.dynamic_slice(red[337:339], (0, me * 128), (2, 128))
    g_small_pack = jnp.concatenate([jnp.sum(dmod_all, axis=0).reshape(6, D), red[0:332], _rows(g_conv_w, 1),
                                    _rows(g_b_gate, 1), jnp.zeros((4, D), F32)], axis=0)
    assert g_small_pack.shape == (small_rows, D) and smalls[0].shape == (small_rows, D)
    small_out = _adam_packed("adam_small", g_small_pack, smalls[0], smalls[1], smalls[2], small_rows)

    shapes = dict(w_ada=w_ada.shape, b_ada=b_ada.shape, w_in=w_in.shape, conv_w=conv_w.shape, conv_b=conv_b.shape,
                  lru_wa=lru_wa.shape, lru_ba=lru_ba.shape, lru_wi=lru_wi.shape, lru_bi=lru_bi.shape,
                  lru_lambda=lru_lambda.shape, w_proj_rnn=w_proj_rnn.shape, w_proj_attn=w_proj_attn.shape,
                  b_gate=b_gate.shape, w_out=w_out.shape, ln1_g=ln1_g.shape, ln1_b=ln1_b.shape,
                  w_ffn_gate=w_ffn_gate.shape, w_ffn_up=w_ffn_up.shape, w_ffn_down=w_ffn_down.shape,
                  ln2_g=ln2_g.shape, ln2_b=ln2_b.shape)

    def unpack(kind):
        out = {"w_ada": ada_out[kind].reshape(shapes["w_ada"])}
        for name, _ in PACK_ROWS:
            out[name] = big_out[name][kind].reshape(shapes[name])
        o = 0
        for name, r in small_spec:
            size = 1
            for dim in shapes[name]:
                size *= dim
            out[name] = small_out[kind][o:o + r].reshape(-1)[:size].reshape(shapes[name])
            o += r
        return [out[name] for name in shapes]

    return (loss, grad_x[None], *unpack(0), *unpack(1), *unpack(2), *unpack(3))
```

```python
import functools

import jax
import jax.numpy as jnp
from jax import lax
from jax.experimental import pallas as pl
from jax.experimental.pallas import tpu as pltpu

F32 = jnp.float32
BF16 = jnp.bfloat16
MESH = pl.DeviceIdType.MESH

N_DEV = 8
D = 1024
DR = 1280
N_BLK = 10
BW = 128
HEAD_PAIR = 128
N_PAIR = 8
QB = 128
DFF = 2816
CONV_W = 4
DILATIONS = (1, 4, 16)
ALPHA = 2.0 ** 0.25
LN_EPS = 1e-5
LRU_C = 8.0
SCALE = 0.125
NEG = -1e30
ADAM_LR, ADAM_B1, ADAM_B2, ADAM_EPS, ADAM_WD, ADAM_STEP = 0.001, 0.9, 0.999, 1e-08, 0.01, 10

VMEM_LIMIT = 48 * 1024 * 1024
ROW_TILE = 512
RNN_CHUNK = 256

PACK_ROWS = (("w_in", 960), ("w_ffn_gate", 352), ("w_ffn_up", 352), ("w_proj_rnn", 160),
             ("w_proj_attn", 128), ("w_out", 128), ("w_ffn_down", 352))
LATE_ROWS = PACK_ROWS[1:]


def _cparams(sem):
    return pltpu.CompilerParams(dimension_semantics=sem, vmem_limit_bytes=VMEM_LIMIT)


def _ln(z):
    mu = jnp.mean(z, axis=-1, keepdims=True)
    zc = z - mu
    var = jnp.mean(zc * zc, axis=-1, keepdims=True)
    rstd = lax.rsqrt(var + LN_EPS)
    return zc * rstd, rstd


def _ln_bwd(dn, n, rstd):
    return rstd * (dn - jnp.mean(dn, axis=-1, keepdims=True) - n * jnp.mean(dn * n, axis=-1, keepdims=True))


def _sigmoid(x):
    return 0.5 * jnp.tanh(0.5 * x) + 0.5


_GELU_K = 0.7978845608028654
_GELU_C = 0.044715


def _gelu(y):
    t = jnp.tanh(_GELU_K * (y + _GELU_C * y * y * y))
    return 0.5 * y * (1.0 + t), t


def _gelu_grad(y, t):
    return 0.5 * (1.0 + t) + 0.5 * y * (1.0 - t * t) * _GELU_K * (1.0 + 3.0 * _GELU_C * y * y)


def _rowsum(v):
    return jnp.sum(v, axis=0, keepdims=True)


def _rowwise(body, name, n_rows, tm, row_ins, full_ins, row_outs, acc_outs, carry=None):
    nri, nfi, nro, nao = len(row_ins), len(full_ins), len(row_outs), len(acc_outs)
    n_c = len(carry.arrays) if carry else 0
    n_in = nri + nfi + n_c

    def kern(*refs):
        ri, fi = refs[:nri], refs[nri:nri + nfi]
        ro, ao = refs[n_in:n_in + nro], refs[n_in + nro:n_in + nro + nao]
        if carry:
            i = pl.program_id(0)
            _carry(carry, refs[nri + nfi:n_in], refs[n_in + nro + nao:n_in + nro + nao + n_c], refs[-1], i == 0,
                   i == n_rows // tm - 1)
        if ao:
            @pl.when(pl.program_id(0) == 0)
            def _():
                for a in ao:
                    a[...] = jnp.zeros(a.shape, a.dtype)
        body(ri, fi, ro, ao)

    in_specs = [pl.BlockSpec((tm, w), functools.partial(lambda i, cb: (i, cb), cb=cb)) for _, w, cb in row_ins]
    in_specs += [pl.BlockSpec(a.shape, lambda i: (0, 0)) for a in full_ins]
    out_specs = [pl.BlockSpec((tm, w), lambda i: (i, 0)) for w, _ in row_outs]
    out_specs += [pl.BlockSpec(s, lambda i: (0, 0)) for s, _ in acc_outs]
    out_shape = [jax.ShapeDtypeStruct((n_rows, w), dt) for w, dt in row_outs]
    out_shape += [jax.ShapeDtypeStruct(s, dt) for s, dt in acc_outs]
    operands = [a for a, _, _ in row_ins] + list(full_ins)
    scratch = []
    if carry:
        in_specs, out_specs, out_shape = in_specs + carry.in_specs, out_specs + carry.out_specs, out_shape + carry.out_shape
        operands, scratch = operands + carry.arrays, carry.scratch
    return pl.pallas_call(
        kern, name=name, grid=(n_rows // tm,), in_specs=in_specs, out_specs=out_specs, out_shape=out_shape,
        scratch_shapes=scratch, compiler_params=_cparams(("arbitrary",)),
    )(*operands)


NT_DIMS = (((1,), (1,)), ((), ()))


def _pair_cols(pair):
    return pair[1].shape[0] if len(pair) == 3 else pair[1].shape[1]


def _pair_specs(pairs, tm, tn):
    in_specs, flat = [], []
    for pair in pairs:
        a, w = pair[0], pair[1]
        k = a.shape[1]
        in_specs.append(pl.BlockSpec((tm, k), lambda j, i: (i, 0)))
        if len(pair) == 3:
            in_specs.append(pl.BlockSpec((tn, k), lambda j, i: (j, 0)))
        else:
            in_specs.append(pl.BlockSpec((k, tn), lambda j, i: (0, j)))
        flat += [a, w]
    return in_specs, flat


def _pair_dot(pair, a_ref, w_ref):
    if len(pair) == 3:
        return lax.dot_general(a_ref[...], w_ref[...], NT_DIMS, preferred_element_type=F32)
    return jnp.dot(a_ref[...], w_ref[...], preferred_element_type=F32)


def _mm(name, pairs, out_dtype, tm, tn, carry=None):
    m_rows, n_cols = pairs[0][0].shape[0], _pair_cols(pairs[0])
    n_pairs = len(pairs)
    n_c = len(carry.arrays) if carry else 0
    grid = (n_cols // tn, m_rows // tm)

    def kern(*refs):
        o_ref = refs[2 * n_pairs + n_c]
        if carry:
            step = pl.program_id(0) * grid[1] + pl.program_id(1)
            _carry(carry, refs[2 * n_pairs:2 * n_pairs + n_c], refs[2 * n_pairs + n_c + 1:2 * n_pairs + 2 * n_c + 1],
                   refs[-1], step == 0, step == grid[0] * grid[1] - 1)
        acc = None
        for p in range(n_pairs):
            t = _pair_dot(pairs[p], refs[2 * p], refs[2 * p + 1])
            acc = t if acc is None else acc + t
        o_ref[...] = acc.astype(o_ref.dtype)

    in_specs, flat = _pair_specs(pairs, tm, tn)
    out_specs = [pl.BlockSpec((tm, tn), lambda j, i: (i, j))]
    out_shape = [jax.ShapeDtypeStruct((m_rows, n_cols), out_dtype)]
    if carry:
        res = pl.pallas_call(
            kern, name=name, grid=grid, in_specs=in_specs + carry.in_specs, out_specs=out_specs + carry.out_specs,
            out_shape=out_shape + carry.out_shape, scratch_shapes=carry.scratch,
            compiler_params=_cparams(("arbitrary", "arbitrary")),
        )(*flat, *carry.arrays)
        return res[0], res[1:]
    return pl.pallas_call(
        kern, name=name, grid=grid, in_specs=in_specs, out_specs=out_specs[0], out_shape=out_shape[0],
        compiler_params=_cparams(("parallel", "parallel")),
    )(*flat)


def _mm_tn(name, a, g, tm, tn):
    m_rows, k = a.shape
    n_cols = g.shape[1]

    def kern(a_ref, g_ref, o_ref):
        @pl.when(pl.program_id(1) == 0)
        def _():
            o_ref[...] = jnp.zeros(o_ref.shape, F32)
        o_ref[...] += lax.dot_general(a_ref[...], g_ref[...], (((0,), (0,)), ((), ())), preferred_element_type=F32)

    return pl.pallas_call(
        kern, name=name, grid=(n_cols // tn, m_rows // tm),
        in_specs=[pl.BlockSpec((tm, k), lambda j, m: (m, 0)), pl.BlockSpec((tm, tn), lambda j, m: (m, j))],
        out_specs=pl.BlockSpec((k, tn), lambda j, m: (0, j)),
        out_shape=jax.ShapeDtypeStruct((k, n_cols), F32),
        compiler_params=_cparams(("parallel", "arbitrary")),
    )(a, g)


def _mm_fused(name, pairs, tile_ins, full_ins, outs, accs, epilogue, tm, tn, sub=None):
    m_rows, n_cols = pairs[0][0].shape[0], _pair_cols(pairs[0])
    n_p, n_t, n_f, n_o = len(pairs), len(tile_ins), len(full_ins), len(outs)
    sub = sub or tm

    def kern(*refs):
        base = 2 * n_p
        t_refs, f_refs = refs[base:base + n_t], refs[base + n_t:base + n_t + n_f]
        o_refs, a_refs = refs[base + n_t + n_f:base + n_t + n_f + n_o], refs[base + n_t + n_f + n_o:]
        if a_refs:
            @pl.when((pl.program_id(0) == 0) & (pl.program_id(1) == 0))
            def _():
                for a in a_refs:
                    a[...] = jnp.zeros(a.shape, F32)
        for h in range(tm // sub):
            rows = pl.ds(h * sub, sub)
            dots = [_pair_dot(pairs[p], refs[2 * p].at[rows, :], refs[2 * p + 1]) for p in range(n_p)]
            epilogue(dots, [t.at[rows, :] for t in t_refs], f_refs, [o.at[rows, :] for o in o_refs], a_refs)

    in_specs, flat = _pair_specs(pairs, tm, tn)
    for arr, cb in tile_ins:
        in_specs.append(pl.BlockSpec((tm, tn), functools.partial(lambda j, i, cb: (i, cb + j), cb=cb)))
        flat.append(arr)
    for arr in full_ins:
        in_specs.append(pl.BlockSpec(arr.shape, lambda j, i: (0, 0)))
        flat.append(arr)
    out_specs = [pl.BlockSpec((tm, f * tn), lambda j, i: (i, j)) for _, f in outs]
    out_specs += [pl.BlockSpec(s, lambda j, i: (0, 0)) for s in accs]
    out_shape = [jax.ShapeDtypeStruct((m_rows, f * n_cols), dt) for dt, f in outs]
    out_shape += [jax.ShapeDtypeStruct(s, F32) for s in accs]
    sem = ("arbitrary", "arbitrary") if accs else ("parallel", "parallel")
    return pl.pallas_call(
        kern, name=name, grid=(n_cols // tn, m_rows // tm), in_specs=in_specs, out_specs=out_specs,
        out_shape=out_shape, compiler_params=_cparams(sem),
    )(*flat)


def _peers():
    x, y, c = lax.axis_index("x"), lax.axis_index("y"), lax.axis_index("c")
    me = 4 * x + 2 * y + c
    peers = []
    for k in range(1, N_DEV):
        px = 1 - x if (k >> 2) & 1 else x
        py = 1 - y if (k >> 1) & 1 else y
        pc = 1 - c if k & 1 else c
        peers.append(((px, py, pc), 4 * px + 2 * py + pc))
    return me, peers


def _exchange(me, peers, items, send_sems, recv_sems, t0=0):
    started = []
    for t, (src_of, dst_of) in enumerate(items, start=t0):
        for k, (pid, plin) in enumerate(peers):
            cp = pltpu.make_async_remote_copy(
                src_ref=src_of(plin), dst_ref=dst_of(me), send_sem=send_sems.at[t, k], recv_sem=recv_sems.at[t, k],
                device_id=pid, device_id_type=MESH)
            cp.start()
            started.append(cp)
    for t, (src_of, dst_of) in enumerate(items, start=t0):
        for k, (pid, plin) in enumerate(peers):
            pltpu.make_async_remote_copy(
                src_ref=src_of(plin), dst_ref=dst_of(plin), send_sem=send_sems.at[t, k], recv_sem=recv_sems.at[t, k],
                device_id=pid, device_id_type=MESH).wait_recv()
    for cp in started:
        cp.wait_send()


def _hbm_spec():
    return pl.BlockSpec(memory_space=pltpu.HBM)


class _ChipGather:
    def __init__(self, shard):
        self.arrays = [shard]
        self.out_shape = [jax.ShapeDtypeStruct((N_DEV,) + shard.shape, shard.dtype)]
        self.in_specs, self.out_specs = [_hbm_spec()], [_hbm_spec()]
        self.scratch = [pltpu.SemaphoreType.DMA((SEMS_PER_ITEM,))]

    def _parts(self, ins, outs, sems):
        src, out = ins[0], outs[0]
        x, y, c = lax.axis_index("x"), lax.axis_index("y"), lax.axis_index("c")
        lin = lambda px, py, pc: 4 * px + 2 * py + pc
        sibling = (x, y, 1 - c)
        chips = [(1 - x, y), (x, 1 - y), (1 - x, 1 - y)]

        def copy(k, slot, to, from_src):
            return pltpu.make_async_remote_copy(
                src_ref=src if from_src else out.at[slot], dst_ref=out.at[slot], send_sem=sems.at[k],
                recv_sem=sems.at[N_DEV - 1 + k], device_id=to, device_id_type=MESH)

        own = pltpu.make_async_copy(src, out.at[lin(x, y, c)], sems.at[2 * (N_DEV - 1)])
        first = [copy(0, lin(x, y, c), sibling, True)]
        first += [copy(1 + j, lin(x, y, c), (px, py, c), True) for j, (px, py) in enumerate(chips)]
        passed = [copy(4 + j, lin(px, py, c), sibling, False) for j, (px, py) in enumerate(chips)]
        landed = [copy(1 + j, lin(px, py, c), sibling, True) for j, (px, py) in enumerate(chips)]
        from_sibling = [copy(0, lin(x, y, 1 - c), sibling, True)]
        from_sibling += [copy(4 + j, lin(px, py, 1 - c), sibling, False) for j, (px, py) in enumerate(chips)]
        return own, first, passed, landed, from_sibling

    def start(self, ins, outs, sems):
        own, first, _, _, _ = self._parts(ins, outs, sems)
        own.start()
        for cp in first:
            cp.start()

    def wait(self, ins, outs, sems):
        own, first, passed, landed, from_sibling = self._parts(ins, outs, sems)
        for arrival, forward in zip(landed, passed):
            arrival.wait_recv()
            forward.start()
        for cp in from_sibling:
            cp.wait_recv()
        for cp in first + passed:
            cp.wait_send()
        own.wait()


SEMS_PER_ITEM = 2 * (N_DEV - 1) + 1


class _Exchange:
    def __init__(self, items):
        self.arrays = [a for a, _ in items]
        self.scatter = [s for _, s in items]
        self.out_shape = [jax.ShapeDtypeStruct(a.shape if s else (N_DEV,) + a.shape, a.dtype) for a, s in items]
        self.in_specs = [_hbm_spec() for _ in items]
        self.out_specs = [_hbm_spec() for _ in items]
        self.scratch = [pltpu.SemaphoreType.DMA((SEMS_PER_ITEM * len(items),))]

    def _copies(self, ins, outs, sems, landing):
        me, peers = _peers()
        cps = []
        for t, scatter in enumerate(self.scatter):
            base = t * SEMS_PER_ITEM
            for k, (pid, plin) in enumerate(peers):
                src = ins[t].at[plin] if scatter else ins[t]
                dst = outs[t].at[plin if landing else me]
                cps.append(pltpu.make_async_remote_copy(
                    src_ref=src, dst_ref=dst, send_sem=sems.at[base + k], recv_sem=sems.at[base + N_DEV - 1 + k],
                    device_id=pid, device_id_type=MESH))
        return cps

    def _own(self, ins, outs, sems):
        me, _ = _peers()
        return [pltpu.make_async_copy(ins[t].at[me] if scatter else ins[t], outs[t].at[me],
                                      sems.at[t * SEMS_PER_ITEM + 2 * (N_DEV - 1)])
                for t, scatter in enumerate(self.scatter)]

    def start(self, ins, outs, sems):
        for cp in self._own(ins, outs, sems) + self._copies(ins, outs, sems, False):
            cp.start()

    def wait(self, ins, outs, sems):
        for cp in self._copies(ins, outs, sems, True):
            cp.wait_recv()
        for cp in self._copies(ins, outs, sems, False):
            cp.wait_send()
        for cp in self._own(ins, outs, sems):
            cp.wait()


def _carry(exchange, ins, outs, sems, first, last):
    @pl.when(first)
    def _():
        exchange.start(ins, outs, sems)

    @pl.when(last)
    def _():
        exchange.wait(ins, outs, sems)


def _exchange_alone(name, exchange):
    n = len(exchange.arrays)

    def body(*refs):
        exchange.start(refs[:n], refs[n:2 * n], refs[2 * n])
        exchange.wait(refs[:n], refs[n:2 * n], refs[2 * n])

    return pl.pallas_call(body, name=name, in_specs=exchange.in_specs, out_specs=exchange.out_specs,
                          out_shape=exchange.out_shape, scratch_shapes=exchange.scratch)(*exchange.arrays)


def _ada_modulation(c8, w_ada, b_ada_cols):
    wcols = w_ada.shape[1]

    def body(c_ref, w_ref, b_ref, call_ref, cond_ref, mod_ref, res, send_sems, recv_sems):
        me, peers = _peers()
        call_ref[me] = c_ref[...]
        _exchange(me, peers, [(lambda p: c_ref, lambda s: call_ref.at[s])], send_sems, recv_sems, t0=0)
        for dev in range(N_DEV):
            cv = call_ref[dev]
            cond = cv * _sigmoid(cv)
            cond_ref[dev] = cond
            res[dev] = jnp.dot(cond, w_ref[...], preferred_element_type=F32,
                               precision=lax.Precision.HIGHEST) + b_ref[...]
        mod_ref[me] = res[me]
        _exchange(me, peers, [(lambda p: res.at[p], lambda s: mod_ref.at[s])], send_sems, recv_sems, t0=1)

    vm = pl.BlockSpec(memory_space=pltpu.VMEM)
    return pl.pallas_call(
        body, name="ada_modulation", in_specs=[vm, vm, vm], out_specs=[vm, vm, vm],
        out_shape=[jax.ShapeDtypeStruct((N_DEV, 8, D), F32), jax.ShapeDtypeStruct((N_DEV, 8, D), F32),
                   jax.ShapeDtypeStruct((N_DEV, 8, wcols), F32)],
        scratch_shapes=[pltpu.VMEM((N_DEV, 8, wcols), F32),
                        pltpu.SemaphoreType.DMA((2, N_DEV - 1)), pltpu.SemaphoreType.DMA((2, N_DEV - 1))],
        compiler_params=pltpu.CompilerParams(vmem_limit_bytes=VMEM_LIMIT),
    )(c8, w_ada, b_ada_cols)


def _lru_gates(xc, wa_ref, ba_ref, wi_ref, bi_ref, sp_ref, row0):
    xcb = xc.astype(BF16)
    pre_r, pre_i = [], []
    for n in range(N_BLK):
        xb = xcb[:, n * BW:(n + 1) * BW]
        pre_r.append(jnp.dot(xb, wa_ref[n], preferred_element_type=F32))
        pre_i.append(jnp.dot(xb, wi_ref[n], preferred_element_type=F32))
    r = _sigmoid(jnp.concatenate(pre_r, axis=1) + ba_ref[...])
    ig = _sigmoid(jnp.concatenate(pre_i, axis=1) + bi_ref[...])
    log_a = (-LRU_C) * r * sp_ref[...]
    a = jnp.exp(log_a)
    e2 = jnp.exp(2.0 * log_a)
    mult_raw = jnp.sqrt(1.0 - e2)
    rows = row0 + lax.broadcasted_iota(jnp.int32, xc.shape, 0)
    start = rows == 0
    mult = jnp.where(start, 1.0, mult_raw)
    return xcb, r, ig, a, e2, mult_raw, mult, start


def _conv(xpad, cw_ref, cb_ref, tc):
    out = cb_ref[...]
    for tap in range(CONV_W):
        out = out + cw_ref[tap:tap + 1, :] * xpad[pl.ds(8 - (CONV_W - 1) + tap, tc), :]
    return out


def _rnn_fwd(proj_rnn, cw, cb, wa, ba, wi, bi, sp, carry):
    s_len = proj_rnn.shape[0]
    tc = RNN_CHUNK
    n_c = len(carry.arrays)
    n_out = 5

    def kern(*refs):
        y_ref, xr_ref, cw_ref, cb_ref, wa_ref, ba_ref, wi_ref, bi_ref, sp_ref = refs[:9]
        hs_ref, ur_ref, xc_ref, r_ref, ig_ref = refs[9 + n_c:9 + n_c + n_out]
        xpad, a_scr, u_scr, h_scr = refs[9 + n_out + 2 * n_c:13 + n_out + 2 * n_c]
        i = pl.program_id(0)
        _carry(carry, refs[9:9 + n_c], refs[9 + n_out + n_c:9 + n_out + 2 * n_c], refs[-1], i == 0,
               i == s_len // tc - 1)

        @pl.when(i == 0)
        def _():
            xpad[0:8, :] = jnp.zeros((8, DR), F32)
            h_scr[...] = jnp.zeros((1, DR), F32)

        xpad[8:8 + tc, :] = xr_ref[...].astype(F32)
        xc = _conv(xpad, cw_ref, cb_ref, tc)
        xpad[0:8, :] = xpad[tc:tc + 8, :]
        xcb, r, ig, a, e2, mult_raw, mult, start = _lru_gates(xc, wa_ref, ba_ref, wi_ref, bi_ref, sp_ref, i * tc)
        a_scr[...] = a
        u_scr[...] = mult * (ig * xc)
        xc_ref[...] = xcb
        r_ref[...] = r.astype(BF16)
        ig_ref[...] = ig.astype(BF16)

        def step(t, h):
            h = a_scr[pl.ds(t, 1), :] * h + u_scr[pl.ds(t, 1), :]
            hs_ref[pl.ds(t, 1), :] = h
            return h

        h_scr[...] = lax.fori_loop(0, tc, step, h_scr[...], unroll=8)
        gy, _ = _gelu(y_ref[...].astype(F32))
        ur_ref[...] = (gy * hs_ref[...]).astype(BF16)

    full = lambda a: pl.BlockSpec(a.shape, lambda i: (0,) * a.ndim)
    res = pl.pallas_call(
        kern, name="rnn_fwd", grid=(s_len // tc,),
        in_specs=[pl.BlockSpec((tc, DR), lambda i: (i, 0)), pl.BlockSpec((tc, DR), lambda i: (i, 1)),
                  full(cw), full(cb), full(wa), full(ba), full(wi), full(bi), full(sp)] + carry.in_specs,
        out_specs=[pl.BlockSpec((tc, DR), lambda i: (i, 0))] * n_out + carry.out_specs,
        out_shape=[jax.ShapeDtypeStruct((s_len, DR), F32)] + [jax.ShapeDtypeStruct((s_len, DR), BF16)] * (n_out - 1)
        + carry.out_shape,
        scratch_shapes=[pltpu.VMEM((tc + 8, DR), F32), pltpu.VMEM((tc, DR), F32), pltpu.VMEM((tc, DR), F32),
                        pltpu.VMEM((1, DR), F32)] + carry.scratch,
        compiler_params=_cparams(("arbitrary",)),
    )(proj_rnn, proj_rnn, cw, cb, wa, ba, wi, bi, sp, *carry.arrays)
    return res[0], res[1], res[2:n_out], res[n_out:]


def _rnn_bwd(proj_rnn, hs, du, xc_b, r_b, ig_b, cw, wat, wit, sp, dsp_dlam):
    s_len = proj_rnn.shape[0]
    tc = RNN_CHUNK
    nch = s_len // tc

    def kern(y_ref, xr_ref, xrp_ref, hs_ref, hsp_ref, du_ref, xc_ref, r_ref, ig_ref, cw_ref, wat_ref, wit_ref, sp_ref,
             dspl_ref, drnn_ref, dwa_ref, dwi_ref, vec_ref, xpad, hpad, a_scr, d_scr, g_scr, dxcpad, ag_scr):
        j = pl.program_id(0)
        n = nch - 1 - j

        @pl.when(j == 0)
        def _():
            dwa_ref[...] = jnp.zeros(dwa_ref.shape, F32)
            dwi_ref[...] = jnp.zeros(dwi_ref.shape, F32)
            vec_ref[...] = jnp.zeros(vec_ref.shape, F32)
            ag_scr[...] = jnp.zeros((1, DR), F32)
            dxcpad[tc:tc + 8, :] = jnp.zeros((8, DR), F32)

        has_prev = n > 0
        xpad[0:8, :] = jnp.where(has_prev, xrp_ref[8:16, :].astype(F32), 0.0)
        xpad[8:8 + tc, :] = xr_ref[...].astype(F32)
        hpad[0:8, :] = jnp.where(has_prev, hsp_ref[...], 0.0)
        hpad[8:8 + tc, :] = hs_ref[...]
        xcb = xc_ref[...]
        xc, r, ig = xcb.astype(F32), r_ref[...].astype(F32), ig_ref[...].astype(F32)
        log_a = (-LRU_C) * r * sp_ref[...]
        a = jnp.exp(log_a)
        e2 = jnp.exp(2.0 * log_a)
        mult_raw = jnp.sqrt(1.0 - e2)
        start = (n * tc + lax.broadcasted_iota(jnp.int32, (tc, DR), 0)) == 0
        mult = jnp.where(start, 1.0, mult_raw)

        y = y_ref[...].astype(F32)
        gy, th = _gelu(y)
        duv = du_ref[...].astype(F32)
        drnn_ref[:, 0:DR] = (duv * hs_ref[...] * _gelu_grad(y, th)).astype(BF16)
        a_scr[...] = a
        d_scr[...] = duv * gy

        def step(tt, ag):
            t = tc - 1 - tt
            g = d_scr[pl.ds(t, 1), :] + ag
            g_scr[pl.ds(t, 1), :] = g
            return a_scr[pl.ds(t, 1), :] * g

        ag_scr[...] = lax.fori_loop(0, tc, step, ag_scr[...], unroll=8)
        g = g_scr[...]
        da = g * hpad[pl.ds(7, tc), :]
        gx = g * xc
        dmult = jnp.where(start, 0.0, gx * ig)
        di = gx * mult
        dxc = g * mult * ig
        dlog_a = da * a - jnp.where(start, 0.0, dmult * e2 / mult_raw)
        dr = dlog_a * ((-LRU_C) * sp_ref[...])
        vec_ref[7:8, :] += _rowsum(dlog_a * ((-LRU_C) * r))
        dpr = dr * r * (1.0 - r)
        dpi = di * ig * (1.0 - ig)
        vec_ref[5:6, :] += _rowsum(dpr)
        vec_ref[6:7, :] += _rowsum(dpi)
        dprb, dpib = dpr.astype(BF16), dpi.astype(BF16)
        extra = []
        for b in range(N_BLK):
            sl = slice(b * BW, (b + 1) * BW)
            extra.append(jnp.dot(dprb[:, sl], wat_ref[b], preferred_element_type=F32)
                         + jnp.dot(dpib[:, sl], wit_ref[b], preferred_element_type=F32))
            dn = (((0,), (0,)), ((), ()))
            dwa_ref[b] += lax.dot_general(xcb[:, sl], dprb[:, sl], dn, preferred_element_type=F32)
            dwi_ref[b] += lax.dot_general(xcb[:, sl], dpib[:, sl], dn, preferred_element_type=F32)
        dxc = dxc + jnp.concatenate(extra, axis=1)
        vec_ref[4:5, :] += _rowsum(dxc)
        dxcpad[0:tc, :] = dxc
        dxr = jnp.zeros((tc, DR), F32)
        for tap in range(CONV_W):
            shift = CONV_W - 1 - tap
            dxr = dxr + cw_ref[tap:tap + 1, :] * dxcpad[pl.ds(shift, tc), :]
            vec_ref[tap:tap + 1, :] += _rowsum(dxc * xpad[pl.ds(8 - shift, tc), :])
        dxcpad[tc:tc + 8, :] = dxcpad[0:8, :]
        drnn_ref[:, DR:2 * DR] = dxr.astype(BF16)

        @pl.when(j == nch - 1)
        def _():
            vec_ref[7:8, :] = vec_ref[7:8, :] * dspl_ref[...]

    full = lambda a: pl.BlockSpec(a.shape, lambda j: (0,) * a.ndim)
    rev = lambda j: nch - 1 - j
    prev8 = lambda j: jnp.maximum((nch - 1 - j) * (tc // 8) - 1, 0)
    prev16 = lambda j: jnp.maximum((nch - 1 - j) * (tc // 16) - 1, 0)
    chunk = pl.BlockSpec((tc, DR), lambda j: (rev(j), 0))
    return pl.pallas_call(
        kern, name="rnn_bwd", grid=(nch,),
        in_specs=[chunk, pl.BlockSpec((tc, DR), lambda j: (rev(j), 1)), pl.BlockSpec((16, DR), lambda j: (prev16(j), 1)),
                  chunk, pl.BlockSpec((8, DR), lambda j: (prev8(j), 0)), chunk, chunk, chunk, chunk,
                  full(cw), full(wat), full(wit), full(sp), full(dsp_dlam)],
        out_specs=[pl.BlockSpec((tc, 2 * DR), lambda j: (rev(j), 0)),
                   pl.BlockSpec((N_BLK, BW, BW), lambda j: (0, 0, 0)), pl.BlockSpec((N_BLK, BW, BW), lambda j: (0, 0, 0)),
                   pl.BlockSpec((8, DR), lambda j: (0, 0))],
        out_shape=[jax.ShapeDtypeStruct((s_len, 2 * DR), BF16), jax.ShapeDtypeStruct((N_BLK, BW, BW), F32),
                   jax.ShapeDtypeStruct((N_BLK, BW, BW), F32), jax.ShapeDtypeStruct((8, DR), F32)],
        scratch_shapes=[pltpu.VMEM((tc + 8, DR), F32), pltpu.VMEM((tc + 8, DR), F32), pltpu.VMEM((tc, DR), F32),
                        pltpu.VMEM((tc, DR), F32), pltpu.VMEM((tc, DR), F32), pltpu.VMEM((tc + 8, DR), F32),
                        pltpu.VMEM((1, DR), F32)],
        compiler_params=_cparams(("arbitrary",)),
    )(proj_rnn, proj_rnn, proj_rnn, hs, hs, du, xc_b, r_b, ig_b, cw, wat, wit, sp, dsp_dlam)


ATT_BLK = 2048


def _attn_units():
    return [(g, d, b, b * QB * d + r)
            for g, d in enumerate(DILATIONS) for b in range(ATT_BLK // (QB * d)) for r in range(d)]


def _rows_at(ref, start, n, d):
    return ref[pl.ds(start, n, stride=d), :] if d > 1 else ref[start:start + n, :]


def _add_rows_at(ref, start, n, d, val):
    if d > 1:
        ref[pl.ds(start, n, stride=d), :] = ref[pl.ds(start, n, stride=d), :] + val
    else:
        ref[start:start + n, :] = ref[start:start + n, :] + val


def _stack_heads(x, first):
    zero = jnp.zeros_like(x)
    return jnp.concatenate([jnp.where(first, x, zero), jnp.where(first, zero, x)], axis=0)


def _unstack_heads(x, first):
    return jnp.where(first, x[:QB], x[QB:])


def _head_column(x, lane, half):
    return jnp.sum(jnp.where(lane == 64 * half, x, 0.0), axis=-1, keepdims=True)


def _band_masks():
    row = lax.broadcasted_iota(jnp.int32, (2 * QB, 2 * QB), 0) & (QB - 1)
    col = lax.broadcasted_iota(jnp.int32, (2 * QB, 2 * QB), 1)
    dist = QB + row - col
    lane = lax.broadcasted_iota(jnp.int32, (1, HEAD_PAIR), 1)
    return (dist >= 0) & (dist <= QB), col >= QB, lane


def _attention_specs(nblk, blk_of):
    cur = lambda off: pl.BlockSpec((ATT_BLK, HEAD_PAIR), lambda hp, j: (blk_of(j), off + hp))
    prev = lambda off: pl.BlockSpec((ATT_BLK, HEAD_PAIR), lambda hp, j: (jnp.maximum(blk_of(j) - 1, 0), off + hp))
    return cur, prev


def _attention_fwd(qkv):
    s_len = qkv.shape[0]
    nblk = s_len // ATT_BLK
    units = _attn_units()
    nt = (((1,), (1,)), ((), ()))

    def kern(q_ref, k_ref, v_ref, o_ref, lse_ref, kbuf, vbuf, og, lg):
        blk = pl.program_id(1)

        @pl.when(blk == 0)
        def _():
            kbuf[0:ATT_BLK, :] = jnp.zeros((ATT_BLK, HEAD_PAIR), F32)
            vbuf[0:ATT_BLK, :] = jnp.zeros((ATT_BLK, HEAD_PAIR), F32)

        kbuf[ATT_BLK:2 * ATT_BLK, :] = k_ref[...]
        vbuf[ATT_BLK:2 * ATT_BLK, :] = v_ref[...]
        band, later, lane = _band_masks()
        first = lane < 64
        band0 = band & (later | (blk > 0))
        for g, d, b, start in units:
            kstart = ATT_BLK + start - QB * d
            qs = _stack_heads((_rows_at(q_ref, start, QB, d) * SCALE).astype(BF16), first)
            k2 = _rows_at(kbuf, kstart, 2 * QB, d).astype(BF16)
            v2 = _rows_at(vbuf, kstart, 2 * QB, d).astype(BF16)
            s = lax.dot_general(qs, k2, nt, preferred_element_type=F32)
            valid = band if b > 0 else band0
            ps, dens, lses = [], [], []
            for r0 in (0, QB):
                sh = jnp.where(valid[r0:r0 + QB], s[r0:r0 + QB], NEG)
                mx = jnp.max(sh, axis=-1, keepdims=True)
                ph = jnp.exp(sh - mx)
                den = jnp.sum(ph, axis=-1, keepdims=True)
                ps.append(ph.astype(BF16))
                dens.append(jnp.broadcast_to(den, (QB, HEAD_PAIR)))
                lses.append(jnp.broadcast_to(mx + jnp.log(den), (QB, HEAD_PAIR)))
            out = (jnp.dot(jnp.concatenate(ps, axis=0), v2, preferred_element_type=F32)
                   / jnp.concatenate(dens, axis=0))
            lse2 = jnp.concatenate(lses, axis=0)
            if d > 1:
                og[g, pl.ds(start, QB, stride=d), :] = _unstack_heads(out, first)
                lg[g, pl.ds(start, QB, stride=d), :] = _unstack_heads(lse2, first)
            else:
                og[g, start:start + QB, :] = _unstack_heads(out, first)
                lg[g, start:start + QB, :] = _unstack_heads(lse2, first)
        kbuf[0:ATT_BLK, :] = kbuf[ATT_BLK:2 * ATT_BLK, :]
        vbuf[0:ATT_BLK, :] = vbuf[ATT_BLK:2 * ATT_BLK, :]
        mx = jnp.maximum(jnp.maximum(lg[0], lg[1]), lg[2])
        es = [jnp.exp(lg[g] - mx) for g in range(3)]
        tot = es[0] + es[1] + es[2]
        o_ref[...] = ((es[0] * og[0] + es[1] * og[1] + es[2] * og[2]) / tot).astype(BF16)
        lse_ref[...] = mx + jnp.log(tot)

    cur, _ = _attention_specs(nblk, lambda j: j)
    out_spec = pl.BlockSpec((ATT_BLK, HEAD_PAIR), lambda hp, j: (j, hp))
    return pl.pallas_call(
        kern, name="attention_fwd", grid=(N_PAIR, nblk), in_specs=[cur(0), cur(N_PAIR), cur(2 * N_PAIR)],
        out_specs=[out_spec, out_spec],
        out_shape=[jax.ShapeDtypeStruct((s_len, D), BF16), jax.ShapeDtypeStruct((s_len, D), F32)],
        scratch_shapes=[pltpu.VMEM((2 * ATT_BLK, HEAD_PAIR), F32), pltpu.VMEM((2 * ATT_BLK, HEAD_PAIR), F32),
                        pltpu.VMEM((3, ATT_BLK, HEAD_PAIR), F32), pltpu.VMEM((3, ATT_BLK, HEAD_PAIR), F32)],
        compiler_params=_cparams(("arbitrary", "arbitrary")),
    )(qkv, qkv, qkv)


def _attention_bwd(qkv, do, o, lse, carry):
    s_len = qkv.shape[0]
    nblk = s_len // ATT_BLK
    units = _attn_units()
    nt = (((1,), (1,)), ((), ()))
    tn = (((0,), (0,)), ((), ()))
    n_c = len(carry.arrays) if carry else 0
    c_in, c_out = (carry.in_specs, carry.out_specs) if carry else ([], [])
    c_shape, c_scratch, c_arrays = (carry.out_shape, carry.scratch, carry.arrays) if carry else ([], [], [])

    def kern(*refs):
        q_ref, k_ref, v_ref, kp_ref, vp_ref, do_ref, o_ref, lse_ref = refs[:8]
        dq_ref, dk_ref, dv_ref = refs[8 + n_c:11 + n_c]
        kbuf, vbuf, dkbuf, dvbuf, dq_scr, dsum = refs[11 + 2 * n_c:17 + 2 * n_c]
        hp, j = pl.program_id(0), pl.program_id(1)
        blk = nblk - 1 - j
        if carry:
            _carry(carry, refs[8:8 + n_c], refs[11 + n_c:11 + 2 * n_c], refs[-1], (hp == 0) & (j == 0),
                   (hp == N_PAIR - 1) & (j == nblk - 1))
        zeros = jnp.zeros((ATT_BLK, HEAD_PAIR), F32)

        @pl.when(j == 0)
        def _():
            dkbuf[ATT_BLK:2 * ATT_BLK, :] = zeros
            dvbuf[ATT_BLK:2 * ATT_BLK, :] = zeros

        @pl.when(j > 0)
        def _():
            dkbuf[ATT_BLK:2 * ATT_BLK, :] = dkbuf[0:ATT_BLK, :]
            dvbuf[ATT_BLK:2 * ATT_BLK, :] = dvbuf[0:ATT_BLK, :]

        dkbuf[0:ATT_BLK, :] = zeros
        dvbuf[0:ATT_BLK, :] = zeros
        dq_scr[...] = zeros
        kbuf[0:ATT_BLK, :] = kp_ref[...]
        kbuf[ATT_BLK:2 * ATT_BLK, :] = k_ref[...]
        vbuf[0:ATT_BLK, :] = vp_ref[...]
        vbuf[ATT_BLK:2 * ATT_BLK, :] = v_ref[...]
        band, later, lane = _band_masks()
        first = lane < 64
        band0 = band & (later | (blk > 0))
        prod = do_ref[...] * o_ref[...].astype(F32)
        dsum[...] = jnp.where(first, jnp.sum(jnp.where(first, prod, 0.0), axis=-1, keepdims=True),
                              jnp.sum(jnp.where(first, 0.0, prod), axis=-1, keepdims=True))
        for g, d, b, start in units:
            kstart = ATT_BLK + start - QB * d
            qs = _stack_heads((_rows_at(q_ref, start, QB, d) * SCALE).astype(BF16), first)
            dos = _stack_heads(_rows_at(do_ref, start, QB, d).astype(BF16), first)
            k2 = _rows_at(kbuf, kstart, 2 * QB, d).astype(BF16)
            v2 = _rows_at(vbuf, kstart, 2 * QB, d).astype(BF16)
            ds_rows = _rows_at(dsum, start, QB, d)
            lse_rows = _rows_at(lse_ref, start, QB, d)
            dcol = jnp.concatenate([_head_column(ds_rows, lane, 0), _head_column(ds_rows, lane, 1)], axis=0)
            lcol = jnp.concatenate([_head_column(lse_rows, lane, 0), _head_column(lse_rows, lane, 1)], axis=0)
            s = lax.dot_general(qs, k2, nt, preferred_element_type=F32)
            p = jnp.exp(jnp.where(band if b > 0 else band0, s, NEG) - lcol)
            dp = lax.dot_general(dos, v2, nt, preferred_element_type=F32)
            ds = (p * (dp - dcol)).astype(BF16)
            _add_rows_at(dq_scr, start, QB, d,
                         _unstack_heads(jnp.dot(ds, k2, preferred_element_type=F32), first) * SCALE)
            _add_rows_at(dkbuf, kstart, 2 * QB, d, lax.dot_general(ds, qs, tn, preferred_element_type=F32))
            _add_rows_at(dvbuf, kstart, 2 * QB, d, lax.dot_general(p.astype(BF16), dos, tn, preferred_element_type=F32))
        dq_ref[...] = dq_scr[...].astype(BF16)
        dk_ref[...] = dkbuf[ATT_BLK:2 * ATT_BLK, :].astype(BF16)
        dv_ref[...] = dvbuf[ATT_BLK:2 * ATT_BLK, :].astype(BF16)

    rev = lambda j: nblk - 1 - j
    cur, prev = _attention_specs(nblk, rev)
    buf = lambda rows: pltpu.VMEM((rows, HEAD_PAIR), F32)
    res = pl.pallas_call(
        kern, name="attention_bwd", grid=(N_PAIR, nblk),
        in_specs=[cur(0), cur(N_PAIR), cur(2 * N_PAIR), prev(N_PAIR), prev(2 * N_PAIR), cur(0), cur(0), cur(0)] + c_in,
        out_specs=[cur(0)] * 3 + c_out,
        out_shape=[jax.ShapeDtypeStruct((s_len, D), BF16)] * 3 + c_shape,
        scratch_shapes=[buf(2 * ATT_BLK), buf(2 * ATT_BLK), buf(2 * ATT_BLK), buf(2 * ATT_BLK), buf(ATT_BLK),
                        buf(ATT_BLK)] + c_scratch,
        compiler_params=_cparams(("arbitrary", "arbitrary")),
    )(qkv, qkv, qkv, qkv, qkv, do, o, lse, *c_arrays)
    return res[0], res[1], res[2], res[3:]


def _adamw(w, g, m, v):
    m = ADAM_B1 * m + (1.0 - ADAM_B1) * g
    v = ADAM_B2 * v + (1.0 - ADAM_B2) * (g * g)
    m_hat = m / (1.0 - ADAM_B1 ** ADAM_STEP)
    v_hat = v / (1.0 - ADAM_B2 ** ADAM_STEP)
    delta = -ADAM_LR * (m_hat / (jnp.sqrt(v_hat) + ADAM_EPS) + ADAM_WD * w)
    return delta, m, v


def _adam_packed(name, recv, w, m, v, tm):
    n_rows, width = w.shape
    summed = recv.ndim == 3

    def kern(r_ref, w_ref, m_ref, v_ref, g_out, d_out, m_out, v_out):
        if summed:
            g = r_ref[0].astype(F32)
            for s in range(1, N_DEV):
                g = g + r_ref[s].astype(F32)
        else:
            g = r_ref[...]
        delta, mn, vn = _adamw(w_ref[...], g, m_ref[...], v_ref[...])
        g_out[...] = g
        d_out[...] = delta
        m_out[...] = mn
        v_out[...] = vn

    tile = pl.BlockSpec((tm, width), lambda i: (i, 0))
    rspec = pl.BlockSpec((N_DEV, tm, width), lambda i: (0, i, 0)) if summed else tile
    return pl.pallas_call(
        kern, name=name, grid=(n_rows // tm,), in_specs=[rspec, tile, tile, tile], out_specs=[tile] * 4,
        out_shape=[jax.ShapeDtypeStruct((n_rows, width), F32)] * 4, compiler_params=_cparams(("parallel",)),
    )(recv, w, m, v)


def _sum_slots(recv):
    _, n_rows, width = recv.shape

    def kern(r_ref, o_ref):
        g = r_ref[0]
        for s in range(1, N_DEV):
            g = g + r_ref[s]
        o_ref[...] = g

    return pl.pallas_call(
        kern, name="sum_small", grid=(1,), in_specs=[pl.BlockSpec(recv.shape, lambda i: (0, 0, 0))],
        out_specs=pl.BlockSpec((n_rows, width), lambda i: (0, 0)),
        out_shape=jax.ShapeDtypeStruct((n_rows, width), F32), compiler_params=_cparams(("arbitrary",)),
    )(recv)


def _adam_w_ada(cond_t, dmod_cols, w, m, v):
    n_rows, width = w.shape
    tm = ROW_TILE

    def kern(c_ref, d_ref, w_ref, m_ref, v_ref, g_out, d_out, m_out, v_out):
        g = c_ref[:, 0:1] * d_ref[0:1, :]
        for b in range(1, N_DEV):
            g = g + c_ref[:, b:b + 1] * d_ref[b:b + 1, :]
        delta, mn, vn = _adamw(w_ref[...], g, m_ref[...], v_ref[...])
        g_out[...] = g
        d_out[...] = delta
        m_out[...] = mn
        v_out[...] = vn

    tile = pl.BlockSpec((tm, width), lambda i: (i, 0))
    return pl.pallas_call(
        kern, name="adam_w_ada", grid=(n_rows // tm,),
        in_specs=[pl.BlockSpec((tm, N_DEV), lambda i: (i, 0)), pl.BlockSpec((N_DEV, width), lambda i: (0, 0)),
                  tile, tile, tile],
        out_specs=[tile] * 4, out_shape=[jax.ShapeDtypeStruct((n_rows, width), F32)] * 4,
        compiler_params=_cparams(("parallel",)),
    )(cond_t, dmod_cols, w, m, v)


def _rows(a, n_rows=None):
    flat = a.reshape(-1)
    need = (n_rows if n_rows is not None else -(-flat.shape[0] // D)) * D
    if need != flat.shape[0]:
        flat = jnp.concatenate([flat, jnp.zeros((need - flat.shape[0],), flat.dtype)])
    return flat.reshape(-1, D)


def _slots_to_cols(s, k, cols):
    return s.reshape(N_DEV, k, cols).transpose(1, 0, 2).reshape(k, N_DEV * cols)


def kernel(x, c, w_ada, b_ada, w_in, conv_w, conv_b, lru_wa, lru_ba, lru_wi, lru_bi, lru_lambda, w_proj_rnn, w_proj_attn, b_gate, w_out, ln1_g, ln1_b, w_ffn_gate, w_ffn_up, w_ffn_down, ln2_g, ln2_b, loss_target, m_w_ada, m_b_ada, m_w_in, m_conv_w, m_conv_b, m_lru_wa, m_lru_ba, m_lru_wi, m_lru_bi, m_lru_lambda, m_w_proj_rnn, m_w_proj_attn, m_b_gate, m_w_out, m_ln1_g, m_ln1_b, m_w_ffn_gate, m_w_ffn_up, m_w_ffn_down, m_ln2_g, m_ln2_b, v_w_ada, v_b_ada, v_w_in, v_conv_w, v_conv_b, v_lru_wa, v_lru_ba, v_lru_wi, v_lru_bi, v_lru_lambda, v_w_proj_rnn, v_w_proj_attn, v_b_gate, v_w_out, v_ln1_g, v_ln1_b, v_w_ffn_gate, v_w_ffn_up, v_w_ffn_down, v_ln2_g, v_ln2_b):
    s_len = x.shape[1]
    me = 4 * lax.axis_index("x") + 2 * lax.axis_index("y") + lax.axis_index("c")
    xs = x[0]
    tgt = loss_target[0]
    tm = ROW_TILE

    b_ada_cols = lax.dynamic_slice(b_ada, (0, me * 768), (1, 768))
    pad_cols = lambda a: jnp.concatenate([a, jnp.zeros((a.shape[0], D - a.shape[1]), F32)], axis=1)
    c8 = jnp.concatenate([c, pad_cols(conv_w[0]), pad_cols(b_gate[0]), jnp.zeros((1, D), F32)], axis=0)
    c_all, cond_blocks, mod_parts = _ada_modulation(c8, w_ada[0], b_ada_cols)
    cond_all = cond_blocks[:, 0, :]
    mod = mod_parts[:, 0, :].reshape(6, D)
    mod8 = jnp.concatenate([mod, jnp.zeros((2, D), F32)], axis=0)
    cw = c_all[:, 1:1 + CONV_W, :DR // N_DEV].transpose(1, 0, 2).reshape(CONV_W, DR)
    bg = c_all[:, 1 + CONV_W:3 + CONV_W, :D // N_DEV].transpose(1, 0, 2).reshape(2, D)
    bg8 = jnp.concatenate([bg, jnp.zeros((6, D), F32)], axis=0)

    late = dict(w_ffn_gate=w_ffn_gate, w_ffn_up=w_ffn_up, w_proj_rnn=w_proj_rnn, w_proj_attn=w_proj_attn,
                w_out=w_out, w_ffn_down=w_ffn_down)
    late_shard = jnp.concatenate([late[name][0].reshape(r, D) for name, r in LATE_ROWS], axis=0).astype(BF16)

    cb = conv_b
    wa_b, wi_b = lru_wa[0].astype(BF16), lru_wi[0].astype(BF16)
    wat_b, wit_b = jnp.swapaxes(wa_b, 1, 2), jnp.swapaxes(wi_b, 1, 2)
    ba, bi = lru_ba.reshape(1, DR), lru_bi.reshape(1, DR)
    sp = jax.nn.softplus(-lru_lambda)
    dsp_dlam = -jax.nn.sigmoid(-lru_lambda)
    ln1 = jnp.concatenate([ln1_g, ln1_b, jnp.zeros((6, D), F32)], axis=0)
    ln2 = jnp.concatenate([ln2_g, ln2_b, jnp.zeros((6, D), F32)], axis=0)

    def f1(ri, fi, ro, ao):
        n, _ = _ln(ri[0][...])
        ro[0][...] = (n * (1.0 + fi[0][1:2, :]) + fi[0][0:1, :]).astype(BF16)

    h1, w_in_slots = _rowwise(f1, "ln_mod1", s_len, tm, [(xs, D, 0)], [mod8], [(D, BF16)], [],
                              carry=_ChipGather(w_in[0].astype(BF16)))
    w_in_f = w_in_slots.transpose(1, 0, 2).reshape(D, N_DEV * 960)
    w_rnn, w_qkv, w_gates = w_in_f[:, :2 * DR], w_in_f[:, 2 * DR:2 * DR + 3 * D], w_in_f[:, 2 * DR + 3 * D:]
    proj_rnn = _mm("mm_in_rnn", [(h1, w_rnn)], BF16, 512, DR)
    qkv = _mm("mm_in_qkv", [(h1, w_qkv)], F32, 512, D)
    gates = _mm("mm_in_gates", [(h1, w_gates)], BF16, 512, D)

    hs, u_rnn, kept, (late_all,) = _rnn_fwd(proj_rnn, cw, cb, wa_b, ba, wi_b, bi, sp, _Exchange([(late_shard, False)]))
    offs, o = {}, 0
    for name, r in LATE_ROWS:
        offs[name] = (o, o + r)
        o += r
    part = lambda name: late_all[:, offs[name][0]:offs[name][1], :]
    w_g, w_u = _slots_to_cols(part("w_ffn_gate"), D, 352), _slots_to_cols(part("w_ffn_up"), D, 352)
    w_pr = part("w_proj_rnn").reshape(DR, D)
    w_pa = part("w_proj_attn").reshape(D, D)
    w_o = part("w_out").reshape(D, D)
    w_dn = part("w_ffn_down").reshape(DFF, D)

    u_attn, lse = _attention_fwd(qkv)

    def e_merge(dots, ti, fi, ro, ao):
        g_r = _sigmoid(ti[0][...] + fi[0][0:1, :])
        g_a = _sigmoid(ti[1][...] + fi[0][1:2, :])
        ro[0][...] = dots[0].astype(BF16)
        ro[1][...] = dots[1].astype(BF16)
        ro[2][...] = (g_r * dots[0] + g_a * dots[1]).astype(BF16)

    pr, pa, merged = _mm_fused("mm_proj_merge", [(u_rnn, w_pr), (u_attn, w_pa)], [(gates, 0), (gates, 1)], [bg8],
                               [(BF16, 1), (BF16, 1), (BF16, 1)], [], e_merge, 512, D, sub=256)
    def e_post1(dots, ti, fi, ro, ao):
        md, l1 = fi[0], fi[1]
        fv = dots[0]
        n1, _ = _ln(ALPHA * ti[0][...] + md[2:3, :] * fv)
        x1 = n1 * l1[0:1, :] + l1[1:2, :]
        n0, _ = _ln(x1)
        ro[0][...] = fv
        ro[1][...] = x1
        ro[2][...] = (n0 * (1.0 + md[4:5, :]) + md[3:4, :]).astype(BF16)

    f, x1, h2 = _mm_fused("mm_out_post1", [(merged, w_o)], [(xs, 0)], [mod8, ln1], [(F32, 1), (F32, 1), (BF16, 1)], [],
                          e_post1, 512, D, sub=256)

    def e_swiglu(dots, ti, fi, ro, ao):
        gp, up = dots
        ro[0][...] = gp.astype(BF16)
        ro[1][...] = up.astype(BF16)
        ro[2][...] = (gp * _sigmoid(gp) * up).astype(BF16)

    gpre, upre, act = _mm_fused("mm_ffn_in", [(h2, w_g), (h2, w_u)], [], [], [(BF16, 1)] * 3, [], e_swiglu, 512, DFF // 2)
    def e_loss(dots, ti, fi, ro, ao):
        md, l2 = fi[0], fi[1]
        f2v = dots[0]
        n2, rstd = _ln(ALPHA * ti[0][...] + md[5:6, :] * f2v)
        err = n2 * l2[0:1, :] + l2[1:2, :] - ti[1][...]
        dx2 = err * (1.0 / D)
        dz2 = _ln_bwd(dx2 * l2[0:1, :], n2, rstd)
        ro[0][...] = dz2
        ro[1][...] = (md[5:6, :] * dz2).astype(BF16)
        acc = ao[0]
        acc[0:1, :] += _rowsum(err * err) * (0.5 / D)
        acc[1:2, :] += _rowsum(dx2 * n2)
        acc[2:3, :] += _rowsum(dx2)
        acc[3:4, :] += _rowsum(dz2 * f2v)

    dz2, df2, acc5 = _mm_fused("mm_ffn_out_loss", [(act, w_dn)], [(x1, 0), (tgt, 0)], [mod8, ln2],
                               [(F32, 1), (BF16, 1)], [(8, D)], e_loss, 512, D, sub=256)

    d_w_dn = _mm_tn("mmt_ffn_down", act, df2, 1024, 512)

    def e_dswiglu(dots, ti, fi, ro, ao):
        da = dots[0]
        gp, up = ti[0][...].astype(F32), ti[1][...].astype(F32)
        sg = _sigmoid(gp)
        ro[0][...] = (da * up * sg * (1.0 + gp * (1.0 - sg))).astype(BF16)
        ro[1][...] = (da * gp * sg).astype(BF16)

    dgp, dup = _mm_fused("mm_d_ffn_out", [(df2, w_dn, True)], [(gpre, 0), (upre, 0)], [], [(BF16, 1)] * 2, [],
                         e_dswiglu, 512, DFF // 2, sub=256)
    d_w_g = _mm_tn("mmt_ffn_gate", h2, dgp, 2048, DFF // 2)
    d_w_u = _mm_tn("mmt_ffn_up", h2, dup, 2048, DFF // 2)

    def e_dpost1(dots, ti, fi, ro, ao):
        md, l1 = fi[0], fi[1]
        dh2v = dots[0] + dots[1]
        fv = ti[1][...]
        n1, rstd1 = _ln(ALPHA * ti[0][...] + md[2:3, :] * fv)
        n0, rstd0 = _ln(n1 * l1[0:1, :] + l1[1:2, :])
        dx1 = ALPHA * ti[2][...] + _ln_bwd(dh2v * (1.0 + md[4:5, :]), n0, rstd0)
        dz1 = _ln_bwd(dx1 * l1[0:1, :], n1, rstd1)
        ro[0][...] = ALPHA * dz1
        ro[1][...] = (md[2:3, :] * dz1).astype(BF16)
        acc = ao[0]
        acc[0:1, :] += _rowsum(dh2v * n0)
        acc[1:2, :] += _rowsum(dh2v)
        acc[2:3, :] += _rowsum(dx1 * n1)
        acc[3:4, :] += _rowsum(dx1)
        acc[4:5, :] += _rowsum(dz1 * fv)

    dxp, df, acc2 = _mm_fused("mm_d_ffn_in_post1", [(dgp, w_g, True), (dup, w_u, True)], [(xs, 0), (f, 0), (dz2, 0)],
                              [mod8, ln1], [(F32, 1), (BF16, 1)], [(8, D)], e_dpost1, 256, D)
    d_w_o = _mm_tn("mmt_out", merged, df, 2048, D)

    def e_dmerge(dots, ti, fi, ro, ao):
        dm = dots[0]
        g_r = _sigmoid(ti[0][...] + fi[0][0:1, :])
        g_a = _sigmoid(ti[1][...] + fi[0][1:2, :])
        ro[0][...] = (dm * g_r).astype(BF16)
        ro[1][...] = (dm * g_a).astype(BF16)
        dga = dm * ti[2][...].astype(F32) * g_r * (1.0 - g_r)
        dgb = dm * ti[3][...].astype(F32) * g_a * (1.0 - g_a)
        ro[2][:, 0:D] = dga.astype(BF16)
        ro[2][:, D:2 * D] = dgb.astype(BF16)
        ao[0][0:1, :] += _rowsum(dga)
        ao[0][1:2, :] += _rowsum(dgb)

    dpr, dpa, dgates, acc3 = _mm_fused("mm_d_out", [(df, w_o, True)], [(gates, 0), (gates, 1), (pr, 0), (pa, 0)], [bg8],
                                       [(BF16, 1), (BF16, 1), (BF16, 2)], [(8, D)], e_dmerge, 512, D, sub=256)
    du_rnn = _mm("mm_d_proj_rnn", [(dpr, w_pr, True)], F32, 512, DR)
    du_attn = _mm("mm_d_proj_attn", [(dpa, w_pa, True)], F32, 512, D)
    d_w_pr = _mm_tn("mmt_proj_rnn", u_rnn, dpr, 2048, D)
    d_w_pa = _mm_tn("mmt_proj_attn", u_attn, dpa, 2048, D)

    drnn, d_wa, d_wi, vec = _rnn_bwd(proj_rnn, hs, du_rnn, *kept, cw, wat_b, wit_b, sp, dsp_dlam)

    col_slots = lambda g, cols: g.reshape(g.shape[0], N_DEV, cols).transpose(1, 0, 2).astype(BF16)
    row_slots = lambda g: g.reshape(N_DEV, g.shape[0] // N_DEV, g.shape[1]).astype(BF16)
    early = [("w_ffn_gate", col_slots(d_w_g, 352)), ("w_ffn_up", col_slots(d_w_u, 352)), ("w_ffn_down", row_slots(d_w_dn)),
             ("w_proj_rnn", row_slots(d_w_pr)), ("w_proj_attn", row_slots(d_w_pa)), ("w_out", row_slots(d_w_o))]
    g_small = jnp.concatenate([
        _rows(vec[4], 2), d_wa.reshape(160, D), _rows(vec[5], 2), d_wi.reshape(160, D), _rows(vec[6], 2), _rows(vec[7], 2),
        acc2[2:4], acc5[1:3],
        _rows(vec[0:4], 5), acc3[0:2], acc5[0:1], jnp.zeros((4, D), F32)], axis=0)
    dq, dk, dv, recvd = _attention_bwd(qkv, du_attn, u_attn, lse,
                                       _Exchange([(g, True) for _, g in early] + [(g_small, False)]))
    recv = {name: r for (name, _), r in zip(early, recvd)}
    recv_small = recvd[-1]

    d_w_in = jnp.concatenate([_mm_tn("mmt_in_rnn", h1, drnn, 2048, DR), _mm_tn("mmt_in_q", h1, dq, 2048, D),
                              _mm_tn("mmt_in_k", h1, dk, 2048, D), _mm_tn("mmt_in_v", h1, dv, 2048, D),
                              _mm_tn("mmt_in_gates", h1, dgates, 2048, D)], axis=1)
    g_in = col_slots(d_w_in, 960)
    dh1, (recv_in,) = _mm("mm_d_in", [(drnn, w_rnn, True), (dq, w_qkv[:, :D], True), (dk, w_qkv[:, D:2 * D], True),
                                      (dv, w_qkv[:, 2 * D:], True), (dgates, w_gates, True)], F32, 512, 512,
                          carry=_Exchange([(g_in, True)]))
    recv["w_in"] = recv_in

    def b5(ri, fi, ro, ao):
        md = fi[0]
        n0, rstd0 = _ln(ri[0][...])
        dh = ri[1][...]
        ro[0][...] = ri[2][...] + _ln_bwd(dh * (1.0 + md[1:2, :]), n0, rstd0)
        ao[0][0:1, :] += _rowsum(dh * n0)
        ao[0][1:2, :] += _rowsum(dh)

    grad_x, acc1 = _rowwise(b5, "d_ln_mod1", s_len, tm, [(xs, D, 0), (dh1, D, 0), (dxp, D, 0)], [mod8], [(D, F32)],
                            [((8, D), F32)])

    dmod = jnp.concatenate([acc1[1:2], acc1[0:1], acc2[4:5], acc2[1:2], acc2[0:1], acc5[3:4], jnp.zeros((2, D), F32)],
                           axis=0)
    (recv_dmod,) = _exchange_alone("gather_dmod", _Exchange([(dmod, False)]))
    dmod_all = recv_dmod[:, 0:6, :].reshape(N_DEV, 6 * D)
    red = _sum_slots(recv_small)
    loss = jnp.sum(red[339])

    wmv = (dict(w_in=w_in, w_ffn_gate=w_ffn_gate, w_ffn_up=w_ffn_up, w_proj_rnn=w_proj_rnn, w_proj_attn=w_proj_attn,
                w_out=w_out, w_ffn_down=w_ffn_down),
           dict(w_in=m_w_in, w_ffn_gate=m_w_ffn_gate, w_ffn_up=m_w_ffn_up, w_proj_rnn=m_w_proj_rnn,
                w_proj_attn=m_w_proj_attn, w_out=m_w_out, w_ffn_down=m_w_ffn_down),
           dict(w_in=v_w_in, w_ffn_gate=v_w_ffn_gate, w_ffn_up=v_w_ffn_up, w_proj_rnn=v_w_proj_rnn,
                w_proj_attn=v_w_proj_attn, w_out=v_w_out, w_ffn_down=v_w_ffn_down))
    adam_tile = dict(w_in=128, w_ffn_gate=256, w_ffn_up=256, w_ffn_down=176, w_proj_rnn=160, w_proj_attn=128, w_out=128)
    big_out = {name: _adam_packed("adam_" + name, recv[name], wmv[0][name][0], wmv[1][name][0], wmv[2][name][0], tile)
               for name, tile in adam_tile.items()}
    ada_out = _adam_w_ada(cond_all.T, lax.dynamic_slice(dmod_all, (0, me * 768), (N_DEV, 768)),
                          w_ada[0], m_w_ada[0], v_w_ada[0])

    small_spec = (("b_ada", 6), ("conv_b", 2), ("lru_wa", 160), ("lru_ba", 2), ("lru_wi", 160), ("lru_bi", 2),
                  ("lru_lambda", 2), ("ln1_g", 1), ("ln1_b", 1), ("ln2_g", 1), ("ln2_b", 1), ("conv_w", 1), ("b_gate", 1))
    small_rows = 344

    def pack_small(ps):
        return jnp.concatenate([_rows(ps[name], r) for name, r in small_spec] + [jnp.zeros((4, D), F32)], axis=0)

    names = [n for n, _ in small_spec]
    smalls = [pack_small(dict(zip(names, t))) for t in (
        (b_ada, conv_b, lru_wa, lru_ba, lru_wi, lru_bi, lru_lambda, ln1_g, ln1_b, ln2_g, ln2_b, conv_w, b_gate),
        (m_b_ada, m_conv_b, m_lru_wa, m_lru_ba, m_lru_wi, m_lru_bi, m_lru_lambda, m_ln1_g, m_ln1_b, m_ln2_g, m_ln2_b,
         m_conv_w, m_b_gate),
        (v_b_ada, v_conv_b, v_lru_wa, v_lru_ba, v_lru_wi, v_lru_bi, v_lru_lambda, v_ln1_g, v_ln1_b, v_ln2_g, v_ln2_b,
         v_conv_w, v_b_gate))]
    g_conv_w = lax.dynamic_slice(red[332:337].reshape(-1)[:CONV_W * DR].reshape(CONV_W, DR), (0, me * 160), (CONV_W, 160))
    g_b_gate = lax.dynamic_slice(red[337:339], (0, me * 128), (2, 128))
    g_small_pack = jnp.concatenate([jnp.sum(dmod_all, axis=0).reshape(6, D), red[0:332], _rows(g_conv_w, 1),
                                    _rows(g_b_gate, 1), jnp.zeros((4, D), F32)], axis=0)
    assert g_small_pack.shape == (small_rows, D) and smalls[0].shape == (small_rows, D)
    small_out = _adam_packed("adam_small", g_small_pack, smalls[0], smalls[1], smalls[2], small_rows)

    shapes = dict(w_ada=w_ada.shape, b_ada=b_ada.shape, w_in=w_in.shape, conv_w=conv_w.shape, conv_b=conv_b.shape,
                  lru_wa=lru_wa.shape, lru_ba=lru_ba.shape, lru_wi=lru_wi.shape, lru_bi=lru_bi.shape,
                  lru_lambda=lru_lambda.shape, w_proj_rnn=w_proj_rnn.shape, w_proj_attn=w_proj_attn.shape,
                  b_gate=b_gate.shape, w_out=w_out.shape, ln1_g=ln1_g.shape, ln1_b=ln1_b.shape,
                  w_ffn_gate=w_ffn_gate.shape, w_ffn_up=w_ffn_up.shape, w_ffn_down=w_ffn_down.shape,
                  ln2_g=ln2_g.shape, ln2_b=ln2_b.shape)

    def unpack(kind):
        out = {"w_ada": ada_out[kind].reshape(shapes["w_ada"])}
        for name, _ in PACK_ROWS:
            out[name] = big_out[name][kind].reshape(shapes[name])
        o = 0
        for name, r in small_spec:
            size = 1
            for dim in shapes[name]:
                size *= dim
            out[name] = small_out[kind][o:o + r].reshape(-1)[:size].reshape(shapes[name])
            o += r
        return [out[name] for name in shapes]

    return (loss, grad_x[None], *unpack(0), *unpack(1), *unpack(2), *unpack(3))
```
